```python
import math
import jax
import jax.numpy as jnp
from jax import lax
import numpy as np

D_MODEL = 1024
BATCH = 4
SEQ = 4096
DEPTH = 2
DEC_BATCH = 32
DEC_SEQ = 1
PAST_LEN = 16384
PAGE_SIZE = 128

N_EVEN = (DEPTH + 1) // 2
N_ODD = DEPTH // 2
NORM_EPS = 1e-6

ATT_HEADS = 4
ATT_D = 64
ATT_WIDTH = ATT_HEADS * 2 * ATT_D
Q_BLOCK = 128

S5_WIDTH = D_MODEL // 2
S5_GROUP = 16
S5_GROUPS = S5_WIDTH // S5_GROUP
S5_STATE = 64

EVEN_IN = 3 * ATT_WIDTH + S5_WIDTH
EVEN_MIX = ATT_WIDTH + S5_WIDTH

SSD_INNER = 2 * D_MODEL
SSD_HEADDIM = 64
SSD_HEADS = SSD_INNER // SSD_HEADDIM
SSD_GROUPS = 4
SSD_HPG = SSD_HEADS // SSD_GROUPS
SSD_STATE = 128
SSD_CONV = 4
SSD_CONV_DIM = SSD_INNER + 2 * SSD_GROUPS * SSD_STATE
SSD_IN = SSD_INNER + SSD_CONV_DIM + SSD_HEADS
SSD_CHUNK = 128

MOE_GROUPS = 4
MOE_EPG = 8
MOE_EXPERTS = MOE_GROUPS * MOE_EPG
MOE_TOPK = 2
MOE_FF = D_MODEL // 4

kernel_name = 'hybrid_s5_diffattn_ssd_hmoe_step'

F32 = jnp.float32


def _rmsnorm(x, g):
    xf = x.astype(F32)
    xf = xf * lax.rsqrt(jnp.mean(xf * xf, axis=-1, keepdims=True) + NORM_EPS)
    return (xf * g.astype(F32)).astype(x.dtype)


def _diff_lambda(lq1, lk1, lq2, lk2, lam_init):
    return (jnp.exp(jnp.sum(lq1.astype(F32) * lk1.astype(F32)))
            - jnp.exp(jnp.sum(lq2.astype(F32) * lk2.astype(F32))) + lam_init)


def _diff_weights(scores, lam):
    p = jax.nn.softmax(scores, axis=-1)
    return p[:, 0] - lam * p[:, 1]


def _diff_attn_prompt(q, k, v, lam):
    b, s = q.shape[:2]
    nb = s // Q_BLOCK
    kk = k.reshape(b, s, ATT_HEADS, 2, ATT_D)
    qb = jnp.moveaxis(q.reshape(b, nb, Q_BLOCK, ATT_HEADS, 2, ATT_D), 1, 0)
    k_pos = jnp.arange(s)
    scale = ATT_D ** -0.5

    def block(args):
        q_blk, start = args
        sc = jnp.einsum('bqhcd,bkhcd->bchqk', q_blk, kk).astype(F32) * scale
        q_pos = start + jnp.arange(Q_BLOCK)
        sc = jnp.where(k_pos[None, :] <= q_pos[:, None], sc, -jnp.inf)
        w = _diff_weights(sc, lam).astype(v.dtype)
        return jnp.einsum('bhqk,bkhe->bqhe', w, v)

    out = lax.map(block, (qb, jnp.arange(nb) * Q_BLOCK))
    return jnp.moveaxis(out, 0, 1).reshape(b, s, ATT_HEADS, 2 * ATT_D)


def _diff_attn_sample(q, k_new, v_new, k_past, v_past, lam):
    b, t = q.shape[:2]
    n_past = k_past.shape[1]
    kn = k_new.reshape(b, t, ATT_HEADS, 2, ATT_D)
    scale = ATT_D ** -0.5
    s_past = jnp.einsum('bqhcd,bkhcd->bchqk', q, k_past).astype(F32) * scale
    s_new = jnp.einsum('bqhcd,bkhcd->bchqk', q, kn).astype(F32) * scale
    s_new = jnp.where(jnp.tril(jnp.ones((t, t), bool)), s_new, -jnp.inf)
    w = _diff_weights(jnp.concatenate([s_past, s_new], axis=-1), lam).astype(v_new.dtype)
    return (jnp.einsum('bhqk,bkhe->bqhe', w[..., :n_past], v_past)
            + jnp.einsum('bhqk,bkhe->bqhe', w[..., n_past:], v_new))


def _lin_combine(c1, c2):
    a1, b1 = c1
    a2, b2 = c2
    return a1 * a2, a2 * b1 + b2


def _s5(u, h0_re, h0_im, lam_re, lam_im, log_dt, b_re, b_im, c_re, c_im, d_skip):
    lam = lax.complex(lam_re.astype(F32), lam_im.astype(F32))
    dt = jnp.exp(log_dt.astype(F32))[:, None]
    a_bar = jnp.exp(lam * dt)
    bmat = lax.complex(b_re.astype(F32), b_im.astype(F32))
    b_bar = ((a_bar - 1.0) / lam)[..., None] * bmat
    cmat = lax.complex(c_re.astype(F32), c_im.astype(F32))
    bu = jnp.einsum('gpc,blgc->blgp', b_bar, u.astype(jnp.complex64))
    h0 = lax.complex(h0_re.astype(F32), h0_im.astype(F32))
    bu = bu.at[:, 0].add(a_bar * h0)
    a = jnp.broadcast_to(a_bar, bu.shape)
    _, h = lax.associative_scan(_lin_combine, (a, bu), axis=1)
    y = jnp.real(jnp.einsum('gcp,blgp->blgc', cmat, h)) + d_skip.astype(F32) * u
    return y, h[:, -1]


def _even_mixer(h, p, lam_init, h0_re, h0_im, k_past=None, v_past=None):
    b, l, _ = h.shape
    proj = h @ p['w_in']
    q = proj[..., :ATT_WIDTH].reshape(b, l, ATT_HEADS, 2, ATT_D)
    k = proj[..., ATT_WIDTH:2 * ATT_WIDTH].reshape(b, l, ATT_HEADS, 2 * ATT_D)
    v = proj[..., 2 * ATT_WIDTH:3 * ATT_WIDTH].reshape(b, l, ATT_HEADS, 2 * ATT_D)
    u = proj[..., 3 * ATT_WIDTH:].reshape(b, l, S5_GROUPS, S5_GROUP)
    lam = _diff_lambda(p['lq1'], p['lk1'], p['lq2'], p['lk2'], lam_init)
    if k_past is None:
        o = _diff_attn_prompt(q, k, v, lam)
    else:
        o = _diff_attn_sample(q, k, v, k_past, v_past, lam)
    o = _rmsnorm(o, p['subln']) * (1.0 - lam_init)
    attn_out = o.reshape(b, l, ATT_WIDTH)
    y, h_last = _s5(u.astype(F32), h0_re, h0_im, p['lam_re'], p['lam_im'], p['log_dt'],
                    p['b_re'], p['b_im'], p['c_re'], p['c_im'], p['d'])
    g = jax.nn.gelu(y.reshape(b, l, S5_WIDTH))
    s5_out = (g * jax.nn.sigmoid(g @ p['w_glu'].astype(F32) + p['b_glu'].astype(F32))).astype(h.dtype)
    out = jnp.concatenate([attn_out, s5_out], axis=-1) @ p['w_out']
    return out, k, v, jnp.real(h_last), jnp.imag(h_last)


def _causal_conv(xbc, buf, w, bias):
    l = xbc.shape[1]
    full = jnp.concatenate([buf.astype(xbc.dtype), xbc], axis=1)
    out = bias
    for i in range(SSD_CONV):
        out = out + full[:, i:i + l] * w[i]
    return out, full[:, l:]


def _ssd_scan(x, dt, a, bm, cm, h0):
    b, l = x.shape[:2]
    q = SSD_CHUNK if l >= SSD_CHUNK else l
    nc = -(-l // q)
    pad = nc * q - l
    if pad:
        def padl(t):
            return jnp.pad(t, [(0, 0), (0, pad)] + [(0, 0)] * (t.ndim - 2))
        x, dt, bm, cm = padl(x), padl(dt), padl(bm), padl(cm)
    x = x.reshape(b, nc, q, SSD_GROUPS, SSD_HPG, SSD_HEADDIM)
    dt = dt.reshape(b, nc, q, SSD_GROUPS, SSD_HPG)
    bm = bm.reshape(b, nc, q, SSD_GROUPS, SSD_STATE)
    cm = cm.reshape(b, nc, q, SSD_GROUPS, SSD_STATE)
    xdt = x * dt[..., None]
    a_cs = jnp.cumsum(dt * a, axis=2)
    seg = a_cs[:, :, :, None] - a_cs[:, :, None, :]
    causal = jnp.tril(jnp.ones((q, q), bool))[:, :, None, None]
    decay = jnp.exp(jnp.where(causal, seg, -jnp.inf))
    cb = jnp.einsum('bclgn,bcsgn->bclsg', cm, bm)
    y_diag = jnp.einsum('bclsgr,bcsgrp->bclgrp', cb[..., None] * decay, xdt)
    decay_s = jnp.exp(a_cs[:, :, -1:] - a_cs)
    states = jnp.einsum('bcsgn,bcsgr,bcsgrp->bcgrpn', bm, decay_s, xdt)
    chunk_decay = jnp.exp(a_cs[:, :, -1])

    def step(hc, inp):
        dec, st = inp
        return dec[..., None, None] * hc + st, hc

    h_last, h_start = lax.scan(step, h0, (jnp.moveaxis(chunk_decay, 1, 0), jnp.moveaxis(states, 1, 0)))
    h_start = jnp.moveaxis(h_start, 0, 1)
    y_off = jnp.einsum('bclgn,bcgrpn,bclgr->bclgrp', cm, h_start, jnp.exp(a_cs))
    y = (y_diag + y_off).reshape(b, nc * q, SSD_GROUPS, SSD_HPG, SSD_HEADDIM)[:, :l]
    return y, h_last


def _ssd_mixer(h, p, h0, conv_buf):
    b, l, _ = h.shape
    zxbcdt = h @ p['w_in']
    z = zxbcdt[..., :SSD_INNER]
    xbc = zxbcdt[..., SSD_INNER:SSD_INNER + SSD_CONV_DIM]
    dt_raw = zxbcdt[..., SSD_INNER + SSD_CONV_DIM:]
    xbc_c, new_buf = _causal_conv(xbc, conv_buf, p['conv_w'], p['conv_b'])
    xbc_c = jax.nn.silu(xbc_c).astype(F32)
    xs = xbc_c[..., :SSD_INNER].reshape(b, l, SSD_GROUPS, SSD_HPG, SSD_HEADDIM)
    bm = xbc_c[..., SSD_INNER:SSD_INNER + SSD_GROUPS * SSD_STATE].reshape(b, l, SSD_GROUPS, SSD_STATE)
    cm = xbc_c[..., SSD_INNER + SSD_GROUPS * SSD_STATE:].reshape(b, l, SSD_GROUPS, SSD_STATE)
    dt = jax.nn.softplus(dt_raw.astype(F32) + p['dt_bias'].astype(F32)).reshape(b, l, SSD_GROUPS, SSD_HPG)
    a = -jnp.exp(p['a_log'].astype(F32)).reshape(SSD_GROUPS, SSD_HPG)
    h0g = h0.astype(F32).reshape(b, SSD_GROUPS, SSD_HPG, SSD_HEADDIM, SSD_STATE)
    y, h_last = _ssd_scan(xs, dt, a, bm, cm, h0g)
    y = y + p['d'].astype(F32).reshape(SSD_GROUPS, SSD_HPG)[..., None] * xs
    y = y.reshape(b, l, SSD_INNER) * jax.nn.silu(z.astype(F32))
    y = _rmsnorm(y, p['norm']).astype(h.dtype)
    return y @ p['w_out'], h_last.reshape(b, SSD_HEADS, SSD_HEADDIM, SSD_STATE), new_buf


def _hier_moe(x, w_group, b_group, w_expert, b_expert, w_gate, w_up, w_down):
    b, l, d = x.shape
    t = x.reshape(-1, d)
    g_prob = jax.nn.softmax((t @ w_group).astype(F32) + b_group.astype(F32), axis=-1)
    g_p, g_idx = lax.top_k(g_prob, 1)
    e_logits = ((t @ w_expert).astype(F32) + b_expert.astype(F32)).reshape(-1, MOE_GROUPS, MOE_EPG)
    e_in = jnp.take_along_axis(e_logits, g_idx[:, :, None], axis=1)[:, 0]
    e_l, e_idx = lax.top_k(e_in, MOE_TOPK)
    gates = g_p * jax.nn.softmax(e_l, axis=-1)
    ids = g_idx * MOE_EPG + e_idx
    combine = jnp.sum(jax.nn.one_hot(ids, MOE_EXPERTS, dtype=F32) * gates[..., None], axis=1)
    y = jnp.zeros((t.shape[0], d), F32)
    for e in range(MOE_EXPERTS):
        hdn = jax.nn.silu(t @ w_gate[e]) * (t @ w_up[e])
        y = y + combine[:, e:e + 1] * (hdn @ w_down[e]).astype(F32)
    return y.astype(x.dtype).reshape(b, l, d)


def setup_inputs(seed: int = 0) -> dict:
    key = jax.random.key(seed)
    split = jax.random.split(key, 64)
    it = iter([split[i] for i in range(64)])

    def nrm(shape, scale):
        return jax.random.normal(next(it), shape, F32) * scale

    def uni(shape, lo, hi):
        return jax.random.uniform(next(it), shape, F32, lo, hi)

    n_pages = PAST_LEN // PAGE_SIZE
    n_pool = (5 * DEC_BATCH * n_pages) // 4
    perm = jax.random.permutation(next(it), n_pool)
    page_table = perm[:DEC_BATCH * n_pages].reshape(DEC_BATCH, n_pages).astype(jnp.int32)
    kv_shape = (N_EVEN, n_pool, PAGE_SIZE, ATT_HEADS, 2 * ATT_D)
    ssd_dt = jnp.exp(uni((N_ODD, SSD_HEADS), math.log(1e-3), math.log(1e-1)))
    return {
        'x_prompt': nrm((BATCH, SEQ, D_MODEL), 1.0),
        'x_sample': nrm((DEC_BATCH, DEC_SEQ, D_MODEL), 1.0),
        'cache_k': nrm(kv_shape, 1.0),
        'cache_v': nrm(kv_shape, 1.0),
        'page_table': page_table,
        'state_s5_re': nrm((N_EVEN, DEC_BATCH, S5_GROUPS, S5_STATE), 0.1),
        'state_s5_im': nrm((N_EVEN, DEC_BATCH, S5_GROUPS, S5_STATE), 0.1),
        'state_ssd': nrm((N_ODD, DEC_BATCH, SSD_HEADS, SSD_HEADDIM, SSD_STATE), 0.1),
        'state_conv': nrm((N_ODD, DEC_BATCH, SSD_CONV - 1, SSD_CONV_DIM), 1.0),
        'norm_mix': 1.0 + nrm((DEPTH, D_MODEL), 0.02),
        'norm_ffn': 1.0 + nrm((DEPTH, D_MODEL), 0.02),
        'norm_final': 1.0 + nrm((D_MODEL,), 0.02),
        'even_w_in': nrm((N_EVEN, D_MODEL, EVEN_IN), D_MODEL ** -0.5),
        'even_w_out': nrm((N_EVEN, EVEN_MIX, D_MODEL), EVEN_MIX ** -0.5),
        'diff_lam_q1': nrm((N_EVEN, ATT_D), 0.1),
        'diff_lam_k1': nrm((N_EVEN, ATT_D), 0.1),
        'diff_lam_q2': nrm((N_EVEN, ATT_D), 0.1),
        'diff_lam_k2': nrm((N_EVEN, ATT_D), 0.1),
        'diff_subln': 1.0 + nrm((N_EVEN, 2 * ATT_D), 0.02),
        's5_lam_re': -0.5 + nrm((N_EVEN, S5_GROUPS, S5_STATE), 0.01),
        's5_lam_im': math.pi * jnp.arange(S5_STATE, dtype=F32) + nrm((N_EVEN, S5_GROUPS, S5_STATE), 0.01),
        's5_log_dt': uni((N_EVEN, S5_GROUPS), math.log(1e-3), math.log(1e-1)),
        's5_b_re': nrm((N_EVEN, S5_GROUPS, S5_STATE, S5_GROUP), (2 * S5_GROUP) ** -0.5),
        's5_b_im': nrm((N_EVEN, S5_GROUPS, S5_STATE, S5_GROUP), (2 * S5_GROUP) ** -0.5),
        's5_c_re': nrm((N_EVEN, S5_GROUPS, S5_GROUP, S5_STATE), (2 * S5_STATE) ** -0.5),
        's5_c_im': nrm((N_EVEN, S5_GROUPS, S5_GROUP, S5_STATE), (2 * S5_STATE) ** -0.5),
        's5_d': nrm((N_EVEN, S5_GROUPS, S5_GROUP), 1.0),
        's5_w_glu': nrm((N_EVEN, S5_WIDTH, S5_WIDTH), S5_WIDTH ** -0.5),
        's5_b_glu': nrm((N_EVEN, S5_WIDTH), 0.01),
        'ssd_w_in': nrm((N_ODD, D_MODEL, SSD_IN), D_MODEL ** -0.5),
        'ssd_conv_w': nrm((N_ODD, SSD_CONV, SSD_CONV_DIM), SSD_CONV ** -0.5),
        'ssd_conv_b': nrm((N_ODD, SSD_CONV_DIM), 0.01),
        'ssd_dt_bias': ssd_dt + jnp.log(-jnp.expm1(-ssd_dt)),
        'ssd_a_log': jnp.log(uni((N_ODD, SSD_HEADS), 1.0, 16.0)),
        'ssd_d': 1.0 + nrm((N_ODD, SSD_HEADS), 0.02),
        'ssd_norm': 1.0 + nrm((N_ODD, SSD_INNER), 0.02),
        'ssd_w_out': nrm((N_ODD, SSD_INNER, D_MODEL), SSD_INNER ** -0.5),
        'moe_w_group': nrm((DEPTH, D_MODEL, MOE_GROUPS), D_MODEL ** -0.5),
        'moe_b_group': nrm((DEPTH, MOE_GROUPS), 0.01),
        'moe_w_expert': nrm((DEPTH, D_MODEL, MOE_EXPERTS), D_MODEL ** -0.5),
        'moe_b_expert': nrm((DEPTH, MOE_EXPERTS), 0.01),
        'moe_w_gate': nrm((DEPTH, MOE_EXPERTS, D_MODEL, MOE_FF), D_MODEL ** -0.5),
        'moe_w_up': nrm((DEPTH, MOE_EXPERTS, D_MODEL, MOE_FF), D_MODEL ** -0.5),
        'moe_w_down': nrm((DEPTH, MOE_EXPERTS, MOE_FF, D_MODEL), MOE_FF ** -0.5),
    }


def reference(x_prompt, x_sample, cache_k, cache_v, page_table, state_s5_re, state_s5_im, state_ssd, state_conv,
              norm_mix, norm_ffn, norm_final, even_w_in, even_w_out, diff_lam_q1, diff_lam_k1, diff_lam_q2,
              diff_lam_k2, diff_subln, s5_lam_re, s5_lam_im, s5_log_dt, s5_b_re, s5_b_im, s5_c_re, s5_c_im, s5_d,
              s5_w_glu, s5_b_glu, ssd_w_in, ssd_conv_w, ssd_conv_b, ssd_dt_bias, ssd_a_log, ssd_d, ssd_norm,
              ssd_w_out, moe_w_group, moe_b_group, moe_w_expert, moe_b_expert, moe_w_gate, moe_w_up, moe_w_down):
    b_p = x_prompt.shape[0]
    b_s = x_sample.shape[0]
    y_p, y_s = x_prompt, x_sample
    k_p, v_p, k_s, v_s = [], [], [], []
    s5re_p, s5im_p, s5re_s, s5im_s = [], [], [], []
    ssd_p, ssd_s, conv_p, conv_s = [], [], [], []
    for li in range(DEPTH):
        if li % 2 == 0:
            e = li // 2
            lam_init = 0.8 - 0.6 * math.exp(-0.3 * li)
            p = {'w_in': even_w_in[e], 'w_out': even_w_out[e], 'lq1': diff_lam_q1[e], 'lk1': diff_lam_k1[e],
                 'lq2': diff_lam_q2[e], 'lk2': diff_lam_k2[e], 'subln': diff_subln[e],
                 'lam_re': s5_lam_re[e], 'lam_im': s5_lam_im[e], 'log_dt': s5_log_dt[e],
                 'b_re': s5_b_re[e], 'b_im': s5_b_im[e], 'c_re': s5_c_re[e], 'c_im': s5_c_im[e],
                 'd': s5_d[e], 'w_glu': s5_w_glu[e], 'b_glu': s5_b_glu[e]}
            z0 = jnp.zeros((b_p, S5_GROUPS, S5_STATE), F32)
            out, k, v, hr, hi = _even_mixer(_rmsnorm(y_p, norm_mix[li]), p, lam_init, z0, z0)
            y_p = y_p + out
            k_p.append(k); v_p.append(v); s5re_p.append(hr); s5im_p.append(hi)
            kp = cache_k[e, page_table].reshape(b_s, -1, ATT_HEADS, 2, ATT_D)
            vp = cache_v[e, page_table].reshape(b_s, -1, ATT_HEADS, 2 * ATT_D)
            out, k, v, hr, hi = _even_mixer(_rmsnorm(y_s, norm_mix[li]), p, lam_init,
                                            state_s5_re[e], state_s5_im[e], kp, vp)
            y_s = y_s + out
            k_s.append(k); v_s.append(v); s5re_s.append(hr); s5im_s.append(hi)
        else:
            o = li // 2
            p = {'w_in': ssd_w_in[o], 'conv_w': ssd_conv_w[o], 'conv_b': ssd_conv_b[o], 'dt_bias': ssd_dt_bias[o],
                 'a_log': ssd_a_log[o], 'd': ssd_d[o], 'norm': ssd_norm[o], 'w_out': ssd_w_out[o]}
            h0 = jnp.zeros((b_p, SSD_HEADS, SSD_HEADDIM, SSD_STATE), F32)
            buf0 = jnp.zeros((b_p, SSD_CONV - 1, SSD_CONV_DIM), y_p.dtype)
            out, hs, cb = _ssd_mixer(_rmsnorm(y_p, norm_mix[li]), p, h0, buf0)
            y_p = y_p + out
            ssd_p.append(hs); conv_p.append(cb)
            out, hs, cb = _ssd_mixer(_rmsnorm(y_s, norm_mix[li]), p, state_ssd[o], state_conv[o])
            y_s = y_s + out
            ssd_s.append(hs); conv_s.append(cb)
        y_p = y_p + _hier_moe(_rmsnorm(y_p, norm_ffn[li]), moe_w_group[li], moe_b_group[li], moe_w_expert[li],
                              moe_b_expert[li], moe_w_gate[li], moe_w_up[li], moe_w_down[li])
        y_s = y_s + _hier_moe(_rmsnorm(y_s, norm_ffn[li]), moe_w_group[li], moe_b_group[li], moe_w_expert[li],
                              moe_b_expert[li], moe_w_gate[li], moe_w_up[li], moe_w_down[li])
    y_prompt = _rmsnorm(y_p, norm_final)
    y_sample = _rmsnorm(y_s, norm_final)
    return (y_prompt, y_sample,
            jnp.stack(k_p), jnp.stack(v_p), jnp.stack(k_s), jnp.stack(v_s),
            jnp.stack(s5re_p), jnp.stack(s5im_p), jnp.stack(s5re_s), jnp.stack(s5im_s),
            jnp.stack(ssd_p), jnp.stack(ssd_s), jnp.stack(conv_p), jnp.stack(conv_s))
```

```python
import functools
import math

import jax
import jax.numpy as jnp
from jax import lax
from jax.experimental import pallas as pl
from jax.experimental.pallas import tpu as pltpu

F32 = jnp.float32
BF16 = jnp.bfloat16
HIGHEST = lax.Precision.HIGHEST

D_MODEL = 1024
NORM_EPS = 1e-6
PAGE_SIZE = 128
ATT_HEADS = 4
ATT_D = 64
ATT_WIDTH = ATT_HEADS * 2 * ATT_D
HEAD_W = 2 * ATT_D
S5_WIDTH = 512
S5_GROUP = 16
S5_GROUPS = 32
S5_STATE = 64
S5_LANES = S5_GROUPS * S5_STATE
SSD_INNER = 2048
SSD_HEADDIM = 64
SSD_HEADS = 32
SSD_GROUPS = 4
SSD_HPG = 8
SSD_STATE = 128
SSD_CONV = 4
SSD_CONV_DIM = SSD_INNER + 2 * SSD_GROUPS * SSD_STATE
SSD_CHUNK = 128
SSD_GW = SSD_HPG * SSD_HEADDIM
MOE_GROUPS = 4
MOE_EPG = 8
MOE_EXPERTS = 32
MOE_FF = 256
LANES = 128
NEG = -1e30

_NT = (((1,), (1,)), ((), ()))
_TN = (((0,), (0,)), ((), ()))


def _rms(x, g):
    return x * lax.rsqrt(jnp.mean(x * x, axis=-1, keepdims=True) + NORM_EPS) * g


def _split(x):
    hi = x.astype(BF16)
    return hi, (x - hi.astype(F32)).astype(BF16)


def _mm(a, b, precise=False, dims=None):
    dot = jnp.dot if dims is None else functools.partial(lax.dot_general, dimension_numbers=dims)
    if not precise:
        return dot(a.astype(BF16), b.astype(BF16), preferred_element_type=F32)
    m = a.shape[0]
    a_hi, a_lo = _split(a)
    b_hi, b_lo = _split(b)
    r = dot(jnp.concatenate([a_hi, a_lo], axis=0), b_hi, preferred_element_type=F32)
    return r[:m] + r[m:] + dot(a_hi, b_lo, preferred_element_type=F32)


def _norm_matmul_kernel(x_ref, g_ref, *refs, out_dtypes, precise):
    n = len(out_dtypes)
    w_refs = refs[:n]
    o_refs = refs[n:]
    h = _rms(x_ref[...], g_ref[...])
    if not precise:
        h = h.astype(BF16)
    k = 0
    for w_ref, dts in zip(w_refs, out_dtypes):
        r = _mm(h, w_ref[...], precise)
        for dt in dts:
            o_refs[k][...] = r.astype(dt)
            k += 1


def _norm_matmul(x, g, ws, out_dtypes, tm, name, precise=False):
    m, kdim = x.shape
    out_shape, out_specs = [], []
    for w, dts in zip(ws, out_dtypes):
        for dt in dts:
            out_shape.append(jax.ShapeDtypeStruct((m, w.shape[1]), dt))
            out_specs.append(pl.BlockSpec((tm, w.shape[1]), lambda i: (i, 0)))
    return pl.pallas_call(
        functools.partial(_norm_matmul_kernel, out_dtypes=out_dtypes, precise=precise),
        grid=(m // tm,),
        in_specs=[pl.BlockSpec((tm, kdim), lambda i: (i, 0)), pl.BlockSpec((1, kdim), lambda i: (0, 0))]
        + [pl.BlockSpec(w.shape, lambda i: (0, 0)) for w in ws],
        out_specs=out_specs,
        out_shape=out_shape,
        compiler_params=pltpu.CompilerParams(dimension_semantics=("parallel",)),
        name=name,
    )(x, g.reshape(1, kdim), *ws)


def _matmul_res_kernel(res_ref, *refs, precise):
    n = (len(refs) - 1) // 2
    acc = res_ref[...]
    for a_ref, w_ref in zip(refs[:n], refs[n:2 * n]):
        acc = acc + _mm(a_ref[...], w_ref[...], precise)
    refs[-1][...] = acc


def _matmul_res(res, a_list, w_list, tm, name, precise=False):
    m, n = res.shape
    return pl.pallas_call(
        functools.partial(_matmul_res_kernel, precise=precise),
        grid=(m // tm,),
        in_specs=[pl.BlockSpec((tm, n), lambda i: (i, 0))]
        + [pl.BlockSpec((tm, a.shape[1]), lambda i: (i, 0)) for a in a_list]
        + [pl.BlockSpec(w.shape, lambda i: (0, 0)) for w in w_list],
        out_specs=pl.BlockSpec((tm, n), lambda i: (i, 0)),
        out_shape=jax.ShapeDtypeStruct((m, n), F32),
        compiler_params=pltpu.CompilerParams(dimension_semantics=("parallel",)),
        name=name,
    )(res, *a_list, *w_list)


def _gated_norm_matmul_kernel(res_ref, y_ref, z_ref, g_ref, w_ref, o_ref, *, precise):
    y = y_ref[...] * jax.nn.silu(z_ref[...])
    o_ref[...] = res_ref[...] + _mm(_rms(y, g_ref[...]), w_ref[...], precise)


def _gated_norm_matmul(res, y, z, g, w, tm, name, precise=False):
    m, n = res.shape
    kdim = y.shape[1]
    return pl.pallas_call(
        functools.partial(_gated_norm_matmul_kernel, precise=precise),
        grid=(m // tm,),
        in_specs=[pl.BlockSpec((tm, n), lambda i: (i, 0)), pl.BlockSpec((tm, kdim), lambda i: (i, 0)),
                  pl.BlockSpec((tm, kdim), lambda i: (i, 0)), pl.BlockSpec((1, kdim), lambda i: (0, 0)),
                  pl.BlockSpec(w.shape, lambda i: (0, 0))],
        out_specs=pl.BlockSpec((tm, n), lambda i: (i, 0)),
        out_shape=jax.ShapeDtypeStruct((m, n), F32),
        compiler_params=pltpu.CompilerParams(dimension_semantics=("parallel",)),
        name=name,
    )(res, y, z, g.reshape(1, kdim), w)


def _rmsnorm_kernel(x_ref, g_ref, o_ref):
    o_ref[...] = _rms(x_ref[...], g_ref[...])


def _rmsnorm(x, g, tm, name):
    m, n = x.shape
    return pl.pallas_call(
        _rmsnorm_kernel,
        grid=(m // tm,),
        in_specs=[pl.BlockSpec((tm, n), lambda i: (i, 0)), pl.BlockSpec((1, n), lambda i: (0, 0))],
        out_specs=pl.BlockSpec((tm, n), lambda i: (i, 0)),
        out_shape=jax.ShapeDtypeStruct((m, n), F32),
        compiler_params=pltpu.CompilerParams(dimension_semantics=("parallel",)),
        name=name,
    )(x, g.reshape(1, n))


def _split_q(q):
    lane = lax.broadcasted_iota(jnp.int32, q.shape, 1) % HEAD_W
    scale = ATT_D ** -0.5
    qs = q * jnp.asarray(scale, q.dtype)
    zero = jnp.zeros_like(qs)
    return jnp.where(lane < ATT_D, qs, zero), jnp.where(lane >= ATT_D, qs, zero)


def _subln(o, g, out_scale):
    return _rms(o, g) * out_scale


def _attn_prompt_kernel(lam_ref, q_ref, k_ref, v_ref, g_ref, o_ref, *, tq, out_scale):
    qi = pl.program_id(2)
    q1, q2 = _split_q(q_ref[0])
    row = lax.broadcasted_iota(jnp.int32, (tq, tq), 0)
    col = lax.broadcasted_iota(jnp.int32, (tq, tq), 1)
    causal = row >= col

    def update(s, vt, m, l, acc):
        mn = jnp.maximum(m, jnp.max(s, axis=-1, keepdims=True))
        p = jnp.exp(s - mn)
        alpha = jnp.exp(m - mn)
        l = alpha * l + jnp.sum(p, axis=-1, keepdims=True)
        acc = alpha * acc + jnp.dot(p.astype(BF16), vt, preferred_element_type=F32)
        return mn, l, acc

    def block(j, carry, masked):
        m1, l1, a1, m2, l2, a2 = carry
        start = pl.multiple_of(j * tq, tq)
        kt = k_ref[0, pl.ds(start, tq), :]
        vt = v_ref[0, pl.ds(start, tq), :]
        s1 = lax.dot_general(q1, kt, _NT, preferred_element_type=F32)
        s2 = lax.dot_general(q2, kt, _NT, preferred_element_type=F32)
        if masked:
            s1 = jnp.where(causal, s1, NEG)
            s2 = jnp.where(causal, s2, NEG)
        m1, l1, a1 = update(s1, vt, m1, l1, a1)
        m2, l2, a2 = update(s2, vt, m2, l2, a2)
        return m1, l1, a1, m2, l2, a2

    m0 = jnp.full((tq, 1), NEG, F32)
    l0 = jnp.zeros((tq, 1), F32)
    a0 = jnp.zeros((tq, HEAD_W), F32)
    carry = lax.fori_loop(0, qi, lambda j, c: block(j, c, False), (m0, l0, a0, m0, l0, a0))
    m1, l1, a1, m2, l2, a2 = block(qi, carry, True)
    o = a1 / l1 - lam_ref[0, 0] * (a2 / l2)
    o_ref[0] = _subln(o, g_ref[...], out_scale).astype(o_ref.dtype)


def _attn_prompt(q, k, v, lam, subln, out_scale, tq=256):
    b, s, _ = q.shape
    kern = functools.partial(_attn_prompt_kernel, tq=tq, out_scale=out_scale)
    return pl.pallas_call(
        kern,
        grid=(b, ATT_HEADS, s // tq),
        in_specs=[pl.BlockSpec(memory_space=pltpu.SMEM),
                  pl.BlockSpec((1, tq, HEAD_W), lambda bi, h, i: (bi, i, h)),
                  pl.BlockSpec((1, s, HEAD_W), lambda bi, h, i: (bi, 0, h)),
                  pl.BlockSpec((1, s, HEAD_W), lambda bi, h, i: (bi, 0, h)),
                  pl.BlockSpec((1, HEAD_W), lambda bi, h, i: (0, 0))],
        out_specs=pl.BlockSpec((1, tq, HEAD_W), lambda bi, h, i: (bi, i, h)),
        out_shape=jax.ShapeDtypeStruct((b, s, ATT_WIDTH), BF16),
        compiler_params=pltpu.CompilerParams(dimension_semantics=("parallel", "parallel", "parallel")),
        name="attn_prompt",
    )(lam.reshape(1, 1), q, k, v, subln.reshape(1, HEAD_W))


_DEC_PAGES = 16
_DEC_PROW = PAGE_SIZE * ATT_HEADS
_DEC_COLS = _DEC_PAGES * _DEC_PROW
_DEC_ROWS = 2 * ATT_HEADS


def _attn_decode_kernel(pt_ref, lam_ref, q_ref, kn_ref, vn_ref, g_ref, *refs, steps, out_scale):
    del pt_ref
    k_refs = refs[:_DEC_PAGES]
    v_refs = refs[_DEC_PAGES:2 * _DEC_PAGES]
    o_ref = refs[2 * _DEC_PAGES]
    s_sc, m_sc, wn_sc, acc_sc = refs[2 * _DEC_PAGES + 1:]
    j = pl.program_id(1)
    lam = lam_ref[0, 0]

    q4 = q_ref[0] * (ATT_D ** -0.5)
    q8 = jnp.concatenate([q4, q4], axis=0)
    r8 = lax.broadcasted_iota(jnp.int32, (_DEC_ROWS, HEAD_W), 0)
    l8 = lax.broadcasted_iota(jnp.int32, (_DEC_ROWS, HEAD_W), 1)
    q8 = jnp.where(l8 // ATT_D == r8 // ATT_HEADS, q8, 0.0)

    @pl.when(j == 0)
    def _():
        m_sc[...] = jnp.full(m_sc.shape, NEG, F32)

    @pl.when(j < steps)
    def _():
        s = jnp.concatenate([_mm(q8, k_ref[0], True, _NT) for k_ref in k_refs], axis=1)
        rs = lax.broadcasted_iota(jnp.int32, s.shape, 0)
        cs = lax.broadcasted_iota(jnp.int32, s.shape, 1)
        s = jnp.where(cs % ATT_HEADS == rs % ATT_HEADS, s, NEG)
        s_sc[j] = s
        m_sc[...] = jnp.maximum(m_sc[...], jnp.max(s, axis=-1, keepdims=True))

    @pl.when(j == steps - 1)
    def _():
        kn = jnp.concatenate([kn_ref[0], kn_ref[0]], axis=0)
        sn = jnp.sum(q8 * kn, axis=-1, keepdims=True)
        m = jnp.maximum(m_sc[...], sn)
        pn = jnp.exp(sn - m)
        l = pn
        for t in range(steps):
            p = jnp.exp(s_sc[t] - m)
            s_sc[t] = p
            l = l + jnp.sum(p, axis=-1, keepdims=True)
        zeros = jnp.zeros((ATT_HEADS, _DEC_COLS), F32)
        for t in range(steps):
            p = s_sc[t] / l
            s_sc[t] = jnp.concatenate([p[:ATT_HEADS] - lam * p[ATT_HEADS:], zeros], axis=0)
        pn = pn / l
        wn_sc[...] = jnp.concatenate([pn[:ATT_HEADS] - lam * pn[ATT_HEADS:], jnp.zeros((ATT_HEADS, 1), F32)], axis=0)

    @pl.when(j == steps)
    def _():
        acc_sc[...] = jnp.zeros(acc_sc.shape, F32)

    @pl.when(j >= steps)
    def _():
        w = s_sc[j - steps]
        acc = acc_sc[...]
        for i, v_ref in enumerate(v_refs):
            acc = acc + _mm(w[:, i * _DEC_PROW:(i + 1) * _DEC_PROW], v_ref[0], True)
        acc_sc[...] = acc

    @pl.when(j == 2 * steps - 1)
    def _():
        o4 = acc_sc[:ATT_HEADS] + wn_sc[:ATT_HEADS] * vn_ref[0]
        o_ref[0] = _subln(o4, g_ref[...], out_scale)


def _attn_decode(q, k_new, v_new, cache_k, cache_v, page_table, lam, subln, out_scale):
    b = q.shape[0]
    n_pages = page_table.shape[1]
    steps = n_pages // _DEC_PAGES
    ck = cache_k.reshape(-1, _DEC_PROW, HEAD_W)
    cv = cache_v.reshape(-1, _DEC_PROW, HEAD_W)

    def k_map(i):
        return lambda bi, j, pt: (pt[bi, jnp.minimum(j, steps - 1) * _DEC_PAGES + i], 0, 0)

    def v_map(i):
        return lambda bi, j, pt: (pt[bi, jnp.maximum(j - steps, 0) * _DEC_PAGES + i], 0, 0)

    row = lambda bi, j, pt: (bi, 0, 0)
    head_rows = pl.BlockSpec((1, ATT_HEADS, HEAD_W), row)
    grid_spec = pltpu.PrefetchScalarGridSpec(
        num_scalar_prefetch=1,
        grid=(b, 2 * steps),
        in_specs=[pl.BlockSpec(memory_space=pltpu.SMEM), head_rows, head_rows, head_rows,
                  pl.BlockSpec((1, HEAD_W), lambda bi, j, pt: (0, 0))]
        + [pl.BlockSpec((1, _DEC_PROW, HEAD_W), k_map(i)) for i in range(_DEC_PAGES)]
        + [pl.BlockSpec((1, _DEC_PROW, HEAD_W), v_map(i)) for i in range(_DEC_PAGES)],
        out_specs=head_rows,
        scratch_shapes=[pltpu.VMEM((steps, _DEC_ROWS, _DEC_COLS), F32), pltpu.VMEM((_DEC_ROWS, 1), F32),
                        pltpu.VMEM((_DEC_ROWS, 1), F32), pltpu.VMEM((_DEC_ROWS, HEAD_W), F32)],
    )
    out = pl.pallas_call(
        functools.partial(_attn_decode_kernel, steps=steps, out_scale=out_scale),
        grid_spec=grid_spec,
        out_shape=jax.ShapeDtypeStruct((b, ATT_HEADS, HEAD_W), F32),
        compiler_params=pltpu.CompilerParams(dimension_semantics=("parallel", "arbitrary")),
        name="attn_decode",
    )(page_table, lam.reshape(1, 1), q.reshape(b, ATT_HEADS, HEAD_W), k_new.reshape(b, ATT_HEADS, HEAD_W),
      v_new.reshape(b, ATT_HEADS, HEAD_W), subln.reshape(1, HEAD_W), *([ck] * _DEC_PAGES), *([cv] * _DEC_PAGES))
    return out.reshape(b, ATT_WIDTH)


_S5_KT = 2
_S5_KW = S5_WIDTH // _S5_KT
_S5_KL = S5_LANES // _S5_KT
_S5_NT = S5_LANES // LANES
_S5_SCAN_TILES = 8


def _s5_input(u, bmat_ref, precise):
    re, im = [], []
    for kt in range(_S5_KT):
        r = _mm(u[:, kt * _S5_KW:(kt + 1) * _S5_KW], bmat_ref[kt], precise)
        re.append(r[:, :_S5_KL])
        im.append(r[:, _S5_KL:])
    return jnp.concatenate(re, axis=1), jnp.concatenate(im, axis=1)


def _s5_output(h_re, h_im, u, cmat_ref, d_ref, wglu_ref, bglu_ref, precise):
    if not precise:
        h_re, h_im = h_re.astype(BF16), h_im.astype(BF16)
    ys = []
    for kt in range(_S5_KT):
        sl = slice(kt * _S5_KL, (kt + 1) * _S5_KL)
        y = _mm(h_re[:, sl], cmat_ref[kt, :_S5_KL, :], precise)
        y = y + _mm(h_im[:, sl], cmat_ref[kt, _S5_KL:, :], precise)
        ys.append(y)
    y = jnp.concatenate(ys, axis=1) + d_ref[...] * u
    g = jax.nn.gelu(y)
    gate = _mm(g, wglu_ref[...], precise) + bglu_ref[...]
    return g * jax.nn.sigmoid(gate)


def _s5_scan_kernel(u_ref, are_ref, aim_ref, bmat_ref, cmat_ref, d_ref, wglu_ref, bglu_ref,
                    o_ref, hre_ref, him_ref, bu_re, bu_im, *, nb, lc, bp):
    c = pl.program_id(0)

    @pl.when(c == 0)
    def _():
        bu_re[...] = jnp.zeros(bu_re.shape, F32)
        bu_im[...] = jnp.zeros(bu_im.shape, F32)
        hre_ref[...] = jnp.zeros(hre_ref.shape, F32)
        him_ref[...] = jnp.zeros(him_ref.shape, F32)

    for b in range(nb):
        re, im = _s5_input(u_ref[b], bmat_ref, False)
        for jt in range(_S5_NT):
            bu_re[jt, pl.ds(b, lc, stride=bp), :] = re[:, jt * LANES:(jt + 1) * LANES]
            bu_im[jt, pl.ds(b, lc, stride=bp), :] = im[:, jt * LANES:(jt + 1) * LANES]

    for j0 in range(0, _S5_NT, _S5_SCAN_TILES):
        tiles = pl.ds(j0, _S5_SCAN_TILES)
        a_re = jnp.broadcast_to(are_ref[tiles], (_S5_SCAN_TILES, bp, LANES))
        a_im = jnp.broadcast_to(aim_ref[tiles], (_S5_SCAN_TILES, bp, LANES))

        def step(t, carry, tiles=tiles, a_re=a_re, a_im=a_im):
            hr, hi = carry
            rows = pl.ds(pl.multiple_of(t * bp, bp), bp)
            nr = hr * a_re - hi * a_im + bu_re[tiles, rows, :]
            ni = hr * a_im + hi * a_re + bu_im[tiles, rows, :]
            bu_re[tiles, rows, :] = nr
            bu_im[tiles, rows, :] = ni
            return nr, ni

        hr, hi = lax.fori_loop(0, lc, step, (hre_ref[tiles], him_ref[tiles]), unroll=2)
        hre_ref[tiles] = hr
        him_ref[tiles] = hi

    for b in range(nb):
        h_re = jnp.concatenate([bu_re[jt, pl.ds(b, lc, stride=bp), :] for jt in range(_S5_NT)], axis=1)
        h_im = jnp.concatenate([bu_im[jt, pl.ds(b, lc, stride=bp), :] for jt in range(_S5_NT)], axis=1)
        o_ref[b] = _s5_output(h_re, h_im, u_ref[b], cmat_ref, d_ref, wglu_ref, bglu_ref, False).astype(o_ref.dtype)


def _s5_prompt(u, prm, lc=128):
    nb, l, _ = u.shape
    bp = 8
    full = lambda shape: pl.BlockSpec(shape, lambda c: (0,) * len(shape))
    o, h_re, h_im = pl.pallas_call(
        functools.partial(_s5_scan_kernel, nb=nb, lc=lc, bp=bp),
        grid=(l // lc,),
        in_specs=[pl.BlockSpec((nb, lc, S5_WIDTH), lambda c: (0, c, 0)),
                  full((_S5_NT, 1, LANES)), full((_S5_NT, 1, LANES)),
                  full((_S5_KT, _S5_KW, 2 * _S5_KL)), full((_S5_KT, 2 * _S5_KL, _S5_KW)),
                  full((1, S5_WIDTH)), full((S5_WIDTH, S5_WIDTH)), full((1, S5_WIDTH))],
        out_specs=[pl.BlockSpec((nb, lc, S5_WIDTH), lambda c: (0, c, 0)),
                   full((_S5_NT, bp, LANES)), full((_S5_NT, bp, LANES))],
        out_shape=[jax.ShapeDtypeStruct((nb, l, S5_WIDTH), BF16),
                   jax.ShapeDtypeStruct((_S5_NT, bp, LANES), F32), jax.ShapeDtypeStruct((_S5_NT, bp, LANES), F32)],
        scratch_shapes=[pltpu.VMEM((_S5_NT, lc * bp, LANES), F32) for _ in range(2)],
        compiler_params=pltpu.CompilerParams(dimension_semantics=("arbitrary",)),
        name="s5_prompt",
    )(u, prm["a_re"].reshape(_S5_NT, 1, LANES), prm["a_im"].reshape(_S5_NT, 1, LANES), prm["bmat"].astype(BF16),
      prm["cmat"].astype(BF16), prm["d"], prm["w_glu"].astype(BF16), prm["b_glu"])
    rows = lambda h: jnp.transpose(h, (1, 0, 2)).reshape(bp, S5_LANES)[:nb]
    return o, rows(h_re), rows(h_im)


def _s5_step_kernel(u_ref, h0re_ref, h0im_ref, are_ref, aim_ref, bmat_ref, cmat_ref, d_ref, wglu_ref, bglu_ref,
                    o_ref, hre_ref, him_ref):
    u = u_ref[...]
    bu_re, bu_im = _s5_input(u, bmat_ref, True)
    h_re, h_im = h0re_ref[...], h0im_ref[...]
    a_re, a_im = are_ref[...], aim_ref[...]
    n_re = h_re * a_re - h_im * a_im + bu_re
    n_im = h_re * a_im + h_im * a_re + bu_im
    hre_ref[...] = n_re
    him_ref[...] = n_im
    o_ref[...] = _s5_output(n_re, n_im, u, cmat_ref, d_ref, wglu_ref, bglu_ref, True)


def _s5_step(u, h0_re, h0_im, prm):
    nb = u.shape[0]
    return pl.pallas_call(
        _s5_step_kernel,
        out_shape=[jax.ShapeDtypeStruct((nb, S5_WIDTH), F32),
                   jax.ShapeDtypeStruct((nb, S5_LANES), F32), jax.ShapeDtypeStruct((nb, S5_LANES), F32)],
        name="s5_step",
    )(u, h0_re, h0_im, prm["a_re"], prm["a_im"], prm["bmat"], prm["cmat"], prm["d"], prm["w_glu"], prm["b_glu"])


def _s5_params(lam_re, lam_im, log_dt, b_re, b_im, c_re, c_im, d, w_glu, b_glu):
    lam = lax.complex(lam_re, lam_im)
    dt = jnp.exp(log_dt)[:, None]
    a_bar = jnp.exp(lam * dt)
    b_bar = ((a_bar - 1.0) / lam)[..., None] * lax.complex(b_re, b_im)
    gk = S5_GROUPS // _S5_KT
    eye = jnp.eye(gk, dtype=F32)

    def in_tile(x):
        x = x.reshape(_S5_KT, gk, S5_STATE, S5_GROUP)
        return jnp.einsum("kgpc,gh->kgchp", x, eye).reshape(_S5_KT, _S5_KW, _S5_KL)

    def out_tile(x):
        x = x.reshape(_S5_KT, gk, S5_GROUP, S5_STATE)
        return jnp.einsum("kgcp,gh->kgphc", x, eye).reshape(_S5_KT, _S5_KL, _S5_KW)

    bmat = jnp.concatenate([in_tile(jnp.real(b_bar)), in_tile(jnp.imag(b_bar))], axis=2)
    cmat = jnp.concatenate([out_tile(c_re), out_tile(-c_im)], axis=1)
    return {
        "a_re": jnp.real(a_bar).reshape(1, S5_LANES), "a_im": jnp.imag(a_bar).reshape(1, S5_LANES),
        "bmat": bmat, "cmat": cmat, "d": d.reshape(1, S5_WIDTH), "w_glu": w_glu, "b_glu": b_glu.reshape(1, S5_WIDTH),
    }


_CONV_PAD = 8


def _conv_silu(x, xf, w_ref, b_ref, rows):
    xf[_CONV_PAD:_CONV_PAD + rows, :] = x
    w = w_ref[...]
    out = b_ref[...]
    for i in range(SSD_CONV):
        lo = _CONV_PAD - (SSD_CONV - 1) + i
        out = out + w[i:i + 1] * xf[lo:lo + rows, :]
    xf[_CONV_PAD - (SSD_CONV - 1):_CONV_PAD, :] = xf[_CONV_PAD + rows - (SSD_CONV - 1):_CONV_PAD + rows, :]
    return jax.nn.silu(out)


def _ssd_chunk_kernel(xs_ref, bm_ref, cm_ref, dt_ref, cwx_ref, cwb_ref, cwc_ref, cbx_ref, cbb_ref, cbc_ref,
                      dtb_ref, a_ref, dexp_ref, y_ref, st_ref, xf_x, xf_b, xf_c):
    c = pl.program_id(2)
    q = SSD_CHUNK

    @pl.when(c == 0)
    def _():
        xf_x[0:_CONV_PAD, :] = jnp.zeros((_CONV_PAD, xf_x.shape[1]), F32)
        xf_b[0:_CONV_PAD, :] = jnp.zeros((_CONV_PAD, xf_b.shape[1]), F32)
        xf_c[0:_CONV_PAD, :] = jnp.zeros((_CONV_PAD, xf_c.shape[1]), F32)
        st_ref[...] = jnp.zeros(st_ref.shape, F32)

    xs = _conv_silu(xs_ref[0], xf_x, cwx_ref, cbx_ref, q)
    bm = _conv_silu(bm_ref[0], xf_b, cwb_ref, cbb_ref, q)
    cm = _conv_silu(cm_ref[0], xf_c, cwc_ref, cbc_ref, q)
    bm16 = bm.astype(BF16)
    cm16 = cm.astype(BF16)

    dt = jax.nn.softplus(dt_ref[0] + dtb_ref[0])
    da = dt * a_ref[0]
    row = lax.broadcasted_iota(jnp.int32, (q, q), 0)
    col = lax.broadcasted_iota(jnp.int32, (q, q), 1)
    causal = row >= col
    acs = jnp.dot(causal.astype(F32), da, preferred_element_type=F32, precision=HIGHEST)
    acs_t = acs.T
    acs_last = acs[q - 1:q, :]
    e_acs = jnp.exp(acs)
    dec_s = jnp.exp(acs_last - acs)
    chunk_dec = jnp.exp(acs_last)
    cb = _mm(cm16, bm16, dims=_NT)

    ys = []
    for j in range(SSD_HPG):
        seg = acs[:, j:j + 1] - acs_t[j:j + 1, :]
        lmat = jnp.exp(jnp.where(causal, seg, -jnp.inf))
        xdt = xs[:, j * SSD_HEADDIM:(j + 1) * SSD_HEADDIM] * dt[:, j:j + 1]
        y_diag = _mm(cb * lmat, xdt)
        h = st_ref[0, j]
        y_off = _mm(cm16, h, dims=_NT) * e_acs[:, j:j + 1]
        st_ref[0, j] = chunk_dec[:, j:j + 1] * h + _mm(xdt * dec_s[:, j:j + 1], bm16, dims=_TN)
        ys.append(y_diag + y_off)
    y_ref[0] = jnp.concatenate(ys, axis=1) + dexp_ref[...] * xs


def _ssd_prompt(xbc, dtg, prm):
    nb, l, _ = xbc.shape
    nc = l // SSD_CHUNK
    q = SSD_CHUNK
    boff = SSD_INNER // LANES
    coff = boff + SSD_GROUPS
    colx = lambda b, g, c: (0, g)
    colb = lambda b, g, c: (0, boff + g)
    colc = lambda b, g, c: (0, coff + g)
    grp = lambda b, g, c: (g, 0, 0)
    return pl.pallas_call(
        _ssd_chunk_kernel,
        grid=(nb, SSD_GROUPS, nc),
        in_specs=[pl.BlockSpec((1, q, SSD_GW), lambda b, g, c: (b, c, g)),
                  pl.BlockSpec((1, q, LANES), lambda b, g, c: (b, c, boff + g)),
                  pl.BlockSpec((1, q, LANES), lambda b, g, c: (b, c, coff + g)),
                  pl.BlockSpec((1, q, LANES), lambda b, g, c: (b, c, g)),
                  pl.BlockSpec((SSD_CONV, SSD_GW), colx), pl.BlockSpec((SSD_CONV, LANES), colb),
                  pl.BlockSpec((SSD_CONV, LANES), colc),
                  pl.BlockSpec((1, SSD_GW), colx), pl.BlockSpec((1, LANES), colb), pl.BlockSpec((1, LANES), colc),
                  pl.BlockSpec((1, 1, LANES), grp), pl.BlockSpec((1, 1, LANES), grp),
                  pl.BlockSpec((1, SSD_GW), colx)],
        out_specs=[pl.BlockSpec((1, q, SSD_GW), lambda b, g, c: (b, c, g)),
                   pl.BlockSpec((1, SSD_HPG, SSD_HEADDIM, SSD_STATE), lambda b, g, c: (b, g, 0, 0))],
        out_shape=[jax.ShapeDtypeStruct((nb, l, SSD_INNER), F32),
                   jax.ShapeDtypeStruct((nb, SSD_HEADS, SSD_HEADDIM, SSD_STATE), F32)],
        scratch_shapes=[pltpu.VMEM((_CONV_PAD + q, SSD_GW), F32), pltpu.VMEM((_CONV_PAD + q, LANES), F32),
                        pltpu.VMEM((_CONV_PAD + q, LANES), F32)],
        compiler_params=pltpu.CompilerParams(dimension_semantics=("parallel", "parallel", "arbitrary")),
        name="ssd_prompt",
    )(xbc, xbc, xbc, dtg, prm["conv_w"], prm["conv_w"], prm["conv_w"], prm["conv_b"], prm["conv_b"], prm["conv_b"],
      prm["dt_bias_g"], prm["a_g"], prm["d_exp"])


def _ssd_step_kernel(x_ref, buf_ref, cw_ref, cb_ref, dt64_ref, dtb64_ref, dt128_ref, dtb128_ref, a128_ref,
                     dexp_ref, st_ref, y_ref, so_ref):
    w = cw_ref[...]
    buf = buf_ref[0]
    conv = cb_ref[...]
    for i in range(SSD_CONV - 1):
        conv = conv + w[i:i + 1] * buf[i:i + 1]
    conv = conv + w[SSD_CONV - 1:SSD_CONV] * x_ref[0]
    xc = jax.nn.silu(conv)
    xs = xc[:, :SSD_INNER]
    xdt = xs * jax.nn.softplus(dt64_ref[0] + dtb64_ref[...])
    dt = jax.nn.softplus(dt128_ref[0] + dtb128_ref[...])
    dec = jnp.exp(dt * a128_ref[...])
    row = lax.broadcasted_iota(jnp.int32, (LANES, LANES), 0)
    col = lax.broadcasted_iota(jnp.int32, (LANES, LANES), 1)
    diag = row == col
    ys = []
    for i in range(SSD_HEADS // 2):
        g = (2 * i) // SSD_HPG
        bm = xc[:, SSD_INNER + g * SSD_STATE:SSD_INNER + (g + 1) * SSD_STATE]
        cm = xc[:, SSD_INNER + (SSD_GROUPS + g) * SSD_STATE:SSD_INNER + (SSD_GROUPS + g + 1) * SSD_STATE]
        xp = jnp.broadcast_to(xdt[:, i * LANES:(i + 1) * LANES], (LANES, LANES))
        outer = _mm(jnp.where(diag, xp, 0.0), jnp.broadcast_to(bm, (LANES, LANES)), True)
        h = st_ref[0, 2 * i:2 * i + 2].reshape(LANES, SSD_STATE)
        dpair = jnp.concatenate([jnp.broadcast_to(dec[2 * i:2 * i + 1], (SSD_HEADDIM, LANES)),
                                 jnp.broadcast_to(dec[2 * i + 1:2 * i + 2], (SSD_HEADDIM, LANES))], axis=0)
        hn = dpair * h + outer
        so_ref[0, 2 * i:2 * i + 2] = hn.reshape(2, SSD_HEADDIM, SSD_STATE)
        ys.append(_mm(jnp.broadcast_to(cm, (8, SSD_STATE)), hn, True, _NT)[0:1])
    y_ref[0] = jnp.concatenate(ys, axis=1) + dexp_ref[...] * xs


def _dt_expand_kernel(dt_ref, e64_ref, e128_ref, o64_ref, o128_ref):
    dt = dt_ref[...]
    o64_ref[...] = _mm(dt, e64_ref[...], True)
    o128_ref[...] = _mm(dt, e128_ref[...], True)


def _dt_expand(dt, prm):
    nb = dt.shape[0]
    return pl.pallas_call(
        _dt_expand_kernel,
        out_shape=[jax.ShapeDtypeStruct((nb, SSD_INNER), F32), jax.ShapeDtypeStruct((nb, SSD_HEADS * LANES), F32)],
        name="ssd_dt_expand",
    )(dt, prm["expand_64"], prm["expand_128"])


def _ssd_step(xbc, conv_buf, dt64, dt128, state, prm):
    nb = xbc.shape[0]
    full = lambda shape: pl.BlockSpec(shape, lambda b: (0,) * len(shape))
    y, so = pl.pallas_call(
        _ssd_step_kernel,
        grid=(nb,),
        in_specs=[pl.BlockSpec((1, 1, SSD_CONV_DIM), lambda b: (b, 0, 0)),
                  pl.BlockSpec((1, SSD_CONV - 1, SSD_CONV_DIM), lambda b: (b, 0, 0)),
                  full((SSD_CONV, SSD_CONV_DIM)), full((1, SSD_CONV_DIM)),
                  pl.BlockSpec((1, 1, SSD_INNER), lambda b: (b, 0, 0)), full((1, SSD_INNER)),
                  pl.BlockSpec((1, SSD_HEADS, LANES), lambda b: (b, 0, 0)), full((SSD_HEADS, LANES)),
                  full((SSD_HEADS, LANES)), full((1, SSD_INNER)),
                  pl.BlockSpec((1, SSD_HEADS, SSD_HEADDIM, SSD_STATE), lambda b: (b, 0, 0, 0))],
        out_specs=[pl.BlockSpec((1, 1, SSD_INNER), lambda b: (b, 0, 0)),
                   pl.BlockSpec((1, SSD_HEADS, SSD_HEADDIM, SSD_STATE), lambda b: (b, 0, 0, 0))],
        out_shape=[jax.ShapeDtypeStruct((nb, 1, SSD_INNER), F32),
                   jax.ShapeDtypeStruct((nb, SSD_HEADS, SSD_HEADDIM, SSD_STATE), F32)],
        compiler_params=pltpu.CompilerParams(dimension_semantics=("parallel",)),
        name="ssd_step",
    )(xbc.reshape(nb, 1, SSD_CONV_DIM), conv_buf, prm["conv_w"], prm["conv_b"],
      dt64.reshape(nb, 1, SSD_INNER), prm["dt_bias_64"], dt128.reshape(nb, SSD_HEADS, LANES),
      prm["dt_bias_128"], prm["a_128"], prm["d_exp"], state)
    return y.reshape(nb, SSD_INNER), so


_GROUP_LANE0 = MOE_EXPERTS


def _moe_router_kernel(y_ref, g_ref, wr_ref, br_ref, xn_ref, comb_ref):
    xn = _rms(y_ref[...], g_ref[...])
    xn_ref[...] = xn.astype(xn_ref.dtype)
    lg = _mm(xn, wr_ref[...], True) + br_ref[...]
    lane = lax.broadcasted_iota(jnp.int32, lg.shape, 1)
    is_group = (lane >= _GROUP_LANE0) & (lane < _GROUP_LANE0 + MOE_GROUPS)
    gl = jnp.where(is_group, lg, NEG)
    gmax = jnp.max(gl, axis=-1, keepdims=True)
    g_p = 1.0 / jnp.sum(jnp.exp(gl - gmax), axis=-1, keepdims=True)
    gidx = jnp.min(jnp.where(gl == gmax, lane - _GROUP_LANE0, MOE_GROUPS), axis=-1, keepdims=True)
    el = jnp.where((lane < MOE_EXPERTS) & (lane // MOE_EPG == gidx), lg, NEG)
    m1 = jnp.max(el, axis=-1, keepdims=True)
    i1 = jnp.min(jnp.where(el == m1, lane, LANES), axis=-1, keepdims=True)
    el2 = jnp.where(lane == i1, NEG, el)
    m2 = jnp.max(el2, axis=-1, keepdims=True)
    i2 = jnp.min(jnp.where(el2 == m2, lane, LANES), axis=-1, keepdims=True)
    e2 = jnp.exp(m2 - m1)
    den = 1.0 + e2
    comb_ref[...] = jnp.where(lane == i1, g_p / den, 0.0) + jnp.where(lane == i2, g_p * (e2 / den), 0.0)


def _moe_router(y, g, wr, br, tm, name, xn_dtype):
    m, d = y.shape
    return pl.pallas_call(
        _moe_router_kernel,
        grid=(m // tm,),
        in_specs=[pl.BlockSpec((tm, d), lambda i: (i, 0)), pl.BlockSpec((1, d), lambda i: (0, 0)),
                  pl.BlockSpec((d, LANES), lambda i: (0, 0)), pl.BlockSpec((1, LANES), lambda i: (0, 0))],
        out_specs=[pl.BlockSpec((tm, d), lambda i: (i, 0)), pl.BlockSpec((tm, LANES), lambda i: (i, 0))],
        out_shape=[jax.ShapeDtypeStruct((m, d), xn_dtype), jax.ShapeDtypeStruct((m, LANES), F32)],
        compiler_params=pltpu.CompilerParams(dimension_semantics=("parallel",)),
        name=name,
    )(y, g.reshape(1, d), wr, br)


def _moe_dense_kernel(res_ref, xn_ref, comb_ref, wg_ref, wu_ref, wd_ref, o_ref, acc, *, precise):
    e = pl.program_id(1)

    @pl.when(e == 0)
    def _():
        acc[...] = jnp.zeros(acc.shape, F32)

    x = xn_ref[...]
    hdn = jax.nn.silu(_mm(x, wg_ref[0], precise)) * _mm(x, wu_ref[0], precise)
    comb = comb_ref[...]
    lane = lax.broadcasted_iota(jnp.int32, comb.shape, 1)
    gate = jnp.sum(jnp.where(lane == e, comb, 0.0), axis=-1, keepdims=True)
    acc[...] += gate * _mm(hdn, wd_ref[0], precise)

    @pl.when(e == pl.num_programs(1) - 1)
    def _():
        o_ref[...] = res_ref[...] + acc[...]


def _moe_dense(res, xn, comb, wg, wu, wd, tm, name, precise):
    m, d = res.shape
    return pl.pallas_call(
        functools.partial(_moe_dense_kernel, precise=precise),
        grid=(m // tm, MOE_EXPERTS),
        in_specs=[pl.BlockSpec((tm, d), lambda i, e: (i, 0)), pl.BlockSpec((tm, d), lambda i, e: (i, 0)),
                  pl.BlockSpec((tm, LANES), lambda i, e: (i, 0)),
                  pl.BlockSpec((1, d, MOE_FF), lambda i, e: (e, 0, 0)),
                  pl.BlockSpec((1, d, MOE_FF), lambda i, e: (e, 0, 0)),
                  pl.BlockSpec((1, MOE_FF, d), lambda i, e: (e, 0, 0))],
        out_specs=pl.BlockSpec((tm, d), lambda i, e: (i, 0)),
        out_shape=jax.ShapeDtypeStruct((m, d), F32),
        scratch_shapes=[pltpu.VMEM((tm, d), F32)],
        compiler_params=pltpu.CompilerParams(dimension_semantics=("parallel", "arbitrary")),
        name=name,
    )(res, xn, comb, wg, wu, wd)


def _moe(y, g, prm, tm_r, tm_e, tag, precise):
    sfx = "_f32" if precise else ""
    xn, comb = _moe_router(y, g, prm["wr"], prm["br"], tm_r, "moe_router_" + tag, F32 if precise else BF16)
    return _moe_dense(y, xn, comb, prm["wg" + sfx], prm["wu" + sfx], prm["wd" + sfx], tm_e, "moe_experts_" + tag,
                      precise)


def _moe_params(w_group, b_group, w_expert, b_expert, w_gate, w_up, w_down):
    d = w_group.shape[0]
    pad = LANES - MOE_EXPERTS - MOE_GROUPS
    wr = jnp.concatenate([w_expert, w_group, jnp.zeros((d, pad), F32)], axis=1)
    br = jnp.concatenate([b_expert, b_group, jnp.zeros((pad,), F32)]).reshape(1, LANES)
    return {"wr": wr, "br": br, "wg": w_gate.astype(BF16), "wu": w_up.astype(BF16), "wd": w_down.astype(BF16),
            "wg_f32": w_gate, "wu_f32": w_up, "wd_f32": w_down}


def _ssd_params(w_in, conv_w, conv_b, dt_bias, a_log, d, norm, w_out):
    a = -jnp.exp(a_log)
    w_dt = w_in[:, SSD_INNER + SSD_CONV_DIM:]

    def grouped(x):
        x = x.reshape(x.shape[:-1] + (SSD_GROUPS, SSD_HPG))
        x = jnp.pad(x, [(0, 0)] * (x.ndim - 1) + [(0, LANES - SSD_HPG)])
        return x.reshape(x.shape[:-2] + (SSD_GROUPS * LANES,))

    w_z = w_in[:, :SSD_INNER]
    w_xbc = w_in[:, SSD_INNER:SSD_INNER + SSD_CONV_DIM]
    eye = jnp.eye(SSD_HEADS, dtype=F32)
    return {
        "w_z": w_z.astype(BF16), "w_xbc": w_xbc.astype(BF16), "w_dt_g": grouped(w_dt).astype(BF16),
        "w_z_f32": w_z, "w_xbc_f32": w_xbc, "w_dt_f32": w_dt, "w_out_f32": w_out,
        "expand_64": jnp.repeat(eye, SSD_HEADDIM, axis=1), "expand_128": jnp.repeat(eye, LANES, axis=1),
        "conv_w": conv_w, "conv_b": conv_b.reshape(1, SSD_CONV_DIM),
        "dt_bias_g": grouped(dt_bias).reshape(SSD_GROUPS, 1, LANES),
        "a_g": grouped(a).reshape(SSD_GROUPS, 1, LANES),
        "dt_bias_64": jnp.repeat(dt_bias, SSD_HEADDIM).reshape(1, SSD_INNER),
        "dt_bias_128": jnp.broadcast_to(dt_bias[:, None], (SSD_HEADS, LANES)),
        "a_128": jnp.broadcast_to(a[:, None], (SSD_HEADS, LANES)),
        "d_exp": jnp.repeat(d, SSD_HEADDIM).reshape(1, SSD_INNER),
        "norm": norm, "w_out": w_out.astype(BF16),
    }


_TM = 512


def kernel(x_prompt, x_sample, cache_k, cache_v, page_table, state_s5_re, state_s5_im, state_ssd, state_conv, norm_mix, norm_ffn, norm_final, even_w_in, even_w_out, diff_lam_q1, diff_lam_k1, diff_lam_q2, diff_lam_k2, diff_subln, s5_lam_re, s5_lam_im, s5_log_dt, s5_b_re, s5_b_im, s5_c_re, s5_c_im, s5_d, s5_w_glu, s5_b_glu, ssd_w_in, ssd_conv_w, ssd_conv_b, ssd_dt_bias, ssd_a_log, ssd_d, ssd_norm, ssd_w_out, moe_w_group, moe_b_group, moe_w_expert, moe_b_expert, moe_w_gate, moe_w_up, moe_w_down):
    bp, sp, d = x_prompt.shape
    bs = x_sample.shape[0]
    tp = bp * sp
    depth = norm_mix.shape[0]
    y_p = x_prompt.reshape(tp, d)
    y_s = x_sample.reshape(bs, d)
    outs = {n: [] for n in ("k_p", "v_p", "k_s", "v_s", "re_p", "im_p", "re_s", "im_s", "ssd_p", "ssd_s", "cv_p", "cv_s")}

    for li in range(depth):
        if li % 2 == 0:
            e = li // 2
            lam_init = 0.8 - 0.6 * math.exp(-0.3 * li)
            lam = (jnp.exp(jnp.sum(diff_lam_q1[e] * diff_lam_k1[e])) - jnp.exp(jnp.sum(diff_lam_q2[e] * diff_lam_k2[e]))
                   + lam_init).astype(F32)
            ws32 = [even_w_in[e][:, i * ATT_WIDTH:(i + 1) * ATT_WIDTH] for i in range(4)]
            ws = [w.astype(BF16) for w in ws32]
            w_out32 = [even_w_out[e][:ATT_WIDTH], even_w_out[e][ATT_WIDTH:]]
            w_out_a, w_out_s = [w.astype(BF16) for w in w_out32]
            s5p = _s5_params(s5_lam_re[e], s5_lam_im[e], s5_log_dt[e], s5_b_re[e], s5_b_im[e], s5_c_re[e], s5_c_im[e],
                             s5_d[e], s5_w_glu[e], s5_b_glu[e])
            dts = [(BF16,), (F32, BF16), (F32, BF16), (F32,)]
            q, k, k16, v, v16, u = _norm_matmul(y_p, norm_mix[li], ws, dts, _TM, "even_in_p")
            o = _attn_prompt(q.reshape(bp, sp, ATT_WIDTH), k16.reshape(bp, sp, ATT_WIDTH),
                             v16.reshape(bp, sp, ATT_WIDTH), lam, diff_subln[e], 1.0 - lam_init)
            s5o, h_re, h_im = _s5_prompt(u.reshape(bp, sp, S5_WIDTH), s5p)
            y_p = _matmul_res(y_p, [o.reshape(tp, ATT_WIDTH), s5o.reshape(tp, S5_WIDTH)], [w_out_a, w_out_s], _TM,
                              "even_out_p")
            outs["k_p"].append(k.reshape(bp, sp, ATT_HEADS, HEAD_W))
            outs["v_p"].append(v.reshape(bp, sp, ATT_HEADS, HEAD_W))
            outs["re_p"].append(h_re.reshape(bp, S5_GROUPS, S5_STATE))
            outs["im_p"].append(h_im.reshape(bp, S5_GROUPS, S5_STATE))
            q, k, v, u = _norm_matmul(y_s, norm_mix[li], ws32, [(F32,), (F32,), (F32,), (F32,)], bs, "even_in_s", True)
            o = _attn_decode(q, k, v, cache_k, cache_v, page_table + e * cache_k.shape[1], lam, diff_subln[e],
                             1.0 - lam_init)
            s5o, h_re, h_im = _s5_step(u, state_s5_re[e].reshape(bs, S5_LANES), state_s5_im[e].reshape(bs, S5_LANES), s5p)
            y_s = _matmul_res(y_s, [o, s5o], w_out32, bs, "even_out_s", True)
            outs["k_s"].append(k.reshape(bs, 1, ATT_HEADS, HEAD_W))
            outs["v_s"].append(v.reshape(bs, 1, ATT_HEADS, HEAD_W))
            outs["re_s"].append(h_re.reshape(bs, S5_GROUPS, S5_STATE))
            outs["im_s"].append(h_im.reshape(bs, S5_GROUPS, S5_STATE))
        else:
            o_ = li // 2
            sp_ = _ssd_params(ssd_w_in[o_], ssd_conv_w[o_], ssd_conv_b[o_], ssd_dt_bias[o_], ssd_a_log[o_], ssd_d[o_],
                              ssd_norm[o_], ssd_w_out[o_])
            z, xbc, dtg = _norm_matmul(y_p, norm_mix[li], [sp_["w_z"], sp_["w_xbc"], sp_["w_dt_g"]],
                                       [(F32,), (F32,), (F32,)], 256, "ssd_in_p")
            xbc3 = xbc.reshape(bp, sp, SSD_CONV_DIM)
            yssd, st = _ssd_prompt(xbc3, dtg.reshape(bp, sp, SSD_GROUPS * LANES), sp_)
            y_p = _gated_norm_matmul(y_p, yssd.reshape(tp, SSD_INNER), z, sp_["norm"], sp_["w_out"], 256, "ssd_out_p")
            outs["ssd_p"].append(st)
            outs["cv_p"].append(xbc3[:, sp - (SSD_CONV - 1):, :])
            z, xbc, dtc = _norm_matmul(y_s, norm_mix[li], [sp_["w_z_f32"], sp_["w_xbc_f32"], sp_["w_dt_f32"]],
                                       [(F32,), (F32,), (F32,)], bs, "ssd_in_s", True)
            dt64, dt128 = _dt_expand(dtc, sp_)
            yssd, st = _ssd_step(xbc, state_conv[o_], dt64, dt128, state_ssd[o_], sp_)
            y_s = _gated_norm_matmul(y_s, yssd, z, sp_["norm"], sp_["w_out_f32"], bs, "ssd_out_s", True)
            outs["ssd_s"].append(st)
            outs["cv_s"].append(jnp.concatenate([state_conv[o_][:, 1:], xbc[:, None, :]], axis=1))
        mp = _moe_params(moe_w_group[li], moe_b_group[li], moe_w_expert[li], moe_b_expert[li], moe_w_gate[li],
                         moe_w_up[li], moe_w_down[li])
        y_p = _moe(y_p, norm_ffn[li], mp, _TM, 1024, "p", False)
        y_s = _moe(y_s, norm_ffn[li], mp, bs, bs, "s", li + 1 < depth)

    y_prompt = _rmsnorm(y_p, norm_final, _TM, "final_p").reshape(bp, sp, d)
    y_sample = _rmsnorm(y_s, norm_final, bs, "final_s").reshape(bs, 1, d)
    st = lambda n: jnp.stack(outs[n])
    return (y_prompt, y_sample, st("k_p"), st("v_p"), st("k_s"), st("v_s"), st("re_p"), st("im_p"), st("re_s"),
            st("im_s"), st("ssd_p"), st("ssd_s"), st("cv_p"), st("cv_s"))
```

```python
import functools
import math

import jax
import jax.numpy as jnp
from jax import lax
from jax.experimental import pallas as pl
from jax.experimental.pallas import tpu as pltpu

F32 = jnp.float32
BF16 = jnp.bfloat16
HIGHEST = lax.Precision.HIGHEST

D_MODEL = 1024
NORM_EPS = 1e-6
PAGE_SIZE = 128
ATT_HEADS = 4
ATT_D = 64
ATT_WIDTH = ATT_HEADS * 2 * ATT_D
HEAD_W = 2 * ATT_D
S5_WIDTH = 512
S5_GROUP = 16
S5_GROUPS = 32
S5_STATE = 64
S5_LANES = S5_GROUPS * S5_STATE
SSD_INNER = 2048
SSD_HEADDIM = 64
SSD_HEADS = 32
SSD_GROUPS = 4
SSD_HPG = 8
SSD_STATE = 128
SSD_CONV = 4
SSD_CONV_DIM = SSD_INNER + 2 * SSD_GROUPS * SSD_STATE
SSD_CHUNK = 128
SSD_GW = SSD_HPG * SSD_HEADDIM
MOE_GROUPS = 4
MOE_EPG = 8
MOE_EXPERTS = 32
MOE_FF = 256
LANES = 128
NEG = -1e30

_NT = (((1,), (1,)), ((), ()))
_TN = (((0,), (0,)), ((), ()))


def _rms(x, g):
    return x * lax.rsqrt(jnp.mean(x * x, axis=-1, keepdims=True) + NORM_EPS) * g


def _split(x):
    hi = x.astype(BF16)
    return hi, (x - hi.astype(F32)).astype(BF16)


def _mm(a, b, precise=False, dims=None):
    dot = jnp.dot if dims is None else functools.partial(lax.dot_general, dimension_numbers=dims)
    if not precise:
        return dot(a.astype(BF16), b.astype(BF16), preferred_element_type=F32)
    m = a.shape[0]
    a_hi, a_lo = _split(a)
    b_hi, b_lo = _split(b)
    r = dot(jnp.concatenate([a_hi, a_lo], axis=0), b_hi, preferred_element_type=F32)
    return r[:m] + r[m:] + dot(a_hi, b_lo, preferred_element_type=F32)


def _norm_matmul_kernel(x_ref, g_ref, *refs, out_dtypes, precise):
    n = len(out_dtypes)
    w_refs = refs[:n]
    o_refs = refs[n:]
    h = _rms(x_ref[...], g_ref[...])
    if not precise:
        h = h.astype(BF16)
    k = 0
    for w_ref, dts in zip(w_refs, out_dtypes):
        r = _mm(h, w_ref[...], precise)
        for dt in dts:
            o_refs[k][...] = r.astype(dt)
            k += 1


def _norm_matmul(x, g, ws, out_dtypes, tm, name, precise=False):
    m, kdim = x.shape
    out_shape, out_specs = [], []
    for w, dts in zip(ws, out_dtypes):
        for dt in dts:
            out_shape.append(jax.ShapeDtypeStruct((m, w.shape[1]), dt))
            out_specs.append(pl.BlockSpec((tm, w.shape[1]), lambda i: (i, 0)))
    return pl.pallas_call(
        functools.partial(_norm_matmul_kernel, out_dtypes=out_dtypes, precise=precise),
        grid=(m // tm,),
        in_specs=[pl.BlockSpec((tm, kdim), lambda i: (i, 0)), pl.BlockSpec((1, kdim), lambda i: (0, 0))]
        + [pl.BlockSpec(w.shape, lambda i: (0, 0)) for w in ws],
        out_specs=out_specs,
        out_shape=out_shape,
        compiler_params=pltpu.CompilerParams(dimension_semantics=("parallel",)),
        name=name,
    )(x, g.reshape(1, kdim), *ws)


def _matmul_res_kernel(res_ref, *refs, precise):
    n = (len(refs) - 1) // 2
    acc = res_ref[...]
    for a_ref, w_ref in zip(refs[:n], refs[n:2 * n]):
        acc = acc + _mm(a_ref[...], w_ref[...], precise)
    refs[-1][...] = acc


def _matmul_res(res, a_list, w_list, tm, name, precise=False):
    m, n = res.shape
    return pl.pallas_call(
        functools.partial(_matmul_res_kernel, precise=precise),
        grid=(m // tm,),
        in_specs=[pl.BlockSpec((tm, n), lambda i: (i, 0))]
        + [pl.BlockSpec((tm, a.shape[1]), lambda i: (i, 0)) for a in a_list]
        + [pl.BlockSpec(w.shape, lambda i: (0, 0)) for w in w_list],
        out_specs=pl.BlockSpec((tm, n), lambda i: (i, 0)),
        out_shape=jax.ShapeDtypeStruct((m, n), F32),
        compiler_params=pltpu.CompilerParams(dimension_semantics=("parallel",)),
        name=name,
    )(res, *a_list, *w_list)


def _gated_norm_matmul_kernel(res_ref, y_ref, z_ref, g_ref, w_ref, o_ref, *, precise):
    y = y_ref[...] * jax.nn.silu(z_ref[...])
    o_ref[...] = res_ref[...] + _mm(_rms(y, g_ref[...]), w_ref[...], precise)


def _gated_norm_matmul(res, y, z, g, w, tm, name, precise=False):
    m, n = res.shape
    kdim = y.shape[1]
    return pl.pallas_call(
        functools.partial(_gated_norm_matmul_kernel, precise=precise),
        grid=(m // tm,),
        in_specs=[pl.BlockSpec((tm, n), lambda i: (i, 0)), pl.BlockSpec((tm, kdim), lambda i: (i, 0)),
                  pl.BlockSpec((tm, kdim), lambda i: (i, 0)), pl.BlockSpec((1, kdim), lambda i: (0, 0)),
                  pl.BlockSpec(w.shape, lambda i: (0, 0))],
        out_specs=pl.BlockSpec((tm, n), lambda i: (i, 0)),
        out_shape=jax.ShapeDtypeStruct((m, n), F32),
        compiler_params=pltpu.CompilerParams(dimension_semantics=("parallel",)),
        name=name,
    )(res, y, z, g.reshape(1, kdim), w)


def _rmsnorm_kernel(x_ref, g_ref, o_ref):
    o_ref[...] = _rms(x_ref[...], g_ref[...])


def _rmsnorm(x, g, tm, name):
    m, n = x.shape
    return pl.pallas_call(
        _rmsnorm_kernel,
        grid=(m // tm,),
        in_specs=[pl.BlockSpec((tm, n), lambda i: (i, 0)), pl.BlockSpec((1, n), lambda i: (0, 0))],
        out_specs=pl.BlockSpec((tm, n), lambda i: (i, 0)),
        out_shape=jax.ShapeDtypeStruct((m, n), F32),
        compiler_params=pltpu.CompilerParams(dimension_semantics=("parallel",)),
        name=name,
    )(x, g.reshape(1, n))


def _split_q(q):
    lane = lax.broadcasted_iota(jnp.int32, q.shape, 1) % HEAD_W
    scale = ATT_D ** -0.5
    qs = q * jnp.asarray(scale, q.dtype)
    zero = jnp.zeros_like(qs)
    return jnp.where(lane < ATT_D, qs, zero), jnp.where(lane >= ATT_D, qs, zero)


def _subln(o, g, out_scale):
    return _rms(o, g) * out_scale


def _attn_prompt_kernel(lam_ref, q_ref, k_ref, vt_ref, g_ref, o_ref, qt_sc, m_sc, l_sc, acc_sc, *, tq, tk, out_scale):
    qi = pl.program_id(1)
    heads = range(ATT_HEADS)
    for h in heads:
        q1, q2 = _split_q(q_ref[0, :, h * HEAD_W:(h + 1) * HEAD_W])
        qt_sc[h] = jnp.concatenate([q1, q2], axis=0).astype(F32).T.astype(BF16)
    m_sc[...] = jnp.full(m_sc.shape, NEG, F32)
    l_sc[...] = jnp.zeros(l_sc.shape, F32)
    acc_sc[...] = jnp.zeros(acc_sc.shape, F32)

    def kv_step(j, masked):
        start = pl.multiple_of(j * tk, tk)
        for h in heads:
            kt = k_ref[0, pl.ds(start, tk), h * HEAD_W:(h + 1) * HEAD_W]
            st = jnp.dot(kt, qt_sc[h], preferred_element_type=F32)
            if masked:
                k_pos = start + lax.broadcasted_iota(jnp.int32, st.shape, 0)
                q_pos = qi * tq + lax.broadcasted_iota(jnp.int32, st.shape, 1) % tq
                st = jnp.where(k_pos <= q_pos, st, NEG)
            m_old = m_sc[h]
            m_new = jnp.maximum(m_old, jnp.max(st, axis=0, keepdims=True))
            p = jnp.exp(st - m_new)
            alpha = jnp.exp(m_old - m_new)
            l_sc[h] = alpha * l_sc[h] + jnp.sum(p, axis=0, keepdims=True)
            m_sc[h] = m_new
            acc_sc[h] = alpha * acc_sc[h] + jnp.dot(vt_ref[0, h, j], p.astype(BF16), preferred_element_type=F32)

    n_full = (qi * tq) // tk

    def full_step(j, carry):
        kv_step(j, False)
        return carry

    lax.fori_loop(0, n_full, full_step, 0)
    kv_step(n_full, True)
    for h in heads:
        acc, l = acc_sc[h], l_sc[h]
        ot = acc[:, 0:tq] / l[:, 0:tq] - lam_ref[0, 0] * (acc[:, tq:2 * tq] / l[:, tq:2 * tq])
        o_ref[0, :, h * HEAD_W:(h + 1) * HEAD_W] = _subln(ot.T, g_ref[...], out_scale).astype(o_ref.dtype)


def _attn_prompt(q, k, v, lam, subln, out_scale, tq=256, tk=256):
    b, s, _ = q.shape
    nblk = s // tk
    vt = jnp.transpose(v.reshape(b, nblk, tk, ATT_HEADS, HEAD_W), (0, 3, 1, 4, 2))
    kern = functools.partial(_attn_prompt_kernel, tq=tq, tk=tk, out_scale=out_scale)
    return pl.pallas_call(
        kern,
        grid=(b, s // tq),
        in_specs=[pl.BlockSpec(memory_space=pltpu.SMEM),
                  pl.BlockSpec((1, tq, ATT_WIDTH), lambda bi, i: (bi, i, 0)),
                  pl.BlockSpec((1, s, ATT_WIDTH), lambda bi, i: (bi, 0, 0)),
                  pl.BlockSpec((1, ATT_HEADS, nblk, HEAD_W, tk), lambda bi, i: (bi, 0, 0, 0, 0)),
                  pl.BlockSpec((1, HEAD_W), lambda bi, i: (0, 0))],
        out_specs=pl.BlockSpec((1, tq, ATT_WIDTH), lambda bi, i: (bi, i, 0)),
        out_shape=jax.ShapeDtypeStruct((b, s, ATT_WIDTH), BF16),
        scratch_shapes=[pltpu.VMEM((ATT_HEADS, HEAD_W, 2 * tq), BF16), pltpu.VMEM((ATT_HEADS, 1, 2 * tq), F32),
                        pltpu.VMEM((ATT_HEADS, 1, 2 * tq), F32), pltpu.VMEM((ATT_HEADS, HEAD_W, 2 * tq), F32)],
        compiler_params=pltpu.CompilerParams(dimension_semantics=("parallel", "parallel")),
        name="attn_prompt",
    )(lam.reshape(1, 1), q, k, vt, subln.reshape(1, HEAD_W))


_DEC_PAGES = 16
_DEC_PROW = PAGE_SIZE * ATT_HEADS
_DEC_COLS = _DEC_PAGES * _DEC_PROW
_DEC_ROWS = 2 * ATT_HEADS


def _attn_decode_kernel(pt_ref, lam_ref, q_ref, kn_ref, vn_ref, g_ref, *refs, steps, out_scale):
    del pt_ref
    k_refs = refs[:_DEC_PAGES]
    v_refs = refs[_DEC_PAGES:2 * _DEC_PAGES]
    o_ref = refs[2 * _DEC_PAGES]
    s_sc, m_sc, wn_sc, acc_sc = refs[2 * _DEC_PAGES + 1:]
    j = pl.program_id(1)
    lam = lam_ref[0, 0]

    q4 = q_ref[0] * (ATT_D ** -0.5)
    q8 = jnp.concatenate([q4, q4], axis=0)
    r8 = lax.broadcasted_iota(jnp.int32, (_DEC_ROWS, HEAD_W), 0)
    l8 = lax.broadcasted_iota(jnp.int32, (_DEC_ROWS, HEAD_W), 1)
    q8 = jnp.where(l8 // ATT_D == r8 // ATT_HEADS, q8, 0.0)

    @pl.when(j == 0)
    def _():
        m_sc[...] = jnp.full(m_sc.shape, NEG, F32)

    @pl.when(j < steps)
    def _():
        s = jnp.concatenate([_mm(q8, k_ref[0], True, _NT) for k_ref in k_refs], axis=1)
        rs = lax.broadcasted_iota(jnp.int32, s.shape, 0)
        cs = lax.broadcasted_iota(jnp.int32, s.shape, 1)
        s = jnp.where(cs % ATT_HEADS == rs % ATT_HEADS, s, NEG)
        s_sc[j] = s
        m_sc[...] = jnp.maximum(m_sc[...], jnp.max(s, axis=-1, keepdims=True))

    @pl.when(j == steps - 1)
    def _():
        kn = jnp.concatenate([kn_ref[0], kn_ref[0]], axis=0)
        sn = jnp.sum(q8 * kn, axis=-1, keepdims=True)
        m = jnp.maximum(m_sc[...], sn)
        pn = jnp.exp(sn - m)
        l = pn
        for t in range(steps):
            p = jnp.exp(s_sc[t] - m)
            s_sc[t] = p
            l = l + jnp.sum(p, axis=-1, keepdims=True)
        zeros = jnp.zeros((ATT_HEADS, _DEC_COLS), F32)
        for t in range(steps):
            p = s_sc[t] / l
            s_sc[t] = jnp.concatenate([p[:ATT_HEADS] - lam * p[ATT_HEADS:], zeros], axis=0)
        pn = pn / l
        wn_sc[...] = jnp.concatenate([pn[:ATT_HEADS] - lam * pn[ATT_HEADS:], jnp.zeros((ATT_HEADS, 1), F32)], axis=0)

    @pl.when(j == steps)
    def _():
        acc_sc[...] = jnp.zeros(acc_sc.shape, F32)

    @pl.when(j >= steps)
    def _():
        w = s_sc[j - steps]
        acc = acc_sc[...]
        for i, v_ref in enumerate(v_refs):
            acc = acc + _mm(w[:, i * _DEC_PROW:(i + 1) * _DEC_PROW], v_ref[0], True)
        acc_sc[...] = acc

    @pl.when(j == 2 * steps - 1)
    def _():
        o4 = acc_sc[:ATT_HEADS] + wn_sc[:ATT_HEADS] * vn_ref[0]
        o_ref[0] = _subln(o4, g_ref[...], out_scale)


def _attn_decode(q, k_new, v_new, cache_k, cache_v, page_table, lam, subln, out_scale):
    b = q.shape[0]
    n_pages = page_table.shape[1]
    steps = n_pages // _DEC_PAGES
    ck = cache_k.reshape(-1, _DEC_PROW, HEAD_W)
    cv = cache_v.reshape(-1, _DEC_PROW, HEAD_W)

    def k_map(i):
        return lambda bi, j, pt: (pt[bi, jnp.minimum(j, steps - 1) * _DEC_PAGES + i], 0, 0)

    def v_map(i):
        return lambda bi, j, pt: (pt[bi, jnp.maximum(j - steps, 0) * _DEC_PAGES + i], 0, 0)

    row = lambda bi, j, pt: (bi, 0, 0)
    head_rows = pl.BlockSpec((1, ATT_HEADS, HEAD_W), row)
    grid_spec = pltpu.PrefetchScalarGridSpec(
        num_scalar_prefetch=1,
        grid=(b, 2 * steps),
        in_specs=[pl.BlockSpec(memory_space=pltpu.SMEM), head_rows, head_rows, head_rows,
                  pl.BlockSpec((1, HEAD_W), lambda bi, j, pt: (0, 0))]
        + [pl.BlockSpec((1, _DEC_PROW, HEAD_W), k_map(i)) for i in range(_DEC_PAGES)]
        + [pl.BlockSpec((1, _DEC_PROW, HEAD_W), v_map(i)) for i in range(_DEC_PAGES)],
        out_specs=head_rows,
        scratch_shapes=[pltpu.VMEM((steps, _DEC_ROWS, _DEC_COLS), F32), pltpu.VMEM((_DEC_ROWS, 1), F32),
                        pltpu.VMEM((_DEC_ROWS, 1), F32), pltpu.VMEM((_DEC_ROWS, HEAD_W), F32)],
    )
    out = pl.pallas_call(
        functools.partial(_attn_decode_kernel, steps=steps, out_scale=out_scale),
        grid_spec=grid_spec,
        out_shape=jax.ShapeDtypeStruct((b, ATT_HEADS, HEAD_W), F32),
        compiler_params=pltpu.CompilerParams(dimension_semantics=("parallel", "arbitrary")),
        name="attn_decode",
    )(page_table, lam.reshape(1, 1), q.reshape(b, ATT_HEADS, HEAD_W), k_new.reshape(b, ATT_HEADS, HEAD_W),
      v_new.reshape(b, ATT_HEADS, HEAD_W), subln.reshape(1, HEAD_W), *([ck] * _DEC_PAGES), *([cv] * _DEC_PAGES))
    return out.reshape(b, ATT_WIDTH)


_S5_KT = 2
_S5_KW = S5_WIDTH // _S5_KT
_S5_KL = S5_LANES // _S5_KT
_S5_NT = S5_LANES // LANES
_S5_SCAN_TILES = 8


def _s5_input(u, bmat_ref, precise):
    re, im = [], []
    for kt in range(_S5_KT):
        r = _mm(u[:, kt * _S5_KW:(kt + 1) * _S5_KW], bmat_ref[kt], precise)
        re.append(r[:, :_S5_KL])
        im.append(r[:, _S5_KL:])
    return jnp.concatenate(re, axis=1), jnp.concatenate(im, axis=1)


def _s5_output(h_re, h_im, u, cmat_ref, d_ref, wglu_ref, bglu_ref, precise):
    if not precise:
        h_re, h_im = h_re.astype(BF16), h_im.astype(BF16)
    ys = []
    for kt in range(_S5_KT):
        sl = slice(kt * _S5_KL, (kt + 1) * _S5_KL)
        y = _mm(h_re[:, sl], cmat_ref[kt, :_S5_KL, :], precise)
        y = y + _mm(h_im[:, sl], cmat_ref[kt, _S5_KL:, :], precise)
        ys.append(y)
    y = jnp.concatenate(ys, axis=1) + d_ref[...] * u
    g = jax.nn.gelu(y)
    gate = _mm(g, wglu_ref[...], precise) + bglu_ref[...]
    return g * jax.nn.sigmoid(gate)


def _s5_scan_kernel(u_ref, are_ref, aim_ref, bmat_ref, cmat_ref, d_ref, wglu_ref, bglu_ref,
                    o_ref, hre_ref, him_ref, bu_re, bu_im, *, nb, lc, bp):
    c = pl.program_id(0)

    @pl.when(c == 0)
    def _():
        bu_re[...] = jnp.zeros(bu_re.shape, F32)
        bu_im[...] = jnp.zeros(bu_im.shape, F32)
        hre_ref[...] = jnp.zeros(hre_ref.shape, F32)
        him_ref[...] = jnp.zeros(him_ref.shape, F32)

    for b in range(nb):
        re, im = _s5_input(u_ref[b], bmat_ref, False)
        for jt in range(_S5_NT):
            bu_re[jt, pl.ds(b, lc, stride=bp), :] = re[:, jt * LANES:(jt + 1) * LANES]
            bu_im[jt, pl.ds(b, lc, stride=bp), :] = im[:, jt * LANES:(jt + 1) * LANES]

    for j0 in range(0, _S5_NT, _S5_SCAN_TILES):
        tiles = pl.ds(j0, _S5_SCAN_TILES)
        a_re = jnp.broadcast_to(are_ref[tiles], (_S5_SCAN_TILES, bp, LANES))
        a_im = jnp.broadcast_to(aim_ref[tiles], (_S5_SCAN_TILES, bp, LANES))

        def step(t, carry, tiles=tiles, a_re=a_re, a_im=a_im):
            hr, hi = carry
            rows = pl.ds(pl.multiple_of(t * bp, bp), bp)
            nr = hr * a_re - hi * a_im + bu_re[tiles, rows, :]
            ni = hr * a_im + hi * a_re + bu_im[tiles, rows, :]
            bu_re[tiles, rows, :] = nr
            bu_im[tiles, rows, :] = ni
            return nr, ni

        hr, hi = lax.fori_loop(0, lc, step, (hre_ref[tiles], him_ref[tiles]), unroll=2)
        hre_ref[tiles] = hr
        him_ref[tiles] = hi

    for b in range(nb):
        h_re = jnp.concatenate([bu_re[jt, pl.ds(b, lc, stride=bp), :] for jt in range(_S5_NT)], axis=1)
        h_im = jnp.concatenate([bu_im[jt, pl.ds(b, lc, stride=bp), :] for jt in range(_S5_NT)], axis=1)
        o_ref[b] = _s5_output(h_re, h_im, u_ref[b], cmat_ref, d_ref, wglu_ref, bglu_ref, False).astype(o_ref.dtype)


def _s5_prompt(u, prm, lc=128):
    nb, l, _ = u.shape
    bp = 8
    full = lambda shape: pl.BlockSpec(shape, lambda c: (0,) * len(shape))
    o, h_re, h_im = pl.pallas_call(
        functools.partial(_s5_scan_kernel, nb=nb, lc=lc, bp=bp),
        grid=(l // lc,),
        in_specs=[pl.BlockSpec((nb, lc, S5_WIDTH), lambda c: (0, c, 0)),
                  full((_S5_NT, 1, LANES)), full((_S5_NT, 1, LANES)),
                  full((_S5_KT, _S5_KW, 2 * _S5_KL)), full((_S5_KT, 2 * _S5_KL, _S5_KW)),
                  full((1, S5_WIDTH)), full((S5_WIDTH, S5_WIDTH)), full((1, S5_WIDTH))],
        out_specs=[pl.BlockSpec((nb, lc, S5_WIDTH), lambda c: (0, c, 0)),
                   full((_S5_NT, bp, LANES)), full((_S5_NT, bp, LANES))],
        out_shape=[jax.ShapeDtypeStruct((nb, l, S5_WIDTH), BF16),
                   jax.ShapeDtypeStruct((_S5_NT, bp, LANES), F32), jax.ShapeDtypeStruct((_S5_NT, bp, LANES), F32)],
        scratch_shapes=[pltpu.VMEM((_S5_NT, lc * bp, LANES), F32) for _ in range(2)],
        compiler_params=pltpu.CompilerParams(dimension_semantics=("arbitrary",)),
        name="s5_prompt",
    )(u, prm["a_re"].reshape(_S5_NT, 1, LANES), prm["a_im"].reshape(_S5_NT, 1, LANES), prm["bmat"].astype(BF16),
      prm["cmat"].astype(BF16), prm["d"], prm["w_glu"].astype(BF16), prm["b_glu"])
    rows = lambda h: jnp.transpose(h, (1, 0, 2)).reshape(bp, S5_LANES)[:nb]
    return o, rows(h_re), rows(h_im)


def _s5_step_kernel(u_ref, h0re_ref, h0im_ref, are_ref, aim_ref, bmat_ref, cmat_ref, d_ref, wglu_ref, bglu_ref,
                    o_ref, hre_ref, him_ref):
    u = u_ref[...]
    bu_re, bu_im = _s5_input(u, bmat_ref, True)
    h_re, h_im = h0re_ref[...], h0im_ref[...]
    a_re, a_im = are_ref[...], aim_ref[...]
    n_re = h_re * a_re - h_im * a_im + bu_re
    n_im = h_re * a_im + h_im * a_re + bu_im
    hre_ref[...] = n_re
    him_ref[...] = n_im
    o_ref[...] = _s5_output(n_re, n_im, u, cmat_ref, d_ref, wglu_ref, bglu_ref, True)


def _s5_step(u, h0_re, h0_im, prm):
    nb = u.shape[0]
    return pl.pallas_call(
        _s5_step_kernel,
        out_shape=[jax.ShapeDtypeStruct((nb, S5_WIDTH), F32),
                   jax.ShapeDtypeStruct((nb, S5_LANES), F32), jax.ShapeDtypeStruct((nb, S5_LANES), F32)],
        name="s5_step",
    )(u, h0_re, h0_im, prm["a_re"], prm["a_im"], prm["bmat"], prm["cmat"], prm["d"], prm["w_glu"], prm["b_glu"])


def _s5_params(lam_re, lam_im, log_dt, b_re, b_im, c_re, c_im, d, w_glu, b_glu):
    lam = lax.complex(lam_re, lam_im)
    dt = jnp.exp(log_dt)[:, None]
    a_bar = jnp.exp(lam * dt)
    b_bar = ((a_bar - 1.0) / lam)[..., None] * lax.complex(b_re, b_im)
    gk = S5_GROUPS // _S5_KT
    eye = jnp.eye(gk, dtype=F32)

    def in_tile(x):
        x = x.reshape(_S5_KT, gk, S5_STATE, S5_GROUP)
        return jnp.einsum("kgpc,gh->kgchp", x, eye).reshape(_S5_KT, _S5_KW, _S5_KL)

    def out_tile(x):
        x = x.reshape(_S5_KT, gk, S5_GROUP, S5_STATE)
        return jnp.einsum("kgcp,gh->kgphc", x, eye).reshape(_S5_KT, _S5_KL, _S5_KW)

    bmat = jnp.concatenate([in_tile(jnp.real(b_bar)), in_tile(jnp.imag(b_bar))], axis=2)
    cmat = jnp.concatenate([out_tile(c_re), out_tile(-c_im)], axis=1)
    return {
        "a_re": jnp.real(a_bar).reshape(1, S5_LANES), "a_im": jnp.imag(a_bar).reshape(1, S5_LANES),
        "bmat": bmat, "cmat": cmat, "d": d.reshape(1, S5_WIDTH), "w_glu": w_glu, "b_glu": b_glu.reshape(1, S5_WIDTH),
    }


_CONV_PAD = 8


def _conv_silu(x, xf, w_ref, b_ref, rows):
    xf[_CONV_PAD:_CONV_PAD + rows, :] = x
    w = w_ref[...]
    out = b_ref[...]
    for i in range(SSD_CONV):
        lo = _CONV_PAD - (SSD_CONV - 1) + i
        out = out + w[i:i + 1] * xf[lo:lo + rows, :]
    xf[_CONV_PAD - (SSD_CONV - 1):_CONV_PAD, :] = xf[_CONV_PAD + rows - (SSD_CONV - 1):_CONV_PAD + rows, :]
    return jax.nn.silu(out)


def _ssd_chunk_kernel(xs_ref, bm_ref, cm_ref, dt_ref, cwx_ref, cwb_ref, cwc_ref, cbx_ref, cbb_ref, cbc_ref,
                      dtb_ref, a_ref, dexp_ref, y_ref, st_ref, xf_x, xf_b, xf_c):
    c = pl.program_id(2)
    q = SSD_CHUNK

    @pl.when(c == 0)
    def _():
        xf_x[0:_CONV_PAD, :] = jnp.zeros((_CONV_PAD, xf_x.shape[1]), F32)
        xf_b[0:_CONV_PAD, :] = jnp.zeros((_CONV_PAD, xf_b.shape[1]), F32)
        xf_c[0:_CONV_PAD, :] = jnp.zeros((_CONV_PAD, xf_c.shape[1]), F32)
        st_ref[...] = jnp.zeros(st_ref.shape, F32)

    xs = _conv_silu(xs_ref[0], xf_x, cwx_ref, cbx_ref, q)
    bm = _conv_silu(bm_ref[0], xf_b, cwb_ref, cbb_ref, q)
    cm = _conv_silu(cm_ref[0], xf_c, cwc_ref, cbc_ref, q)
    bm16 = bm.astype(BF16)
    cm16 = cm.astype(BF16)

    dt = jax.nn.softplus(dt_ref[0] + dtb_ref[0])
    da = dt * a_ref[0]
    row = lax.broadcasted_iota(jnp.int32, (q, q), 0)
    col = lax.broadcasted_iota(jnp.int32, (q, q), 1)
    causal = row >= col
    acs = jnp.dot(causal.astype(F32), da, preferred_element_type=F32, precision=HIGHEST)
    acs_t = acs.T
    acs_last = acs[q - 1:q, :]
    e_acs = jnp.exp(acs)
    dec_s = jnp.exp(acs_last - acs)
    chunk_dec = jnp.exp(acs_last)
    cb = _mm(cm16, bm16, dims=_NT)

    ys = []
    for j in range(SSD_HPG):
        seg = acs[:, j:j + 1] - acs_t[j:j + 1, :]
        lmat = jnp.exp(jnp.where(causal, seg, -jnp.inf))
        xdt = xs[:, j * SSD_HEADDIM:(j + 1) * SSD_HEADDIM] * dt[:, j:j + 1]
        y_diag = _mm(cb * lmat, xdt)
        h = st_ref[0, j]
        y_off = _mm(cm16, h, dims=_NT) * e_acs[:, j:j + 1]
        st_ref[0, j] = chunk_dec[:, j:j + 1] * h + _mm(xdt * dec_s[:, j:j + 1], bm16, dims=_TN)
        ys.append(y_diag + y_off)
    y_ref[0] = jnp.concatenate(ys, axis=1) + dexp_ref[...] * xs


def _ssd_prompt(xbc, dtg, prm):
    nb, l, _ = xbc.shape
    nc = l // SSD_CHUNK
    q = SSD_CHUNK
    boff = SSD_INNER // LANES
    coff = boff + SSD_GROUPS
    colx = lambda b, g, c: (0, g)
    colb = lambda b, g, c: (0, boff + g)
    colc = lambda b, g, c: (0, coff + g)
    grp = lambda b, g, c: (g, 0, 0)
    return pl.pallas_call(
        _ssd_chunk_kernel,
        grid=(nb, SSD_GROUPS, nc),
        in_specs=[pl.BlockSpec((1, q, SSD_GW), lambda b, g, c: (b, c, g)),
                  pl.BlockSpec((1, q, LANES), lambda b, g, c: (b, c, boff + g)),
                  pl.BlockSpec((1, q, LANES), lambda b, g, c: (b, c, coff + g)),
                  pl.BlockSpec((1, q, LANES), lambda b, g, c: (b, c, g)),
                  pl.BlockSpec((SSD_CONV, SSD_GW), colx), pl.BlockSpec((SSD_CONV, LANES), colb),
                  pl.BlockSpec((SSD_CONV, LANES), colc),
                  pl.BlockSpec((1, SSD_GW), colx), pl.BlockSpec((1, LANES), colb), pl.BlockSpec((1, LANES), colc),
                  pl.BlockSpec((1, 1, LANES), grp), pl.BlockSpec((1, 1, LANES), grp),
                  pl.BlockSpec((1, SSD_GW), colx)],
        out_specs=[pl.BlockSpec((1, q, SSD_GW), lambda b, g, c: (b, c, g)),
                   pl.BlockSpec((1, SSD_HPG, SSD_HEADDIM, SSD_STATE), lambda b, g, c: (b, g, 0, 0))],
        out_shape=[jax.ShapeDtypeStruct((nb, l, SSD_INNER), F32),
                   jax.ShapeDtypeStruct((nb, SSD_HEADS, SSD_HEADDIM, SSD_STATE), F32)],
        scratch_shapes=[pltpu.VMEM((_CONV_PAD + q, SSD_GW), F32), pltpu.VMEM((_CONV_PAD + q, LANES), F32),
                        pltpu.VMEM((_CONV_PAD + q, LANES), F32)],
        compiler_params=pltpu.CompilerParams(dimension_semantics=("parallel", "parallel", "arbitrary")),
        name="ssd_prompt",
    )(xbc, xbc, xbc, dtg, prm["conv_w"], prm["conv_w"], prm["conv_w"], prm["conv_b"], prm["conv_b"], prm["conv_b"],
      prm["dt_bias_g"], prm["a_g"], prm["d_exp"])


def _ssd_step_kernel(x_ref, buf_ref, cw_ref, cb_ref, dt64_ref, dtb64_ref, dt128_ref, dtb128_ref, a128_ref,
                     dexp_ref, st_ref, y_ref, so_ref):
    w = cw_ref[...]
    buf = buf_ref[0]
    conv = cb_ref[...]
    for i in range(SSD_CONV - 1):
        conv = conv + w[i:i + 1] * buf[i:i + 1]
    conv = conv + w[SSD_CONV - 1:SSD_CONV] * x_ref[0]
    xc = jax.nn.silu(conv)
    xs = xc[:, :SSD_INNER]
    xdt = xs * jax.nn.softplus(dt64_ref[0] + dtb64_ref[...])
    dt = jax.nn.softplus(dt128_ref[0] + dtb128_ref[...])
    dec = jnp.exp(dt * a128_ref[...])
    row = lax.broadcasted_iota(jnp.int32, (LANES, LANES), 0)
    col = lax.broadcasted_iota(jnp.int32, (LANES, LANES), 1)
    diag = row == col
    ys = []
    for i in range(SSD_HEADS // 2):
        g = (2 * i) // SSD_HPG
        bm = xc[:, SSD_INNER + g * SSD_STATE:SSD_INNER + (g + 1) * SSD_STATE]
        cm = xc[:, SSD_INNER + (SSD_GROUPS + g) * SSD_STATE:SSD_INNER + (SSD_GROUPS + g + 1) * SSD_STATE]
        xp = jnp.broadcast_to(xdt[:, i * LANES:(i + 1) * LANES], (LANES, LANES))
        outer = _mm(jnp.where(diag, xp, 0.0), jnp.broadcast_to(bm, (LANES, LANES)), True)
        h = st_ref[0, 2 * i:2 * i + 2].reshape(LANES, SSD_STATE)
        dpair = jnp.concatenate([jnp.broadcast_to(dec[2 * i:2 * i + 1], (SSD_HEADDIM, LANES)),
                                 jnp.broadcast_to(dec[2 * i + 1:2 * i + 2], (SSD_HEADDIM, LANES))], axis=0)
        hn = dpair * h + outer
        so_ref[0, 2 * i:2 * i + 2] = hn.reshape(2, SSD_HEADDIM, SSD_STATE)
        ys.append(_mm(jnp.broadcast_to(cm, (8, SSD_STATE)), hn, True, _NT)[0:1])
    y_ref[0] = jnp.concatenate(ys, axis=1) + dexp_ref[...] * xs


def _dt_expand_kernel(dt_ref, e64_ref, e128_ref, o64_ref, o128_ref):
    dt = dt_ref[...]
    o64_ref[...] = _mm(dt, e64_ref[...], True)
    o128_ref[...] = _mm(dt, e128_ref[...], True)


def _dt_expand(dt, prm):
    nb = dt.shape[0]
    return pl.pallas_call(
        _dt_expand_kernel,
        out_shape=[jax.ShapeDtypeStruct((nb, SSD_INNER), F32), jax.ShapeDtypeStruct((nb, SSD_HEADS * LANES), F32)],
        name="ssd_dt_expand",
    )(dt, prm["expand_64"], prm["expand_128"])


def _ssd_step(xbc, conv_buf, dt64, dt128, state, prm):
    nb = xbc.shape[0]
    full = lambda shape: pl.BlockSpec(shape, lambda b: (0,) * len(shape))
    y, so = pl.pallas_call(
        _ssd_step_kernel,
        grid=(nb,),
        in_specs=[pl.BlockSpec((1, 1, SSD_CONV_DIM), lambda b: (b, 0, 0)),
                  pl.BlockSpec((1, SSD_CONV - 1, SSD_CONV_DIM), lambda b: (b, 0, 0)),
                  full((SSD_CONV, SSD_CONV_DIM)), full((1, SSD_CONV_DIM)),
                  pl.BlockSpec((1, 1, SSD_INNER), lambda b: (b, 0, 0)), full((1, SSD_INNER)),
                  pl.BlockSpec((1, SSD_HEADS, LANES), lambda b: (b, 0, 0)), full((SSD_HEADS, LANES)),
                  full((SSD_HEADS, LANES)), full((1, SSD_INNER)),
                  pl.BlockSpec((1, SSD_HEADS, SSD_HEADDIM, SSD_STATE), lambda b: (b, 0, 0, 0))],
        out_specs=[pl.BlockSpec((1, 1, SSD_INNER), lambda b: (b, 0, 0)),
                   pl.BlockSpec((1, SSD_HEADS, SSD_HEADDIM, SSD_STATE), lambda b: (b, 0, 0, 0))],
        out_shape=[jax.ShapeDtypeStruct((nb, 1, SSD_INNER), F32),
                   jax.ShapeDtypeStruct((nb, SSD_HEADS, SSD_HEADDIM, SSD_STATE), F32)],
        compiler_params=pltpu.CompilerParams(dimension_semantics=("parallel",)),
        name="ssd_step",
    )(xbc.reshape(nb, 1, SSD_CONV_DIM), conv_buf, prm["conv_w"], prm["conv_b"],
      dt64.reshape(nb, 1, SSD_INNER), prm["dt_bias_64"], dt128.reshape(nb, SSD_HEADS, LANES),
      prm["dt_bias_128"], prm["a_128"], prm["d_exp"], state)
    return y.reshape(nb, SSD_INNER), so


_GROUP_LANE0 = MOE_EXPERTS
_ROUTE_LANE0 = 64


def _moe_router_kernel(y_ref, g_ref, wr_ref, br_ref, xn_ref, comb_ref):
    xn = _rms(y_ref[...], g_ref[...])
    xn_ref[...] = xn.astype(xn_ref.dtype)
    lg = _mm(xn, wr_ref[...], True) + br_ref[...]
    lane = lax.broadcasted_iota(jnp.int32, lg.shape, 1)
    is_group = (lane >= _GROUP_LANE0) & (lane < _GROUP_LANE0 + MOE_GROUPS)
    gl = jnp.where(is_group, lg, NEG)
    gmax = jnp.max(gl, axis=-1, keepdims=True)
    g_p = 1.0 / jnp.sum(jnp.exp(gl - gmax), axis=-1, keepdims=True)
    gidx = jnp.min(jnp.where(gl == gmax, lane - _GROUP_LANE0, MOE_GROUPS), axis=-1, keepdims=True)
    el = jnp.where((lane < MOE_EXPERTS) & (lane // MOE_EPG == gidx), lg, NEG)
    m1 = jnp.max(el, axis=-1, keepdims=True)
    i1 = jnp.min(jnp.where(el == m1, lane, LANES), axis=-1, keepdims=True)
    el2 = jnp.where(lane == i1, NEG, el)
    m2 = jnp.max(el2, axis=-1, keepdims=True)
    i2 = jnp.min(jnp.where(el2 == m2, lane, LANES), axis=-1, keepdims=True)
    e2 = jnp.exp(m2 - m1)
    den = 1.0 + e2
    g1 = g_p / den
    g2 = g_p * (e2 / den)
    comb = jnp.where(lane == i1, g1, 0.0) + jnp.where(lane == i2, g2, 0.0)
    for k, v in enumerate((i1.astype(F32), i2.astype(F32), g1, g2)):
        comb = jnp.where(lane == _ROUTE_LANE0 + k, v, comb)
    comb_ref[...] = comb


def _moe_router(y, g, wr, br, tm, name, xn_dtype):
    m, d = y.shape
    return pl.pallas_call(
        _moe_router_kernel,
        grid=(m // tm,),
        in_specs=[pl.BlockSpec((tm, d), lambda i: (i, 0)), pl.BlockSpec((1, d), lambda i: (0, 0)),
                  pl.BlockSpec((d, LANES), lambda i: (0, 0)), pl.BlockSpec((1, LANES), lambda i: (0, 0))],
        out_specs=[pl.BlockSpec((tm, d), lambda i: (i, 0)), pl.BlockSpec((tm, LANES), lambda i: (i, 0))],
        out_shape=[jax.ShapeDtypeStruct((m, d), xn_dtype), jax.ShapeDtypeStruct((m, LANES), F32)],
        compiler_params=pltpu.CompilerParams(dimension_semantics=("parallel",)),
        name=name,
    )(y, g.reshape(1, d), wr, br)


def _moe_dense_kernel(res_ref, xn_ref, comb_ref, wg_ref, wu_ref, wd_ref, o_ref, acc, *, precise):
    e = pl.program_id(1)

    @pl.when(e == 0)
    def _():
        acc[...] = jnp.zeros(acc.shape, F32)

    x = xn_ref[...]
    hdn = jax.nn.silu(_mm(x, wg_ref[0], precise)) * _mm(x, wu_ref[0], precise)
    comb = comb_ref[...]
    lane = lax.broadcasted_iota(jnp.int32, comb.shape, 1)
    gate = jnp.sum(jnp.where(lane == e, comb, 0.0), axis=-1, keepdims=True)
    acc[...] += gate * _mm(hdn, wd_ref[0], precise)

    @pl.when(e == pl.num_programs(1) - 1)
    def _():
        o_ref[...] = res_ref[...] + acc[...]


def _moe_dense(res, xn, comb, wg, wu, wd, tm, name, precise):
    m, d = res.shape
    return pl.pallas_call(
        functools.partial(_moe_dense_kernel, precise=precise),
        grid=(m // tm, MOE_EXPERTS),
        in_specs=[pl.BlockSpec((tm, d), lambda i, e: (i, 0)), pl.BlockSpec((tm, d), lambda i, e: (i, 0)),
                  pl.BlockSpec((tm, LANES), lambda i, e: (i, 0)),
                  pl.BlockSpec((1, d, MOE_FF), lambda i, e: (e, 0, 0)),
                  pl.BlockSpec((1, d, MOE_FF), lambda i, e: (e, 0, 0)),
                  pl.BlockSpec((1, MOE_FF, d), lambda i, e: (e, 0, 0))],
        out_specs=pl.BlockSpec((tm, d), lambda i, e: (i, 0)),
        out_shape=jax.ShapeDtypeStruct((m, d), F32),
        scratch_shapes=[pltpu.VMEM((tm, d), F32)],
        compiler_params=pltpu.CompilerParams(dimension_semantics=("parallel", "arbitrary")),
        name=name,
    )(res, xn, comb, wg, wu, wd)


def _moe(y, g, prm, tm_r, tm_e, tag, precise):
    sfx = "_f32" if precise else ""
    xn, comb = _moe_router(y, g, prm["wr"], prm["br"], tm_r, "moe_router_" + tag, F32 if precise else BF16)
    return _moe_dense(y, xn, comb, prm["wg" + sfx], prm["wu" + sfx], prm["wd" + sfx], tm_e, "moe_experts_" + tag,
                      precise)


def _moe_grouped_kernel(te_ref, nt_ref, x_ref, wg_ref, wu_ref, wd_ref, o_ref):
    del te_ref
    active = pl.program_id(0) < nt_ref[0]

    @pl.when(active)
    def _():
        x = x_ref[...]
        hdn = jax.nn.silu(_mm(x, wg_ref[0])) * _mm(x, wu_ref[0])
        o_ref[...] = _mm(hdn, wd_ref[0]).astype(o_ref.dtype)

    @pl.when(jnp.logical_not(active))
    def _():
        o_ref[...] = jnp.zeros(o_ref.shape, o_ref.dtype)


def _moe_grouped(x_rows, tile_expert, n_tiles, wg, wu, wd, tm, name):
    r, d = x_rows.shape
    grid_spec = pltpu.PrefetchScalarGridSpec(
        num_scalar_prefetch=2,
        grid=(r // tm,),
        in_specs=[pl.BlockSpec((tm, d), lambda i, te, nt: (i, 0)),
                  pl.BlockSpec((1, d, MOE_FF), lambda i, te, nt: (te[i], 0, 0)),
                  pl.BlockSpec((1, d, MOE_FF), lambda i, te, nt: (te[i], 0, 0)),
                  pl.BlockSpec((1, MOE_FF, d), lambda i, te, nt: (te[i], 0, 0))],
        out_specs=pl.BlockSpec((tm, d), lambda i, te, nt: (i, 0)),
    )
    return pl.pallas_call(
        _moe_grouped_kernel,
        grid_spec=grid_spec,
        out_shape=jax.ShapeDtypeStruct((r, d), F32),
        compiler_params=pltpu.CompilerParams(dimension_semantics=("arbitrary",)),
        name=name,
    )(tile_expert, n_tiles, x_rows, wg, wu, wd)


def _moe_sparse(y, g, prm, tm_r, tm_e, tag):
    t, d = y.shape
    xn, route = _moe_router(y, g, prm["wr"], prm["br"], tm_r, "moe_router_" + tag, BF16)
    ids = route[:, _ROUTE_LANE0:_ROUTE_LANE0 + 2].astype(jnp.int32).reshape(-1)
    gates = route[:, _ROUTE_LANE0 + 2:_ROUTE_LANE0 + 4]
    n = 2 * t
    rows = n + MOE_EXPERTS * tm_e
    order = jnp.argsort(ids, stable=True).astype(jnp.int32)
    rank = jnp.argsort(order).astype(jnp.int32)
    counts = jnp.sum(jax.nn.one_hot(ids, MOE_EXPERTS, dtype=jnp.int32), axis=0)
    start = jnp.cumsum(counts) - counts
    pcounts = ((counts + tm_e - 1) // tm_e) * tm_e
    pend = jnp.cumsum(pcounts)
    pstart = pend - pcounts
    tile_expert = jnp.minimum(jnp.searchsorted(pend, jnp.arange(rows // tm_e, dtype=jnp.int32) * tm_e, side="right"),
                              MOE_EXPERTS - 1).astype(jnp.int32)
    n_tiles = (pend[-1:] // tm_e).astype(jnp.int32)
    row_expert = jnp.repeat(tile_expert, tm_e)
    off = jnp.arange(rows, dtype=jnp.int32) - pstart[row_expert]
    src = jnp.where(off < counts[row_expert], order[jnp.minimum(start[row_expert] + off, n - 1)], 0)
    x_rows = xn[src // 2]
    out_rows = _moe_grouped(x_rows, tile_expert, n_tiles, prm["wg"], prm["wu"], prm["wd"], tm_e, "moe_experts_" + tag)
    pos = (pstart[ids] + rank - start[ids]).reshape(t, 2)
    return y + gates[:, 0:1] * out_rows[pos[:, 0]] + gates[:, 1:2] * out_rows[pos[:, 1]]


def _moe_params(w_group, b_group, w_expert, b_expert, w_gate, w_up, w_down):
    d = w_group.shape[0]
    pad = LANES - MOE_EXPERTS - MOE_GROUPS
    wr = jnp.concatenate([w_expert, w_group, jnp.zeros((d, pad), F32)], axis=1)
    br = jnp.concatenate([b_expert, b_group, jnp.zeros((pad,), F32)]).reshape(1, LANES)
    return {"wr": wr, "br": br, "wg": w_gate.astype(BF16), "wu": w_up.astype(BF16), "wd": w_down.astype(BF16),
            "wg_f32": w_gate, "wu_f32": w_up, "wd_f32": w_down}


def _ssd_params(w_in, conv_w, conv_b, dt_bias, a_log, d, norm, w_out):
    a = -jnp.exp(a_log)
    w_dt = w_in[:, SSD_INNER + SSD_CONV_DIM:]

    def grouped(x):
        x = x.reshape(x.shape[:-1] + (SSD_GROUPS, SSD_HPG))
        x = jnp.pad(x, [(0, 0)] * (x.ndim - 1) + [(0, LANES - SSD_HPG)])
        return x.reshape(x.shape[:-2] + (SSD_GROUPS * LANES,))

    w_z = w_in[:, :SSD_INNER]
    w_xbc = w_in[:, SSD_INNER:SSD_INNER + SSD_CONV_DIM]
    eye = jnp.eye(SSD_HEADS, dtype=F32)
    return {
        "w_z": w_z.astype(BF16), "w_xbc": w_xbc.astype(BF16), "w_dt_g": grouped(w_dt).astype(BF16),
        "w_z_f32": w_z, "w_xbc_f32": w_xbc, "w_dt_f32": w_dt, "w_out_f32": w_out,
        "expand_64": jnp.repeat(eye, SSD_HEADDIM, axis=1), "expand_128": jnp.repeat(eye, LANES, axis=1),
        "conv_w": conv_w, "conv_b": conv_b.reshape(1, SSD_CONV_DIM),
        "dt_bias_g": grouped(dt_bias).reshape(SSD_GROUPS, 1, LANES),
        "a_g": grouped(a).reshape(SSD_GROUPS, 1, LANES),
        "dt_bias_64": jnp.repeat(dt_bias, SSD_HEADDIM).reshape(1, SSD_INNER),
        "dt_bias_128": jnp.broadcast_to(dt_bias[:, None], (SSD_HEADS, LANES)),
        "a_128": jnp.broadcast_to(a[:, None], (SSD_HEADS, LANES)),
        "d_exp": jnp.repeat(d, SSD_HEADDIM).reshape(1, SSD_INNER),
        "norm": norm, "w_out": w_out.astype(BF16),
    }


_TM = 512


def kernel(x_prompt, x_sample, cache_k, cache_v, page_table, state_s5_re, state_s5_im, state_ssd, state_conv, norm_mix, norm_ffn, norm_final, even_w_in, even_w_out, diff_lam_q1, diff_lam_k1, diff_lam_q2, diff_lam_k2, diff_subln, s5_lam_re, s5_lam_im, s5_log_dt, s5_b_re, s5_b_im, s5_c_re, s5_c_im, s5_d, s5_w_glu, s5_b_glu, ssd_w_in, ssd_conv_w, ssd_conv_b, ssd_dt_bias, ssd_a_log, ssd_d, ssd_norm, ssd_w_out, moe_w_group, moe_b_group, moe_w_expert, moe_b_expert, moe_w_gate, moe_w_up, moe_w_down):
    bp, sp, d = x_prompt.shape
    bs = x_sample.shape[0]
    tp = bp * sp
    depth = norm_mix.shape[0]
    y_p = x_prompt.reshape(tp, d)
    y_s = x_sample.reshape(bs, d)
    outs = {n: [] for n in ("k_p", "v_p", "k_s", "v_s", "re_p", "im_p", "re_s", "im_s", "ssd_p", "ssd_s", "cv_p", "cv_s")}

    for li in range(depth):
        if li % 2 == 0:
            e = li // 2
            lam_init = 0.8 - 0.6 * math.exp(-0.3 * li)
            lam = (jnp.exp(jnp.sum(diff_lam_q1[e] * diff_lam_k1[e])) - jnp.exp(jnp.sum(diff_lam_q2[e] * diff_lam_k2[e]))
                   + lam_init).astype(F32)
            ws32 = [even_w_in[e][:, i * ATT_WIDTH:(i + 1) * ATT_WIDTH] for i in range(4)]
            ws = [w.astype(BF16) for w in ws32]
            w_out32 = [even_w_out[e][:ATT_WIDTH], even_w_out[e][ATT_WIDTH:]]
            w_out_a, w_out_s = [w.astype(BF16) for w in w_out32]
            s5p = _s5_params(s5_lam_re[e], s5_lam_im[e], s5_log_dt[e], s5_b_re[e], s5_b_im[e], s5_c_re[e], s5_c_im[e],
                             s5_d[e], s5_w_glu[e], s5_b_glu[e])
            dts = [(BF16,), (F32, BF16), (F32, BF16), (F32,)]
            q, k, k16, v, v16, u = _norm_matmul(y_p, norm_mix[li], ws, dts, _TM, "even_in_p")
            o = _attn_prompt(q.reshape(bp, sp, ATT_WIDTH), k16.reshape(bp, sp, ATT_WIDTH),
                             v16.reshape(bp, sp, ATT_WIDTH), lam, diff_subln[e], 1.0 - lam_init)
            s5o, h_re, h_im = _s5_prompt(u.reshape(bp, sp, S5_WIDTH), s5p)
            y_p = _matmul_res(y_p, [o.reshape(tp, ATT_WIDTH), s5o.reshape(tp, S5_WIDTH)], [w_out_a, w_out_s], _TM,
                              "even_out_p")
            outs["k_p"].append(k.reshape(bp, sp, ATT_HEADS, HEAD_W))
            outs["v_p"].append(v.reshape(bp, sp, ATT_HEADS, HEAD_W))
            outs["re_p"].append(h_re.reshape(bp, S5_GROUPS, S5_STATE))
            outs["im_p"].append(h_im.reshape(bp, S5_GROUPS, S5_STATE))
            q, k, v, u = _norm_matmul(y_s, norm_mix[li], ws32, [(F32,), (F32,), (F32,), (F32,)], bs, "even_in_s", True)
            o = _attn_decode(q, k, v, cache_k, cache_v, page_table + e * cache_k.shape[1], lam, diff_subln[e],
                             1.0 - lam_init)
            s5o, h_re, h_im = _s5_step(u, state_s5_re[e].reshape(bs, S5_LANES), state_s5_im[e].reshape(bs, S5_LANES), s5p)
            y_s = _matmul_res(y_s, [o, s5o], w_out32, bs, "even_out_s", True)
            outs["k_s"].append(k.reshape(bs, 1, ATT_HEADS, HEAD_W))
            outs["v_s"].append(v.reshape(bs, 1, ATT_HEADS, HEAD_W))
            outs["re_s"].append(h_re.reshape(bs, S5_GROUPS, S5_STATE))
            outs["im_s"].append(h_im.reshape(bs, S5_GROUPS, S5_STATE))
        else:
            o_ = li // 2
            sp_ = _ssd_params(ssd_w_in[o_], ssd_conv_w[o_], ssd_conv_b[o_], ssd_dt_bias[o_], ssd_a_log[o_], ssd_d[o_],
                              ssd_norm[o_], ssd_w_out[o_])
            z, xbc, dtg = _norm_matmul(y_p, norm_mix[li], [sp_["w_z"], sp_["w_xbc"], sp_["w_dt_g"]],
                                       [(F32,), (F32,), (F32,)], 256, "ssd_in_p")
            xbc3 = xbc.reshape(bp, sp, SSD_CONV_DIM)
            yssd, st = _ssd_prompt(xbc3, dtg.reshape(bp, sp, SSD_GROUPS * LANES), sp_)
            y_p = _gated_norm_matmul(y_p, yssd.reshape(tp, SSD_INNER), z, sp_["norm"], sp_["w_out"], 256, "ssd_out_p")
            outs["ssd_p"].append(st)
            outs["cv_p"].append(xbc3[:, sp - (SSD_CONV - 1):, :])
            z, xbc, dtc = _norm_matmul(y_s, norm_mix[li], [sp_["w_z_f32"], sp_["w_xbc_f32"], sp_["w_dt_f32"]],
                                       [(F32,), (F32,), (F32,)], bs, "ssd_in_s", True)
            dt64, dt128 = _dt_expand(dtc, sp_)
            yssd, st = _ssd_step(xbc, state_conv[o_], dt64, dt128, state_ssd[o_], sp_)
            y_s = _gated_norm_matmul(y_s, yssd, z, sp_["norm"], sp_["w_out_f32"], bs, "ssd_out_s", True)
            outs["ssd_s"].append(st)
            outs["cv_s"].append(jnp.concatenate([state_conv[o_][:, 1:], xbc[:, None, :]], axis=1))
        mp = _moe_params(moe_w_group[li], moe_b_group[li], moe_w_expert[li], moe_b_expert[li], moe_w_gate[li],
                         moe_w_up[li], moe_w_down[li])
        y_p = _moe_sparse(y_p, norm_ffn[li], mp, _TM, 256, "p")
        y_s = _moe(y_s, norm_ffn[li], mp, bs, bs, "s", li + 1 < depth)

    y_prompt = _rmsnorm(y_p, norm_final, _TM, "final_p").reshape(bp, sp, d)
    y_sample = _rmsnorm(y_s, norm_final, bs, "final_s").reshape(bs, 1, d)
    st = lambda n: jnp.stack(outs[n])
    return (y_prompt, y_sample, st("k_p"), st("v_p"), st("k_s"), st("v_s"), st("re_p"), st("im_p"), st("re_s"),
            st("im_s"), st("ssd_p"), st("ssd_s"), st("cv_p"), st("cv_s"))
```

```python
import functools
import math

import jax
import jax.numpy as jnp
from jax import lax
from jax.experimental import pallas as pl
from jax.experimental.pallas import tpu as pltpu

F32 = jnp.float32
BF16 = jnp.bfloat16
HIGHEST = lax.Precision.HIGHEST

D_MODEL = 1024
NORM_EPS = 1e-6
PAGE_SIZE = 128
ATT_HEADS = 4
ATT_D = 64
ATT_WIDTH = ATT_HEADS * 2 * ATT_D
HEAD_W = 2 * ATT_D
S5_WIDTH = 512
S5_GROUP = 16
S5_GROUPS = 32
S5_STATE = 64
S5_LANES = S5_GROUPS * S5_STATE
SSD_INNER = 2048
SSD_HEADDIM = 64
SSD_HEADS = 32
SSD_GROUPS = 4
SSD_HPG = 8
SSD_STATE = 128
SSD_CONV = 4
SSD_CONV_DIM = SSD_INNER + 2 * SSD_GROUPS * SSD_STATE
SSD_CHUNK = 128
SSD_GW = SSD_HPG * SSD_HEADDIM
MOE_GROUPS = 4
MOE_EPG = 8
MOE_EXPERTS = 32
MOE_FF = 256
LANES = 128
NEG = -1e30

_NT = (((1,), (1,)), ((), ()))
_TN = (((0,), (0,)), ((), ()))


def _rms(x, g):
    return x * lax.rsqrt(jnp.mean(x * x, axis=-1, keepdims=True) + NORM_EPS) * g


def _split(x):
    hi = x.astype(BF16)
    return hi, (x - hi.astype(F32)).astype(BF16)


def _mm(a, b, precise=False, dims=None):
    dot = jnp.dot if dims is None else functools.partial(lax.dot_general, dimension_numbers=dims)
    if not precise:
        return dot(a.astype(BF16), b.astype(BF16), preferred_element_type=F32)
    m = a.shape[0]
    a_hi, a_lo = _split(a)
    b_hi, b_lo = _split(b)
    r = dot(jnp.concatenate([a_hi, a_lo], axis=0), b_hi, preferred_element_type=F32)
    return r[:m] + r[m:] + dot(a_hi, b_lo, preferred_element_type=F32)


def _norm_matmul_kernel(x_ref, g_ref, *refs, out_dtypes, precise):
    n = len(out_dtypes)
    w_refs = refs[:n]
    o_refs = refs[n:]
    h = _rms(x_ref[...], g_ref[...])
    if not precise:
        h = h.astype(BF16)
    k = 0
    for w_ref, dts in zip(w_refs, out_dtypes):
        r = _mm(h, w_ref[...], precise)
        for dt in dts:
            o_refs[k][...] = r.astype(dt)
            k += 1


def _norm_matmul(x, g, ws, out_dtypes, tm, name, precise=False):
    m, kdim = x.shape
    out_shape, out_specs = [], []
    for w, dts in zip(ws, out_dtypes):
        for dt in dts:
            out_shape.append(jax.ShapeDtypeStruct((m, w.shape[1]), dt))
            out_specs.append(pl.BlockSpec((tm, w.shape[1]), lambda i: (i, 0)))
    return pl.pallas_call(
        functools.partial(_norm_matmul_kernel, out_dtypes=out_dtypes, precise=precise),
        grid=(m // tm,),
        in_specs=[pl.BlockSpec((tm, kdim), lambda i: (i, 0)), pl.BlockSpec((1, kdim), lambda i: (0, 0))]
        + [pl.BlockSpec(w.shape, lambda i: (0, 0)) for w in ws],
        out_specs=out_specs,
        out_shape=out_shape,
        compiler_params=pltpu.CompilerParams(dimension_semantics=("parallel",)),
        name=name,
    )(x, g.reshape(1, kdim), *ws)


def _matmul_res_kernel(res_ref, *refs, precise):
    n = (len(refs) - 1) // 2
    acc = res_ref[...]
    for a_ref, w_ref in zip(refs[:n], refs[n:2 * n]):
        acc = acc + _mm(a_ref[...], w_ref[...], precise)
    refs[-1][...] = acc


def _matmul_res(res, a_list, w_list, tm, name, precise=False):
    m, n = res.shape
    return pl.pallas_call(
        functools.partial(_matmul_res_kernel, precise=precise),
        grid=(m // tm,),
        in_specs=[pl.BlockSpec((tm, n), lambda i: (i, 0))]
        + [pl.BlockSpec((tm, a.shape[1]), lambda i: (i, 0)) for a in a_list]
        + [pl.BlockSpec(w.shape, lambda i: (0, 0)) for w in w_list],
        out_specs=pl.BlockSpec((tm, n), lambda i: (i, 0)),
        out_shape=jax.ShapeDtypeStruct((m, n), F32),
        compiler_params=pltpu.CompilerParams(dimension_semantics=("parallel",)),
        name=name,
    )(res, *a_list, *w_list)


def _gated_norm_matmul_kernel(res_ref, y_ref, z_ref, g_ref, w_ref, o_ref, *, precise):
    y = y_ref[...] * jax.nn.silu(z_ref[...])
    o_ref[...] = res_ref[...] + _mm(_rms(y, g_ref[...]), w_ref[...], precise)


def _gated_norm_matmul(res, y, z, g, w, tm, name, precise=False):
    m, n = res.shape
    kdim = y.shape[1]
    return pl.pallas_call(
        functools.partial(_gated_norm_matmul_kernel, precise=precise),
        grid=(m // tm,),
        in_specs=[pl.BlockSpec((tm, n), lambda i: (i, 0)), pl.BlockSpec((tm, kdim), lambda i: (i, 0)),
                  pl.BlockSpec((tm, kdim), lambda i: (i, 0)), pl.BlockSpec((1, kdim), lambda i: (0, 0)),
                  pl.BlockSpec(w.shape, lambda i: (0, 0))],
        out_specs=pl.BlockSpec((tm, n), lambda i: (i, 0)),
        out_shape=jax.ShapeDtypeStruct((m, n), F32),
        compiler_params=pltpu.CompilerParams(dimension_semantics=("parallel",)),
        name=name,
    )(res, y, z, g.reshape(1, kdim), w)


def _rmsnorm_kernel(x_ref, g_ref, o_ref):
    o_ref[...] = _rms(x_ref[...], g_ref[...])


def _rmsnorm(x, g, tm, name):
    m, n = x.shape
    return pl.pallas_call(
        _rmsnorm_kernel,
        grid=(m // tm,),
        in_specs=[pl.BlockSpec((tm, n), lambda i: (i, 0)), pl.BlockSpec((1, n), lambda i: (0, 0))],
        out_specs=pl.BlockSpec((tm, n), lambda i: (i, 0)),
        out_shape=jax.ShapeDtypeStruct((m, n), F32),
        compiler_params=pltpu.CompilerParams(dimension_semantics=("parallel",)),
        name=name,
    )(x, g.reshape(1, n))


def _split_q(q):
    lane = lax.broadcasted_iota(jnp.int32, q.shape, 1) % HEAD_W
    scale = ATT_D ** -0.5
    qs = q * jnp.asarray(scale, q.dtype)
    zero = jnp.zeros_like(qs)
    return jnp.where(lane < ATT_D, qs, zero), jnp.where(lane >= ATT_D, qs, zero)


def _subln(o, g, out_scale):
    return _rms(o, g) * out_scale


def _attn_prompt_kernel(lam_ref, q_ref, k_ref, vt_ref, g_ref, o_ref, qt_sc, m_sc, l_sc, acc_sc, *, tq, tk, out_scale):
    qi = pl.program_id(1)
    heads = range(ATT_HEADS)
    for h in heads:
        q1, q2 = _split_q(q_ref[0, :, h * HEAD_W:(h + 1) * HEAD_W])
        qt_sc[h] = jnp.concatenate([q1, q2], axis=0).astype(F32).T.astype(BF16)
    m_sc[...] = jnp.full(m_sc.shape, NEG, F32)
    l_sc[...] = jnp.zeros(l_sc.shape, F32)
    acc_sc[...] = jnp.zeros(acc_sc.shape, F32)

    def kv_step(j, masked):
        start = pl.multiple_of(j * tk, tk)
        for h in heads:
            kt = k_ref[0, pl.ds(start, tk), h * HEAD_W:(h + 1) * HEAD_W]
            st = jnp.dot(kt, qt_sc[h], preferred_element_type=F32)
            if masked:
                k_pos = start + lax.broadcasted_iota(jnp.int32, st.shape, 0)
                q_pos = qi * tq + lax.broadcasted_iota(jnp.int32, st.shape, 1) % tq
                st = jnp.where(k_pos <= q_pos, st, NEG)
            m_old = m_sc[h]
            m_new = jnp.maximum(m_old, jnp.max(st, axis=0, keepdims=True))
            p = jnp.exp(st - m_new)
            alpha = jnp.exp(m_old - m_new)
            l_sc[h] = alpha * l_sc[h] + jnp.sum(p, axis=0, keepdims=True)
            m_sc[h] = m_new
            acc_sc[h] = alpha * acc_sc[h] + jnp.dot(vt_ref[0, h, j], p.astype(BF16), preferred_element_type=F32)

    n_full = (qi * tq) // tk

    def full_step(j, carry):
        kv_step(j, False)
        return carry

    lax.fori_loop(0, n_full, full_step, 0)
    kv_step(n_full, True)
    for h in heads:
        acc, l = acc_sc[h], l_sc[h]
        ot = acc[:, 0:tq] / l[:, 0:tq] - lam_ref[0, 0] * (acc[:, tq:2 * tq] / l[:, tq:2 * tq])
        o_ref[0, :, h * HEAD_W:(h + 1) * HEAD_W] = _subln(ot.T, g_ref[...], out_scale).astype(o_ref.dtype)


def _attn_prompt(q, k, v, lam, subln, out_scale, tq=256, tk=512):
    b, s, _ = q.shape
    nblk = s // tk
    vt = jnp.transpose(v.reshape(b, nblk, tk, ATT_HEADS, HEAD_W), (0, 3, 1, 4, 2))
    kern = functools.partial(_attn_prompt_kernel, tq=tq, tk=tk, out_scale=out_scale)
    return pl.pallas_call(
        kern,
        grid=(b, s // tq),
        in_specs=[pl.BlockSpec(memory_space=pltpu.SMEM),
                  pl.BlockSpec((1, tq, ATT_WIDTH), lambda bi, i: (bi, i, 0)),
                  pl.BlockSpec((1, s, ATT_WIDTH), lambda bi, i: (bi, 0, 0)),
                  pl.BlockSpec((1, ATT_HEADS, nblk, HEAD_W, tk), lambda bi, i: (bi, 0, 0, 0, 0)),
                  pl.BlockSpec((1, HEAD_W), lambda bi, i: (0, 0))],
        out_specs=pl.BlockSpec((1, tq, ATT_WIDTH), lambda bi, i: (bi, i, 0)),
        out_shape=jax.ShapeDtypeStruct((b, s, ATT_WIDTH), BF16),
        scratch_shapes=[pltpu.VMEM((ATT_HEADS, HEAD_W, 2 * tq), BF16), pltpu.VMEM((ATT_HEADS, 1, 2 * tq), F32),
                        pltpu.VMEM((ATT_HEADS, 1, 2 * tq), F32), pltpu.VMEM((ATT_HEADS, HEAD_W, 2 * tq), F32)],
        compiler_params=pltpu.CompilerParams(dimension_semantics=("parallel", "parallel")),
        name="attn_prompt",
    )(lam.reshape(1, 1), q, k, vt, subln.reshape(1, HEAD_W))


_DEC_PAGES = 16
_DEC_PROW = PAGE_SIZE * ATT_HEADS
_DEC_COLS = _DEC_PAGES * _DEC_PROW
_DEC_ROWS = 2 * ATT_HEADS


def _attn_decode_kernel(pt_ref, lam_ref, q_ref, kn_ref, vn_ref, g_ref, *refs, steps, out_scale):
    del pt_ref
    k_refs = refs[:_DEC_PAGES]
    v_refs = refs[_DEC_PAGES:2 * _DEC_PAGES]
    o_ref = refs[2 * _DEC_PAGES]
    s_sc, m_sc, wn_sc, acc_sc = refs[2 * _DEC_PAGES + 1:]
    j = pl.program_id(1)
    lam = lam_ref[0, 0]

    q4 = q_ref[0] * (ATT_D ** -0.5)
    q8 = jnp.concatenate([q4, q4], axis=0)
    r8 = lax.broadcasted_iota(jnp.int32, (_DEC_ROWS, HEAD_W), 0)
    l8 = lax.broadcasted_iota(jnp.int32, (_DEC_ROWS, HEAD_W), 1)
    q8 = jnp.where(l8 // ATT_D == r8 // ATT_HEADS, q8, 0.0)

    @pl.when(j == 0)
    def _():
        m_sc[...] = jnp.full(m_sc.shape, NEG, F32)

    @pl.when(j < steps)
    def _():
        s = jnp.concatenate([_mm(q8, k_ref[0], True, _NT) for k_ref in k_refs], axis=1)
        rs = lax.broadcasted_iota(jnp.int32, s.shape, 0)
        cs = lax.broadcasted_iota(jnp.int32, s.shape, 1)
        s = jnp.where(cs % ATT_HEADS == rs % ATT_HEADS, s, NEG)
        s_sc[j] = s
        m_sc[...] = jnp.maximum(m_sc[...], jnp.max(s, axis=-1, keepdims=True))

    @pl.when(j == steps - 1)
    def _():
        kn = jnp.concatenate([kn_ref[0], kn_ref[0]], axis=0)
        sn = jnp.sum(q8 * kn, axis=-1, keepdims=True)
        m = jnp.maximum(m_sc[...], sn)
        pn = jnp.exp(sn - m)
        l = pn
        for t in range(steps):
            p = jnp.exp(s_sc[t] - m)
            s_sc[t] = p
            l = l + jnp.sum(p, axis=-1, keepdims=True)
        zeros = jnp.zeros((ATT_HEADS, _DEC_COLS), F32)
        for t in range(steps):
            p = s_sc[t] / l
            s_sc[t] = jnp.concatenate([p[:ATT_HEADS] - lam * p[ATT_HEADS:], zeros], axis=0)
        pn = pn / l
        wn_sc[...] = jnp.concatenate([pn[:ATT_HEADS] - lam * pn[ATT_HEADS:], jnp.zeros((ATT_HEADS, 1), F32)], axis=0)

    @pl.when(j == steps)
    def _():
        acc_sc[...] = jnp.zeros(acc_sc.shape, F32)

    @pl.when(j >= steps)
    def _():
        w = s_sc[j - steps]
        acc = acc_sc[...]
        for i, v_ref in enumerate(v_refs):
            acc = acc + _mm(w[:, i * _DEC_PROW:(i + 1) * _DEC_PROW], v_ref[0], True)
        acc_sc[...] = acc

    @pl.when(j == 2 * steps - 1)
    def _():
        o4 = acc_sc[:ATT_HEADS] + wn_sc[:ATT_HEADS] * vn_ref[0]
        o_ref[0] = _subln(o4, g_ref[...], out_scale)


def _attn_decode(q, k_new, v_new, cache_k, cache_v, page_table, lam, subln, out_scale):
    b = q.shape[0]
    n_pages = page_table.shape[1]
    steps = n_pages // _DEC_PAGES
    ck = cache_k.reshape(-1, _DEC_PROW, HEAD_W)
    cv = cache_v.reshape(-1, _DEC_PROW, HEAD_W)

    def k_map(i):
        return lambda bi, j, pt: (pt[bi, jnp.minimum(j, steps - 1) * _DEC_PAGES + i], 0, 0)

    def v_map(i):
        return lambda bi, j, pt: (pt[bi, jnp.maximum(j - steps, 0) * _DEC_PAGES + i], 0, 0)

    row = lambda bi, j, pt: (bi, 0, 0)
    head_rows = pl.BlockSpec((1, ATT_HEADS, HEAD_W), row)
    grid_spec = pltpu.PrefetchScalarGridSpec(
        num_scalar_prefetch=1,
        grid=(b, 2 * steps),
        in_specs=[pl.BlockSpec(memory_space=pltpu.SMEM), head_rows, head_rows, head_rows,
                  pl.BlockSpec((1, HEAD_W), lambda bi, j, pt: (0, 0))]
        + [pl.BlockSpec((1, _DEC_PROW, HEAD_W), k_map(i)) for i in range(_DEC_PAGES)]
        + [pl.BlockSpec((1, _DEC_PROW, HEAD_W), v_map(i)) for i in range(_DEC_PAGES)],
        out_specs=head_rows,
        scratch_shapes=[pltpu.VMEM((steps, _DEC_ROWS, _DEC_COLS), F32), pltpu.VMEM((_DEC_ROWS, 1), F32),
                        pltpu.VMEM((_DEC_ROWS, 1), F32), pltpu.VMEM((_DEC_ROWS, HEAD_W), F32)],
    )
    out = pl.pallas_call(
        functools.partial(_attn_decode_kernel, steps=steps, out_scale=out_scale),
        grid_spec=grid_spec,
        out_shape=jax.ShapeDtypeStruct((b, ATT_HEADS, HEAD_W), F32),
        compiler_params=pltpu.CompilerParams(dimension_semantics=("parallel", "arbitrary")),
        name="attn_decode",
    )(page_table, lam.reshape(1, 1), q.reshape(b, ATT_HEADS, HEAD_W), k_new.reshape(b, ATT_HEADS, HEAD_W),
      v_new.reshape(b, ATT_HEADS, HEAD_W), subln.reshape(1, HEAD_W), *([ck] * _DEC_PAGES), *([cv] * _DEC_PAGES))
    return out.reshape(b, ATT_WIDTH)


_S5_KT = 2
_S5_KW = S5_WIDTH // _S5_KT
_S5_KL = S5_LANES // _S5_KT
_S5_NT = S5_LANES // LANES
_S5_SCAN_TILES = 8


def _s5_input(u, bmat_ref, precise):
    re, im = [], []
    for kt in range(_S5_KT):
        r = _mm(u[:, kt * _S5_KW:(kt + 1) * _S5_KW], bmat_ref[kt], precise)
        re.append(r[:, :_S5_KL])
        im.append(r[:, _S5_KL:])
    return jnp.concatenate(re, axis=1), jnp.concatenate(im, axis=1)


def _s5_output(h_re, h_im, u, cmat_ref, d_ref, wglu_ref, bglu_ref, precise):
    if not precise:
        h_re, h_im = h_re.astype(BF16), h_im.astype(BF16)
    ys = []
    for kt in range(_S5_KT):
        sl = slice(kt * _S5_KL, (kt + 1) * _S5_KL)
        y = _mm(h_re[:, sl], cmat_ref[kt, :_S5_KL, :], precise)
        y = y + _mm(h_im[:, sl], cmat_ref[kt, _S5_KL:, :], precise)
        ys.append(y)
    y = jnp.concatenate(ys, axis=1) + d_ref[...] * u
    g = jax.nn.gelu(y)
    gate = _mm(g, wglu_ref[...], precise) + bglu_ref[...]
    return g * jax.nn.sigmoid(gate)


def _s5_scan_kernel(u_ref, are_ref, aim_ref, bmat_ref, cmat_ref, d_ref, wglu_ref, bglu_ref,
                    o_ref, hre_ref, him_ref, bu_re, bu_im, *, nb, lc, bp):
    c = pl.program_id(0)

    @pl.when(c == 0)
    def _():
        bu_re[...] = jnp.zeros(bu_re.shape, F32)
        bu_im[...] = jnp.zeros(bu_im.shape, F32)
        hre_ref[...] = jnp.zeros(hre_ref.shape, F32)
        him_ref[...] = jnp.zeros(him_ref.shape, F32)

    for b in range(nb):
        re, im = _s5_input(u_ref[b], bmat_ref, False)
        for jt in range(_S5_NT):
            bu_re[jt, pl.ds(b, lc, stride=bp), :] = re[:, jt * LANES:(jt + 1) * LANES]
            bu_im[jt, pl.ds(b, lc, stride=bp), :] = im[:, jt * LANES:(jt + 1) * LANES]

    for j0 in range(0, _S5_NT, _S5_SCAN_TILES):
        tiles = pl.ds(j0, _S5_SCAN_TILES)
        a_re = jnp.broadcast_to(are_ref[tiles], (_S5_SCAN_TILES, bp, LANES))
        a_im = jnp.broadcast_to(aim_ref[tiles], (_S5_SCAN_TILES, bp, LANES))

        def step(t, carry, tiles=tiles, a_re=a_re, a_im=a_im):
            hr, hi = carry
            rows = pl.ds(pl.multiple_of(t * bp, bp), bp)
            nr = hr * a_re - hi * a_im + bu_re[tiles, rows, :]
            ni = hr * a_im + hi * a_re + bu_im[tiles, rows, :]
            bu_re[tiles, rows, :] = nr
            bu_im[tiles, rows, :] = ni
            return nr, ni

        hr, hi = lax.fori_loop(0, lc, step, (hre_ref[tiles], him_ref[tiles]), unroll=2)
        hre_ref[tiles] = hr
        him_ref[tiles] = hi

    for b in range(nb):
        h_re = jnp.concatenate([bu_re[jt, pl.ds(b, lc, stride=bp), :] for jt in range(_S5_NT)], axis=1)
        h_im = jnp.concatenate([bu_im[jt, pl.ds(b, lc, stride=bp), :] for jt in range(_S5_NT)], axis=1)
        o_ref[b] = _s5_output(h_re, h_im, u_ref[b], cmat_ref, d_ref, wglu_ref, bglu_ref, False).astype(o_ref.dtype)


def _s5_prompt(u, prm, lc=128):
    nb, l, _ = u.shape
    bp = 8
    full = lambda shape: pl.BlockSpec(shape, lambda c: (0,) * len(shape))
    o, h_re, h_im = pl.pallas_call(
        functools.partial(_s5_scan_kernel, nb=nb, lc=lc, bp=bp),
        grid=(l // lc,),
        in_specs=[pl.BlockSpec((nb, lc, S5_WIDTH), lambda c: (0, c, 0)),
                  full((_S5_NT, 1, LANES)), full((_S5_NT, 1, LANES)),
                  full((_S5_KT, _S5_KW, 2 * _S5_KL)), full((_S5_KT, 2 * _S5_KL, _S5_KW)),
                  full((1, S5_WIDTH)), full((S5_WIDTH, S5_WIDTH)), full((1, S5_WIDTH))],
        out_specs=[pl.BlockSpec((nb, lc, S5_WIDTH), lambda c: (0, c, 0)),
                   full((_S5_NT, bp, LANES)), full((_S5_NT, bp, LANES))],
        out_shape=[jax.ShapeDtypeStruct((nb, l, S5_WIDTH), BF16),
                   jax.ShapeDtypeStruct((_S5_NT, bp, LANES), F32), jax.ShapeDtypeStruct((_S5_NT, bp, LANES), F32)],
        scratch_shapes=[pltpu.VMEM((_S5_NT, lc * bp, LANES), F32) for _ in range(2)],
        compiler_params=pltpu.CompilerParams(dimension_semantics=("arbitrary",)),
        name="s5_prompt",
    )(u, prm["a_re"].reshape(_S5_NT, 1, LANES), prm["a_im"].reshape(_S5_NT, 1, LANES), prm["bmat"].astype(BF16),
      prm["cmat"].astype(BF16), prm["d"], prm["w_glu"].astype(BF16), prm["b_glu"])
    rows = lambda h: jnp.transpose(h, (1, 0, 2)).reshape(bp, S5_LANES)[:nb]
    return o, rows(h_re), rows(h_im)


def _s5_step_kernel(u_ref, h0re_ref, h0im_ref, are_ref, aim_ref, bmat_ref, cmat_ref, d_ref, wglu_ref, bglu_ref,
                    o_ref, hre_ref, him_ref):
    u = u_ref[...]
    bu_re, bu_im = _s5_input(u, bmat_ref, True)
    h_re, h_im = h0re_ref[...], h0im_ref[...]
    a_re, a_im = are_ref[...], aim_ref[...]
    n_re = h_re * a_re - h_im * a_im + bu_re
    n_im = h_re * a_im + h_im * a_re + bu_im
    hre_ref[...] = n_re
    him_ref[...] = n_im
    o_ref[...] = _s5_output(n_re, n_im, u, cmat_ref, d_ref, wglu_ref, bglu_ref, True)


def _s5_step(u, h0_re, h0_im, prm):
    nb = u.shape[0]
    return pl.pallas_call(
        _s5_step_kernel,
        out_shape=[jax.ShapeDtypeStruct((nb, S5_WIDTH), F32),
                   jax.ShapeDtypeStruct((nb, S5_LANES), F32), jax.ShapeDtypeStruct((nb, S5_LANES), F32)],
        name="s5_step",
    )(u, h0_re, h0_im, prm["a_re"], prm["a_im"], prm["bmat"], prm["cmat"], prm["d"], prm["w_glu"], prm["b_glu"])


def _s5_params(lam_re, lam_im, log_dt, b_re, b_im, c_re, c_im, d, w_glu, b_glu):
    lam = lax.complex(lam_re, lam_im)
    dt = jnp.exp(log_dt)[:, None]
    a_bar = jnp.exp(lam * dt)
    b_bar = ((a_bar - 1.0) / lam)[..., None] * lax.complex(b_re, b_im)
    gk = S5_GROUPS // _S5_KT
    eye = jnp.eye(gk, dtype=F32)

    def in_tile(x):
        x = x.reshape(_S5_KT, gk, S5_STATE, S5_GROUP)
        return jnp.einsum("kgpc,gh->kgchp", x, eye).reshape(_S5_KT, _S5_KW, _S5_KL)

    def out_tile(x):
        x = x.reshape(_S5_KT, gk, S5_GROUP, S5_STATE)
        return jnp.einsum("kgcp,gh->kgphc", x, eye).reshape(_S5_KT, _S5_KL, _S5_KW)

    bmat = jnp.concatenate([in_tile(jnp.real(b_bar)), in_tile(jnp.imag(b_bar))], axis=2)
    cmat = jnp.concatenate([out_tile(c_re), out_tile(-c_im)], axis=1)
    return {
        "a_re": jnp.real(a_bar).reshape(1, S5_LANES), "a_im": jnp.imag(a_bar).reshape(1, S5_LANES),
        "bmat": bmat, "cmat": cmat, "d": d.reshape(1, S5_WIDTH), "w_glu": w_glu, "b_glu": b_glu.reshape(1, S5_WIDTH),
    }


_CONV_PAD = 8


def _conv_silu(x, xf, w_ref, b_ref, rows):
    xf[_CONV_PAD:_CONV_PAD + rows, :] = x
    w = w_ref[...]
    out = b_ref[...]
    for i in range(SSD_CONV):
        lo = _CONV_PAD - (SSD_CONV - 1) + i
        out = out + w[i:i + 1] * xf[lo:lo + rows, :]
    xf[_CONV_PAD - (SSD_CONV - 1):_CONV_PAD, :] = xf[_CONV_PAD + rows - (SSD_CONV - 1):_CONV_PAD + rows, :]
    return jax.nn.silu(out)


def _ssd_chunk_kernel(xs_ref, bm_ref, cm_ref, dt_ref, cwx_ref, cwb_ref, cwc_ref, cbx_ref, cbb_ref, cbc_ref,
                      dtb_ref, a_ref, dexp_ref, y_ref, st_ref, xf_x, xf_b, xf_c):
    c = pl.program_id(2)
    q = SSD_CHUNK

    @pl.when(c == 0)
    def _():
        xf_x[0:_CONV_PAD, :] = jnp.zeros((_CONV_PAD, xf_x.shape[1]), F32)
        xf_b[0:_CONV_PAD, :] = jnp.zeros((_CONV_PAD, xf_b.shape[1]), F32)
        xf_c[0:_CONV_PAD, :] = jnp.zeros((_CONV_PAD, xf_c.shape[1]), F32)
        st_ref[...] = jnp.zeros(st_ref.shape, F32)

    xs = _conv_silu(xs_ref[0], xf_x, cwx_ref, cbx_ref, q)
    bm = _conv_silu(bm_ref[0], xf_b, cwb_ref, cbb_ref, q)
    cm = _conv_silu(cm_ref[0], xf_c, cwc_ref, cbc_ref, q)
    bm16 = bm.astype(BF16)
    cm16 = cm.astype(BF16)

    dt = jax.nn.softplus(dt_ref[0] + dtb_ref[0])
    da = dt * a_ref[0]
    row = lax.broadcasted_iota(jnp.int32, (q, q), 0)
    col = lax.broadcasted_iota(jnp.int32, (q, q), 1)
    causal = row >= col
    acs = jnp.dot(causal.astype(F32), da, preferred_element_type=F32, precision=HIGHEST)
    acs_t = acs.T
    acs_last = acs[q - 1:q, :]
    e_acs = jnp.exp(acs)
    dec_s = jnp.exp(acs_last - acs)
    chunk_dec = jnp.exp(acs_last)
    cb = _mm(cm16, bm16, dims=_NT)

    ys = []
    for j in range(SSD_HPG):
        seg = acs[:, j:j + 1] - acs_t[j:j + 1, :]
        lmat = jnp.exp(jnp.where(causal, seg, -jnp.inf))
        xdt = xs[:, j * SSD_HEADDIM:(j + 1) * SSD_HEADDIM] * dt[:, j:j + 1]
        y_diag = _mm(cb * lmat, xdt)
        h = st_ref[0, j]
        y_off = _mm(cm16, h, dims=_NT) * e_acs[:, j:j + 1]
        st_ref[0, j] = chunk_dec[:, j:j + 1] * h + _mm(xdt * dec_s[:, j:j + 1], bm16, dims=_TN)
        ys.append(y_diag + y_off)
    y_ref[0] = jnp.concatenate(ys, axis=1) + dexp_ref[...] * xs


def _ssd_prompt(xbc, dtg, prm):
    nb, l, _ = xbc.shape
    nc = l // SSD_CHUNK
    q = SSD_CHUNK
    boff = SSD_INNER // LANES
    coff = boff + SSD_GROUPS
    colx = lambda b, g, c: (0, g)
    colb = lambda b, g, c: (0, boff + g)
    colc = lambda b, g, c: (0, coff + g)
    grp = lambda b, g, c: (g, 0, 0)
    return pl.pallas_call(
        _ssd_chunk_kernel,
        grid=(nb, SSD_GROUPS, nc),
        in_specs=[pl.BlockSpec((1, q, SSD_GW), lambda b, g, c: (b, c, g)),
                  pl.BlockSpec((1, q, LANES), lambda b, g, c: (b, c, boff + g)),
                  pl.BlockSpec((1, q, LANES), lambda b, g, c: (b, c, coff + g)),
                  pl.BlockSpec((1, q, LANES), lambda b, g, c: (b, c, g)),
                  pl.BlockSpec((SSD_CONV, SSD_GW), colx), pl.BlockSpec((SSD_CONV, LANES), colb),
                  pl.BlockSpec((SSD_CONV, LANES), colc),
                  pl.BlockSpec((1, SSD_GW), colx), pl.BlockSpec((1, LANES), colb), pl.BlockSpec((1, LANES), colc),
                  pl.BlockSpec((1, 1, LANES), grp), pl.BlockSpec((1, 1, LANES), grp),
                  pl.BlockSpec((1, SSD_GW), colx)],
        out_specs=[pl.BlockSpec((1, q, SSD_GW), lambda b, g, c: (b, c, g)),
                   pl.BlockSpec((1, SSD_HPG, SSD_HEADDIM, SSD_STATE), lambda b, g, c: (b, g, 0, 0))],
        out_shape=[jax.ShapeDtypeStruct((nb, l, SSD_INNER), F32),
                   jax.ShapeDtypeStruct((nb, SSD_HEADS, SSD_HEADDIM, SSD_STATE), F32)],
        scratch_shapes=[pltpu.VMEM((_CONV_PAD + q, SSD_GW), F32), pltpu.VMEM((_CONV_PAD + q, LANES), F32),
                        pltpu.VMEM((_CONV_PAD + q, LANES), F32)],
        compiler_params=pltpu.CompilerParams(dimension_semantics=("parallel", "parallel", "arbitrary")),
        name="ssd_prompt",
    )(xbc, xbc, xbc, dtg, prm["conv_w"], prm["conv_w"], prm["conv_w"], prm["conv_b"], prm["conv_b"], prm["conv_b"],
      prm["dt_bias_g"], prm["a_g"], prm["d_exp"])


def _ssd_step_kernel(x_ref, buf_ref, cw_ref, cb_ref, dt64_ref, dtb64_ref, dt128_ref, dtb128_ref, a128_ref,
                     dexp_ref, st_ref, y_ref, so_ref):
    w = cw_ref[...]
    buf = buf_ref[0]
    conv = cb_ref[...]
    for i in range(SSD_CONV - 1):
        conv = conv + w[i:i + 1] * buf[i:i + 1]
    conv = conv + w[SSD_CONV - 1:SSD_CONV] * x_ref[0]
    xc = jax.nn.silu(conv)
    xs = xc[:, :SSD_INNER]
    xdt = xs * jax.nn.softplus(dt64_ref[0] + dtb64_ref[...])
    dt = jax.nn.softplus(dt128_ref[0] + dtb128_ref[...])
    dec = jnp.exp(dt * a128_ref[...])
    row = lax.broadcasted_iota(jnp.int32, (LANES, LANES), 0)
    col = lax.broadcasted_iota(jnp.int32, (LANES, LANES), 1)
    diag = row == col
    ys = []
    for i in range(SSD_HEADS // 2):
        g = (2 * i) // SSD_HPG
        bm = xc[:, SSD_INNER + g * SSD_STATE:SSD_INNER + (g + 1) * SSD_STATE]
        cm = xc[:, SSD_INNER + (SSD_GROUPS + g) * SSD_STATE:SSD_INNER + (SSD_GROUPS + g + 1) * SSD_STATE]
        xp = jnp.broadcast_to(xdt[:, i * LANES:(i + 1) * LANES], (LANES, LANES))
        outer = _mm(jnp.where(diag, xp, 0.0), jnp.broadcast_to(bm, (LANES, LANES)), True)
        h = st_ref[0, 2 * i:2 * i + 2].reshape(LANES, SSD_STATE)
        dpair = jnp.concatenate([jnp.broadcast_to(dec[2 * i:2 * i + 1], (SSD_HEADDIM, LANES)),
                                 jnp.broadcast_to(dec[2 * i + 1:2 * i + 2], (SSD_HEADDIM, LANES))], axis=0)
        hn = dpair * h + outer
        so_ref[0, 2 * i:2 * i + 2] = hn.reshape(2, SSD_HEADDIM, SSD_STATE)
        ys.append(_mm(jnp.broadcast_to(cm, (8, SSD_STATE)), hn, True, _NT)[0:1])
    y_ref[0] = jnp.concatenate(ys, axis=1) + dexp_ref[...] * xs


def _dt_expand_kernel(dt_ref, e64_ref, e128_ref, o64_ref, o128_ref):
    dt = dt_ref[...]
    o64_ref[...] = _mm(dt, e64_ref[...], True)
    o128_ref[...] = _mm(dt, e128_ref[...], True)


def _dt_expand(dt, prm):
    nb = dt.shape[0]
    return pl.pallas_call(
        _dt_expand_kernel,
        out_shape=[jax.ShapeDtypeStruct((nb, SSD_INNER), F32), jax.ShapeDtypeStruct((nb, SSD_HEADS * LANES), F32)],
        name="ssd_dt_expand",
    )(dt, prm["expand_64"], prm["expand_128"])


def _ssd_step(xbc, conv_buf, dt64, dt128, state, prm):
    nb = xbc.shape[0]
    full = lambda shape: pl.BlockSpec(shape, lambda b: (0,) * len(shape))
    y, so = pl.pallas_call(
        _ssd_step_kernel,
        grid=(nb,),
        in_specs=[pl.BlockSpec((1, 1, SSD_CONV_DIM), lambda b: (b, 0, 0)),
                  pl.BlockSpec((1, SSD_CONV - 1, SSD_CONV_DIM), lambda b: (b, 0, 0)),
                  full((SSD_CONV, SSD_CONV_DIM)), full((1, SSD_CONV_DIM)),
                  pl.BlockSpec((1, 1, SSD_INNER), lambda b: (b, 0, 0)), full((1, SSD_INNER)),
                  pl.BlockSpec((1, SSD_HEADS, LANES), lambda b: (b, 0, 0)), full((SSD_HEADS, LANES)),
                  full((SSD_HEADS, LANES)), full((1, SSD_INNER)),
                  pl.BlockSpec((1, SSD_HEADS, SSD_HEADDIM, SSD_STATE), lambda b: (b, 0, 0, 0))],
        out_specs=[pl.BlockSpec((1, 1, SSD_INNER), lambda b: (b, 0, 0)),
                   pl.BlockSpec((1, SSD_HEADS, SSD_HEADDIM, SSD_STATE), lambda b: (b, 0, 0, 0))],
        out_shape=[jax.ShapeDtypeStruct((nb, 1, SSD_INNER), F32),
                   jax.ShapeDtypeStruct((nb, SSD_HEADS, SSD_HEADDIM, SSD_STATE), F32)],
        compiler_params=pltpu.CompilerParams(dimension_semantics=("parallel",)),
        name="ssd_step",
    )(xbc.reshape(nb, 1, SSD_CONV_DIM), conv_buf, prm["conv_w"], prm["conv_b"],
      dt64.reshape(nb, 1, SSD_INNER), prm["dt_bias_64"], dt128.reshape(nb, SSD_HEADS, LANES),
      prm["dt_bias_128"], prm["a_128"], prm["d_exp"], state)
    return y.reshape(nb, SSD_INNER), so


_GROUP_LANE0 = MOE_EXPERTS
_ROUTE_LANE0 = 64


def _moe_router_kernel(y_ref, g_ref, wr_ref, br_ref, xn_ref, comb_ref):
    xn = _rms(y_ref[...], g_ref[...])
    xn_ref[...] = xn.astype(xn_ref.dtype)
    lg = _mm(xn, wr_ref[...], True) + br_ref[...]
    lane = lax.broadcasted_iota(jnp.int32, lg.shape, 1)
    is_group = (lane >= _GROUP_LANE0) & (lane < _GROUP_LANE0 + MOE_GROUPS)
    gl = jnp.where(is_group, lg, NEG)
    gmax = jnp.max(gl, axis=-1, keepdims=True)
    g_p = 1.0 / jnp.sum(jnp.exp(gl - gmax), axis=-1, keepdims=True)
    gidx = jnp.min(jnp.where(gl == gmax, lane - _GROUP_LANE0, MOE_GROUPS), axis=-1, keepdims=True)
    el = jnp.where((lane < MOE_EXPERTS) & (lane // MOE_EPG == gidx), lg, NEG)
    m1 = jnp.max(el, axis=-1, keepdims=True)
    i1 = jnp.min(jnp.where(el == m1, lane, LANES), axis=-1, keepdims=True)
    el2 = jnp.where(lane == i1, NEG, el)
    m2 = jnp.max(el2, axis=-1, keepdims=True)
    i2 = jnp.min(jnp.where(el2 == m2, lane, LANES), axis=-1, keepdims=True)
    e2 = jnp.exp(m2 - m1)
    den = 1.0 + e2
    g1 = g_p / den
    g2 = g_p * (e2 / den)
    comb = jnp.where(lane == i1, g1, 0.0) + jnp.where(lane == i2, g2, 0.0)
    for k, v in enumerate((i1.astype(F32), i2.astype(F32), g1, g2)):
        comb = jnp.where(lane == _ROUTE_LANE0 + k, v, comb)
    comb_ref[...] = comb


def _moe_router(y, g, wr, br, tm, name, xn_dtype):
    m, d = y.shape
    return pl.pallas_call(
        _moe_router_kernel,
        grid=(m // tm,),
        in_specs=[pl.BlockSpec((tm, d), lambda i: (i, 0)), pl.BlockSpec((1, d), lambda i: (0, 0)),
                  pl.BlockSpec((d, LANES), lambda i: (0, 0)), pl.BlockSpec((1, LANES), lambda i: (0, 0))],
        out_specs=[pl.BlockSpec((tm, d), lambda i: (i, 0)), pl.BlockSpec((tm, LANES), lambda i: (i, 0))],
        out_shape=[jax.ShapeDtypeStruct((m, d), xn_dtype), jax.ShapeDtypeStruct((m, LANES), F32)],
        compiler_params=pltpu.CompilerParams(dimension_semantics=("parallel",)),
        name=name,
    )(y, g.reshape(1, d), wr, br)


def _moe_dense_kernel(res_ref, xn_ref, comb_ref, wg_ref, wu_ref, wd_ref, o_ref, acc, *, precise):
    e = pl.program_id(1)

    @pl.when(e == 0)
    def _():
        acc[...] = jnp.zeros(acc.shape, F32)

    x = xn_ref[...]
    hdn = jax.nn.silu(_mm(x, wg_ref[0], precise)) * _mm(x, wu_ref[0], precise)
    comb = comb_ref[...]
    lane = lax.broadcasted_iota(jnp.int32, comb.shape, 1)
    gate = jnp.sum(jnp.where(lane == e, comb, 0.0), axis=-1, keepdims=True)
    acc[...] += gate * _mm(hdn, wd_ref[0], precise)

    @pl.when(e == pl.num_programs(1) - 1)
    def _():
        o_ref[...] = res_ref[...] + acc[...]


def _moe_dense(res, xn, comb, wg, wu, wd, tm, name, precise):
    m, d = res.shape
    return pl.pallas_call(
        functools.partial(_moe_dense_kernel, precise=precise),
        grid=(m // tm, MOE_EXPERTS),
        in_specs=[pl.BlockSpec((tm, d), lambda i, e: (i, 0)), pl.BlockSpec((tm, d), lambda i, e: (i, 0)),
                  pl.BlockSpec((tm, LANES), lambda i, e: (i, 0)),
                  pl.BlockSpec((1, d, MOE_FF), lambda i, e: (e, 0, 0)),
                  pl.BlockSpec((1, d, MOE_FF), lambda i, e: (e, 0, 0)),
                  pl.BlockSpec((1, MOE_FF, d), lambda i, e: (e, 0, 0))],
        out_specs=pl.BlockSpec((tm, d), lambda i, e: (i, 0)),
        out_shape=jax.ShapeDtypeStruct((m, d), F32),
        scratch_shapes=[pltpu.VMEM((tm, d), F32)],
        compiler_params=pltpu.CompilerParams(dimension_semantics=("parallel", "arbitrary")),
        name=name,
    )(res, xn, comb, wg, wu, wd)


def _moe(y, g, prm, tm_r, tm_e, tag, precise):
    sfx = "_f32" if precise else ""
    xn, comb = _moe_router(y, g, prm["wr"], prm["br"], tm_r, "moe_router_" + tag, F32 if precise else BF16)
    return _moe_dense(y, xn, comb, prm["wg" + sfx], prm["wu" + sfx], prm["wd" + sfx], tm_e, "moe_experts_" + tag,
                      precise)


def _moe_grouped_kernel(te_ref, nt_ref, x_ref, wg_ref, wu_ref, wd_ref, o_ref):
    del te_ref
    active = pl.program_id(0) < nt_ref[0]

    @pl.when(active)
    def _():
        x = x_ref[...]
        hdn = jax.nn.silu(_mm(x, wg_ref[0])) * _mm(x, wu_ref[0])
        o_ref[...] = _mm(hdn, wd_ref[0]).astype(o_ref.dtype)

    @pl.when(jnp.logical_not(active))
    def _():
        o_ref[...] = jnp.zeros(o_ref.shape, o_ref.dtype)


def _moe_grouped(x_rows, tile_expert, n_tiles, wg, wu, wd, tm, name):
    r, d = x_rows.shape
    grid_spec = pltpu.PrefetchScalarGridSpec(
        num_scalar_prefetch=2,
        grid=(r // tm,),
        in_specs=[pl.BlockSpec((tm, d), lambda i, te, nt: (i, 0)),
                  pl.BlockSpec((1, d, MOE_FF), lambda i, te, nt: (te[i], 0, 0)),
                  pl.BlockSpec((1, d, MOE_FF), lambda i, te, nt: (te[i], 0, 0)),
                  pl.BlockSpec((1, MOE_FF, d), lambda i, te, nt: (te[i], 0, 0))],
        out_specs=pl.BlockSpec((tm, d), lambda i, te, nt: (i, 0)),
    )
    return pl.pallas_call(
        _moe_grouped_kernel,
        grid_spec=grid_spec,
        out_shape=jax.ShapeDtypeStruct((r, d), F32),
        compiler_params=pltpu.CompilerParams(dimension_semantics=("arbitrary",)),
        name=name,
    )(tile_expert, n_tiles, x_rows, wg, wu, wd)


def _moe_sparse(y, g, prm, tm_r, tm_e, tag):
    t, d = y.shape
    xn, route = _moe_router(y, g, prm["wr"], prm["br"], tm_r, "moe_router_" + tag, F32)
    ids = route[:, _ROUTE_LANE0:_ROUTE_LANE0 + 2].astype(jnp.int32).reshape(-1)
    gates = route[:, _ROUTE_LANE0 + 2:_ROUTE_LANE0 + 4]
    n = 2 * t
    rows = n + MOE_EXPERTS * tm_e
    iota = jnp.arange(n, dtype=jnp.int32)
    skey = jnp.sort(ids * n + iota)
    order = skey % n
    sorted_e = skey // n
    counts = jnp.sum(jax.nn.one_hot(ids, MOE_EXPERTS, dtype=jnp.int32), axis=0)
    start = jnp.cumsum(counts) - counts
    pcounts = ((counts + tm_e - 1) // tm_e) * tm_e
    pend = jnp.cumsum(pcounts)
    pstart = pend - pcounts
    dest = pstart[sorted_e] + iota - start[sorted_e]
    pos = (jnp.sort(order * rows + dest) % rows).reshape(t, 2)
    tile_expert = jnp.minimum(jnp.searchsorted(pend, jnp.arange(rows // tm_e, dtype=jnp.int32) * tm_e, side="right"),
                              MOE_EXPERTS - 1).astype(jnp.int32)
    n_tiles = (pend[-1:] // tm_e).astype(jnp.int32)
    row_expert = jnp.repeat(tile_expert, tm_e)
    off = jnp.arange(rows, dtype=jnp.int32) - pstart[row_expert]
    src = jnp.where(off < counts[row_expert], order[jnp.minimum(start[row_expert] + off, n - 1)], 0)
    x_rows = xn[src // 2]
    out_rows = _moe_grouped(x_rows, tile_expert, n_tiles, prm["wg"], prm["wu"], prm["wd"], tm_e, "moe_experts_" + tag)
    return y + gates[:, 0:1] * out_rows[pos[:, 0]] + gates[:, 1:2] * out_rows[pos[:, 1]]


def _moe_params(w_group, b_group, w_expert, b_expert, w_gate, w_up, w_down):
    d = w_group.shape[0]
    pad = LANES - MOE_EXPERTS - MOE_GROUPS
    wr = jnp.concatenate([w_expert, w_group, jnp.zeros((d, pad), F32)], axis=1)
    br = jnp.concatenate([b_expert, b_group, jnp.zeros((pad,), F32)]).reshape(1, LANES)
    return {"wr": wr, "br": br, "wg": w_gate.astype(BF16), "wu": w_up.astype(BF16), "wd": w_down.astype(BF16),
            "wg_f32": w_gate, "wu_f32": w_up, "wd_f32": w_down}


def _ssd_params(w_in, conv_w, conv_b, dt_bias, a_log, d, norm, w_out):
    a = -jnp.exp(a_log)
    w_dt = w_in[:, SSD_INNER + SSD_CONV_DIM:]

    def grouped(x):
        x = x.reshape(x.shape[:-1] + (SSD_GROUPS, SSD_HPG))
        x = jnp.pad(x, [(0, 0)] * (x.ndim - 1) + [(0, LANES - SSD_HPG)])
        return x.reshape(x.shape[:-2] + (SSD_GROUPS * LANES,))

    w_z = w_in[:, :SSD_INNER]
    w_xbc = w_in[:, SSD_INNER:SSD_INNER + SSD_CONV_DIM]
    eye = jnp.eye(SSD_HEADS, dtype=F32)
    return {
        "w_z": w_z.astype(BF16), "w_xbc": w_xbc.astype(BF16), "w_dt_g": grouped(w_dt).astype(BF16),
        "w_z_f32": w_z, "w_xbc_f32": w_xbc, "w_dt_f32": w_dt, "w_out_f32": w_out,
        "expand_64": jnp.repeat(eye, SSD_HEADDIM, axis=1), "expand_128": jnp.repeat(eye, LANES, axis=1),
        "conv_w": conv_w, "conv_b": conv_b.reshape(1, SSD_CONV_DIM),
        "dt_bias_g": grouped(dt_bias).reshape(SSD_GROUPS, 1, LANES),
        "a_g": grouped(a).reshape(SSD_GROUPS, 1, LANES),
        "dt_bias_64": jnp.repeat(dt_bias, SSD_HEADDIM).reshape(1, SSD_INNER),
        "dt_bias_128": jnp.broadcast_to(dt_bias[:, None], (SSD_HEADS, LANES)),
        "a_128": jnp.broadcast_to(a[:, None], (SSD_HEADS, LANES)),
        "d_exp": jnp.repeat(d, SSD_HEADDIM).reshape(1, SSD_INNER),
        "norm": norm, "w_out": w_out.astype(BF16),
    }


_TM = 512


def kernel(x_prompt, x_sample, cache_k, cache_v, page_table, state_s5_re, state_s5_im, state_ssd, state_conv, norm_mix, norm_ffn, norm_final, even_w_in, even_w_out, diff_lam_q1, diff_lam_k1, diff_lam_q2, diff_lam_k2, diff_subln, s5_lam_re, s5_lam_im, s5_log_dt, s5_b_re, s5_b_im, s5_c_re, s5_c_im, s5_d, s5_w_glu, s5_b_glu, ssd_w_in, ssd_conv_w, ssd_conv_b, ssd_dt_bias, ssd_a_log, ssd_d, ssd_norm, ssd_w_out, moe_w_group, moe_b_group, moe_w_expert, moe_b_expert, moe_w_gate, moe_w_up, moe_w_down):
    bp, sp, d = x_prompt.shape
    bs = x_sample.shape[0]
    tp = bp * sp
    depth = norm_mix.shape[0]
    y_p = x_prompt.reshape(tp, d)
    y_s = x_sample.reshape(bs, d)
    outs = {n: [] for n in ("k_p", "v_p", "k_s", "v_s", "re_p", "im_p", "re_s", "im_s", "ssd_p", "ssd_s", "cv_p", "cv_s")}

    for li in range(depth):
        if li % 2 == 0:
            e = li // 2
            lam_init = 0.8 - 0.6 * math.exp(-0.3 * li)
            lam = (jnp.exp(jnp.sum(diff_lam_q1[e] * diff_lam_k1[e])) - jnp.exp(jnp.sum(diff_lam_q2[e] * diff_lam_k2[e]))
                   + lam_init).astype(F32)
            ws32 = [even_w_in[e][:, i * ATT_WIDTH:(i + 1) * ATT_WIDTH] for i in range(4)]
            ws = [w.astype(BF16) for w in ws32]
            w_out32 = [even_w_out[e][:ATT_WIDTH], even_w_out[e][ATT_WIDTH:]]
            w_out_a, w_out_s = [w.astype(BF16) for w in w_out32]
            s5p = _s5_params(s5_lam_re[e], s5_lam_im[e], s5_log_dt[e], s5_b_re[e], s5_b_im[e], s5_c_re[e], s5_c_im[e],
                             s5_d[e], s5_w_glu[e], s5_b_glu[e])
            dts = [(BF16,), (F32, BF16), (F32, BF16), (F32,)]
            q, k, k16, v, v16, u = _norm_matmul(y_p, norm_mix[li], ws, dts, _TM, "even_in_p")
            o = _attn_prompt(q.reshape(bp, sp, ATT_WIDTH), k16.reshape(bp, sp, ATT_WIDTH),
                             v16.reshape(bp, sp, ATT_WIDTH), lam, diff_subln[e], 1.0 - lam_init)
            s5o, h_re, h_im = _s5_prompt(u.reshape(bp, sp, S5_WIDTH), s5p)
            y_p = _matmul_res(y_p, [o.reshape(tp, ATT_WIDTH), s5o.reshape(tp, S5_WIDTH)], [w_out_a, w_out_s], _TM,
                              "even_out_p")
            outs["k_p"].append(k.reshape(bp, sp, ATT_HEADS, HEAD_W))
            outs["v_p"].append(v.reshape(bp, sp, ATT_HEADS, HEAD_W))
            outs["re_p"].append(h_re.reshape(bp, S5_GROUPS, S5_STATE))
            outs["im_p"].append(h_im.reshape(bp, S5_GROUPS, S5_STATE))
            q, k, v, u = _norm_matmul(y_s, norm_mix[li], ws32, [(F32,), (F32,), (F32,), (F32,)], bs, "even_in_s", True)
            o = _attn_decode(q, k, v, cache_k, cache_v, page_table + e * cache_k.shape[1], lam, diff_subln[e],
                             1.0 - lam_init)
            s5o, h_re, h_im = _s5_step(u, state_s5_re[e].reshape(bs, S5_LANES), state_s5_im[e].reshape(bs, S5_LANES), s5p)
            y_s = _matmul_res(y_s, [o, s5o], w_out32, bs, "even_out_s", True)
            outs["k_s"].append(k.reshape(bs, 1, ATT_HEADS, HEAD_W))
            outs["v_s"].append(v.reshape(bs, 1, ATT_HEADS, HEAD_W))
            outs["re_s"].append(h_re.reshape(bs, S5_GROUPS, S5_STATE))
            outs["im_s"].append(h_im.reshape(bs, S5_GROUPS, S5_STATE))
        else:
            o_ = li // 2
            sp_ = _ssd_params(ssd_w_in[o_], ssd_conv_w[o_], ssd_conv_b[o_], ssd_dt_bias[o_], ssd_a_log[o_], ssd_d[o_],
                              ssd_norm[o_], ssd_w_out[o_])
            z, xbc, dtg = _norm_matmul(y_p, norm_mix[li], [sp_["w_z"], sp_["w_xbc"], sp_["w_dt_g"]],
                                       [(F32,), (F32,), (F32,)], 256, "ssd_in_p")
            xbc3 = xbc.reshape(bp, sp, SSD_CONV_DIM)
            yssd, st = _ssd_prompt(xbc3, dtg.reshape(bp, sp, SSD_GROUPS * LANES), sp_)
            y_p = _gated_norm_matmul(y_p, yssd.reshape(tp, SSD_INNER), z, sp_["norm"], sp_["w_out"], 256, "ssd_out_p")
            outs["ssd_p"].append(st)
            outs["cv_p"].append(xbc3[:, sp - (SSD_CONV - 1):, :])
            z, xbc, dtc = _norm_matmul(y_s, norm_mix[li], [sp_["w_z_f32"], sp_["w_xbc_f32"], sp_["w_dt_f32"]],
                                       [(F32,), (F32,), (F32,)], bs, "ssd_in_s", True)
            dt64, dt128 = _dt_expand(dtc, sp_)
            yssd, st = _ssd_step(xbc, state_conv[o_], dt64, dt128, state_ssd[o_], sp_)
            y_s = _gated_norm_matmul(y_s, yssd, z, sp_["norm"], sp_["w_out_f32"], bs, "ssd_out_s", True)
            outs["ssd_s"].append(st)
            outs["cv_s"].append(jnp.concatenate([state_conv[o_][:, 1:], xbc[:, None, :]], axis=1))
        mp = _moe_params(moe_w_group[li], moe_b_group[li], moe_w_expert[li], moe_b_expert[li], moe_w_gate[li],
                         moe_w_up[li], moe_w_down[li])
        y_p = _moe_sparse(y_p, norm_ffn[li], mp, _TM, 256, "p")
        y_s = _moe(y_s, norm_ffn[li], mp, bs, bs, "s", li + 1 < depth)

    y_prompt = _rmsnorm(y_p, norm_final, _TM, "final_p").reshape(bp, sp, d)
    y_sample = _rmsnorm(y_s, norm_final, bs, "final_s").reshape(bs, 1, d)
    st = lambda n: jnp.stack(outs[n])
    return (y_prompt, y_sample, st("k_p"), st("v_p"), st("k_s"), st("v_s"), st("re_p"), st("im_p"), st("re_s"),
            st("im_s"), st("ssd_p"), st("ssd_s"), st("cv_p"), st("cv_s"))
```

```python
import functools
import math

import jax
import jax.numpy as jnp
from jax import lax
from jax.experimental import pallas as pl
from jax.experimental.pallas import tpu as pltpu

F32 = jnp.float32
BF16 = jnp.bfloat16
HIGHEST = lax.Precision.HIGHEST

D_MODEL = 1024
NORM_EPS = 1e-6
PAGE_SIZE = 128
ATT_HEADS = 4
ATT_D = 64
ATT_WIDTH = ATT_HEADS * 2 * ATT_D
HEAD_W = 2 * ATT_D
S5_WIDTH = 512
S5_GROUP = 16
S5_GROUPS = 32
S5_STATE = 64
S5_LANES = S5_GROUPS * S5_STATE
SSD_INNER = 2048
SSD_HEADDIM = 64
SSD_HEADS = 32
SSD_GROUPS = 4
SSD_HPG = 8
SSD_STATE = 128
SSD_CONV = 4
SSD_CONV_DIM = SSD_INNER + 2 * SSD_GROUPS * SSD_STATE
SSD_CHUNK = 128
SSD_GW = SSD_HPG * SSD_HEADDIM
MOE_GROUPS = 4
MOE_EPG = 8
MOE_EXPERTS = 32
MOE_FF = 256
LANES = 128
NEG = -1e30

_NT = (((1,), (1,)), ((), ()))
_TN = (((0,), (0,)), ((), ()))


def _rms(x, g):
    return x * lax.rsqrt(jnp.mean(x * x, axis=-1, keepdims=True) + NORM_EPS) * g


def _split(x):
    hi = x.astype(BF16)
    return hi, (x - hi.astype(F32)).astype(BF16)


def _mm(a, b, precise=False, dims=None):
    dot = jnp.dot if dims is None else functools.partial(lax.dot_general, dimension_numbers=dims)
    if not precise:
        return dot(a.astype(BF16), b.astype(BF16), preferred_element_type=F32)
    m = a.shape[0]
    a_hi, a_lo = _split(a)
    b_hi, b_lo = _split(b)
    r = dot(jnp.concatenate([a_hi, a_lo], axis=0), b_hi, preferred_element_type=F32)
    return r[:m] + r[m:] + dot(a_hi, b_lo, preferred_element_type=F32)


def _norm_matmul_kernel(x_ref, g_ref, *refs, out_dtypes, precise):
    n = len(out_dtypes)
    w_refs = refs[:n]
    o_refs = refs[n:]
    h = _rms(x_ref[...], g_ref[...])
    if not precise:
        h = h.astype(BF16)
    k = 0
    for w_ref, dts in zip(w_refs, out_dtypes):
        r = _mm(h, w_ref[...], precise)
        for dt in dts:
            if dt == _TILE_T:
                o_refs[k][0] = r.T.astype(BF16)
            else:
                o_refs[k][...] = r.astype(dt)
            k += 1


_TILE_T = "bf16 row tiles, each transposed"


def _norm_matmul(x, g, ws, out_dtypes, tm, name, precise=False):
    m, kdim = x.shape
    out_shape, out_specs = [], []
    for w, dts in zip(ws, out_dtypes):
        for dt in dts:
            if dt == _TILE_T:
                out_shape.append(jax.ShapeDtypeStruct((m // tm, w.shape[1], tm), BF16))
                out_specs.append(pl.BlockSpec((1, w.shape[1], tm), lambda i: (i, 0, 0)))
            else:
                out_shape.append(jax.ShapeDtypeStruct((m, w.shape[1]), dt))
                out_specs.append(pl.BlockSpec((tm, w.shape[1]), lambda i: (i, 0)))
    return pl.pallas_call(
        functools.partial(_norm_matmul_kernel, out_dtypes=out_dtypes, precise=precise),
        grid=(m // tm,),
        in_specs=[pl.BlockSpec((tm, kdim), lambda i: (i, 0)), pl.BlockSpec((1, kdim), lambda i: (0, 0))]
        + [pl.BlockSpec(w.shape, lambda i: (0, 0)) for w in ws],
        out_specs=out_specs,
        out_shape=out_shape,
        compiler_params=pltpu.CompilerParams(dimension_semantics=("parallel",)),
        name=name,
    )(x, g.reshape(1, kdim), *ws)


def _matmul_res_kernel(res_ref, *refs, precise):
    n = (len(refs) - 1) // 2
    acc = res_ref[...]
    for a_ref, w_ref in zip(refs[:n], refs[n:2 * n]):
        acc = acc + _mm(a_ref[...], w_ref[...], precise)
    refs[-1][...] = acc


def _matmul_res(res, a_list, w_list, tm, name, precise=False):
    m, n = res.shape
    return pl.pallas_call(
        functools.partial(_matmul_res_kernel, precise=precise),
        grid=(m // tm,),
        in_specs=[pl.BlockSpec((tm, n), lambda i: (i, 0))]
        + [pl.BlockSpec((tm, a.shape[1]), lambda i: (i, 0)) for a in a_list]
        + [pl.BlockSpec(w.shape, lambda i: (0, 0)) for w in w_list],
        out_specs=pl.BlockSpec((tm, n), lambda i: (i, 0)),
        out_shape=jax.ShapeDtypeStruct((m, n), F32),
        compiler_params=pltpu.CompilerParams(dimension_semantics=("parallel",)),
        name=name,
    )(res, *a_list, *w_list)


def _gated_norm_matmul_kernel(res_ref, y_ref, z_ref, g_ref, w_ref, o_ref, *, precise):
    y = y_ref[...] * jax.nn.silu(z_ref[...])
    o_ref[...] = res_ref[...] + _mm(_rms(y, g_ref[...]), w_ref[...], precise)


def _gated_norm_matmul(res, y, z, g, w, tm, name, precise=False):
    m, n = res.shape
    kdim = y.shape[1]
    return pl.pallas_call(
        functools.partial(_gated_norm_matmul_kernel, precise=precise),
        grid=(m // tm,),
        in_specs=[pl.BlockSpec((tm, n), lambda i: (i, 0)), pl.BlockSpec((tm, kdim), lambda i: (i, 0)),
                  pl.BlockSpec((tm, kdim), lambda i: (i, 0)), pl.BlockSpec((1, kdim), lambda i: (0, 0)),
                  pl.BlockSpec(w.shape, lambda i: (0, 0))],
        out_specs=pl.BlockSpec((tm, n), lambda i: (i, 0)),
        out_shape=jax.ShapeDtypeStruct((m, n), F32),
        compiler_params=pltpu.CompilerParams(dimension_semantics=("parallel",)),
        name=name,
    )(res, y, z, g.reshape(1, kdim), w)


def _rmsnorm_kernel(x_ref, g_ref, o_ref):
    o_ref[...] = _rms(x_ref[...], g_ref[...])


def _rmsnorm(x, g, tm, name):
    m, n = x.shape
    return pl.pallas_call(
        _rmsnorm_kernel,
        grid=(m // tm,),
        in_specs=[pl.BlockSpec((tm, n), lambda i: (i, 0)), pl.BlockSpec((1, n), lambda i: (0, 0))],
        out_specs=pl.BlockSpec((tm, n), lambda i: (i, 0)),
        out_shape=jax.ShapeDtypeStruct((m, n), F32),
        compiler_params=pltpu.CompilerParams(dimension_semantics=("parallel",)),
        name=name,
    )(x, g.reshape(1, n))


def _split_q(q):
    lane = lax.broadcasted_iota(jnp.int32, q.shape, 1) % HEAD_W
    scale = ATT_D ** -0.5
    qs = q * jnp.asarray(scale, q.dtype)
    zero = jnp.zeros_like(qs)
    return jnp.where(lane < ATT_D, qs, zero), jnp.where(lane >= ATT_D, qs, zero)


def _subln(o, g, out_scale):
    return _rms(o, g) * out_scale


def _attn_prompt_kernel(lam_ref, q_ref, k_ref, vt_ref, g_ref, o_ref, qt_sc, m_sc, l_sc, acc_sc, *, tq, tk, out_scale):
    qi = pl.program_id(1)
    heads = range(ATT_HEADS)
    for h in heads:
        q1, q2 = _split_q(q_ref[0, :, h * HEAD_W:(h + 1) * HEAD_W])
        qt_sc[h] = jnp.concatenate([q1, q2], axis=0).astype(F32).T.astype(BF16)
    m_sc[...] = jnp.full(m_sc.shape, NEG, F32)
    l_sc[...] = jnp.zeros(l_sc.shape, F32)
    acc_sc[...] = jnp.zeros(acc_sc.shape, F32)

    def kv_step(j, masked):
        start = pl.multiple_of(j * tk, tk)
        for h in heads:
            kt = k_ref[0, pl.ds(start, tk), h * HEAD_W:(h + 1) * HEAD_W]
            st = jnp.dot(kt, qt_sc[h], preferred_element_type=F32)
            if masked:
                k_pos = start + lax.broadcasted_iota(jnp.int32, st.shape, 0)
                q_pos = qi * tq + lax.broadcasted_iota(jnp.int32, st.shape, 1) % tq
                st = jnp.where(k_pos <= q_pos, st, NEG)
            m_old = m_sc[h]
            m_new = jnp.maximum(m_old, jnp.max(st, axis=0, keepdims=True))
            p = jnp.exp(st - m_new)
            alpha = jnp.exp(m_old - m_new)
            l_sc[h] = alpha * l_sc[h] + jnp.sum(p, axis=0, keepdims=True)
            m_sc[h] = m_new
            vt = vt_ref[j, h * HEAD_W:(h + 1) * HEAD_W, :]
            acc_sc[h] = alpha * acc_sc[h] + jnp.dot(vt, p.astype(BF16), preferred_element_type=F32)

    n_full = (qi * tq) // tk

    def full_step(j, carry):
        kv_step(j, False)
        return carry

    lax.fori_loop(0, n_full, full_step, 0)
    kv_step(n_full, True)
    for h in heads:
        acc, l = acc_sc[h], l_sc[h]
        ot = acc[:, 0:tq] / l[:, 0:tq] - lam_ref[0, 0] * (acc[:, tq:2 * tq] / l[:, tq:2 * tq])
        o_ref[0, :, h * HEAD_W:(h + 1) * HEAD_W] = _subln(ot.T, g_ref[...], out_scale).astype(o_ref.dtype)


def _attn_prompt(q, k, vt, lam, subln, out_scale, tq=256):
    b, s, _ = q.shape
    tk = vt.shape[2]
    nblk = s // tk
    kern = functools.partial(_attn_prompt_kernel, tq=tq, tk=tk, out_scale=out_scale)
    return pl.pallas_call(
        kern,
        grid=(b, s // tq),
        in_specs=[pl.BlockSpec(memory_space=pltpu.SMEM),
                  pl.BlockSpec((1, tq, ATT_WIDTH), lambda bi, i: (bi, i, 0)),
                  pl.BlockSpec((1, s, ATT_WIDTH), lambda bi, i: (bi, 0, 0)),
                  pl.BlockSpec((nblk, ATT_WIDTH, tk), lambda bi, i: (bi, 0, 0)),
                  pl.BlockSpec((1, HEAD_W), lambda bi, i: (0, 0))],
        out_specs=pl.BlockSpec((1, tq, ATT_WIDTH), lambda bi, i: (bi, i, 0)),
        out_shape=jax.ShapeDtypeStruct((b, s, ATT_WIDTH), BF16),
        scratch_shapes=[pltpu.VMEM((ATT_HEADS, HEAD_W, 2 * tq), BF16), pltpu.VMEM((ATT_HEADS, 1, 2 * tq), F32),
                        pltpu.VMEM((ATT_HEADS, 1, 2 * tq), F32), pltpu.VMEM((ATT_HEADS, HEAD_W, 2 * tq), F32)],
        compiler_params=pltpu.CompilerParams(dimension_semantics=("parallel", "parallel")),
        name="attn_prompt",
    )(lam.reshape(1, 1), q, k, vt, subln.reshape(1, HEAD_W))


_DEC_PAGES = 16
_DEC_PROW = PAGE_SIZE * ATT_HEADS
_DEC_COLS = _DEC_PAGES * _DEC_PROW
_DEC_ROWS = 2 * ATT_HEADS


def _attn_decode_kernel(pt_ref, lam_ref, q_ref, kn_ref, vn_ref, g_ref, *refs, steps, out_scale):
    del pt_ref
    k_refs = refs[:_DEC_PAGES]
    v_refs = refs[_DEC_PAGES:2 * _DEC_PAGES]
    o_ref = refs[2 * _DEC_PAGES]
    s_sc, m_sc, wn_sc, acc_sc = refs[2 * _DEC_PAGES + 1:]
    j = pl.program_id(1)
    lam = lam_ref[0, 0]

    q4 = q_ref[0] * (ATT_D ** -0.5)
    q8 = jnp.concatenate([q4, q4], axis=0)
    r8 = lax.broadcasted_iota(jnp.int32, (_DEC_ROWS, HEAD_W), 0)
    l8 = lax.broadcasted_iota(jnp.int32, (_DEC_ROWS, HEAD_W), 1)
    q8 = jnp.where(l8 // ATT_D == r8 // ATT_HEADS, q8, 0.0)

    @pl.when(j == 0)
    def _():
        m_sc[...] = jnp.full(m_sc.shape, NEG, F32)

    @pl.when(j < steps)
    def _():
        s = jnp.concatenate([_mm(q8, k_ref[0], True, _NT) for k_ref in k_refs], axis=1)
        rs = lax.broadcasted_iota(jnp.int32, s.shape, 0)
        cs = lax.broadcasted_iota(jnp.int32, s.shape, 1)
        s = jnp.where(cs % ATT_HEADS == rs % ATT_HEADS, s, NEG)
        s_sc[j] = s
        m_sc[...] = jnp.maximum(m_sc[...], jnp.max(s, axis=-1, keepdims=True))

    @pl.when(j == steps - 1)
    def _():
        kn = jnp.concatenate([kn_ref[0], kn_ref[0]], axis=0)
        sn = jnp.sum(q8 * kn, axis=-1, keepdims=True)
        m = jnp.maximum(m_sc[...], sn)
        pn = jnp.exp(sn - m)
        l = pn
        for t in range(steps):
            p = jnp.exp(s_sc[t] - m)
            s_sc[t] = p
            l = l + jnp.sum(p, axis=-1, keepdims=True)
        zeros = jnp.zeros((ATT_HEADS, _DEC_COLS), F32)
        for t in range(steps):
            p = s_sc[t] / l
            s_sc[t] = jnp.concatenate([p[:ATT_HEADS] - lam * p[ATT_HEADS:], zeros], axis=0)
        pn = pn / l
        wn_sc[...] = jnp.concatenate([pn[:ATT_HEADS] - lam * pn[ATT_HEADS:], jnp.zeros((ATT_HEADS, 1), F32)], axis=0)

    @pl.when(j == steps)
    def _():
        acc_sc[...] = jnp.zeros(acc_sc.shape, F32)

    @pl.when(j >= steps)
    def _():
        w = s_sc[j - steps]
        acc = acc_sc[...]
        for i, v_ref in enumerate(v_refs):
            acc = acc + _mm(w[:, i * _DEC_PROW:(i + 1) * _DEC_PROW], v_ref[0], True)
        acc_sc[...] = acc

    @pl.when(j == 2 * steps - 1)
    def _():
        o4 = acc_sc[:ATT_HEADS] + wn_sc[:ATT_HEADS] * vn_ref[0]
        o_ref[0] = _subln(o4, g_ref[...], out_scale)


def _attn_decode(q, k_new, v_new, cache_k, cache_v, page_table, lam, subln, out_scale):
    b = q.shape[0]
    n_pages = page_table.shape[1]
    steps = n_pages // _DEC_PAGES
    ck = cache_k.reshape(-1, _DEC_PROW, HEAD_W)
    cv = cache_v.reshape(-1, _DEC_PROW, HEAD_W)

    def k_map(i):
        return lambda bi, j, pt: (pt[bi, jnp.minimum(j, steps - 1) * _DEC_PAGES + i], 0, 0)

    def v_map(i):
        return lambda bi, j, pt: (pt[bi, jnp.maximum(j - steps, 0) * _DEC_PAGES + i], 0, 0)

    row = lambda bi, j, pt: (bi, 0, 0)
    head_rows = pl.BlockSpec((1, ATT_HEADS, HEAD_W), row)
    grid_spec = pltpu.PrefetchScalarGridSpec(
        num_scalar_prefetch=1,
        grid=(b, 2 * steps),
        in_specs=[pl.BlockSpec(memory_space=pltpu.SMEM), head_rows, head_rows, head_rows,
                  pl.BlockSpec((1, HEAD_W), lambda bi, j, pt: (0, 0))]
        + [pl.BlockSpec((1, _DEC_PROW, HEAD_W), k_map(i)) for i in range(_DEC_PAGES)]
        + [pl.BlockSpec((1, _DEC_PROW, HEAD_W), v_map(i)) for i in range(_DEC_PAGES)],
        out_specs=head_rows,
        scratch_shapes=[pltpu.VMEM((steps, _DEC_ROWS, _DEC_COLS), F32), pltpu.VMEM((_DEC_ROWS, 1), F32),
                        pltpu.VMEM((_DEC_ROWS, 1), F32), pltpu.VMEM((_DEC_ROWS, HEAD_W), F32)],
    )
    out = pl.pallas_call(
        functools.partial(_attn_decode_kernel, steps=steps, out_scale=out_scale),
        grid_spec=grid_spec,
        out_shape=jax.ShapeDtypeStruct((b, ATT_HEADS, HEAD_W), F32),
        compiler_params=pltpu.CompilerParams(dimension_semantics=("parallel", "arbitrary")),
        name="attn_decode",
    )(page_table, lam.reshape(1, 1), q.reshape(b, ATT_HEADS, HEAD_W), k_new.reshape(b, ATT_HEADS, HEAD_W),
      v_new.reshape(b, ATT_HEADS, HEAD_W), subln.reshape(1, HEAD_W), *([ck] * _DEC_PAGES), *([cv] * _DEC_PAGES))
    return out.reshape(b, ATT_WIDTH)


_S5_KT = 2
_S5_KW = S5_WIDTH // _S5_KT
_S5_KL = S5_LANES // _S5_KT
_S5_NT = S5_LANES // LANES
_S5_SCAN_TILES = 8


def _s5_input(u, bmat_ref, precise):
    re, im = [], []
    for kt in range(_S5_KT):
        r = _mm(u[:, kt * _S5_KW:(kt + 1) * _S5_KW], bmat_ref[kt], precise)
        re.append(r[:, :_S5_KL])
        im.append(r[:, _S5_KL:])
    return jnp.concatenate(re, axis=1), jnp.concatenate(im, axis=1)


def _s5_output(h_re, h_im, u, cmat_ref, d_ref, wglu_ref, bglu_ref, precise):
    if not precise:
        h_re, h_im = h_re.astype(BF16), h_im.astype(BF16)
    ys = []
    for kt in range(_S5_KT):
        sl = slice(kt * _S5_KL, (kt + 1) * _S5_KL)
        y = _mm(h_re[:, sl], cmat_ref[kt, :_S5_KL, :], precise)
        y = y + _mm(h_im[:, sl], cmat_ref[kt, _S5_KL:, :], precise)
        ys.append(y)
    y = jnp.concatenate(ys, axis=1) + d_ref[...] * u
    g = jax.nn.gelu(y)
    gate = _mm(g, wglu_ref[...], precise) + bglu_ref[...]
    return g * jax.nn.sigmoid(gate)


def _s5_scan_kernel(u_ref, are_ref, aim_ref, bmat_ref, cmat_ref, d_ref, wglu_ref, bglu_ref,
                    o_ref, hre_ref, him_ref, bu_re, bu_im, *, nb, lc, bp):
    c = pl.program_id(0)

    @pl.when(c == 0)
    def _():
        bu_re[...] = jnp.zeros(bu_re.shape, F32)
        bu_im[...] = jnp.zeros(bu_im.shape, F32)
        hre_ref[...] = jnp.zeros(hre_ref.shape, F32)
        him_ref[...] = jnp.zeros(him_ref.shape, F32)

    for b in range(nb):
        re, im = _s5_input(u_ref[b], bmat_ref, False)
        for jt in range(_S5_NT):
            bu_re[jt, pl.ds(b, lc, stride=bp), :] = re[:, jt * LANES:(jt + 1) * LANES]
            bu_im[jt, pl.ds(b, lc, stride=bp), :] = im[:, jt * LANES:(jt + 1) * LANES]

    for j0 in range(0, _S5_NT, _S5_SCAN_TILES):
        tiles = pl.ds(j0, _S5_SCAN_TILES)
        a_re = jnp.broadcast_to(are_ref[tiles], (_S5_SCAN_TILES, bp, LANES))
        a_im = jnp.broadcast_to(aim_ref[tiles], (_S5_SCAN_TILES, bp, LANES))

        def step(t, carry, tiles=tiles, a_re=a_re, a_im=a_im):
            hr, hi = carry
            rows = pl.ds(pl.multiple_of(t * bp, bp), bp)
            nr = hr * a_re - hi * a_im + bu_re[tiles, rows, :]
            ni = hr * a_im + hi * a_re + bu_im[tiles, rows, :]
            bu_re[tiles, rows, :] = nr
            bu_im[tiles, rows, :] = ni
            return nr, ni

        hr, hi = lax.fori_loop(0, lc, step, (hre_ref[tiles], him_ref[tiles]), unroll=2)
        hre_ref[tiles] = hr
        him_ref[tiles] = hi

    for b in range(nb):
        h_re = jnp.concatenate([bu_re[jt, pl.ds(b, lc, stride=bp), :] for jt in range(_S5_NT)], axis=1)
        h_im = jnp.concatenate([bu_im[jt, pl.ds(b, lc, stride=bp), :] for jt in range(_S5_NT)], axis=1)
        o_ref[b] = _s5_output(h_re, h_im, u_ref[b], cmat_ref, d_ref, wglu_ref, bglu_ref, False).astype(o_ref.dtype)


def _s5_prompt(u, prm, lc=128):
    nb, l, _ = u.shape
    bp = 8
    full = lambda shape: pl.BlockSpec(shape, lambda c: (0,) * len(shape))
    o, h_re, h_im = pl.pallas_call(
        functools.partial(_s5_scan_kernel, nb=nb, lc=lc, bp=bp),
        grid=(l // lc,),
        in_specs=[pl.BlockSpec((nb, lc, S5_WIDTH), lambda c: (0, c, 0)),
                  full((_S5_NT, 1, LANES)), full((_S5_NT, 1, LANES)),
                  full((_S5_KT, _S5_KW, 2 * _S5_KL)), full((_S5_KT, 2 * _S5_KL, _S5_KW)),
                  full((1, S5_WIDTH)), full((S5_WIDTH, S5_WIDTH)), full((1, S5_WIDTH))],
        out_specs=[pl.BlockSpec((nb, lc, S5_WIDTH), lambda c: (0, c, 0)),
                   full((_S5_NT, bp, LANES)), full((_S5_NT, bp, LANES))],
        out_shape=[jax.ShapeDtypeStruct((nb, l, S5_WIDTH), BF16),
                   jax.ShapeDtypeStruct((_S5_NT, bp, LANES), F32), jax.ShapeDtypeStruct((_S5_NT, bp, LANES), F32)],
        scratch_shapes=[pltpu.VMEM((_S5_NT, lc * bp, LANES), F32) for _ in range(2)],
        compiler_params=pltpu.CompilerParams(dimension_semantics=("arbitrary",)),
        name="s5_prompt",
    )(u, prm["a_re"].reshape(_S5_NT, 1, LANES), prm["a_im"].reshape(_S5_NT, 1, LANES), prm["bmat"].astype(BF16),
      prm["cmat"].astype(BF16), prm["d"], prm["w_glu"].astype(BF16), prm["b_glu"])
    rows = lambda h: jnp.transpose(h, (1, 0, 2)).reshape(bp, S5_LANES)[:nb]
    return o, rows(h_re), rows(h_im)


def _s5_step_kernel(u_ref, h0re_ref, h0im_ref, are_ref, aim_ref, bmat_ref, cmat_ref, d_ref, wglu_ref, bglu_ref,
                    o_ref, hre_ref, him_ref):
    u = u_ref[...]
    bu_re, bu_im = _s5_input(u, bmat_ref, True)
    h_re, h_im = h0re_ref[...], h0im_ref[...]
    a_re, a_im = are_ref[...], aim_ref[...]
    n_re = h_re * a_re - h_im * a_im + bu_re
    n_im = h_re * a_im + h_im * a_re + bu_im
    hre_ref[...] = n_re
    him_ref[...] = n_im
    o_ref[...] = _s5_output(n_re, n_im, u, cmat_ref, d_ref, wglu_ref, bglu_ref, True)


def _s5_step(u, h0_re, h0_im, prm):
    nb = u.shape[0]
    return pl.pallas_call(
        _s5_step_kernel,
        out_shape=[jax.ShapeDtypeStruct((nb, S5_WIDTH), F32),
                   jax.ShapeDtypeStruct((nb, S5_LANES), F32), jax.ShapeDtypeStruct((nb, S5_LANES), F32)],
        name="s5_step",
    )(u, h0_re, h0_im, prm["a_re"], prm["a_im"], prm["bmat"], prm["cmat"], prm["d"], prm["w_glu"], prm["b_glu"])


def _s5_params(lam_re, lam_im, log_dt, b_re, b_im, c_re, c_im, d, w_glu, b_glu):
    lam = lax.complex(lam_re, lam_im)
    dt = jnp.exp(log_dt)[:, None]
    a_bar = jnp.exp(lam * dt)
    b_bar = ((a_bar - 1.0) / lam)[..., None] * lax.complex(b_re, b_im)
    gk = S5_GROUPS // _S5_KT
    eye = jnp.eye(gk, dtype=F32)

    def in_tile(x):
        x = x.reshape(_S5_KT, gk, S5_STATE, S5_GROUP)
        return jnp.einsum("kgpc,gh->kgchp", x, eye).reshape(_S5_KT, _S5_KW, _S5_KL)

    def out_tile(x):
        x = x.reshape(_S5_KT, gk, S5_GROUP, S5_STATE)
        return jnp.einsum("kgcp,gh->kgphc", x, eye).reshape(_S5_KT, _S5_KL, _S5_KW)

    bmat = jnp.concatenate([in_tile(jnp.real(b_bar)), in_tile(jnp.imag(b_bar))], axis=2)
    cmat = jnp.concatenate([out_tile(c_re), out_tile(-c_im)], axis=1)
    return {
        "a_re": jnp.real(a_bar).reshape(1, S5_LANES), "a_im": jnp.imag(a_bar).reshape(1, S5_LANES),
        "bmat": bmat, "cmat": cmat, "d": d.reshape(1, S5_WIDTH), "w_glu": w_glu, "b_glu": b_glu.reshape(1, S5_WIDTH),
    }


_CONV_PAD = 8


def _conv_silu(x, prev_ref, w_ref, b_ref):
    rows = x.shape[0]
    w = w_ref[...]
    prev = prev_ref[...]
    sub = lax.broadcasted_iota(jnp.int32, prev.shape, 0)
    out = b_ref[...]
    for i in range(SSD_CONV):
        s = SSD_CONV - 1 - i
        if s == 0:
            shifted = x
        else:
            rolled = pltpu.roll(x, s, 0)
            head = jnp.where(sub < s, pltpu.roll(prev, s, 0), rolled[0:_CONV_PAD])
            shifted = jnp.concatenate([head, rolled[_CONV_PAD:]], axis=0)
        out = out + w[i:i + 1] * shifted
    prev_ref[...] = x[rows - _CONV_PAD:rows, :]
    return jax.nn.silu(out)


def _ssd_in_kernel(x_ref, g_ref, wz_ref, wx_ref, wdt_ref, cw_ref, cb_ref, z_ref, xc_ref, dt_ref, tail_ref, xf,
                   *, tiles_per_seq):
    i = pl.program_id(0)
    tm = x_ref.shape[0]

    @pl.when(i % tiles_per_seq == 0)
    def _():
        xf[...] = jnp.zeros(xf.shape, F32)

    h = _rms(x_ref[...], g_ref[...]).astype(BF16)
    z_ref[...] = _mm(h, wz_ref[...])
    dt_ref[...] = _mm(h, wdt_ref[...])
    xbc = _mm(h, wx_ref[...])
    tail_ref[0] = xbc[tm - _CONV_PAD:tm, :]
    xc_ref[...] = _conv_silu(xbc, xf, cw_ref, cb_ref)


def _ssd_in(x, g, prm, seq_len, tm, name):
    m, kdim = x.shape
    tiles_per_seq = seq_len // tm
    full = lambda a: pl.BlockSpec(a.shape, lambda i: (0,) * a.ndim)
    rows = lambda n: pl.BlockSpec((tm, n), lambda i: (i, 0))
    wz, wx, wdt = prm["w_z"], prm["w_xbc"], prm["w_dt_g"]
    return pl.pallas_call(
        functools.partial(_ssd_in_kernel, tiles_per_seq=tiles_per_seq),
        grid=(m // tm,),
        in_specs=[rows(kdim), pl.BlockSpec((1, kdim), lambda i: (0, 0)), full(wz), full(wx), full(wdt),
                  full(prm["conv_w"]), full(prm["conv_b"])],
        out_specs=[rows(wz.shape[1]), rows(wx.shape[1]), rows(wdt.shape[1]),
                   pl.BlockSpec((1, _CONV_PAD, wx.shape[1]), lambda i: (i // tiles_per_seq, 0, 0))],
        out_shape=[jax.ShapeDtypeStruct((m, wz.shape[1]), F32), jax.ShapeDtypeStruct((m, wx.shape[1]), F32),
                   jax.ShapeDtypeStruct((m, wdt.shape[1]), F32),
                   jax.ShapeDtypeStruct((m // seq_len, _CONV_PAD, wx.shape[1]), F32)],
        scratch_shapes=[pltpu.VMEM((_CONV_PAD, wx.shape[1]), F32)],
        compiler_params=pltpu.CompilerParams(dimension_semantics=("arbitrary",)),
        name=name,
    )(x, g.reshape(1, kdim), wz, wx, wdt, prm["conv_w"], prm["conv_b"])


def _ssd_chunk_kernel(xs_ref, bm_ref, cm_ref, dt_ref, dtb_ref, a_ref, dexp_ref, y_ref, st_ref):
    c = pl.program_id(2)
    q = SSD_CHUNK

    @pl.when(c == 0)
    def _():
        st_ref[...] = jnp.zeros(st_ref.shape, F32)

    xs = xs_ref[0]
    bm16 = bm_ref[0].astype(BF16)
    cm16 = cm_ref[0].astype(BF16)

    dt = jax.nn.softplus(dt_ref[0] + dtb_ref[0])
    da = dt * a_ref[0]
    row = lax.broadcasted_iota(jnp.int32, (q, q), 0)
    col = lax.broadcasted_iota(jnp.int32, (q, q), 1)
    causal = row >= col
    acs = jnp.dot(causal.astype(F32), da, preferred_element_type=F32, precision=HIGHEST)
    acs_t = acs.T
    acs_last = acs[q - 1:q, :]
    e_acs = jnp.exp(acs)
    dec_s = jnp.exp(acs_last - acs)
    chunk_dec = jnp.exp(acs_last)
    cb = _mm(cm16, bm16, dims=_NT)

    first = lax.broadcasted_iota(jnp.int32, (q, LANES), 1) < SSD_HEADDIM

    def pair_cols(v, i):
        return jnp.where(first, v[:, 2 * i:2 * i + 1], v[:, 2 * i + 1:2 * i + 2])

    ys = []
    for i in range(SSD_HPG // 2):
        pair = slice(i * LANES, (i + 1) * LANES)
        xdt = xs[:, pair] * pair_cols(dt, i)
        xdt16 = xdt.astype(BF16)
        y_head = []
        for j in (2 * i, 2 * i + 1):
            seg = acs[:, j:j + 1] - acs_t[j:j + 1, :]
            lmat = jnp.exp(jnp.where(causal, seg, -jnp.inf))
            y_head.append(_mm(cb * lmat, xdt16))
        y_diag = jnp.where(first, y_head[0], y_head[1])
        h = st_ref[0, 2 * i:2 * i + 2].reshape(LANES, SSD_STATE)
        y_off = _mm(cm16, h, dims=_NT) * pair_cols(e_acs, i)
        cdec = jnp.concatenate([jnp.broadcast_to(chunk_dec[:, j:j + 1], (SSD_HEADDIM, SSD_STATE))
                                for j in (2 * i, 2 * i + 1)], axis=0)
        h_new = cdec * h + _mm(xdt * pair_cols(dec_s, i), bm16, dims=_TN)
        st_ref[0, 2 * i:2 * i + 2] = h_new.reshape(2, SSD_HEADDIM, SSD_STATE)
        ys.append(y_diag + y_off)
    y_ref[0] = jnp.concatenate(ys, axis=1) + dexp_ref[...] * xs


def _ssd_prompt(xc, dtg, prm):
    nb, l, _ = xc.shape
    nc = l // SSD_CHUNK
    q = SSD_CHUNK
    boff = SSD_INNER // LANES
    coff = boff + SSD_GROUPS
    grp = lambda b, g, c: (g, 0, 0)
    return pl.pallas_call(
        _ssd_chunk_kernel,
        grid=(nb, SSD_GROUPS, nc),
        in_specs=[pl.BlockSpec((1, q, SSD_GW), lambda b, g, c: (b, c, g)),
                  pl.BlockSpec((1, q, LANES), lambda b, g, c: (b, c, boff + g)),
                  pl.BlockSpec((1, q, LANES), lambda b, g, c: (b, c, coff + g)),
                  pl.BlockSpec((1, q, LANES), lambda b, g, c: (b, c, g)),
                  pl.BlockSpec((1, 1, LANES), grp), pl.BlockSpec((1, 1, LANES), grp),
                  pl.BlockSpec((1, SSD_GW), lambda b, g, c: (0, g))],
        out_specs=[pl.BlockSpec((1, q, SSD_GW), lambda b, g, c: (b, c, g)),
                   pl.BlockSpec((1, SSD_HPG, SSD_HEADDIM, SSD_STATE), lambda b, g, c: (b, g, 0, 0))],
        out_shape=[jax.ShapeDtypeStruct((nb, l, SSD_INNER), F32),
                   jax.ShapeDtypeStruct((nb, SSD_HEADS, SSD_HEADDIM, SSD_STATE), F32)],
        compiler_params=pltpu.CompilerParams(dimension_semantics=("parallel", "parallel", "arbitrary")),
        name="ssd_prompt",
    )(xc, xc, xc, dtg, prm["dt_bias_g"], prm["a_g"], prm["d_exp"])


def _ssd_step_kernel(x_ref, buf_ref, cw_ref, cb_ref, dt64_ref, dtb64_ref, dt128_ref, dtb128_ref, a128_ref,
                     dexp_ref, st_ref, y_ref, so_ref):
    w = cw_ref[...]
    buf = buf_ref[0]
    conv = cb_ref[...]
    for i in range(SSD_CONV - 1):
        conv = conv + w[i:i + 1] * buf[i:i + 1]
    conv = conv + w[SSD_CONV - 1:SSD_CONV] * x_ref[0]
    xc = jax.nn.silu(conv)
    xs = xc[:, :SSD_INNER]
    xdt = xs * jax.nn.softplus(dt64_ref[0] + dtb64_ref[...])
    dt = jax.nn.softplus(dt128_ref[0] + dtb128_ref[...])
    dec = jnp.exp(dt * a128_ref[...])
    row = lax.broadcasted_iota(jnp.int32, (LANES, LANES), 0)
    col = lax.broadcasted_iota(jnp.int32, (LANES, LANES), 1)
    diag = row == col
    ys = []
    for i in range(SSD_HEADS // 2):
        g = (2 * i) // SSD_HPG
        bm = xc[:, SSD_INNER + g * SSD_STATE:SSD_INNER + (g + 1) * SSD_STATE]
        cm = xc[:, SSD_INNER + (SSD_GROUPS + g) * SSD_STATE:SSD_INNER + (SSD_GROUPS + g + 1) * SSD_STATE]
        xp = jnp.broadcast_to(xdt[:, i * LANES:(i + 1) * LANES], (LANES, LANES))
        outer = _mm(jnp.where(diag, xp, 0.0), jnp.broadcast_to(bm, (LANES, LANES)), True)
        h = st_ref[0, 2 * i:2 * i + 2].reshape(LANES, SSD_STATE)
        dpair = jnp.concatenate([jnp.broadcast_to(dec[2 * i:2 * i + 1], (SSD_HEADDIM, LANES)),
                                 jnp.broadcast_to(dec[2 * i + 1:2 * i + 2], (SSD_HEADDIM, LANES))], axis=0)
        hn = dpair * h + outer
        so_ref[0, 2 * i:2 * i + 2] = hn.reshape(2, SSD_HEADDIM, SSD_STATE)
        ys.append(_mm(jnp.broadcast_to(cm, (8, SSD_STATE)), hn, True, _NT)[0:1])
    y_ref[0] = jnp.concatenate(ys, axis=1) + dexp_ref[...] * xs


def _dt_expand_kernel(dt_ref, e64_ref, e128_ref, o64_ref, o128_ref):
    dt = dt_ref[...]
    o64_ref[...] = _mm(dt, e64_ref[...], True)
    o128_ref[...] = _mm(dt, e128_ref[...], True)


def _dt_expand(dt, prm):
    nb = dt.shape[0]
    return pl.pallas_call(
        _dt_expand_kernel,
        out_shape=[jax.ShapeDtypeStruct((nb, SSD_INNER), F32), jax.ShapeDtypeStruct((nb, SSD_HEADS * LANES), F32)],
        name="ssd_dt_expand",
    )(dt, prm["expand_64"], prm["expand_128"])


def _ssd_step(xbc, conv_buf, dt64, dt128, state, prm):
    nb = xbc.shape[0]
    full = lambda shape: pl.BlockSpec(shape, lambda b: (0,) * len(shape))
    y, so = pl.pallas_call(
        _ssd_step_kernel,
        grid=(nb,),
        in_specs=[pl.BlockSpec((1, 1, SSD_CONV_DIM), lambda b: (b, 0, 0)),
                  pl.BlockSpec((1, SSD_CONV - 1, SSD_CONV_DIM), lambda b: (b, 0, 0)),
                  full((SSD_CONV, SSD_CONV_DIM)), full((1, SSD_CONV_DIM)),
                  pl.BlockSpec((1, 1, SSD_INNER), lambda b: (b, 0, 0)), full((1, SSD_INNER)),
                  pl.BlockSpec((1, SSD_HEADS, LANES), lambda b: (b, 0, 0)), full((SSD_HEADS, LANES)),
                  full((SSD_HEADS, LANES)), full((1, SSD_INNER)),
                  pl.BlockSpec((1, SSD_HEADS, SSD_HEADDIM, SSD_STATE), lambda b: (b, 0, 0, 0))],
        out_specs=[pl.BlockSpec((1, 1, SSD_INNER), lambda b: (b, 0, 0)),
                   pl.BlockSpec((1, SSD_HEADS, SSD_HEADDIM, SSD_STATE), lambda b: (b, 0, 0, 0))],
        out_shape=[jax.ShapeDtypeStruct((nb, 1, SSD_INNER), F32),
                   jax.ShapeDtypeStruct((nb, SSD_HEADS, SSD_HEADDIM, SSD_STATE), F32)],
        compiler_params=pltpu.CompilerParams(dimension_semantics=("parallel",)),
        name="ssd_step",
    )(xbc.reshape(nb, 1, SSD_CONV_DIM), conv_buf, prm["conv_w"], prm["conv_b"],
      dt64.reshape(nb, 1, SSD_INNER), prm["dt_bias_64"], dt128.reshape(nb, SSD_HEADS, LANES),
      prm["dt_bias_128"], prm["a_128"], prm["d_exp"], state)
    return y.reshape(nb, SSD_INNER), so


_GROUP_LANE0 = MOE_EXPERTS
_ROUTE_LANE0 = 64


def _pack_bf16_halves(x):
    n = x.shape[1] // 2
    lo = lax.bitcast_convert_type(x[:, :n].astype(BF16).astype(F32), jnp.uint32)
    hi = lax.bitcast_convert_type(x[:, n:].astype(BF16).astype(F32), jnp.uint32)
    return (lo >> 16) | hi


def _unpack_bf16_halves(p):
    lo = lax.bitcast_convert_type(p << 16, F32)
    hi = lax.bitcast_convert_type(p & jnp.uint32(0xFFFF0000), F32)
    return jnp.concatenate([lo, hi], axis=1)


def _moe_router_kernel(y_ref, g_ref, wr_ref, br_ref, xn_ref, comb_ref, cnt_ref):
    @pl.when(pl.program_id(0) == 0)
    def _():
        cnt_ref[...] = jnp.zeros(cnt_ref.shape, F32)

    xn = _rms(y_ref[...], g_ref[...])
    xn_ref[...] = _pack_bf16_halves(xn) if xn_ref.dtype == jnp.uint32 else xn.astype(xn_ref.dtype)
    lg = _mm(xn, wr_ref[...], True) + br_ref[...]
    lane = lax.broadcasted_iota(jnp.int32, lg.shape, 1)
    is_group = (lane >= _GROUP_LANE0) & (lane < _GROUP_LANE0 + MOE_GROUPS)
    gl = jnp.where(is_group, lg, NEG)
    gmax = jnp.max(gl, axis=-1, keepdims=True)
    g_p = 1.0 / jnp.sum(jnp.exp(gl - gmax), axis=-1, keepdims=True)
    gidx = jnp.min(jnp.where(gl == gmax, lane - _GROUP_LANE0, MOE_GROUPS), axis=-1, keepdims=True)
    el = jnp.where((lane < MOE_EXPERTS) & (lane // MOE_EPG == gidx), lg, NEG)
    m1 = jnp.max(el, axis=-1, keepdims=True)
    i1 = jnp.min(jnp.where(el == m1, lane, LANES), axis=-1, keepdims=True)
    el2 = jnp.where(lane == i1, NEG, el)
    m2 = jnp.max(el2, axis=-1, keepdims=True)
    i2 = jnp.min(jnp.where(el2 == m2, lane, LANES), axis=-1, keepdims=True)
    e2 = jnp.exp(m2 - m1)
    den = 1.0 + e2
    g1 = g_p / den
    g2 = g_p * (e2 / den)
    oh1 = lane == i1
    oh2 = lane == i2
    comb = jnp.where(oh1, g1, 0.0) + jnp.where(oh2, g2, 0.0)
    picks = oh1.astype(F32) + oh2.astype(F32)
    tm = picks.shape[0]
    before = lax.broadcasted_iota(jnp.int32, (tm, tm), 0) > lax.broadcasted_iota(jnp.int32, (tm, tm), 1)
    seen = cnt_ref[...] + _mm(before.astype(F32), picks)
    r1 = jnp.sum(jnp.where(oh1, seen, 0.0), axis=-1, keepdims=True)
    r2 = jnp.sum(jnp.where(oh2, seen, 0.0), axis=-1, keepdims=True)
    cnt_ref[...] += jnp.sum(picks, axis=0, keepdims=True)
    for k, v in enumerate((i1.astype(F32), i2.astype(F32), g1, g2, r1, r2)):
        comb = jnp.where(lane == _ROUTE_LANE0 + k, v, comb)
    comb_ref[...] = comb


def _moe_router(y, g, wr, br, tm, name, xn_dtype):
    m, d = y.shape
    dx = d // 2 if xn_dtype == jnp.uint32 else d
    return pl.pallas_call(
        _moe_router_kernel,
        grid=(m // tm,),
        in_specs=[pl.BlockSpec((tm, d), lambda i: (i, 0)), pl.BlockSpec((1, d), lambda i: (0, 0)),
                  pl.BlockSpec((d, LANES), lambda i: (0, 0)), pl.BlockSpec((1, LANES), lambda i: (0, 0))],
        out_specs=[pl.BlockSpec((tm, dx), lambda i: (i, 0)), pl.BlockSpec((tm, LANES), lambda i: (i, 0)),
                   pl.BlockSpec((1, LANES), lambda i: (0, 0))],
        out_shape=[jax.ShapeDtypeStruct((m, dx), xn_dtype), jax.ShapeDtypeStruct((m, LANES), F32),
                   jax.ShapeDtypeStruct((1, LANES), F32)],
        compiler_params=pltpu.CompilerParams(dimension_semantics=("arbitrary",)),
        name=name,
    )(y, g.reshape(1, d), wr, br)


def _moe_dense_kernel(res_ref, xn_ref, comb_ref, wg_ref, wu_ref, wd_ref, o_ref, acc, *, precise):
    e = pl.program_id(1)

    @pl.when(e == 0)
    def _():
        acc[...] = jnp.zeros(acc.shape, F32)

    x = xn_ref[...]
    hdn = jax.nn.silu(_mm(x, wg_ref[0], precise)) * _mm(x, wu_ref[0], precise)
    comb = comb_ref[...]
    lane = lax.broadcasted_iota(jnp.int32, comb.shape, 1)
    gate = jnp.sum(jnp.where(lane == e, comb, 0.0), axis=-1, keepdims=True)
    acc[...] += gate * _mm(hdn, wd_ref[0], precise)

    @pl.when(e == pl.num_programs(1) - 1)
    def _():
        o_ref[...] = res_ref[...] + acc[...]


def _moe_dense(res, xn, comb, wg, wu, wd, tm, name, precise):
    m, d = res.shape
    return pl.pallas_call(
        functools.partial(_moe_dense_kernel, precise=precise),
        grid=(m // tm, MOE_EXPERTS),
        in_specs=[pl.BlockSpec((tm, d), lambda i, e: (i, 0)), pl.BlockSpec((tm, d), lambda i, e: (i, 0)),
                  pl.BlockSpec((tm, LANES), lambda i, e: (i, 0)),
                  pl.BlockSpec((1, d, MOE_FF), lambda i, e: (e, 0, 0)),
                  pl.BlockSpec((1, d, MOE_FF), lambda i, e: (e, 0, 0)),
                  pl.BlockSpec((1, MOE_FF, d), lambda i, e: (e, 0, 0))],
        out_specs=pl.BlockSpec((tm, d), lambda i, e: (i, 0)),
        out_shape=jax.ShapeDtypeStruct((m, d), F32),
        scratch_shapes=[pltpu.VMEM((tm, d), F32)],
        compiler_params=pltpu.CompilerParams(dimension_semantics=("parallel", "arbitrary")),
        name=name,
    )(res, xn, comb, wg, wu, wd)


def _moe(y, g, prm, tm_r, tm_e, tag, precise):
    xn, comb, _ = _moe_router(y, g, prm["wr"], prm["br"], tm_r, "moe_router_" + tag, F32 if precise else BF16)
    return _moe_dense(y, xn, comb, prm["wg"], prm["wu"], prm["wd"], tm_e, "moe_experts_" + tag, precise)


def _moe_grouped_kernel(te_ref, nt_ref, x_ref, wg_ref, wu_ref, wd_ref, o_ref):
    del te_ref
    active = pl.program_id(0) < nt_ref[0]

    @pl.when(active)
    def _():
        x = _unpack_bf16_halves(x_ref[...]).astype(BF16)
        hdn = jax.nn.silu(_mm(x, wg_ref[0])) * _mm(x, wu_ref[0])
        o_ref[...] = _mm(hdn, wd_ref[0]).astype(o_ref.dtype)

    @pl.when(jnp.logical_not(active))
    def _():
        o_ref[...] = jnp.zeros(o_ref.shape, o_ref.dtype)


def _moe_grouped(x_rows, tile_expert, n_tiles, wg, wu, wd, tm, name):
    r = x_rows.shape[0]
    d = wg.shape[1]
    grid_spec = pltpu.PrefetchScalarGridSpec(
        num_scalar_prefetch=2,
        grid=(r // tm,),
        in_specs=[pl.BlockSpec((tm, x_rows.shape[1]), lambda i, te, nt: (i, 0)),
                  pl.BlockSpec((1, d, MOE_FF), lambda i, te, nt: (te[i], 0, 0)),
                  pl.BlockSpec((1, d, MOE_FF), lambda i, te, nt: (te[i], 0, 0)),
                  pl.BlockSpec((1, MOE_FF, d), lambda i, te, nt: (te[i], 0, 0))],
        out_specs=pl.BlockSpec((tm, d), lambda i, te, nt: (i, 0)),
    )
    return pl.pallas_call(
        _moe_grouped_kernel,
        grid_spec=grid_spec,
        out_shape=jax.ShapeDtypeStruct((r, d), F32),
        compiler_params=pltpu.CompilerParams(dimension_semantics=("arbitrary",)),
        name=name,
    )(tile_expert, n_tiles, x_rows, wg, wu, wd)


def _moe_sparse(y, g, prm, tm_r, tm_e, tag):
    t, d = y.shape
    xn, route, cnt = _moe_router(y, g, prm["wr"], prm["br"], tm_r, "moe_router_" + tag, jnp.uint32)
    lanes = lambda k: route[:, _ROUTE_LANE0 + k:_ROUTE_LANE0 + k + 2]
    ids = lanes(0).astype(jnp.int32)
    gates = lanes(2)
    rank = lanes(4).astype(jnp.int32)
    counts = cnt[0, :MOE_EXPERTS].astype(jnp.int32)
    rows = 2 * t + MOE_EXPERTS * tm_e
    pcounts = ((counts + tm_e - 1) // tm_e) * tm_e
    pend = jnp.cumsum(pcounts)
    pstart = pend - pcounts
    pos = pstart[ids] + rank
    tile_start = jnp.arange(rows // tm_e, dtype=jnp.int32) * tm_e
    tile_expert = jnp.minimum(jnp.sum((pend[None, :] <= tile_start[:, None]).astype(jnp.int32), axis=1),
                              MOE_EXPERTS - 1)
    n_tiles = (pend[-1:] // tm_e).astype(jnp.int32)
    token = jnp.broadcast_to(jnp.arange(t, dtype=jnp.int32)[:, None], (t, 2))
    src = jnp.zeros((rows,), jnp.int32).at[pos.reshape(-1)].set(token.reshape(-1), unique_indices=True)
    x_rows = xn[src]
    out_rows = _moe_grouped(x_rows, tile_expert, n_tiles, prm["wg"], prm["wu"], prm["wd"], tm_e, "moe_experts_" + tag)
    return y + gates[:, 0:1] * out_rows[pos[:, 0]] + gates[:, 1:2] * out_rows[pos[:, 1]]


def _moe_params(w_group, b_group, w_expert, b_expert, w_gate, w_up, w_down):
    d = w_group.shape[0]
    pad = LANES - MOE_EXPERTS - MOE_GROUPS
    wr = jnp.concatenate([w_expert, w_group, jnp.zeros((d, pad), F32)], axis=1)
    br = jnp.concatenate([b_expert, b_group, jnp.zeros((pad,), F32)]).reshape(1, LANES)
    return {"wr": wr, "br": br, "wg": w_gate, "wu": w_up, "wd": w_down}


def _ssd_params(w_in, conv_w, conv_b, dt_bias, a_log, d, norm, w_out):
    a = -jnp.exp(a_log)
    w_dt = w_in[:, SSD_INNER + SSD_CONV_DIM:]

    def grouped(x):
        x = x.reshape(x.shape[:-1] + (SSD_GROUPS, SSD_HPG))
        x = jnp.pad(x, [(0, 0)] * (x.ndim - 1) + [(0, LANES - SSD_HPG)])
        return x.reshape(x.shape[:-2] + (SSD_GROUPS * LANES,))

    w_z = w_in[:, :SSD_INNER]
    w_xbc = w_in[:, SSD_INNER:SSD_INNER + SSD_CONV_DIM]
    eye = jnp.eye(SSD_HEADS, dtype=F32)
    return {
        "w_z": w_z.astype(BF16), "w_xbc": w_xbc.astype(BF16), "w_dt_g": grouped(w_dt).astype(BF16),
        "w_z_f32": w_z, "w_xbc_f32": w_xbc, "w_dt_f32": w_dt, "w_out_f32": w_out,
        "expand_64": jnp.repeat(eye, SSD_HEADDIM, axis=1), "expand_128": jnp.repeat(eye, LANES, axis=1),
        "conv_w": conv_w, "conv_b": conv_b.reshape(1, SSD_CONV_DIM),
        "dt_bias_g": grouped(dt_bias).reshape(SSD_GROUPS, 1, LANES),
        "a_g": grouped(a).reshape(SSD_GROUPS, 1, LANES),
        "dt_bias_64": jnp.repeat(dt_bias, SSD_HEADDIM).reshape(1, SSD_INNER),
        "dt_bias_128": jnp.broadcast_to(dt_bias[:, None], (SSD_HEADS, LANES)),
        "a_128": jnp.broadcast_to(a[:, None], (SSD_HEADS, LANES)),
        "d_exp": jnp.repeat(d, SSD_HEADDIM).reshape(1, SSD_INNER),
        "norm": norm, "w_out": w_out.astype(BF16),
    }


_TM = 512


def kernel(x_prompt, x_sample, cache_k, cache_v, page_table, state_s5_re, state_s5_im, state_ssd, state_conv, norm_mix, norm_ffn, norm_final, even_w_in, even_w_out, diff_lam_q1, diff_lam_k1, diff_lam_q2, diff_lam_k2, diff_subln, s5_lam_re, s5_lam_im, s5_log_dt, s5_b_re, s5_b_im, s5_c_re, s5_c_im, s5_d, s5_w_glu, s5_b_glu, ssd_w_in, ssd_conv_w, ssd_conv_b, ssd_dt_bias, ssd_a_log, ssd_d, ssd_norm, ssd_w_out, moe_w_group, moe_b_group, moe_w_expert, moe_b_expert, moe_w_gate, moe_w_up, moe_w_down):
    bp, sp, d = x_prompt.shape
    bs = x_sample.shape[0]
    tp = bp * sp
    depth = norm_mix.shape[0]
    y_p = x_prompt.reshape(tp, d)
    y_s = x_sample.reshape(bs, d)
    outs = {n: [] for n in ("k_p", "v_p", "k_s", "v_s", "re_p", "im_p", "re_s", "im_s", "ssd_p", "ssd_s", "cv_p", "cv_s")}

    for li in range(depth):
        if li % 2 == 0:
            e = li // 2
            lam_init = 0.8 - 0.6 * math.exp(-0.3 * li)
            lam = (jnp.exp(jnp.sum(diff_lam_q1[e] * diff_lam_k1[e])) - jnp.exp(jnp.sum(diff_lam_q2[e] * diff_lam_k2[e]))
                   + lam_init).astype(F32)
            ws32 = [even_w_in[e][:, i * ATT_WIDTH:(i + 1) * ATT_WIDTH] for i in range(4)]
            ws = [w.astype(BF16) for w in ws32]
            w_out32 = [even_w_out[e][:ATT_WIDTH], even_w_out[e][ATT_WIDTH:]]
            w_out_a, w_out_s = [w.astype(BF16) for w in w_out32]
            s5p = _s5_params(s5_lam_re[e], s5_lam_im[e], s5_log_dt[e], s5_b_re[e], s5_b_im[e], s5_c_re[e], s5_c_im[e],
                             s5_d[e], s5_w_glu[e], s5_b_glu[e])
            dts = [(BF16,), (F32, BF16), (F32, _TILE_T), (F32,)]
            q, k, k16, v, vt, u = _norm_matmul(y_p, norm_mix[li], ws, dts, _TM, "even_in_p")
            o = _attn_prompt(q.reshape(bp, sp, ATT_WIDTH), k16.reshape(bp, sp, ATT_WIDTH), vt, lam, diff_subln[e],
                             1.0 - lam_init)
            s5o, h_re, h_im = _s5_prompt(u.reshape(bp, sp, S5_WIDTH), s5p)
            y_p = _matmul_res(y_p, [o.reshape(tp, ATT_WIDTH), s5o.reshape(tp, S5_WIDTH)], [w_out_a, w_out_s], _TM,
                              "even_out_p")
            outs["k_p"].append(k.reshape(bp, sp, ATT_HEADS, HEAD_W))
            outs["v_p"].append(v.reshape(bp, sp, ATT_HEADS, HEAD_W))
            outs["re_p"].append(h_re.reshape(bp, S5_GROUPS, S5_STATE))
            outs["im_p"].append(h_im.reshape(bp, S5_GROUPS, S5_STATE))
            q, k, v, u = _norm_matmul(y_s, norm_mix[li], ws32, [(F32,), (F32,), (F32,), (F32,)], bs, "even_in_s", True)
            o = _attn_decode(q, k, v, cache_k, cache_v, page_table + e * cache_k.shape[1], lam, diff_subln[e],
                             1.0 - lam_init)
            s5o, h_re, h_im = _s5_step(u, state_s5_re[e].reshape(bs, S5_LANES), state_s5_im[e].reshape(bs, S5_LANES), s5p)
            y_s = _matmul_res(y_s, [o, s5o], w_out32, bs, "even_out_s", True)
            outs["k_s"].append(k.reshape(bs, 1, ATT_HEADS, HEAD_W))
            outs["v_s"].append(v.reshape(bs, 1, ATT_HEADS, HEAD_W))
            outs["re_s"].append(h_re.reshape(bs, S5_GROUPS, S5_STATE))
            outs["im_s"].append(h_im.reshape(bs, S5_GROUPS, S5_STATE))
        else:
            o_ = li // 2
            sp_ = _ssd_params(ssd_w_in[o_], ssd_conv_w[o_], ssd_conv_b[o_], ssd_dt_bias[o_], ssd_a_log[o_], ssd_d[o_],
                              ssd_norm[o_], ssd_w_out[o_])
            z, xc, dtg, tail = _ssd_in(y_p, norm_mix[li], sp_, sp, 256, "ssd_in_p")
            yssd, st = _ssd_prompt(xc.reshape(bp, sp, SSD_CONV_DIM), dtg.reshape(bp, sp, SSD_GROUPS * LANES), sp_)
            y_p = _gated_norm_matmul(y_p, yssd.reshape(tp, SSD_INNER), z, sp_["norm"], sp_["w_out"], 256, "ssd_out_p")
            outs["ssd_p"].append(st)
            outs["cv_p"].append(tail[:, _CONV_PAD - (SSD_CONV - 1):, :])
            z, xbc, dtc = _norm_matmul(y_s, norm_mix[li], [sp_["w_z_f32"], sp_["w_xbc_f32"], sp_["w_dt_f32"]],
                                       [(F32,), (F32,), (F32,)], bs, "ssd_in_s", True)
            dt64, dt128 = _dt_expand(dtc, sp_)
            yssd, st = _ssd_step(xbc, state_conv[o_], dt64, dt128, state_ssd[o_], sp_)
            y_s = _gated_norm_matmul(y_s, yssd, z, sp_["norm"], sp_["w_out_f32"], bs, "ssd_out_s", True)
            outs["ssd_s"].append(st)
            outs["cv_s"].append(jnp.concatenate([state_conv[o_][:, 1:], xbc[:, None, :]], axis=1))
        mp = _moe_params(moe_w_group[li], moe_b_group[li], moe_w_expert[li], moe_b_expert[li], moe_w_gate[li],
                         moe_w_up[li], moe_w_down[li])
        y_p = _moe_sparse(y_p, norm_ffn[li], mp, _TM, 256, "p")
        y_s = _moe(y_s, norm_ffn[li], mp, bs, bs, "s", li + 1 < depth)

    y_prompt = _rmsnorm(y_p, norm_final, _TM, "final_p").reshape(bp, sp, d)
    y_sample = _rmsnorm(y_s, norm_final, bs, "final_s").reshape(bs, 1, d)
    st = lambda n: jnp.stack(outs[n])
    return (y_prompt, y_sample, st("k_p"), st("v_p"), st("k_s"), st("v_s"), st("re_p"), st("im_p"), st("re_s"),
            st("im_s"), st("ssd_p"), st("ssd_s"), st("cv_p"), st("cv_s"))
```

```python
import functools
import math

import jax
import jax.numpy as jnp
from jax import lax
from jax.experimental import pallas as pl
from jax.experimental.pallas import tpu as pltpu

F32 = jnp.float32
BF16 = jnp.bfloat16
HIGHEST = lax.Precision.HIGHEST

D_MODEL = 1024
NORM_EPS = 1e-6
PAGE_SIZE = 128
ATT_HEADS = 4
ATT_D = 64
ATT_WIDTH = ATT_HEADS * 2 * ATT_D
HEAD_W = 2 * ATT_D
S5_WIDTH = 512
S5_GROUP = 16
S5_GROUPS = 32
S5_STATE = 64
S5_LANES = S5_GROUPS * S5_STATE
SSD_INNER = 2048
SSD_HEADDIM = 64
SSD_HEADS = 32
SSD_GROUPS = 4
SSD_HPG = 8
SSD_STATE = 128
SSD_CONV = 4
SSD_CONV_DIM = SSD_INNER + 2 * SSD_GROUPS * SSD_STATE
SSD_CHUNK = 128
SSD_GW = SSD_HPG * SSD_HEADDIM
MOE_GROUPS = 4
MOE_EPG = 8
MOE_EXPERTS = 32
MOE_FF = 256
LANES = 128
NEG = -1e30

_NT = (((1,), (1,)), ((), ()))
_TN = (((0,), (0,)), ((), ()))


def _rms(x, g):
    return x * lax.rsqrt(jnp.mean(x * x, axis=-1, keepdims=True) + NORM_EPS) * g


def _split(x):
    hi = x.astype(BF16)
    return hi, (x - hi.astype(F32)).astype(BF16)


def _mm(a, b, precise=False, dims=None):
    dot = jnp.dot if dims is None else functools.partial(lax.dot_general, dimension_numbers=dims)
    if not precise:
        return dot(a.astype(BF16), b.astype(BF16), preferred_element_type=F32)
    m = a.shape[0]
    a_hi, a_lo = _split(a)
    b_hi, b_lo = _split(b)
    r = dot(jnp.concatenate([a_hi, a_lo], axis=0), b_hi, preferred_element_type=F32)
    return r[:m] + r[m:] + dot(a_hi, b_lo, preferred_element_type=F32)


def _norm_matmul_kernel(x_ref, g_ref, *refs, out_dtypes, precise):
    n = len(out_dtypes)
    w_refs = refs[:n]
    o_refs = refs[n:]
    h = _rms(x_ref[...], g_ref[...])
    if not precise:
        h = h.astype(BF16)
    k = 0
    for w_ref, dts in zip(w_refs, out_dtypes):
        r = _mm(h, w_ref[...], precise)
        for dt in dts:
            if dt == _TILE_T:
                o_refs[k][0] = r.T.astype(BF16)
            else:
                o_refs[k][...] = r.astype(dt)
            k += 1


_TILE_T = "bf16 row tiles, each transposed"


def _norm_matmul(x, g, ws, out_dtypes, tm, name, precise=False):
    m, kdim = x.shape
    out_shape, out_specs = [], []
    for w, dts in zip(ws, out_dtypes):
        for dt in dts:
            if dt == _TILE_T:
                out_shape.append(jax.ShapeDtypeStruct((m // tm, w.shape[1], tm), BF16))
                out_specs.append(pl.BlockSpec((1, w.shape[1], tm), lambda i: (i, 0, 0)))
            else:
                out_shape.append(jax.ShapeDtypeStruct((m, w.shape[1]), dt))
                out_specs.append(pl.BlockSpec((tm, w.shape[1]), lambda i: (i, 0)))
    return pl.pallas_call(
        functools.partial(_norm_matmul_kernel, out_dtypes=out_dtypes, precise=precise),
        grid=(m // tm,),
        in_specs=[pl.BlockSpec((tm, kdim), lambda i: (i, 0)), pl.BlockSpec((1, kdim), lambda i: (0, 0))]
        + [pl.BlockSpec(w.shape, lambda i: (0, 0)) for w in ws],
        out_specs=out_specs,
        out_shape=out_shape,
        compiler_params=pltpu.CompilerParams(dimension_semantics=("parallel",)),
        name=name,
    )(x, g.reshape(1, kdim), *ws)


def _matmul_res_kernel(res_ref, *refs, precise):
    n = (len(refs) - 1) // 2
    acc = res_ref[...]
    for a_ref, w_ref in zip(refs[:n], refs[n:2 * n]):
        acc = acc + _mm(a_ref[...], w_ref[...], precise)
    refs[-1][...] = acc


def _matmul_res(res, a_list, w_list, tm, name, precise=False):
    m, n = res.shape
    return pl.pallas_call(
        functools.partial(_matmul_res_kernel, precise=precise),
        grid=(m // tm,),
        in_specs=[pl.BlockSpec((tm, n), lambda i: (i, 0))]
        + [pl.BlockSpec((tm, a.shape[1]), lambda i: (i, 0)) for a in a_list]
        + [pl.BlockSpec(w.shape, lambda i: (0, 0)) for w in w_list],
        out_specs=pl.BlockSpec((tm, n), lambda i: (i, 0)),
        out_shape=jax.ShapeDtypeStruct((m, n), F32),
        compiler_params=pltpu.CompilerParams(dimension_semantics=("parallel",)),
        name=name,
    )(res, *a_list, *w_list)


def _gated_norm_matmul_kernel(res_ref, y_ref, z_ref, g_ref, w_ref, o_ref, *, precise):
    y = y_ref[...] * jax.nn.silu(z_ref[...])
    o_ref[...] = res_ref[...] + _mm(_rms(y, g_ref[...]), w_ref[...], precise)


def _gated_norm_matmul(res, y, z, g, w, tm, name, precise=False):
    m, n = res.shape
    kdim = y.shape[1]
    return pl.pallas_call(
        functools.partial(_gated_norm_matmul_kernel, precise=precise),
        grid=(m // tm,),
        in_specs=[pl.BlockSpec((tm, n), lambda i: (i, 0)), pl.BlockSpec((tm, kdim), lambda i: (i, 0)),
                  pl.BlockSpec((tm, kdim), lambda i: (i, 0)), pl.BlockSpec((1, kdim), lambda i: (0, 0)),
                  pl.BlockSpec(w.shape, lambda i: (0, 0))],
        out_specs=pl.BlockSpec((tm, n), lambda i: (i, 0)),
        out_shape=jax.ShapeDtypeStruct((m, n), F32),
        compiler_params=pltpu.CompilerParams(dimension_semantics=("parallel",)),
        name=name,
    )(res, y, z, g.reshape(1, kdim), w)


def _rmsnorm_kernel(x_ref, g_ref, o_ref):
    o_ref[...] = _rms(x_ref[...], g_ref[...])


def _rmsnorm(x, g, tm, name):
    m, n = x.shape
    return pl.pallas_call(
        _rmsnorm_kernel,
        grid=(m // tm,),
        in_specs=[pl.BlockSpec((tm, n), lambda i: (i, 0)), pl.BlockSpec((1, n), lambda i: (0, 0))],
        out_specs=pl.BlockSpec((tm, n), lambda i: (i, 0)),
        out_shape=jax.ShapeDtypeStruct((m, n), F32),
        compiler_params=pltpu.CompilerParams(dimension_semantics=("parallel",)),
        name=name,
    )(x, g.reshape(1, n))


def _split_q(q):
    lane = lax.broadcasted_iota(jnp.int32, q.shape, 1) % HEAD_W
    scale = ATT_D ** -0.5
    qs = q * jnp.asarray(scale, q.dtype)
    zero = jnp.zeros_like(qs)
    return jnp.where(lane < ATT_D, qs, zero), jnp.where(lane >= ATT_D, qs, zero)


def _subln(o, g, out_scale):
    return _rms(o, g) * out_scale


def _attn_prompt_kernel(lam_ref, q_ref, k_ref, vt_ref, g_ref, o_ref, qt_sc, m_sc, l_sc, acc_sc, *, tq, tk, out_scale):
    qi = pl.program_id(1)
    heads = range(ATT_HEADS)
    for h in heads:
        q1, q2 = _split_q(q_ref[0, :, h * HEAD_W:(h + 1) * HEAD_W])
        qt_sc[h] = jnp.concatenate([q1, q2], axis=0).astype(F32).T.astype(BF16)
    m_sc[...] = jnp.full(m_sc.shape, NEG, F32)
    l_sc[...] = jnp.zeros(l_sc.shape, F32)
    acc_sc[...] = jnp.zeros(acc_sc.shape, F32)

    def kv_step(j, masked):
        start = pl.multiple_of(j * tk, tk)
        for h in heads:
            kt = k_ref[0, pl.ds(start, tk), h * HEAD_W:(h + 1) * HEAD_W]
            st = jnp.dot(kt, qt_sc[h], preferred_element_type=F32)
            if masked:
                k_pos = start + lax.broadcasted_iota(jnp.int32, st.shape, 0)
                q_pos = qi * tq + lax.broadcasted_iota(jnp.int32, st.shape, 1) % tq
                st = jnp.where(k_pos <= q_pos, st, NEG)
            m_old = m_sc[h]
            m_new = jnp.maximum(m_old, jnp.max(st, axis=0, keepdims=True))
            p = jnp.exp(st - m_new)
            alpha = jnp.exp(m_old - m_new)
            l_sc[h] = alpha * l_sc[h] + jnp.sum(p, axis=0, keepdims=True)
            m_sc[h] = m_new
            vt = vt_ref[j, h * HEAD_W:(h + 1) * HEAD_W, :]
            acc_sc[h] = alpha * acc_sc[h] + jnp.dot(vt, p.astype(BF16), preferred_element_type=F32)

    n_full = (qi * tq) // tk

    def full_step(j, carry):
        kv_step(j, False)
        return carry

    lax.fori_loop(0, n_full, full_step, 0)
    kv_step(n_full, True)
    for h in heads:
        acc, l = acc_sc[h], l_sc[h]
        ot = acc[:, 0:tq] / l[:, 0:tq] - lam_ref[0, 0] * (acc[:, tq:2 * tq] / l[:, tq:2 * tq])
        o_ref[0, :, h * HEAD_W:(h + 1) * HEAD_W] = _subln(ot.T, g_ref[...], out_scale).astype(o_ref.dtype)


def _attn_prompt(q, k, vt, lam, subln, out_scale, tq=512):
    b, s, _ = q.shape
    tk = vt.shape[2]
    nblk = s // tk
    kern = functools.partial(_attn_prompt_kernel, tq=tq, tk=tk, out_scale=out_scale)
    return pl.pallas_call(
        kern,
        grid=(b, s // tq),
        in_specs=[pl.BlockSpec(memory_space=pltpu.SMEM),
                  pl.BlockSpec((1, tq, ATT_WIDTH), lambda bi, i: (bi, i, 0)),
                  pl.BlockSpec((1, s, ATT_WIDTH), lambda bi, i: (bi, 0, 0)),
                  pl.BlockSpec((nblk, ATT_WIDTH, tk), lambda bi, i: (bi, 0, 0)),
                  pl.BlockSpec((1, HEAD_W), lambda bi, i: (0, 0))],
        out_specs=pl.BlockSpec((1, tq, ATT_WIDTH), lambda bi, i: (bi, i, 0)),
        out_shape=jax.ShapeDtypeStruct((b, s, ATT_WIDTH), BF16),
        scratch_shapes=[pltpu.VMEM((ATT_HEADS, HEAD_W, 2 * tq), BF16), pltpu.VMEM((ATT_HEADS, 1, 2 * tq), F32),
                        pltpu.VMEM((ATT_HEADS, 1, 2 * tq), F32), pltpu.VMEM((ATT_HEADS, HEAD_W, 2 * tq), F32)],
        compiler_params=pltpu.CompilerParams(dimension_semantics=("parallel", "parallel")),
        name="attn_prompt",
    )(lam.reshape(1, 1), q, k, vt, subln.reshape(1, HEAD_W))


_DEC_PAGES = 16
_DEC_PROW = PAGE_SIZE * ATT_HEADS
_DEC_COLS = _DEC_PAGES * _DEC_PROW
_DEC_ROWS = 2 * ATT_HEADS


def _attn_decode_kernel(pt_ref, lam_ref, q_ref, kn_ref, vn_ref, g_ref, *refs, steps, out_scale):
    del pt_ref
    k_refs = refs[:_DEC_PAGES]
    v_refs = refs[_DEC_PAGES:2 * _DEC_PAGES]
    o_ref = refs[2 * _DEC_PAGES]
    s_sc, m_sc, wn_sc, acc_sc = refs[2 * _DEC_PAGES + 1:]
    j = pl.program_id(1)
    lam = lam_ref[0, 0]

    q4 = q_ref[0] * (ATT_D ** -0.5)
    q8 = jnp.concatenate([q4, q4], axis=0)
    r8 = lax.broadcasted_iota(jnp.int32, (_DEC_ROWS, HEAD_W), 0)
    l8 = lax.broadcasted_iota(jnp.int32, (_DEC_ROWS, HEAD_W), 1)
    q8 = jnp.where(l8 // ATT_D == r8 // ATT_HEADS, q8, 0.0)

    @pl.when(j == 0)
    def _():
        m_sc[...] = jnp.full(m_sc.shape, NEG, F32)

    @pl.when(j < steps)
    def _():
        s = jnp.concatenate([_mm(q8, k_ref[0], True, _NT) for k_ref in k_refs], axis=1)
        rs = lax.broadcasted_iota(jnp.int32, s.shape, 0)
        cs = lax.broadcasted_iota(jnp.int32, s.shape, 1)
        s = jnp.where(cs % ATT_HEADS == rs % ATT_HEADS, s, NEG)
        s_sc[j] = s
        m_sc[...] = jnp.maximum(m_sc[...], jnp.max(s, axis=-1, keepdims=True))

    @pl.when(j == steps - 1)
    def _():
        kn = jnp.concatenate([kn_ref[0], kn_ref[0]], axis=0)
        sn = jnp.sum(q8 * kn, axis=-1, keepdims=True)
        m = jnp.maximum(m_sc[...], sn)
        pn = jnp.exp(sn - m)
        l = pn
        for t in range(steps):
            p = jnp.exp(s_sc[t] - m)
            s_sc[t] = p
            l = l + jnp.sum(p, axis=-1, keepdims=True)
        zeros = jnp.zeros((ATT_HEADS, _DEC_COLS), F32)
        for t in range(steps):
            p = s_sc[t] / l
            s_sc[t] = jnp.concatenate([p[:ATT_HEADS] - lam * p[ATT_HEADS:], zeros], axis=0)
        pn = pn / l
        wn_sc[...] = jnp.concatenate([pn[:ATT_HEADS] - lam * pn[ATT_HEADS:], jnp.zeros((ATT_HEADS, 1), F32)], axis=0)

    @pl.when(j == steps)
    def _():
        acc_sc[...] = jnp.zeros(acc_sc.shape, F32)

    @pl.when(j >= steps)
    def _():
        w = s_sc[j - steps]
        acc = acc_sc[...]
        for i, v_ref in enumerate(v_refs):
            acc = acc + _mm(w[:, i * _DEC_PROW:(i + 1) * _DEC_PROW], v_ref[0], True)
        acc_sc[...] = acc

    @pl.when(j == 2 * steps - 1)
    def _():
        o4 = acc_sc[:ATT_HEADS] + wn_sc[:ATT_HEADS] * vn_ref[0]
        o_ref[0] = _subln(o4, g_ref[...], out_scale)


def _attn_decode(q, k_new, v_new, cache_k, cache_v, page_table, lam, subln, out_scale):
    b = q.shape[0]
    n_pages = page_table.shape[1]
    steps = n_pages // _DEC_PAGES
    ck = cache_k.reshape(-1, _DEC_PROW, HEAD_W)
    cv = cache_v.reshape(-1, _DEC_PROW, HEAD_W)

    def k_map(i):
        return lambda bi, j, pt: (pt[bi, jnp.minimum(j, steps - 1) * _DEC_PAGES + i], 0, 0)

    def v_map(i):
        return lambda bi, j, pt: (pt[bi, jnp.maximum(j - steps, 0) * _DEC_PAGES + i], 0, 0)

    row = lambda bi, j, pt: (bi, 0, 0)
    head_rows = pl.BlockSpec((1, ATT_HEADS, HEAD_W), row)
    grid_spec = pltpu.PrefetchScalarGridSpec(
        num_scalar_prefetch=1,
        grid=(b, 2 * steps),
        in_specs=[pl.BlockSpec(memory_space=pltpu.SMEM), head_rows, head_rows, head_rows,
                  pl.BlockSpec((1, HEAD_W), lambda bi, j, pt: (0, 0))]
        + [pl.BlockSpec((1, _DEC_PROW, HEAD_W), k_map(i)) for i in range(_DEC_PAGES)]
        + [pl.BlockSpec((1, _DEC_PROW, HEAD_W), v_map(i)) for i in range(_DEC_PAGES)],
        out_specs=head_rows,
        scratch_shapes=[pltpu.VMEM((steps, _DEC_ROWS, _DEC_COLS), F32), pltpu.VMEM((_DEC_ROWS, 1), F32),
                        pltpu.VMEM((_DEC_ROWS, 1), F32), pltpu.VMEM((_DEC_ROWS, HEAD_W), F32)],
    )
    out = pl.pallas_call(
        functools.partial(_attn_decode_kernel, steps=steps, out_scale=out_scale),
        grid_spec=grid_spec,
        out_shape=jax.ShapeDtypeStruct((b, ATT_HEADS, HEAD_W), F32),
        compiler_params=pltpu.CompilerParams(dimension_semantics=("parallel", "arbitrary")),
        name="attn_decode",
    )(page_table, lam.reshape(1, 1), q.reshape(b, ATT_HEADS, HEAD_W), k_new.reshape(b, ATT_HEADS, HEAD_W),
      v_new.reshape(b, ATT_HEADS, HEAD_W), subln.reshape(1, HEAD_W), *([ck] * _DEC_PAGES), *([cv] * _DEC_PAGES))
    return out.reshape(b, ATT_WIDTH)


_S5_KT = 2
_S5_KW = S5_WIDTH // _S5_KT
_S5_KL = S5_LANES // _S5_KT
_S5_NT = S5_LANES // LANES
_S5_SCAN_TILES = 8


def _s5_input(u, bmat_ref, precise):
    re, im = [], []
    for kt in range(_S5_KT):
        r = _mm(u[:, kt * _S5_KW:(kt + 1) * _S5_KW], bmat_ref[kt], precise)
        re.append(r[:, :_S5_KL])
        im.append(r[:, _S5_KL:])
    return jnp.concatenate(re, axis=1), jnp.concatenate(im, axis=1)


def _s5_output(h_re, h_im, u, cmat_ref, d_ref, wglu_ref, bglu_ref, precise):
    if not precise:
        h_re, h_im = h_re.astype(BF16), h_im.astype(BF16)
    ys = []
    for kt in range(_S5_KT):
        sl = slice(kt * _S5_KL, (kt + 1) * _S5_KL)
        y = _mm(h_re[:, sl], cmat_ref[kt, :_S5_KL, :], precise)
        y = y + _mm(h_im[:, sl], cmat_ref[kt, _S5_KL:, :], precise)
        ys.append(y)
    y = jnp.concatenate(ys, axis=1) + d_ref[...] * u
    g = jax.nn.gelu(y)
    gate = _mm(g, wglu_ref[...], precise) + bglu_ref[...]
    return g * jax.nn.sigmoid(gate)


def _s5_scan_kernel(u_ref, are_ref, aim_ref, bmat_ref, cmat_ref, d_ref, wglu_ref, bglu_ref,
                    o_ref, hre_ref, him_ref, bu_re, bu_im, *, nb, lc, bp):
    c = pl.program_id(0)

    @pl.when(c == 0)
    def _():
        bu_re[...] = jnp.zeros(bu_re.shape, F32)
        bu_im[...] = jnp.zeros(bu_im.shape, F32)
        hre_ref[...] = jnp.zeros(hre_ref.shape, F32)
        him_ref[...] = jnp.zeros(him_ref.shape, F32)

    for b in range(nb):
        re, im = _s5_input(u_ref[b], bmat_ref, False)
        for jt in range(_S5_NT):
            bu_re[jt, pl.ds(b, lc, stride=bp), :] = re[:, jt * LANES:(jt + 1) * LANES]
            bu_im[jt, pl.ds(b, lc, stride=bp), :] = im[:, jt * LANES:(jt + 1) * LANES]

    for j0 in range(0, _S5_NT, _S5_SCAN_TILES):
        tiles = pl.ds(j0, _S5_SCAN_TILES)
        a_re = jnp.broadcast_to(are_ref[tiles], (_S5_SCAN_TILES, bp, LANES))
        a_im = jnp.broadcast_to(aim_ref[tiles], (_S5_SCAN_TILES, bp, LANES))

        def step(t, carry, tiles=tiles, a_re=a_re, a_im=a_im):
            hr, hi = carry
            rows = pl.ds(pl.multiple_of(t * bp, bp), bp)
            nr = hr * a_re - hi * a_im + bu_re[tiles, rows, :]
            ni = hr * a_im + hi * a_re + bu_im[tiles, rows, :]
            bu_re[tiles, rows, :] = nr
            bu_im[tiles, rows, :] = ni
            return nr, ni

        hr, hi = lax.fori_loop(0, lc, step, (hre_ref[tiles], him_ref[tiles]), unroll=2)
        hre_ref[tiles] = hr
        him_ref[tiles] = hi

    for b in range(nb):
        h_re = jnp.concatenate([bu_re[jt, pl.ds(b, lc, stride=bp), :] for jt in range(_S5_NT)], axis=1)
        h_im = jnp.concatenate([bu_im[jt, pl.ds(b, lc, stride=bp), :] for jt in range(_S5_NT)], axis=1)
        o_ref[b] = _s5_output(h_re, h_im, u_ref[b], cmat_ref, d_ref, wglu_ref, bglu_ref, False).astype(o_ref.dtype)


def _s5_prompt(u, prm, lc=128):
    nb, l, _ = u.shape
    bp = 8
    full = lambda shape: pl.BlockSpec(shape, lambda c: (0,) * len(shape))
    o, h_re, h_im = pl.pallas_call(
        functools.partial(_s5_scan_kernel, nb=nb, lc=lc, bp=bp),
        grid=(l // lc,),
        in_specs=[pl.BlockSpec((nb, lc, S5_WIDTH), lambda c: (0, c, 0)),
                  full((_S5_NT, 1, LANES)), full((_S5_NT, 1, LANES)),
                  full((_S5_KT, _S5_KW, 2 * _S5_KL)), full((_S5_KT, 2 * _S5_KL, _S5_KW)),
                  full((1, S5_WIDTH)), full((S5_WIDTH, S5_WIDTH)), full((1, S5_WIDTH))],
        out_specs=[pl.BlockSpec((nb, lc, S5_WIDTH), lambda c: (0, c, 0)),
                   full((_S5_NT, bp, LANES)), full((_S5_NT, bp, LANES))],
        out_shape=[jax.ShapeDtypeStruct((nb, l, S5_WIDTH), BF16),
                   jax.ShapeDtypeStruct((_S5_NT, bp, LANES), F32), jax.ShapeDtypeStruct((_S5_NT, bp, LANES), F32)],
        scratch_shapes=[pltpu.VMEM((_S5_NT, lc * bp, LANES), F32) for _ in range(2)],
        compiler_params=pltpu.CompilerParams(dimension_semantics=("arbitrary",)),
        name="s5_prompt",
    )(u, prm["a_re"].reshape(_S5_NT, 1, LANES), prm["a_im"].reshape(_S5_NT, 1, LANES), prm["bmat"].astype(BF16),
      prm["cmat"].astype(BF16), prm["d"], prm["w_glu"].astype(BF16), prm["b_glu"])
    rows = lambda h: jnp.transpose(h, (1, 0, 2)).reshape(bp, S5_LANES)[:nb]
    return o, rows(h_re), rows(h_im)


def _s5_step_kernel(u_ref, h0re_ref, h0im_ref, are_ref, aim_ref, bmat_ref, cmat_ref, d_ref, wglu_ref, bglu_ref,
                    o_ref, hre_ref, him_ref):
    u = u_ref[...]
    bu_re, bu_im = _s5_input(u, bmat_ref, True)
    h_re, h_im = h0re_ref[...], h0im_ref[...]
    a_re, a_im = are_ref[...], aim_ref[...]
    n_re = h_re * a_re - h_im * a_im + bu_re
    n_im = h_re * a_im + h_im * a_re + bu_im
    hre_ref[...] = n_re
    him_ref[...] = n_im
    o_ref[...] = _s5_output(n_re, n_im, u, cmat_ref, d_ref, wglu_ref, bglu_ref, True)


def _s5_step(u, h0_re, h0_im, prm):
    nb = u.shape[0]
    return pl.pallas_call(
        _s5_step_kernel,
        out_shape=[jax.ShapeDtypeStruct((nb, S5_WIDTH), F32),
                   jax.ShapeDtypeStruct((nb, S5_LANES), F32), jax.ShapeDtypeStruct((nb, S5_LANES), F32)],
        name="s5_step",
    )(u, h0_re, h0_im, prm["a_re"], prm["a_im"], prm["bmat"], prm["cmat"], prm["d"], prm["w_glu"], prm["b_glu"])


def _s5_params(lam_re, lam_im, log_dt, b_re, b_im, c_re, c_im, d, w_glu, b_glu):
    lam = lax.complex(lam_re, lam_im)
    dt = jnp.exp(log_dt)[:, None]
    a_bar = jnp.exp(lam * dt)
    b_bar = ((a_bar - 1.0) / lam)[..., None] * lax.complex(b_re, b_im)
    gk = S5_GROUPS // _S5_KT
    eye = jnp.eye(gk, dtype=F32)

    def in_tile(x):
        x = x.reshape(_S5_KT, gk, S5_STATE, S5_GROUP)
        return jnp.einsum("kgpc,gh->kgchp", x, eye).reshape(_S5_KT, _S5_KW, _S5_KL)

    def out_tile(x):
        x = x.reshape(_S5_KT, gk, S5_GROUP, S5_STATE)
        return jnp.einsum("kgcp,gh->kgphc", x, eye).reshape(_S5_KT, _S5_KL, _S5_KW)

    bmat = jnp.concatenate([in_tile(jnp.real(b_bar)), in_tile(jnp.imag(b_bar))], axis=2)
    cmat = jnp.concatenate([out_tile(c_re), out_tile(-c_im)], axis=1)
    return {
        "a_re": jnp.real(a_bar).reshape(1, S5_LANES), "a_im": jnp.imag(a_bar).reshape(1, S5_LANES),
        "bmat": bmat, "cmat": cmat, "d": d.reshape(1, S5_WIDTH), "w_glu": w_glu, "b_glu": b_glu.reshape(1, S5_WIDTH),
    }


_CONV_PAD = 8


def _conv_silu(x, prev_ref, w_ref, b_ref):
    rows = x.shape[0]
    w = w_ref[...]
    prev = prev_ref[...]
    sub = lax.broadcasted_iota(jnp.int32, prev.shape, 0)
    out = b_ref[...]
    for i in range(SSD_CONV):
        s = SSD_CONV - 1 - i
        if s == 0:
            shifted = x
        else:
            rolled = pltpu.roll(x, s, 0)
            head = jnp.where(sub < s, pltpu.roll(prev, s, 0), rolled[0:_CONV_PAD])
            shifted = jnp.concatenate([head, rolled[_CONV_PAD:]], axis=0)
        out = out + w[i:i + 1] * shifted
    prev_ref[...] = x[rows - _CONV_PAD:rows, :]
    return jax.nn.silu(out)


def _ssd_in_kernel(x_ref, g_ref, wz_ref, wx_ref, wdt_ref, cw_ref, cb_ref, z_ref, xc_ref, dt_ref, tail_ref, xf,
                   *, tiles_per_seq):
    i = pl.program_id(0)
    tm = x_ref.shape[0]

    @pl.when(i % tiles_per_seq == 0)
    def _():
        xf[...] = jnp.zeros(xf.shape, F32)

    h = _rms(x_ref[...], g_ref[...]).astype(BF16)
    z_ref[...] = _mm(h, wz_ref[...])
    dt_ref[...] = _mm(h, wdt_ref[...])
    xbc = _mm(h, wx_ref[...])
    tail_ref[0] = xbc[tm - _CONV_PAD:tm, :]
    xc_ref[...] = _conv_silu(xbc, xf, cw_ref, cb_ref)


def _ssd_in(x, g, prm, seq_len, tm, name):
    m, kdim = x.shape
    tiles_per_seq = seq_len // tm
    full = lambda a: pl.BlockSpec(a.shape, lambda i: (0,) * a.ndim)
    rows = lambda n: pl.BlockSpec((tm, n), lambda i: (i, 0))
    wz, wx, wdt = prm["w_z"], prm["w_xbc"], prm["w_dt_g"]
    return pl.pallas_call(
        functools.partial(_ssd_in_kernel, tiles_per_seq=tiles_per_seq),
        grid=(m // tm,),
        in_specs=[rows(kdim), pl.BlockSpec((1, kdim), lambda i: (0, 0)), full(wz), full(wx), full(wdt),
                  full(prm["conv_w"]), full(prm["conv_b"])],
        out_specs=[rows(wz.shape[1]), rows(wx.shape[1]), rows(wdt.shape[1]),
                   pl.BlockSpec((1, _CONV_PAD, wx.shape[1]), lambda i: (i // tiles_per_seq, 0, 0))],
        out_shape=[jax.ShapeDtypeStruct((m, wz.shape[1]), F32), jax.ShapeDtypeStruct((m, wx.shape[1]), F32),
                   jax.ShapeDtypeStruct((m, wdt.shape[1]), F32),
                   jax.ShapeDtypeStruct((m // seq_len, _CONV_PAD, wx.shape[1]), F32)],
        scratch_shapes=[pltpu.VMEM((_CONV_PAD, wx.shape[1]), F32)],
        compiler_params=pltpu.CompilerParams(dimension_semantics=("arbitrary",)),
        name=name,
    )(x, g.reshape(1, kdim), wz, wx, wdt, prm["conv_w"], prm["conv_b"])


def _ssd_chunk_kernel(xs_ref, bm_ref, cm_ref, dt_ref, dtb_ref, a_ref, dexp_ref, y_ref, st_ref):
    c = pl.program_id(2)
    q = SSD_CHUNK

    @pl.when(c == 0)
    def _():
        st_ref[...] = jnp.zeros(st_ref.shape, F32)

    xs = xs_ref[0]
    bm16 = bm_ref[0].astype(BF16)
    cm16 = cm_ref[0].astype(BF16)

    dt = jax.nn.softplus(dt_ref[0] + dtb_ref[0])
    da = dt * a_ref[0]
    row = lax.broadcasted_iota(jnp.int32, (q, q), 0)
    col = lax.broadcasted_iota(jnp.int32, (q, q), 1)
    causal = row >= col
    acs = jnp.dot(causal.astype(F32), da, preferred_element_type=F32, precision=HIGHEST)
    acs_t = acs.T
    acs_last = acs[q - 1:q, :]
    e_acs = jnp.exp(acs)
    dec_s = jnp.exp(acs_last - acs)
    chunk_dec = jnp.exp(acs_last)
    cb = _mm(cm16, bm16, dims=_NT)

    first = lax.broadcasted_iota(jnp.int32, (q, LANES), 1) < SSD_HEADDIM

    def pair_cols(v, i):
        return jnp.where(first, v[:, 2 * i:2 * i + 1], v[:, 2 * i + 1:2 * i + 2])

    ys = []
    for i in range(SSD_HPG // 2):
        pair = slice(i * LANES, (i + 1) * LANES)
        xdt = xs[:, pair] * pair_cols(dt, i)
        xdt16 = xdt.astype(BF16)
        y_head = []
        for j in (2 * i, 2 * i + 1):
            seg = acs[:, j:j + 1] - acs_t[j:j + 1, :]
            lmat = jnp.exp(jnp.where(causal, seg, -jnp.inf))
            y_head.append(_mm(cb * lmat, xdt16))
        y_diag = jnp.where(first, y_head[0], y_head[1])
        h = st_ref[0, 2 * i:2 * i + 2].reshape(LANES, SSD_STATE)
        y_off = _mm(cm16, h, dims=_NT) * pair_cols(e_acs, i)
        cdec = jnp.concatenate([jnp.broadcast_to(chunk_dec[:, j:j + 1], (SSD_HEADDIM, SSD_STATE))
                                for j in (2 * i, 2 * i + 1)], axis=0)
        h_new = cdec * h + _mm(xdt * pair_cols(dec_s, i), bm16, dims=_TN)
        st_ref[0, 2 * i:2 * i + 2] = h_new.reshape(2, SSD_HEADDIM, SSD_STATE)
        ys.append(y_diag + y_off)
    y_ref[0] = jnp.concatenate(ys, axis=1) + dexp_ref[...] * xs


def _ssd_prompt(xc, dtg, prm):
    nb, l, _ = xc.shape
    nc = l // SSD_CHUNK
    q = SSD_CHUNK
    boff = SSD_INNER // LANES
    coff = boff + SSD_GROUPS
    grp = lambda b, g, c: (g, 0, 0)
    return pl.pallas_call(
        _ssd_chunk_kernel,
        grid=(nb, SSD_GROUPS, nc),
        in_specs=[pl.BlockSpec((1, q, SSD_GW), lambda b, g, c: (b, c, g)),
                  pl.BlockSpec((1, q, LANES), lambda b, g, c: (b, c, boff + g)),
                  pl.BlockSpec((1, q, LANES), lambda b, g, c: (b, c, coff + g)),
                  pl.BlockSpec((1, q, LANES), lambda b, g, c: (b, c, g)),
                  pl.BlockSpec((1, 1, LANES), grp), pl.BlockSpec((1, 1, LANES), grp),
                  pl.BlockSpec((1, SSD_GW), lambda b, g, c: (0, g))],
        out_specs=[pl.BlockSpec((1, q, SSD_GW), lambda b, g, c: (b, c, g)),
                   pl.BlockSpec((1, SSD_HPG, SSD_HEADDIM, SSD_STATE), lambda b, g, c: (b, g, 0, 0))],
        out_shape=[jax.ShapeDtypeStruct((nb, l, SSD_INNER), F32),
                   jax.ShapeDtypeStruct((nb, SSD_HEADS, SSD_HEADDIM, SSD_STATE), F32)],
        compiler_params=pltpu.CompilerParams(dimension_semantics=("parallel", "parallel", "arbitrary")),
        name="ssd_prompt",
    )(xc, xc, xc, dtg, prm["dt_bias_g"], prm["a_g"], prm["d_exp"])


def _ssd_step_kernel(x_ref, buf_ref, cw_ref, cb_ref, dt64_ref, dtb64_ref, dt128_ref, dtb128_ref, a128_ref,
                     dexp_ref, st_ref, y_ref, so_ref):
    w = cw_ref[...]
    buf = buf_ref[0]
    conv = cb_ref[...]
    for i in range(SSD_CONV - 1):
        conv = conv + w[i:i + 1] * buf[i:i + 1]
    conv = conv + w[SSD_CONV - 1:SSD_CONV] * x_ref[0]
    xc = jax.nn.silu(conv)
    xs = xc[:, :SSD_INNER]
    xdt = xs * jax.nn.softplus(dt64_ref[0] + dtb64_ref[...])
    dt = jax.nn.softplus(dt128_ref[0] + dtb128_ref[...])
    dec = jnp.exp(dt * a128_ref[...])
    row = lax.broadcasted_iota(jnp.int32, (LANES, LANES), 0)
    col = lax.broadcasted_iota(jnp.int32, (LANES, LANES), 1)
    diag = row == col
    ys = []
    for i in range(SSD_HEADS // 2):
        g = (2 * i) // SSD_HPG
        bm = xc[:, SSD_INNER + g * SSD_STATE:SSD_INNER + (g + 1) * SSD_STATE]
        cm = xc[:, SSD_INNER + (SSD_GROUPS + g) * SSD_STATE:SSD_INNER + (SSD_GROUPS + g + 1) * SSD_STATE]
        xp = jnp.broadcast_to(xdt[:, i * LANES:(i + 1) * LANES], (LANES, LANES))
        outer = _mm(jnp.where(diag, xp, 0.0), jnp.broadcast_to(bm, (LANES, LANES)), True)
        h = st_ref[0, 2 * i:2 * i + 2].reshape(LANES, SSD_STATE)
        dpair = jnp.concatenate([jnp.broadcast_to(dec[2 * i:2 * i + 1], (SSD_HEADDIM, LANES)),
                                 jnp.broadcast_to(dec[2 * i + 1:2 * i + 2], (SSD_HEADDIM, LANES))], axis=0)
        hn = dpair * h + outer
        so_ref[0, 2 * i:2 * i + 2] = hn.reshape(2, SSD_HEADDIM, SSD_STATE)
        ys.append(_mm(jnp.broadcast_to(cm, (8, SSD_STATE)), hn, True, _NT)[0:1])
    y_ref[0] = jnp.concatenate(ys, axis=1) + dexp_ref[...] * xs


def _dt_expand_kernel(dt_ref, e64_ref, e128_ref, o64_ref, o128_ref):
    dt = dt_ref[...]
    o64_ref[...] = _mm(dt, e64_ref[...], True)
    o128_ref[...] = _mm(dt, e128_ref[...], True)


def _dt_expand(dt, prm):
    nb = dt.shape[0]
    return pl.pallas_call(
        _dt_expand_kernel,
        out_shape=[jax.ShapeDtypeStruct((nb, SSD_INNER), F32), jax.ShapeDtypeStruct((nb, SSD_HEADS * LANES), F32)],
        name="ssd_dt_expand",
    )(dt, prm["expand_64"], prm["expand_128"])


def _ssd_step(xbc, conv_buf, dt64, dt128, state, prm):
    nb = xbc.shape[0]
    full = lambda shape: pl.BlockSpec(shape, lambda b: (0,) * len(shape))
    y, so = pl.pallas_call(
        _ssd_step_kernel,
        grid=(nb,),
        in_specs=[pl.BlockSpec((1, 1, SSD_CONV_DIM), lambda b: (b, 0, 0)),
                  pl.BlockSpec((1, SSD_CONV - 1, SSD_CONV_DIM), lambda b: (b, 0, 0)),
                  full((SSD_CONV, SSD_CONV_DIM)), full((1, SSD_CONV_DIM)),
                  pl.BlockSpec((1, 1, SSD_INNER), lambda b: (b, 0, 0)), full((1, SSD_INNER)),
                  pl.BlockSpec((1, SSD_HEADS, LANES), lambda b: (b, 0, 0)), full((SSD_HEADS, LANES)),
                  full((SSD_HEADS, LANES)), full((1, SSD_INNER)),
                  pl.BlockSpec((1, SSD_HEADS, SSD_HEADDIM, SSD_STATE), lambda b: (b, 0, 0, 0))],
        out_specs=[pl.BlockSpec((1, 1, SSD_INNER), lambda b: (b, 0, 0)),
                   pl.BlockSpec((1, SSD_HEADS, SSD_HEADDIM, SSD_STATE), lambda b: (b, 0, 0, 0))],
        out_shape=[jax.ShapeDtypeStruct((nb, 1, SSD_INNER), F32),
                   jax.ShapeDtypeStruct((nb, SSD_HEADS, SSD_HEADDIM, SSD_STATE), F32)],
        compiler_params=pltpu.CompilerParams(dimension_semantics=("parallel",)),
        name="ssd_step",
    )(xbc.reshape(nb, 1, SSD_CONV_DIM), conv_buf, prm["conv_w"], prm["conv_b"],
      dt64.reshape(nb, 1, SSD_INNER), prm["dt_bias_64"], dt128.reshape(nb, SSD_HEADS, LANES),
      prm["dt_bias_128"], prm["a_128"], prm["d_exp"], state)
    return y.reshape(nb, SSD_INNER), so


_GROUP_LANE0 = MOE_EXPERTS
_ROUTE_LANE0 = 64


def _pack_bf16_halves(x):
    n = x.shape[1] // 2
    lo = lax.bitcast_convert_type(x[:, :n].astype(BF16).astype(F32), jnp.uint32)
    hi = lax.bitcast_convert_type(x[:, n:].astype(BF16).astype(F32), jnp.uint32)
    return (lo >> 16) | hi


def _unpack_bf16_halves(p):
    lo = lax.bitcast_convert_type(p << 16, F32)
    hi = lax.bitcast_convert_type(p & jnp.uint32(0xFFFF0000), F32)
    return jnp.concatenate([lo, hi], axis=1)


def _moe_router_kernel(y_ref, g_ref, wr_ref, br_ref, xn_ref, comb_ref, cnt_ref):
    @pl.when(pl.program_id(0) == 0)
    def _():
        cnt_ref[...] = jnp.zeros(cnt_ref.shape, F32)

    xn = _rms(y_ref[...], g_ref[...])
    xn_ref[...] = _pack_bf16_halves(xn) if xn_ref.dtype == jnp.uint32 else xn.astype(xn_ref.dtype)
    lg = _mm(xn, wr_ref[...], True) + br_ref[...]
    lane = lax.broadcasted_iota(jnp.int32, lg.shape, 1)
    is_group = (lane >= _GROUP_LANE0) & (lane < _GROUP_LANE0 + MOE_GROUPS)
    gl = jnp.where(is_group, lg, NEG)
    gmax = jnp.max(gl, axis=-1, keepdims=True)
    g_p = 1.0 / jnp.sum(jnp.exp(gl - gmax), axis=-1, keepdims=True)
    gidx = jnp.min(jnp.where(gl == gmax, lane - _GROUP_LANE0, MOE_GROUPS), axis=-1, keepdims=True)
    el = jnp.where((lane < MOE_EXPERTS) & (lane // MOE_EPG == gidx), lg, NEG)
    m1 = jnp.max(el, axis=-1, keepdims=True)
    i1 = jnp.min(jnp.where(el == m1, lane, LANES), axis=-1, keepdims=True)
    el2 = jnp.where(lane == i1, NEG, el)
    m2 = jnp.max(el2, axis=-1, keepdims=True)
    i2 = jnp.min(jnp.where(el2 == m2, lane, LANES), axis=-1, keepdims=True)
    e2 = jnp.exp(m2 - m1)
    den = 1.0 + e2
    g1 = g_p / den
    g2 = g_p * (e2 / den)
    oh1 = lane == i1
    oh2 = lane == i2
    comb = jnp.where(oh1, g1, 0.0) + jnp.where(oh2, g2, 0.0)
    picks = oh1.astype(F32) + oh2.astype(F32)
    tm = picks.shape[0]
    before = lax.broadcasted_iota(jnp.int32, (tm, tm), 0) > lax.broadcasted_iota(jnp.int32, (tm, tm), 1)
    seen = cnt_ref[...] + _mm(before.astype(F32), picks)
    r1 = jnp.sum(jnp.where(oh1, seen, 0.0), axis=-1, keepdims=True)
    r2 = jnp.sum(jnp.where(oh2, seen, 0.0), axis=-1, keepdims=True)
    cnt_ref[...] += jnp.sum(picks, axis=0, keepdims=True)
    for k, v in enumerate((i1.astype(F32), i2.astype(F32), g1, g2, r1, r2)):
        comb = jnp.where(lane == _ROUTE_LANE0 + k, v, comb)
    comb_ref[...] = comb


def _moe_router(y, g, wr, br, tm, name, xn_dtype):
    m, d = y.shape
    dx = d // 2 if xn_dtype == jnp.uint32 else d
    return pl.pallas_call(
        _moe_router_kernel,
        grid=(m // tm,),
        in_specs=[pl.BlockSpec((tm, d), lambda i: (i, 0)), pl.BlockSpec((1, d), lambda i: (0, 0)),
                  pl.BlockSpec((d, LANES), lambda i: (0, 0)), pl.BlockSpec((1, LANES), lambda i: (0, 0))],
        out_specs=[pl.BlockSpec((tm, dx), lambda i: (i, 0)), pl.BlockSpec((tm, LANES), lambda i: (i, 0)),
                   pl.BlockSpec((1, LANES), lambda i: (0, 0))],
        out_shape=[jax.ShapeDtypeStruct((m, dx), xn_dtype), jax.ShapeDtypeStruct((m, LANES), F32),
                   jax.ShapeDtypeStruct((1, LANES), F32)],
        compiler_params=pltpu.CompilerParams(dimension_semantics=("arbitrary",)),
        name=name,
    )(y, g.reshape(1, d), wr, br)


def _moe_dense_kernel(res_ref, xn_ref, comb_ref, wg_ref, wu_ref, wd_ref, o_ref, acc, *, precise):
    e = pl.program_id(1)

    @pl.when(e == 0)
    def _():
        acc[...] = jnp.zeros(acc.shape, F32)

    x = xn_ref[...]
    hdn = jax.nn.silu(_mm(x, wg_ref[0], precise)) * _mm(x, wu_ref[0], precise)
    comb = comb_ref[...]
    lane = lax.broadcasted_iota(jnp.int32, comb.shape, 1)
    gate = jnp.sum(jnp.where(lane == e, comb, 0.0), axis=-1, keepdims=True)
    acc[...] += gate * _mm(hdn, wd_ref[0], precise)

    @pl.when(e == pl.num_programs(1) - 1)
    def _():
        o_ref[...] = res_ref[...] + acc[...]


def _moe_dense(res, xn, comb, wg, wu, wd, e0, tm, name, precise):
    m, d = res.shape
    return pl.pallas_call(
        functools.partial(_moe_dense_kernel, precise=precise),
        grid=(m // tm, MOE_EXPERTS),
        in_specs=[pl.BlockSpec((tm, d), lambda i, e: (i, 0)), pl.BlockSpec((tm, d), lambda i, e: (i, 0)),
                  pl.BlockSpec((tm, LANES), lambda i, e: (i, 0)),
                  pl.BlockSpec((1, d, MOE_FF), lambda i, e: (e0 + e, 0, 0)),
                  pl.BlockSpec((1, d, MOE_FF), lambda i, e: (e0 + e, 0, 0)),
                  pl.BlockSpec((1, MOE_FF, d), lambda i, e: (e0 + e, 0, 0))],
        out_specs=pl.BlockSpec((tm, d), lambda i, e: (i, 0)),
        out_shape=jax.ShapeDtypeStruct((m, d), F32),
        scratch_shapes=[pltpu.VMEM((tm, d), F32)],
        compiler_params=pltpu.CompilerParams(dimension_semantics=("parallel", "arbitrary")),
        name=name,
    )(res, xn, comb, wg, wu, wd)


def _moe(y, g, prm, tm_r, tm_e, tag, precise):
    xn, comb, _ = _moe_router(y, g, prm["wr"], prm["br"], tm_r, "moe_router_" + tag, F32 if precise else BF16)
    return _moe_dense(y, xn, comb, prm["wg"], prm["wu"], prm["wd"], prm["e0"], tm_e, "moe_experts_" + tag, precise)


def _moe_grouped_kernel(te_ref, nt_ref, x_ref, wg_ref, wu_ref, wd_ref, o_ref):
    del te_ref
    active = pl.program_id(0) < nt_ref[0]

    @pl.when(active)
    def _():
        x = _unpack_bf16_halves(x_ref[...]).astype(BF16)
        hdn = jax.nn.silu(_mm(x, wg_ref[0])) * _mm(x, wu_ref[0])
        o_ref[...] = _mm(hdn, wd_ref[0]).astype(o_ref.dtype)

    @pl.when(jnp.logical_not(active))
    def _():
        o_ref[...] = jnp.zeros(o_ref.shape, o_ref.dtype)


def _moe_grouped(x_rows, tile_expert, n_tiles, wg, wu, wd, tm, name):
    r = x_rows.shape[0]
    d = wg.shape[1]
    grid_spec = pltpu.PrefetchScalarGridSpec(
        num_scalar_prefetch=2,
        grid=(r // tm,),
        in_specs=[pl.BlockSpec((tm, x_rows.shape[1]), lambda i, te, nt: (i, 0)),
                  pl.BlockSpec((1, d, MOE_FF), lambda i, te, nt: (te[i], 0, 0)),
                  pl.BlockSpec((1, d, MOE_FF), lambda i, te, nt: (te[i], 0, 0)),
                  pl.BlockSpec((1, MOE_FF, d), lambda i, te, nt: (te[i], 0, 0))],
        out_specs=pl.BlockSpec((tm, d), lambda i, te, nt: (i, 0)),
    )
    return pl.pallas_call(
        _moe_grouped_kernel,
        grid_spec=grid_spec,
        out_shape=jax.ShapeDtypeStruct((r, d), F32),
        compiler_params=pltpu.CompilerParams(dimension_semantics=("arbitrary",)),
        name=name,
    )(tile_expert, n_tiles, x_rows, wg, wu, wd)


def _moe_sparse(y, g, prm, tm_r, tm_e, tag):
    t, d = y.shape
    xn, route, cnt = _moe_router(y, g, prm["wr"], prm["br"], tm_r, "moe_router_" + tag, jnp.uint32)
    lanes = lambda k: route[:, _ROUTE_LANE0 + k:_ROUTE_LANE0 + k + 2]
    ids = lanes(0).astype(jnp.int32)
    gates = lanes(2)
    rank = lanes(4).astype(jnp.int32)
    counts = cnt[0, :MOE_EXPERTS].astype(jnp.int32)
    rows = 2 * t + MOE_EXPERTS * tm_e
    pcounts = ((counts + tm_e - 1) // tm_e) * tm_e
    pend = jnp.cumsum(pcounts)
    pstart = pend - pcounts
    pos = pstart[ids] + rank
    tile_start = jnp.arange(rows // tm_e, dtype=jnp.int32) * tm_e
    tile_expert = jnp.minimum(jnp.sum((pend[None, :] <= tile_start[:, None]).astype(jnp.int32), axis=1),
                              MOE_EXPERTS - 1)
    n_tiles = (pend[-1:] // tm_e).astype(jnp.int32)
    token = jnp.broadcast_to(jnp.arange(t, dtype=jnp.int32)[:, None], (t, 2))
    src = jnp.zeros((rows,), jnp.int32).at[pos.reshape(-1)].set(token.reshape(-1), unique_indices=True)
    x_rows = xn[src]
    out_rows = _moe_grouped(x_rows, tile_expert + prm["e0"], n_tiles, prm["wg"], prm["wu"], prm["wd"], tm_e,
                            "moe_experts_" + tag)
    return y + gates[:, 0:1] * out_rows[pos[:, 0]] + gates[:, 1:2] * out_rows[pos[:, 1]]


def _moe_params(layer, w_group, b_group, w_expert, b_expert, w_gate, w_up, w_down):
    d = w_group.shape[0]
    pad = LANES - MOE_EXPERTS - MOE_GROUPS
    wr = jnp.concatenate([w_expert, w_group, jnp.zeros((d, pad), F32)], axis=1)
    br = jnp.concatenate([b_expert, b_group, jnp.zeros((pad,), F32)]).reshape(1, LANES)
    flat = lambda w: w.reshape((-1,) + w.shape[2:])
    return {"wr": wr, "br": br, "wg": flat(w_gate), "wu": flat(w_up), "wd": flat(w_down), "e0": layer * MOE_EXPERTS}


def _ssd_params(w_in, conv_w, conv_b, dt_bias, a_log, d, norm, w_out):
    a = -jnp.exp(a_log)
    w_dt = w_in[:, SSD_INNER + SSD_CONV_DIM:]

    def grouped(x):
        x = x.reshape(x.shape[:-1] + (SSD_GROUPS, SSD_HPG))
        x = jnp.pad(x, [(0, 0)] * (x.ndim - 1) + [(0, LANES - SSD_HPG)])
        return x.reshape(x.shape[:-2] + (SSD_GROUPS * LANES,))

    w_z = w_in[:, :SSD_INNER]
    w_xbc = w_in[:, SSD_INNER:SSD_INNER + SSD_CONV_DIM]
    eye = jnp.eye(SSD_HEADS, dtype=F32)
    return {
        "w_z": w_z.astype(BF16), "w_xbc": w_xbc.astype(BF16), "w_dt_g": grouped(w_dt).astype(BF16),
        "w_z_f32": w_z, "w_xbc_f32": w_xbc, "w_dt_f32": w_dt, "w_out_f32": w_out,
        "expand_64": jnp.repeat(eye, SSD_HEADDIM, axis=1), "expand_128": jnp.repeat(eye, LANES, axis=1),
        "conv_w": conv_w, "conv_b": conv_b.reshape(1, SSD_CONV_DIM),
        "dt_bias_g": grouped(dt_bias).reshape(SSD_GROUPS, 1, LANES),
        "a_g": grouped(a).reshape(SSD_GROUPS, 1, LANES),
        "dt_bias_64": jnp.repeat(dt_bias, SSD_HEADDIM).reshape(1, SSD_INNER),
        "dt_bias_128": jnp.broadcast_to(dt_bias[:, None], (SSD_HEADS, LANES)),
        "a_128": jnp.broadcast_to(a[:, None], (SSD_HEADS, LANES)),
        "d_exp": jnp.repeat(d, SSD_HEADDIM).reshape(1, SSD_INNER),
        "norm": norm, "w_out": w_out.astype(BF16),
    }


_TM = 512


def kernel(x_prompt, x_sample, cache_k, cache_v, page_table, state_s5_re, state_s5_im, state_ssd, state_conv, norm_mix, norm_ffn, norm_final, even_w_in, even_w_out, diff_lam_q1, diff_lam_k1, diff_lam_q2, diff_lam_k2, diff_subln, s5_lam_re, s5_lam_im, s5_log_dt, s5_b_re, s5_b_im, s5_c_re, s5_c_im, s5_d, s5_w_glu, s5_b_glu, ssd_w_in, ssd_conv_w, ssd_conv_b, ssd_dt_bias, ssd_a_log, ssd_d, ssd_norm, ssd_w_out, moe_w_group, moe_b_group, moe_w_expert, moe_b_expert, moe_w_gate, moe_w_up, moe_w_down):
    bp, sp, d = x_prompt.shape
    bs = x_sample.shape[0]
    tp = bp * sp
    depth = norm_mix.shape[0]
    y_p = x_prompt.reshape(tp, d)
    y_s = x_sample.reshape(bs, d)
    outs = {n: [] for n in ("k_p", "v_p", "k_s", "v_s", "re_p", "im_p", "re_s", "im_s", "ssd_p", "ssd_s", "cv_p", "cv_s")}

    for li in range(depth):
        if li % 2 == 0:
            e = li // 2
            lam_init = 0.8 - 0.6 * math.exp(-0.3 * li)
            lam = (jnp.exp(jnp.sum(diff_lam_q1[e] * diff_lam_k1[e])) - jnp.exp(jnp.sum(diff_lam_q2[e] * diff_lam_k2[e]))
                   + lam_init).astype(F32)
            ws32 = [even_w_in[e][:, i * ATT_WIDTH:(i + 1) * ATT_WIDTH] for i in range(4)]
            ws = [w.astype(BF16) for w in ws32]
            w_out32 = [even_w_out[e][:ATT_WIDTH], even_w_out[e][ATT_WIDTH:]]
            w_out_a, w_out_s = [w.astype(BF16) for w in w_out32]
            s5p = _s5_params(s5_lam_re[e], s5_lam_im[e], s5_log_dt[e], s5_b_re[e], s5_b_im[e], s5_c_re[e], s5_c_im[e],
                             s5_d[e], s5_w_glu[e], s5_b_glu[e])
            dts = [(BF16,), (F32, BF16), (F32, _TILE_T), (F32,)]
            q, k, k16, v, vt, u = _norm_matmul(y_p, norm_mix[li], ws, dts, _TM, "even_in_p")
            o = _attn_prompt(q.reshape(bp, sp, ATT_WIDTH), k16.reshape(bp, sp, ATT_WIDTH), vt, lam, diff_subln[e],
                             1.0 - lam_init)
            s5o, h_re, h_im = _s5_prompt(u.reshape(bp, sp, S5_WIDTH), s5p)
            y_p = _matmul_res(y_p, [o.reshape(tp, ATT_WIDTH), s5o.reshape(tp, S5_WIDTH)], [w_out_a, w_out_s], _TM,
                              "even_out_p")
            outs["k_p"].append(k.reshape(bp, sp, ATT_HEADS, HEAD_W))
            outs["v_p"].append(v.reshape(bp, sp, ATT_HEADS, HEAD_W))
            outs["re_p"].append(h_re.reshape(bp, S5_GROUPS, S5_STATE))
            outs["im_p"].append(h_im.reshape(bp, S5_GROUPS, S5_STATE))
            q, k, v, u = _norm_matmul(y_s, norm_mix[li], ws32, [(F32,), (F32,), (F32,), (F32,)], bs, "even_in_s", True)
            o = _attn_decode(q, k, v, cache_k, cache_v, page_table + e * cache_k.shape[1], lam, diff_subln[e],
                             1.0 - lam_init)
            s5o, h_re, h_im = _s5_step(u, state_s5_re[e].reshape(bs, S5_LANES), state_s5_im[e].reshape(bs, S5_LANES), s5p)
            y_s = _matmul_res(y_s, [o, s5o], w_out32, bs, "even_out_s", True)
            outs["k_s"].append(k.reshape(bs, 1, ATT_HEADS, HEAD_W))
            outs["v_s"].append(v.reshape(bs, 1, ATT_HEADS, HEAD_W))
            outs["re_s"].append(h_re.reshape(bs, S5_GROUPS, S5_STATE))
            outs["im_s"].append(h_im.reshape(bs, S5_GROUPS, S5_STATE))
        else:
            o_ = li // 2
            sp_ = _ssd_params(ssd_w_in[o_], ssd_conv_w[o_], ssd_conv_b[o_], ssd_dt_bias[o_], ssd_a_log[o_], ssd_d[o_],
                              ssd_norm[o_], ssd_w_out[o_])
            z, xc, dtg, tail = _ssd_in(y_p, norm_mix[li], sp_, sp, 256, "ssd_in_p")
            yssd, st = _ssd_prompt(xc.reshape(bp, sp, SSD_CONV_DIM), dtg.reshape(bp, sp, SSD_GROUPS * LANES), sp_)
            y_p = _gated_norm_matmul(y_p, yssd.reshape(tp, SSD_INNER), z, sp_["norm"], sp_["w_out"], 256, "ssd_out_p")
            outs["ssd_p"].append(st)
            outs["cv_p"].append(tail[:, _CONV_PAD - (SSD_CONV - 1):, :])
            z, xbc, dtc = _norm_matmul(y_s, norm_mix[li], [sp_["w_z_f32"], sp_["w_xbc_f32"], sp_["w_dt_f32"]],
                                       [(F32,), (F32,), (F32,)], bs, "ssd_in_s", True)
            dt64, dt128 = _dt_expand(dtc, sp_)
            yssd, st = _ssd_step(xbc, state_conv[o_], dt64, dt128, state_ssd[o_], sp_)
            y_s = _gated_norm_matmul(y_s, yssd, z, sp_["norm"], sp_["w_out_f32"], bs, "ssd_out_s", True)
            outs["ssd_s"].append(st)
            outs["cv_s"].append(jnp.concatenate([state_conv[o_][:, 1:], xbc[:, None, :]], axis=1))
        mp = _moe_params(li, moe_w_group[li], moe_b_group[li], moe_w_expert[li], moe_b_expert[li], moe_w_gate,
                         moe_w_up, moe_w_down)
        y_p = _moe_sparse(y_p, norm_ffn[li], mp, _TM, 256, "p")
        y_s = _moe(y_s, norm_ffn[li], mp, bs, bs, "s", li + 1 < depth)

    y_prompt = _rmsnorm(y_p, norm_final, _TM, "final_p").reshape(bp, sp, d)
    y_sample = _rmsnorm(y_s, norm_final, bs, "final_s").reshape(bs, 1, d)
    st = lambda n: jnp.stack(outs[n])
    return (y_prompt, y_sample, st("k_p"), st("v_p"), st("k_s"), st("v_s"), st("re_p"), st("im_p"), st("re_s"),
            st("im_s"), st("ssd_p"), st("ssd_s"), st("cv_p"), st("cv_s"))
```

```python
import functools
import math

import jax
import jax.numpy as jnp
from jax import lax
from jax.experimental import pallas as pl
from jax.experimental.pallas import tpu as pltpu

F32 = jnp.float32
BF16 = jnp.bfloat16
HIGHEST = lax.Precision.HIGHEST

D_MODEL = 1024
NORM_EPS = 1e-6
PAGE_SIZE = 128
ATT_HEADS = 4
ATT_D = 64
ATT_WIDTH = ATT_HEADS * 2 * ATT_D
HEAD_W = 2 * ATT_D
S5_WIDTH = 512
S5_GROUP = 16
S5_GROUPS = 32
S5_STATE = 64
S5_LANES = S5_GROUPS * S5_STATE
SSD_INNER = 2048
SSD_HEADDIM = 64
SSD_HEADS = 32
SSD_GROUPS = 4
SSD_HPG = 8
SSD_STATE = 128
SSD_CONV = 4
SSD_CONV_DIM = SSD_INNER + 2 * SSD_GROUPS * SSD_STATE
SSD_CHUNK = 128
SSD_GW = SSD_HPG * SSD_HEADDIM
MOE_GROUPS = 4
MOE_EPG = 8
MOE_EXPERTS = 32
MOE_FF = 256
LANES = 128
NEG = -1e30

_NT = (((1,), (1,)), ((), ()))
_TN = (((0,), (0,)), ((), ()))


def _rms(x, g):
    return x * lax.rsqrt(jnp.mean(x * x, axis=-1, keepdims=True) + NORM_EPS) * g


def _split(x):
    hi = x.astype(BF16)
    return hi, (x - hi.astype(F32)).astype(BF16)


def _mm(a, b, precise=False, dims=None):
    dot = jnp.dot if dims is None else functools.partial(lax.dot_general, dimension_numbers=dims)
    if not precise:
        return dot(a.astype(BF16), b.astype(BF16), preferred_element_type=F32)
    m = a.shape[0]
    a_hi, a_lo = _split(a)
    b_hi, b_lo = _split(b)
    r = dot(jnp.concatenate([a_hi, a_lo], axis=0), b_hi, preferred_element_type=F32)
    return r[:m] + r[m:] + dot(a_hi, b_lo, preferred_element_type=F32)


def _norm_matmul_kernel(x_ref, g_ref, *refs, out_dtypes, precise):
    n = len(out_dtypes)
    w_refs = refs[:n]
    o_refs = refs[n:]
    h = _rms(x_ref[...], g_ref[...])
    if not precise:
        h = h.astype(BF16)
    k = 0
    for w_ref, dts in zip(w_refs, out_dtypes):
        r = _mm(h, w_ref[...], precise)
        for dt in dts:
            if dt == _TILE_T:
                o_refs[k][0] = r.T.astype(BF16)
            else:
                o_refs[k][...] = r.astype(dt)
            k += 1


_TILE_T = "bf16 row tiles, each transposed"


def _norm_matmul(x, g, ws, out_dtypes, tm, name, precise=False):
    m, kdim = x.shape
    out_shape, out_specs = [], []
    for w, dts in zip(ws, out_dtypes):
        for dt in dts:
            if dt == _TILE_T:
                out_shape.append(jax.ShapeDtypeStruct((m // tm, w.shape[1], tm), BF16))
                out_specs.append(pl.BlockSpec((1, w.shape[1], tm), lambda i: (i, 0, 0)))
            else:
                out_shape.append(jax.ShapeDtypeStruct((m, w.shape[1]), dt))
                out_specs.append(pl.BlockSpec((tm, w.shape[1]), lambda i: (i, 0)))
    return pl.pallas_call(
        functools.partial(_norm_matmul_kernel, out_dtypes=out_dtypes, precise=precise),
        grid=(m // tm,),
        in_specs=[pl.BlockSpec((tm, kdim), lambda i: (i, 0)), pl.BlockSpec((1, kdim), lambda i: (0, 0))]
        + [pl.BlockSpec(w.shape, lambda i: (0, 0)) for w in ws],
        out_specs=out_specs,
        out_shape=out_shape,
        compiler_params=pltpu.CompilerParams(dimension_semantics=("parallel",)),
        name=name,
    )(x, g.reshape(1, kdim), *ws)


def _matmul_res_kernel(res_ref, *refs, precise):
    n = (len(refs) - 1) // 2
    acc = res_ref[...]
    for a_ref, w_ref in zip(refs[:n], refs[n:2 * n]):
        acc = acc + _mm(a_ref[...], w_ref[...], precise)
    refs[-1][...] = acc


def _matmul_res(res, a_list, w_list, tm, name, precise=False):
    m, n = res.shape
    return pl.pallas_call(
        functools.partial(_matmul_res_kernel, precise=precise),
        grid=(m // tm,),
        in_specs=[pl.BlockSpec((tm, n), lambda i: (i, 0))]
        + [pl.BlockSpec((tm, a.shape[1]), lambda i: (i, 0)) for a in a_list]
        + [pl.BlockSpec(w.shape, lambda i: (0, 0)) for w in w_list],
        out_specs=pl.BlockSpec((tm, n), lambda i: (i, 0)),
        out_shape=jax.ShapeDtypeStruct((m, n), F32),
        compiler_params=pltpu.CompilerParams(dimension_semantics=("parallel",)),
        name=name,
    )(res, *a_list, *w_list)


def _gated_norm_matmul_kernel(res_ref, y_ref, z_ref, g_ref, w_ref, o_ref, *, precise):
    y = y_ref[...] * jax.nn.silu(z_ref[...])
    o_ref[...] = res_ref[...] + _mm(_rms(y, g_ref[...]), w_ref[...], precise)


def _gated_norm_matmul(res, y, z, g, w, tm, name, precise=False):
    m, n = res.shape
    kdim = y.shape[1]
    return pl.pallas_call(
        functools.partial(_gated_norm_matmul_kernel, precise=precise),
        grid=(m // tm,),
        in_specs=[pl.BlockSpec((tm, n), lambda i: (i, 0)), pl.BlockSpec((tm, kdim), lambda i: (i, 0)),
                  pl.BlockSpec((tm, kdim), lambda i: (i, 0)), pl.BlockSpec((1, kdim), lambda i: (0, 0)),
                  pl.BlockSpec(w.shape, lambda i: (0, 0))],
        out_specs=pl.BlockSpec((tm, n), lambda i: (i, 0)),
        out_shape=jax.ShapeDtypeStruct((m, n), F32),
        compiler_params=pltpu.CompilerParams(dimension_semantics=("parallel",)),
        name=name,
    )(res, y, z, g.reshape(1, kdim), w)


def _rmsnorm_kernel(x_ref, g_ref, o_ref):
    o_ref[...] = _rms(x_ref[...], g_ref[...])


def _rmsnorm(x, g, tm, name):
    m, n = x.shape
    return pl.pallas_call(
        _rmsnorm_kernel,
        grid=(m // tm,),
        in_specs=[pl.BlockSpec((tm, n), lambda i: (i, 0)), pl.BlockSpec((1, n), lambda i: (0, 0))],
        out_specs=pl.BlockSpec((tm, n), lambda i: (i, 0)),
        out_shape=jax.ShapeDtypeStruct((m, n), F32),
        compiler_params=pltpu.CompilerParams(dimension_semantics=("parallel",)),
        name=name,
    )(x, g.reshape(1, n))


def _split_q(q):
    lane = lax.broadcasted_iota(jnp.int32, q.shape, 1) % HEAD_W
    scale = ATT_D ** -0.5
    qs = q * jnp.asarray(scale, q.dtype)
    zero = jnp.zeros_like(qs)
    return jnp.where(lane < ATT_D, qs, zero), jnp.where(lane >= ATT_D, qs, zero)


def _subln(o, g, out_scale):
    return _rms(o, g) * out_scale


def _attn_prompt_kernel(lam_ref, q_ref, k_ref, vt_ref, g_ref, o_ref, qt_sc, m_sc, l_sc, acc_sc, *, tq, tk, out_scale):
    qi = pl.program_id(1)
    heads = range(ATT_HEADS)
    for h in heads:
        q1, q2 = _split_q(q_ref[0, :, h * HEAD_W:(h + 1) * HEAD_W])
        qt_sc[h] = jnp.concatenate([q1, q2], axis=0).astype(F32).T.astype(BF16)
    m_sc[...] = jnp.full(m_sc.shape, NEG, F32)
    l_sc[...] = jnp.zeros(l_sc.shape, F32)
    acc_sc[...] = jnp.zeros(acc_sc.shape, F32)

    def kv_step(j, masked):
        start = pl.multiple_of(j * tk, tk)
        for h in heads:
            kt = k_ref[0, pl.ds(start, tk), h * HEAD_W:(h + 1) * HEAD_W]
            st = jnp.dot(kt, qt_sc[h], preferred_element_type=F32)
            if masked:
                k_pos = start + lax.broadcasted_iota(jnp.int32, st.shape, 0)
                q_pos = qi * tq + lax.broadcasted_iota(jnp.int32, st.shape, 1) % tq
                st = jnp.where(k_pos <= q_pos, st, NEG)
            m_old = m_sc[h]
            m_new = jnp.maximum(m_old, jnp.max(st, axis=0, keepdims=True))
            p = jnp.exp(st - m_new)
            alpha = jnp.exp(m_old - m_new)
            l_sc[h] = alpha * l_sc[h] + jnp.sum(p, axis=0, keepdims=True)
            m_sc[h] = m_new
            vt = vt_ref[j, h * HEAD_W:(h + 1) * HEAD_W, :]
            acc_sc[h] = alpha * acc_sc[h] + jnp.dot(vt, p.astype(BF16), preferred_element_type=F32)

    n_full = (qi * tq) // tk

    def full_step(j, carry):
        kv_step(j, False)
        return carry

    lax.fori_loop(0, n_full, full_step, 0)
    kv_step(n_full, True)
    for h in heads:
        acc, l = acc_sc[h], l_sc[h]
        ot = acc[:, 0:tq] / l[:, 0:tq] - lam_ref[0, 0] * (acc[:, tq:2 * tq] / l[:, tq:2 * tq])
        o_ref[0, :, h * HEAD_W:(h + 1) * HEAD_W] = _subln(ot.T, g_ref[...], out_scale).astype(o_ref.dtype)


def _attn_prompt(q, k, vt, lam, subln, out_scale, tq=512):
    b, s, _ = q.shape
    tk = vt.shape[2]
    nblk = s // tk
    kern = functools.partial(_attn_prompt_kernel, tq=tq, tk=tk, out_scale=out_scale)
    return pl.pallas_call(
        kern,
        grid=(b, s // tq),
        in_specs=[pl.BlockSpec(memory_space=pltpu.SMEM),
                  pl.BlockSpec((1, tq, ATT_WIDTH), lambda bi, i: (bi, i, 0)),
                  pl.BlockSpec((1, s, ATT_WIDTH), lambda bi, i: (bi, 0, 0)),
                  pl.BlockSpec((nblk, ATT_WIDTH, tk), lambda bi, i: (bi, 0, 0)),
                  pl.BlockSpec((1, HEAD_W), lambda bi, i: (0, 0))],
        out_specs=pl.BlockSpec((1, tq, ATT_WIDTH), lambda bi, i: (bi, i, 0)),
        out_shape=jax.ShapeDtypeStruct((b, s, ATT_WIDTH), BF16),
        scratch_shapes=[pltpu.VMEM((ATT_HEADS, HEAD_W, 2 * tq), BF16), pltpu.VMEM((ATT_HEADS, 1, 2 * tq), F32),
                        pltpu.VMEM((ATT_HEADS, 1, 2 * tq), F32), pltpu.VMEM((ATT_HEADS, HEAD_W, 2 * tq), F32)],
        compiler_params=pltpu.CompilerParams(dimension_semantics=("parallel", "parallel")),
        name="attn_prompt",
    )(lam.reshape(1, 1), q, k, vt, subln.reshape(1, HEAD_W))


_DEC_PAGES = 16
_DEC_PROW = PAGE_SIZE * ATT_HEADS
_DEC_COLS = _DEC_PAGES * _DEC_PROW
_DEC_ROWS = 2 * ATT_HEADS


def _attn_decode_kernel(pt_ref, lam_ref, q_ref, kn_ref, vn_ref, g_ref, *refs, out_scale):
    del pt_ref
    k_refs = refs[:_DEC_PAGES]
    v_refs = refs[_DEC_PAGES:2 * _DEC_PAGES]
    o_ref = refs[2 * _DEC_PAGES]
    m_sc, l_sc, acc_sc = refs[2 * _DEC_PAGES + 1:]
    j = pl.program_id(1)

    q4 = q_ref[0] * (ATT_D ** -0.5)
    q8 = jnp.concatenate([q4, q4], axis=0)
    r8 = lax.broadcasted_iota(jnp.int32, (_DEC_ROWS, HEAD_W), 0)
    l8 = lax.broadcasted_iota(jnp.int32, (_DEC_ROWS, HEAD_W), 1)
    q8 = jnp.where(l8 // ATT_D == r8 // ATT_HEADS, q8, 0.0)

    @pl.when(j == 0)
    def _():
        m_sc[...] = jnp.full(m_sc.shape, NEG, F32)
        l_sc[...] = jnp.zeros(l_sc.shape, F32)
        acc_sc[...] = jnp.zeros(acc_sc.shape, F32)

    s = jnp.concatenate([_mm(q8, k_ref[0], True, _NT) for k_ref in k_refs], axis=1)
    rs = lax.broadcasted_iota(jnp.int32, s.shape, 0)
    cs = lax.broadcasted_iota(jnp.int32, s.shape, 1)
    s = jnp.where(cs % ATT_HEADS == rs % ATT_HEADS, s, NEG)
    m = m_sc[...]
    mn = jnp.maximum(m, jnp.max(s, axis=-1, keepdims=True))
    p = jnp.exp(s - mn)
    alpha = jnp.exp(m - mn)
    l = alpha * l_sc[...] + jnp.sum(p, axis=-1, keepdims=True)
    acc = alpha * acc_sc[...]
    for i, v_ref in enumerate(v_refs):
        acc = acc + _mm(p[:, i * _DEC_PROW:(i + 1) * _DEC_PROW], v_ref[0], True)
    m_sc[...] = mn
    l_sc[...] = l
    acc_sc[...] = acc

    @pl.when(j == pl.num_programs(1) - 1)
    def _():
        kn = jnp.concatenate([kn_ref[0], kn_ref[0]], axis=0)
        vn = jnp.concatenate([vn_ref[0], vn_ref[0]], axis=0)
        sn = jnp.sum(q8 * kn, axis=-1, keepdims=True)
        mf = jnp.maximum(mn, sn)
        pn = jnp.exp(sn - mf)
        af = jnp.exp(mn - mf)
        o8 = (af * acc + pn * vn) / (af * l + pn)
        o4 = o8[:ATT_HEADS] - lam_ref[0, 0] * o8[ATT_HEADS:]
        o_ref[0] = _subln(o4, g_ref[...], out_scale)


def _attn_decode(q, k_new, v_new, cache_k, cache_v, page_table, lam, subln, out_scale):
    b = q.shape[0]
    n_pages = page_table.shape[1]
    steps = n_pages // _DEC_PAGES
    ck = cache_k.reshape(-1, _DEC_PROW, HEAD_W)
    cv = cache_v.reshape(-1, _DEC_PROW, HEAD_W)

    def page_map(i):
        return lambda bi, j, pt: (pt[bi, j * _DEC_PAGES + i], 0, 0)

    row = lambda bi, j, pt: (bi, 0, 0)
    head_rows = pl.BlockSpec((1, ATT_HEADS, HEAD_W), row)
    page_specs = [pl.BlockSpec((1, _DEC_PROW, HEAD_W), page_map(i)) for i in range(_DEC_PAGES)]
    grid_spec = pltpu.PrefetchScalarGridSpec(
        num_scalar_prefetch=1,
        grid=(b, steps),
        in_specs=[pl.BlockSpec(memory_space=pltpu.SMEM), head_rows, head_rows, head_rows,
                  pl.BlockSpec((1, HEAD_W), lambda bi, j, pt: (0, 0))] + page_specs + page_specs,
        out_specs=head_rows,
        scratch_shapes=[pltpu.VMEM((_DEC_ROWS, 1), F32), pltpu.VMEM((_DEC_ROWS, 1), F32),
                        pltpu.VMEM((_DEC_ROWS, HEAD_W), F32)],
    )
    out = pl.pallas_call(
        functools.partial(_attn_decode_kernel, out_scale=out_scale),
        grid_spec=grid_spec,
        out_shape=jax.ShapeDtypeStruct((b, ATT_HEADS, HEAD_W), F32),
        compiler_params=pltpu.CompilerParams(dimension_semantics=("parallel", "arbitrary")),
        name="attn_decode",
    )(page_table, lam.reshape(1, 1), q.reshape(b, ATT_HEADS, HEAD_W), k_new.reshape(b, ATT_HEADS, HEAD_W),
      v_new.reshape(b, ATT_HEADS, HEAD_W), subln.reshape(1, HEAD_W), *([ck] * _DEC_PAGES), *([cv] * _DEC_PAGES))
    return out.reshape(b, ATT_WIDTH)


_S5_KT = 2
_S5_KW = S5_WIDTH // _S5_KT
_S5_KL = S5_LANES // _S5_KT
_S5_NT = S5_LANES // LANES
_S5_SCAN_TILES = 8


def _s5_input(u, bmat_ref, precise):
    re, im = [], []
    for kt in range(_S5_KT):
        r = _mm(u[:, kt * _S5_KW:(kt + 1) * _S5_KW], bmat_ref[kt], precise)
        re.append(r[:, :_S5_KL])
        im.append(r[:, _S5_KL:])
    return jnp.concatenate(re, axis=1), jnp.concatenate(im, axis=1)


def _s5_output(h_re, h_im, u, cmat_ref, d_ref, wglu_ref, bglu_ref, precise):
    if not precise:
        h_re, h_im = h_re.astype(BF16), h_im.astype(BF16)
    ys = []
    for kt in range(_S5_KT):
        sl = slice(kt * _S5_KL, (kt + 1) * _S5_KL)
        y = _mm(h_re[:, sl], cmat_ref[kt, :_S5_KL, :], precise)
        y = y + _mm(h_im[:, sl], cmat_ref[kt, _S5_KL:, :], precise)
        ys.append(y)
    y = jnp.concatenate(ys, axis=1) + d_ref[...] * u
    g = jax.nn.gelu(y)
    gate = _mm(g, wglu_ref[...], precise) + bglu_ref[...]
    return g * jax.nn.sigmoid(gate)


def _s5_scan_kernel(u_ref, are_ref, aim_ref, bmat_ref, cmat_ref, d_ref, wglu_ref, bglu_ref,
                    o_ref, hre_ref, him_ref, bu_re, bu_im, *, nb, lc, bp):
    c = pl.program_id(0)

    @pl.when(c == 0)
    def _():
        bu_re[...] = jnp.zeros(bu_re.shape, F32)
        bu_im[...] = jnp.zeros(bu_im.shape, F32)
        hre_ref[...] = jnp.zeros(hre_ref.shape, F32)
        him_ref[...] = jnp.zeros(him_ref.shape, F32)

    for b in range(nb):
        re, im = _s5_input(u_ref[b], bmat_ref, False)
        for jt in range(_S5_NT):
            bu_re[jt, pl.ds(b, lc, stride=bp), :] = re[:, jt * LANES:(jt + 1) * LANES]
            bu_im[jt, pl.ds(b, lc, stride=bp), :] = im[:, jt * LANES:(jt + 1) * LANES]

    for j0 in range(0, _S5_NT, _S5_SCAN_TILES):
        tiles = pl.ds(j0, _S5_SCAN_TILES)
        a_re = jnp.broadcast_to(are_ref[tiles], (_S5_SCAN_TILES, bp, LANES))
        a_im = jnp.broadcast_to(aim_ref[tiles], (_S5_SCAN_TILES, bp, LANES))

        def step(t, carry, tiles=tiles, a_re=a_re, a_im=a_im):
            hr, hi = carry
            rows = pl.ds(pl.multiple_of(t * bp, bp), bp)
            nr = hr * a_re - hi * a_im + bu_re[tiles, rows, :]
            ni = hr * a_im + hi * a_re + bu_im[tiles, rows, :]
            bu_re[tiles, rows, :] = nr
            bu_im[tiles, rows, :] = ni
            return nr, ni

        hr, hi = lax.fori_loop(0, lc, step, (hre_ref[tiles], him_ref[tiles]), unroll=2)
        hre_ref[tiles] = hr
        him_ref[tiles] = hi

    for b in range(nb):
        h_re = jnp.concatenate([bu_re[jt, pl.ds(b, lc, stride=bp), :] for jt in range(_S5_NT)], axis=1)
        h_im = jnp.concatenate([bu_im[jt, pl.ds(b, lc, stride=bp), :] for jt in range(_S5_NT)], axis=1)
        o_ref[b] = _s5_output(h_re, h_im, u_ref[b], cmat_ref, d_ref, wglu_ref, bglu_ref, False).astype(o_ref.dtype)


def _s5_prompt(u, prm, lc=128):
    nb, l, _ = u.shape
    bp = 8
    full = lambda shape: pl.BlockSpec(shape, lambda c: (0,) * len(shape))
    o, h_re, h_im = pl.pallas_call(
        functools.partial(_s5_scan_kernel, nb=nb, lc=lc, bp=bp),
        grid=(l // lc,),
        in_specs=[pl.BlockSpec((nb, lc, S5_WIDTH), lambda c: (0, c, 0)),
                  full((_S5_NT, 1, LANES)), full((_S5_NT, 1, LANES)),
                  full((_S5_KT, _S5_KW, 2 * _S5_KL)), full((_S5_KT, 2 * _S5_KL, _S5_KW)),
                  full((1, S5_WIDTH)), full((S5_WIDTH, S5_WIDTH)), full((1, S5_WIDTH))],
        out_specs=[pl.BlockSpec((nb, lc, S5_WIDTH), lambda c: (0, c, 0)),
                   full((_S5_NT, bp, LANES)), full((_S5_NT, bp, LANES))],
        out_shape=[jax.ShapeDtypeStruct((nb, l, S5_WIDTH), BF16),
                   jax.ShapeDtypeStruct((_S5_NT, bp, LANES), F32), jax.ShapeDtypeStruct((_S5_NT, bp, LANES), F32)],
        scratch_shapes=[pltpu.VMEM((_S5_NT, lc * bp, LANES), F32) for _ in range(2)],
        compiler_params=pltpu.CompilerParams(dimension_semantics=("arbitrary",)),
        name="s5_prompt",
    )(u, prm["a_re"].reshape(_S5_NT, 1, LANES), prm["a_im"].reshape(_S5_NT, 1, LANES), prm["bmat"].astype(BF16),
      prm["cmat"].astype(BF16), prm["d"], prm["w_glu"].astype(BF16), prm["b_glu"])
    rows = lambda h: jnp.transpose(h, (1, 0, 2)).reshape(bp, S5_LANES)[:nb]
    return o, rows(h_re), rows(h_im)


def _s5_step_kernel(u_ref, h0re_ref, h0im_ref, are_ref, aim_ref, bmat_ref, cmat_ref, d_ref, wglu_ref, bglu_ref,
                    o_ref, hre_ref, him_ref):
    u = u_ref[...]
    bu_re, bu_im = _s5_input(u, bmat_ref, True)
    h_re, h_im = h0re_ref[...], h0im_ref[...]
    a_re, a_im = are_ref[...], aim_ref[...]
    n_re = h_re * a_re - h_im * a_im + bu_re
    n_im = h_re * a_im + h_im * a_re + bu_im
    hre_ref[...] = n_re
    him_ref[...] = n_im
    o_ref[...] = _s5_output(n_re, n_im, u, cmat_ref, d_ref, wglu_ref, bglu_ref, True)


def _s5_step(u, h0_re, h0_im, prm):
    nb = u.shape[0]
    return pl.pallas_call(
        _s5_step_kernel,
        out_shape=[jax.ShapeDtypeStruct((nb, S5_WIDTH), F32),
                   jax.ShapeDtypeStruct((nb, S5_LANES), F32), jax.ShapeDtypeStruct((nb, S5_LANES), F32)],
        name="s5_step",
    )(u, h0_re, h0_im, prm["a_re"], prm["a_im"], prm["bmat"], prm["cmat"], prm["d"], prm["w_glu"], prm["b_glu"])


def _s5_params(lam_re, lam_im, log_dt, b_re, b_im, c_re, c_im, d, w_glu, b_glu):
    lam = lax.complex(lam_re, lam_im)
    dt = jnp.exp(log_dt)[:, None]
    a_bar = jnp.exp(lam * dt)
    b_bar = ((a_bar - 1.0) / lam)[..., None] * lax.complex(b_re, b_im)
    gk = S5_GROUPS // _S5_KT
    eye = jnp.eye(gk, dtype=F32)

    def in_tile(x):
        x = x.reshape(_S5_KT, gk, S5_STATE, S5_GROUP)
        return jnp.einsum("kgpc,gh->kgchp", x, eye).reshape(_S5_KT, _S5_KW, _S5_KL)

    def out_tile(x):
        x = x.reshape(_S5_KT, gk, S5_GROUP, S5_STATE)
        return jnp.einsum("kgcp,gh->kgphc", x, eye).reshape(_S5_KT, _S5_KL, _S5_KW)

    bmat = jnp.concatenate([in_tile(jnp.real(b_bar)), in_tile(jnp.imag(b_bar))], axis=2)
    cmat = jnp.concatenate([out_tile(c_re), out_tile(-c_im)], axis=1)
    return {
        "a_re": jnp.real(a_bar).reshape(1, S5_LANES), "a_im": jnp.imag(a_bar).reshape(1, S5_LANES),
        "bmat": bmat, "cmat": cmat, "d": d.reshape(1, S5_WIDTH), "w_glu": w_glu, "b_glu": b_glu.reshape(1, S5_WIDTH),
    }


_CONV_PAD = 8


def _conv_silu(x, prev_ref, w_ref, b_ref):
    rows = x.shape[0]
    w = w_ref[...]
    prev = prev_ref[...]
    sub = lax.broadcasted_iota(jnp.int32, prev.shape, 0)
    out = b_ref[...]
    for i in range(SSD_CONV):
        s = SSD_CONV - 1 - i
        if s == 0:
            shifted = x
        else:
            rolled = pltpu.roll(x, s, 0)
            head = jnp.where(sub < s, pltpu.roll(prev, s, 0), rolled[0:_CONV_PAD])
            shifted = jnp.concatenate([head, rolled[_CONV_PAD:]], axis=0)
        out = out + w[i:i + 1] * shifted
    prev_ref[...] = x[rows - _CONV_PAD:rows, :]
    return jax.nn.silu(out)


def _ssd_in_kernel(x_ref, g_ref, wz_ref, wx_ref, wdt_ref, cw_ref, cb_ref, z_ref, xc_ref, dt_ref, tail_ref, xf,
                   *, tiles_per_seq):
    i = pl.program_id(0)
    tm = x_ref.shape[0]

    @pl.when(i % tiles_per_seq == 0)
    def _():
        xf[...] = jnp.zeros(xf.shape, F32)

    h = _rms(x_ref[...], g_ref[...]).astype(BF16)
    z_ref[...] = _mm(h, wz_ref[...])
    dt_ref[...] = _mm(h, wdt_ref[...])
    xbc = _mm(h, wx_ref[...])
    tail_ref[0] = xbc[tm - _CONV_PAD:tm, :]
    xc_ref[...] = _conv_silu(xbc, xf, cw_ref, cb_ref)


def _ssd_in(x, g, prm, seq_len, tm, name):
    m, kdim = x.shape
    tiles_per_seq = seq_len // tm
    full = lambda a: pl.BlockSpec(a.shape, lambda i: (0,) * a.ndim)
    rows = lambda n: pl.BlockSpec((tm, n), lambda i: (i, 0))
    wz, wx, wdt = prm["w_z"], prm["w_xbc"], prm["w_dt_g"]
    return pl.pallas_call(
        functools.partial(_ssd_in_kernel, tiles_per_seq=tiles_per_seq),
        grid=(m // tm,),
        in_specs=[rows(kdim), pl.BlockSpec((1, kdim), lambda i: (0, 0)), full(wz), full(wx), full(wdt),
                  full(prm["conv_w"]), full(prm["conv_b"])],
        out_specs=[rows(wz.shape[1]), rows(wx.shape[1]), rows(wdt.shape[1]),
                   pl.BlockSpec((1, _CONV_PAD, wx.shape[1]), lambda i: (i // tiles_per_seq, 0, 0))],
        out_shape=[jax.ShapeDtypeStruct((m, wz.shape[1]), F32), jax.ShapeDtypeStruct((m, wx.shape[1]), F32),
                   jax.ShapeDtypeStruct((m, wdt.shape[1]), F32),
                   jax.ShapeDtypeStruct((m // seq_len, _CONV_PAD, wx.shape[1]), F32)],
        scratch_shapes=[pltpu.VMEM((_CONV_PAD, wx.shape[1]), F32)],
        compiler_params=pltpu.CompilerParams(dimension_semantics=("arbitrary",)),
        name=name,
    )(x, g.reshape(1, kdim), wz, wx, wdt, prm["conv_w"], prm["conv_b"])


def _ssd_chunk_kernel(xs_ref, bm_ref, cm_ref, dt_ref, dtb_ref, a_ref, dexp_ref, y_ref, st_ref):
    c = pl.program_id(2)
    q = SSD_CHUNK

    @pl.when(c == 0)
    def _():
        st_ref[...] = jnp.zeros(st_ref.shape, F32)

    xs = xs_ref[0]
    bm16 = bm_ref[0].astype(BF16)
    cm16 = cm_ref[0].astype(BF16)

    dt = jax.nn.softplus(dt_ref[0] + dtb_ref[0])
    da = dt * a_ref[0]
    row = lax.broadcasted_iota(jnp.int32, (q, q), 0)
    col = lax.broadcasted_iota(jnp.int32, (q, q), 1)
    causal = row >= col
    acs = jnp.dot(causal.astype(F32), da, preferred_element_type=F32, precision=HIGHEST)
    acs_t = acs.T
    acs_last = acs[q - 1:q, :]
    e_acs = jnp.exp(acs)
    dec_s = jnp.exp(acs_last - acs)
    chunk_dec = jnp.exp(acs_last)
    cb = _mm(cm16, bm16, dims=_NT)

    first = lax.broadcasted_iota(jnp.int32, (q, LANES), 1) < SSD_HEADDIM

    def pair_cols(v, i):
        return jnp.where(first, v[:, 2 * i:2 * i + 1], v[:, 2 * i + 1:2 * i + 2])

    ys = []
    for i in range(SSD_HPG // 2):
        pair = slice(i * LANES, (i + 1) * LANES)
        xdt = xs[:, pair] * pair_cols(dt, i)
        xdt16 = xdt.astype(BF16)
        y_head = []
        for j in (2 * i, 2 * i + 1):
            seg = acs[:, j:j + 1] - acs_t[j:j + 1, :]
            lmat = jnp.exp(jnp.where(causal, seg, -jnp.inf))
            y_head.append(_mm(cb * lmat, xdt16))
        y_diag = jnp.where(first, y_head[0], y_head[1])
        h = st_ref[0, 2 * i:2 * i + 2].reshape(LANES, SSD_STATE)
        y_off = _mm(cm16, h, dims=_NT) * pair_cols(e_acs, i)
        cdec = jnp.concatenate([jnp.broadcast_to(chunk_dec[:, j:j + 1], (SSD_HEADDIM, SSD_STATE))
                                for j in (2 * i, 2 * i + 1)], axis=0)
        h_new = cdec * h + _mm(xdt * pair_cols(dec_s, i), bm16, dims=_TN)
        st_ref[0, 2 * i:2 * i + 2] = h_new.reshape(2, SSD_HEADDIM, SSD_STATE)
        ys.append(y_diag + y_off)
    y_ref[0] = jnp.concatenate(ys, axis=1) + dexp_ref[...] * xs


def _ssd_prompt(xc, dtg, prm):
    nb, l, _ = xc.shape
    nc = l // SSD_CHUNK
    q = SSD_CHUNK
    boff = SSD_INNER // LANES
    coff = boff + SSD_GROUPS
    grp = lambda b, g, c: (g, 0, 0)
    return pl.pallas_call(
        _ssd_chunk_kernel,
        grid=(nb, SSD_GROUPS, nc),
        in_specs=[pl.BlockSpec((1, q, SSD_GW), lambda b, g, c: (b, c, g)),
                  pl.BlockSpec((1, q, LANES), lambda b, g, c: (b, c, boff + g)),
                  pl.BlockSpec((1, q, LANES), lambda b, g, c: (b, c, coff + g)),
                  pl.BlockSpec((1, q, LANES), lambda b, g, c: (b, c, g)),
                  pl.BlockSpec((1, 1, LANES), grp), pl.BlockSpec((1, 1, LANES), grp),
                  pl.BlockSpec((1, SSD_GW), lambda b, g, c: (0, g))],
        out_specs=[pl.BlockSpec((1, q, SSD_GW), lambda b, g, c: (b, c, g)),
                   pl.BlockSpec((1, SSD_HPG, SSD_HEADDIM, SSD_STATE), lambda b, g, c: (b, g, 0, 0))],
        out_shape=[jax.ShapeDtypeStruct((nb, l, SSD_INNER), F32),
                   jax.ShapeDtypeStruct((nb, SSD_HEADS, SSD_HEADDIM, SSD_STATE), F32)],
        compiler_params=pltpu.CompilerParams(dimension_semantics=("parallel", "parallel", "arbitrary")),
        name="ssd_prompt",
    )(xc, xc, xc, dtg, prm["dt_bias_g"], prm["a_g"], prm["d_exp"])


def _ssd_step_kernel(x_ref, buf_ref, cw_ref, cb_ref, dt64_ref, dtb64_ref, dt128_ref, dtb128_ref, a128_ref,
                     dexp_ref, st_ref, y_ref, so_ref):
    w = cw_ref[...]
    buf = buf_ref[0]
    conv = cb_ref[...]
    for i in range(SSD_CONV - 1):
        conv = conv + w[i:i + 1] * buf[i:i + 1]
    conv = conv + w[SSD_CONV - 1:SSD_CONV] * x_ref[0]
    xc = jax.nn.silu(conv)
    xs = xc[:, :SSD_INNER]
    xdt = xs * jax.nn.softplus(dt64_ref[0] + dtb64_ref[...])
    dt = jax.nn.softplus(dt128_ref[0] + dtb128_ref[...])
    dec = jnp.exp(dt * a128_ref[...])
    row = lax.broadcasted_iota(jnp.int32, (LANES, LANES), 0)
    col = lax.broadcasted_iota(jnp.int32, (LANES, LANES), 1)
    diag = row == col
    ys = []
    for i in range(SSD_HEADS // 2):
        g = (2 * i) // SSD_HPG
        bm = xc[:, SSD_INNER + g * SSD_STATE:SSD_INNER + (g + 1) * SSD_STATE]
        cm = xc[:, SSD_INNER + (SSD_GROUPS + g) * SSD_STATE:SSD_INNER + (SSD_GROUPS + g + 1) * SSD_STATE]
        xp = jnp.broadcast_to(xdt[:, i * LANES:(i + 1) * LANES], (LANES, LANES))
        outer = _mm(jnp.where(diag, xp, 0.0), jnp.broadcast_to(bm, (LANES, LANES)), True)
        h = st_ref[0, 2 * i:2 * i + 2].reshape(LANES, SSD_STATE)
        dpair = jnp.concatenate([jnp.broadcast_to(dec[2 * i:2 * i + 1], (SSD_HEADDIM, LANES)),
                                 jnp.broadcast_to(dec[2 * i + 1:2 * i + 2], (SSD_HEADDIM, LANES))], axis=0)
        hn = dpair * h + outer
        so_ref[0, 2 * i:2 * i + 2] = hn.reshape(2, SSD_HEADDIM, SSD_STATE)
        ys.append(_mm(jnp.broadcast_to(cm, (8, SSD_STATE)), hn, True, _NT)[0:1])
    y_ref[0] = jnp.concatenate(ys, axis=1) + dexp_ref[...] * xs


def _dt_expand_kernel(dt_ref, e64_ref, e128_ref, o64_ref, o128_ref):
    dt = dt_ref[...]
    o64_ref[...] = _mm(dt, e64_ref[...], True)
    o128_ref[...] = _mm(dt, e128_ref[...], True)


def _dt_expand(dt, prm):
    nb = dt.shape[0]
    return pl.pallas_call(
        _dt_expand_kernel,
        out_shape=[jax.ShapeDtypeStruct((nb, SSD_INNER), F32), jax.ShapeDtypeStruct((nb, SSD_HEADS * LANES), F32)],
        name="ssd_dt_expand",
    )(dt, prm["expand_64"], prm["expand_128"])


def _ssd_step(xbc, conv_buf, dt64, dt128, state, prm):
    nb = xbc.shape[0]
    full = lambda shape: pl.BlockSpec(shape, lambda b: (0,) * len(shape))
    y, so = pl.pallas_call(
        _ssd_step_kernel,
        grid=(nb,),
        in_specs=[pl.BlockSpec((1, 1, SSD_CONV_DIM), lambda b: (b, 0, 0)),
                  pl.BlockSpec((1, SSD_CONV - 1, SSD_CONV_DIM), lambda b: (b, 0, 0)),
                  full((SSD_CONV, SSD_CONV_DIM)), full((1, SSD_CONV_DIM)),
                  pl.BlockSpec((1, 1, SSD_INNER), lambda b: (b, 0, 0)), full((1, SSD_INNER)),
                  pl.BlockSpec((1, SSD_HEADS, LANES), lambda b: (b, 0, 0)), full((SSD_HEADS, LANES)),
                  full((SSD_HEADS, LANES)), full((1, SSD_INNER)),
                  pl.BlockSpec((1, SSD_HEADS, SSD_HEADDIM, SSD_STATE), lambda b: (b, 0, 0, 0))],
        out_specs=[pl.BlockSpec((1, 1, SSD_INNER), lambda b: (b, 0, 0)),
                   pl.BlockSpec((1, SSD_HEADS, SSD_HEADDIM, SSD_STATE), lambda b: (b, 0, 0, 0))],
        out_shape=[jax.ShapeDtypeStruct((nb, 1, SSD_INNER), F32),
                   jax.ShapeDtypeStruct((nb, SSD_HEADS, SSD_HEADDIM, SSD_STATE), F32)],
        compiler_params=pltpu.CompilerParams(dimension_semantics=("parallel",)),
        name="ssd_step",
    )(xbc.reshape(nb, 1, SSD_CONV_DIM), conv_buf, prm["conv_w"], prm["conv_b"],
      dt64.reshape(nb, 1, SSD_INNER), prm["dt_bias_64"], dt128.reshape(nb, SSD_HEADS, LANES),
      prm["dt_bias_128"], prm["a_128"], prm["d_exp"], state)
    return y.reshape(nb, SSD_INNER), so


_GROUP_LANE0 = MOE_EXPERTS
_ROUTE_LANE0 = 64


def _pack_bf16_halves(x):
    n = x.shape[1] // 2
    lo = lax.bitcast_convert_type(x[:, :n].astype(BF16).astype(F32), jnp.uint32)
    hi = lax.bitcast_convert_type(x[:, n:].astype(BF16).astype(F32), jnp.uint32)
    return (lo >> 16) | hi


def _unpack_bf16_halves(p):
    lo = lax.bitcast_convert_type(p << 16, F32)
    hi = lax.bitcast_convert_type(p & jnp.uint32(0xFFFF0000), F32)
    return jnp.concatenate([lo, hi], axis=1)


def _moe_router_kernel(y_ref, g_ref, wr_ref, br_ref, xn_ref, comb_ref, cnt_ref):
    @pl.when(pl.program_id(0) == 0)
    def _():
        cnt_ref[...] = jnp.zeros(cnt_ref.shape, F32)

    xn = _rms(y_ref[...], g_ref[...])
    xn_ref[...] = _pack_bf16_halves(xn) if xn_ref.dtype == jnp.uint32 else xn.astype(xn_ref.dtype)
    lg = _mm(xn, wr_ref[...], True) + br_ref[...]
    lane = lax.broadcasted_iota(jnp.int32, lg.shape, 1)
    is_group = (lane >= _GROUP_LANE0) & (lane < _GROUP_LANE0 + MOE_GROUPS)
    gl = jnp.where(is_group, lg, NEG)
    gmax = jnp.max(gl, axis=-1, keepdims=True)
    g_p = 1.0 / jnp.sum(jnp.exp(gl - gmax), axis=-1, keepdims=True)
    gidx = jnp.min(jnp.where(gl == gmax, lane - _GROUP_LANE0, MOE_GROUPS), axis=-1, keepdims=True)
    el = jnp.where((lane < MOE_EXPERTS) & (lane // MOE_EPG == gidx), lg, NEG)
    m1 = jnp.max(el, axis=-1, keepdims=True)
    i1 = jnp.min(jnp.where(el == m1, lane, LANES), axis=-1, keepdims=True)
    el2 = jnp.where(lane == i1, NEG, el)
    m2 = jnp.max(el2, axis=-1, keepdims=True)
    i2 = jnp.min(jnp.where(el2 == m2, lane, LANES), axis=-1, keepdims=True)
    e2 = jnp.exp(m2 - m1)
    den = 1.0 + e2
    g1 = g_p / den
    g2 = g_p * (e2 / den)
    oh1 = lane == i1
    oh2 = lane == i2
    comb = jnp.where(oh1, g1, 0.0) + jnp.where(oh2, g2, 0.0)
    picks = oh1.astype(F32) + oh2.astype(F32)
    tm = picks.shape[0]
    before = lax.broadcasted_iota(jnp.int32, (tm, tm), 0) > lax.broadcasted_iota(jnp.int32, (tm, tm), 1)
    seen = cnt_ref[...] + _mm(before.astype(F32), picks)
    r1 = jnp.sum(jnp.where(oh1, seen, 0.0), axis=-1, keepdims=True)
    r2 = jnp.sum(jnp.where(oh2, seen, 0.0), axis=-1, keepdims=True)
    cnt_ref[...] += jnp.sum(picks, axis=0, keepdims=True)
    for k, v in enumerate((i1.astype(F32), i2.astype(F32), g1, g2, r1, r2)):
        comb = jnp.where(lane == _ROUTE_LANE0 + k, v, comb)
    comb_ref[...] = comb


def _moe_router(y, g, wr, br, tm, name, xn_dtype):
    m, d = y.shape
    dx = d // 2 if xn_dtype == jnp.uint32 else d
    return pl.pallas_call(
        _moe_router_kernel,
        grid=(m // tm,),
        in_specs=[pl.BlockSpec((tm, d), lambda i: (i, 0)), pl.BlockSpec((1, d), lambda i: (0, 0)),
                  pl.BlockSpec((d, LANES), lambda i: (0, 0)), pl.BlockSpec((1, LANES), lambda i: (0, 0))],
        out_specs=[pl.BlockSpec((tm, dx), lambda i: (i, 0)), pl.BlockSpec((tm, LANES), lambda i: (i, 0)),
                   pl.BlockSpec((1, LANES), lambda i: (0, 0))],
        out_shape=[jax.ShapeDtypeStruct((m, dx), xn_dtype), jax.ShapeDtypeStruct((m, LANES), F32),
                   jax.ShapeDtypeStruct((1, LANES), F32)],
        compiler_params=pltpu.CompilerParams(dimension_semantics=("arbitrary",)),
        name=name,
    )(y, g.reshape(1, d), wr, br)


def _moe_dense_kernel(res_ref, xn_ref, comb_ref, wg_ref, wu_ref, wd_ref, o_ref, acc, *, precise):
    e = pl.program_id(1)

    @pl.when(e == 0)
    def _():
        acc[...] = jnp.zeros(acc.shape, F32)

    x = xn_ref[...]
    hdn = jax.nn.silu(_mm(x, wg_ref[0], precise)) * _mm(x, wu_ref[0], precise)
    comb = comb_ref[...]
    lane = lax.broadcasted_iota(jnp.int32, comb.shape, 1)
    gate = jnp.sum(jnp.where(lane == e, comb, 0.0), axis=-1, keepdims=True)
    acc[...] += gate * _mm(hdn, wd_ref[0], precise)

    @pl.when(e == pl.num_programs(1) - 1)
    def _():
        o_ref[...] = res_ref[...] + acc[...]


def _moe_dense(res, xn, comb, wg, wu, wd, e0, tm, name, precise):
    m, d = res.shape
    return pl.pallas_call(
        functools.partial(_moe_dense_kernel, precise=precise),
        grid=(m // tm, MOE_EXPERTS),
        in_specs=[pl.BlockSpec((tm, d), lambda i, e: (i, 0)), pl.BlockSpec((tm, d), lambda i, e: (i, 0)),
                  pl.BlockSpec((tm, LANES), lambda i, e: (i, 0)),
                  pl.BlockSpec((1, d, MOE_FF), lambda i, e: (e0 + e, 0, 0)),
                  pl.BlockSpec((1, d, MOE_FF), lambda i, e: (e0 + e, 0, 0)),
                  pl.BlockSpec((1, MOE_FF, d), lambda i, e: (e0 + e, 0, 0))],
        out_specs=pl.BlockSpec((tm, d), lambda i, e: (i, 0)),
        out_shape=jax.ShapeDtypeStruct((m, d), F32),
        scratch_shapes=[pltpu.VMEM((tm, d), F32)],
        compiler_params=pltpu.CompilerParams(dimension_semantics=("parallel", "arbitrary")),
        name=name,
    )(res, xn, comb, wg, wu, wd)


def _moe(y, g, prm, tm_r, tm_e, tag, precise):
    xn, comb, _ = _moe_router(y, g, prm["wr"], prm["br"], tm_r, "moe_router_" + tag, F32 if precise else BF16)
    return _moe_dense(y, xn, comb, prm["wg"], prm["wu"], prm["wd"], prm["e0"], tm_e, "moe_experts_" + tag, precise)


def _moe_grouped_kernel(te_ref, nt_ref, x_ref, wg_ref, wu_ref, wd_ref, o_ref):
    del te_ref
    active = pl.program_id(0) < nt_ref[0]

    @pl.when(active)
    def _():
        x = _unpack_bf16_halves(x_ref[...]).astype(BF16)
        hdn = jax.nn.silu(_mm(x, wg_ref[0])) * _mm(x, wu_ref[0])
        o_ref[...] = _mm(hdn, wd_ref[0]).astype(o_ref.dtype)

    @pl.when(jnp.logical_not(active))
    def _():
        o_ref[...] = jnp.zeros(o_ref.shape, o_ref.dtype)


def _moe_grouped(x_rows, tile_expert, n_tiles, wg, wu, wd, tm, name):
    r = x_rows.shape[0]
    d = wg.shape[1]
    grid_spec = pltpu.PrefetchScalarGridSpec(
        num_scalar_prefetch=2,
        grid=(r // tm,),
        in_specs=[pl.BlockSpec((tm, x_rows.shape[1]), lambda i, te, nt: (i, 0)),
                  pl.BlockSpec((1, d, MOE_FF), lambda i, te, nt: (te[i], 0, 0)),
                  pl.BlockSpec((1, d, MOE_FF), lambda i, te, nt: (te[i], 0, 0)),
                  pl.BlockSpec((1, MOE_FF, d), lambda i, te, nt: (te[i], 0, 0))],
        out_specs=pl.BlockSpec((tm, d), lambda i, te, nt: (i, 0)),
    )
    return pl.pallas_call(
        _moe_grouped_kernel,
        grid_spec=grid_spec,
        out_shape=jax.ShapeDtypeStruct((r, d), F32),
        compiler_params=pltpu.CompilerParams(dimension_semantics=("arbitrary",)),
        name=name,
    )(tile_expert, n_tiles, x_rows, wg, wu, wd)


def _moe_sparse(y, g, prm, tm_r, tm_e, tag):
    t, d = y.shape
    xn, route, cnt = _moe_router(y, g, prm["wr"], prm["br"], tm_r, "moe_router_" + tag, jnp.uint32)
    lanes = lambda k: route[:, _ROUTE_LANE0 + k:_ROUTE_LANE0 + k + 2]
    ids = lanes(0).astype(jnp.int32)
    gates = lanes(2)
    rank = lanes(4).astype(jnp.int32)
    counts = cnt[0, :MOE_EXPERTS].astype(jnp.int32)
    rows = 2 * t + MOE_EXPERTS * tm_e
    pcounts = ((counts + tm_e - 1) // tm_e) * tm_e
    pend = jnp.cumsum(pcounts)
    pstart = pend - pcounts
    pos = pstart[ids] + rank
    tile_start = jnp.arange(rows // tm_e, dtype=jnp.int32) * tm_e
    tile_expert = jnp.minimum(jnp.sum((pend[None, :] <= tile_start[:, None]).astype(jnp.int32), axis=1),
                              MOE_EXPERTS - 1)
    n_tiles = (pend[-1:] // tm_e).astype(jnp.int32)
    token = jnp.broadcast_to(jnp.arange(t, dtype=jnp.int32)[:, None], (t, 2))
    src = jnp.zeros((rows,), jnp.int32).at[pos.reshape(-1)].set(token.reshape(-1), unique_indices=True)
    x_rows = xn[src]
    out_rows = _moe_grouped(x_rows, tile_expert + prm["e0"], n_tiles, prm["wg"], prm["wu"], prm["wd"], tm_e,
                            "moe_experts_" + tag)
    return y + gates[:, 0:1] * out_rows[pos[:, 0]] + gates[:, 1:2] * out_rows[pos[:, 1]]


def _moe_params(layer, w_group, b_group, w_expert, b_expert, w_gate, w_up, w_down):
    d = w_group.shape[0]
    pad = LANES - MOE_EXPERTS - MOE_GROUPS
    wr = jnp.concatenate([w_expert, w_group, jnp.zeros((d, pad), F32)], axis=1)
    br = jnp.concatenate([b_expert, b_group, jnp.zeros((pad,), F32)]).reshape(1, LANES)
    flat = lambda w: w.reshape((-1,) + w.shape[2:])
    return {"wr": wr, "br": br, "wg": flat(w_gate), "wu": flat(w_up), "wd": flat(w_down), "e0": layer * MOE_EXPERTS}


def _ssd_params(w_in, conv_w, conv_b, dt_bias, a_log, d, norm, w_out):
    a = -jnp.exp(a_log)
    w_dt = w_in[:, SSD_INNER + SSD_CONV_DIM:]

    def grouped(x):
        x = x.reshape(x.shape[:-1] + (SSD_GROUPS, SSD_HPG))
        x = jnp.pad(x, [(0, 0)] * (x.ndim - 1) + [(0, LANES - SSD_HPG)])
        return x.reshape(x.shape[:-2] + (SSD_GROUPS * LANES,))

    w_z = w_in[:, :SSD_INNER]
    w_xbc = w_in[:, SSD_INNER:SSD_INNER + SSD_CONV_DIM]
    eye = jnp.eye(SSD_HEADS, dtype=F32)
    return {
        "w_z": w_z.astype(BF16), "w_xbc": w_xbc.astype(BF16), "w_dt_g": grouped(w_dt).astype(BF16),
        "w_z_f32": w_z, "w_xbc_f32": w_xbc, "w_dt_f32": w_dt, "w_out_f32": w_out,
        "expand_64": jnp.repeat(eye, SSD_HEADDIM, axis=1), "expand_128": jnp.repeat(eye, LANES, axis=1),
        "conv_w": conv_w, "conv_b": conv_b.reshape(1, SSD_CONV_DIM),
        "dt_bias_g": grouped(dt_bias).reshape(SSD_GROUPS, 1, LANES),
        "a_g": grouped(a).reshape(SSD_GROUPS, 1, LANES),
        "dt_bias_64": jnp.repeat(dt_bias, SSD_HEADDIM).reshape(1, SSD_INNER),
        "dt_bias_128": jnp.broadcast_to(dt_bias[:, None], (SSD_HEADS, LANES)),
        "a_128": jnp.broadcast_to(a[:, None], (SSD_HEADS, LANES)),
        "d_exp": jnp.repeat(d, SSD_HEADDIM).reshape(1, SSD_INNER),
        "norm": norm, "w_out": w_out.astype(BF16),
    }


_TM = 512


def kernel(x_prompt, x_sample, cache_k, cache_v, page_table, state_s5_re, state_s5_im, state_ssd, state_conv, norm_mix, norm_ffn, norm_final, even_w_in, even_w_out, diff_lam_q1, diff_lam_k1, diff_lam_q2, diff_lam_k2, diff_subln, s5_lam_re, s5_lam_im, s5_log_dt, s5_b_re, s5_b_im, s5_c_re, s5_c_im, s5_d, s5_w_glu, s5_b_glu, ssd_w_in, ssd_conv_w, ssd_conv_b, ssd_dt_bias, ssd_a_log, ssd_d, ssd_norm, ssd_w_out, moe_w_group, moe_b_group, moe_w_expert, moe_b_expert, moe_w_gate, moe_w_up, moe_w_down):
    bp, sp, d = x_prompt.shape
    bs = x_sample.shape[0]
    tp = bp * sp
    depth = norm_mix.shape[0]
    y_p = x_prompt.reshape(tp, d)
    y_s = x_sample.reshape(bs, d)
    outs = {n: [] for n in ("k_p", "v_p", "k_s", "v_s", "re_p", "im_p", "re_s", "im_s", "ssd_p", "ssd_s", "cv_p", "cv_s")}

    for li in range(depth):
        if li % 2 == 0:
            e = li // 2
            lam_init = 0.8 - 0.6 * math.exp(-0.3 * li)
            lam = (jnp.exp(jnp.sum(diff_lam_q1[e] * diff_lam_k1[e])) - jnp.exp(jnp.sum(diff_lam_q2[e] * diff_lam_k2[e]))
                   + lam_init).astype(F32)
            ws32 = [even_w_in[e][:, i * ATT_WIDTH:(i + 1) * ATT_WIDTH] for i in range(4)]
            ws = [w.astype(BF16) for w in ws32]
            w_out32 = [even_w_out[e][:ATT_WIDTH], even_w_out[e][ATT_WIDTH:]]
            w_out_a, w_out_s = [w.astype(BF16) for w in w_out32]
            s5p = _s5_params(s5_lam_re[e], s5_lam_im[e], s5_log_dt[e], s5_b_re[e], s5_b_im[e], s5_c_re[e], s5_c_im[e],
                             s5_d[e], s5_w_glu[e], s5_b_glu[e])
            dts = [(BF16,), (F32, BF16), (F32, _TILE_T), (F32,)]
            q, k, k16, v, vt, u = _norm_matmul(y_p, norm_mix[li], ws, dts, _TM, "even_in_p")
            o = _attn_prompt(q.reshape(bp, sp, ATT_WIDTH), k16.reshape(bp, sp, ATT_WIDTH), vt, lam, diff_subln[e],
                             1.0 - lam_init)
            s5o, h_re, h_im = _s5_prompt(u.reshape(bp, sp, S5_WIDTH), s5p)
            y_p = _matmul_res(y_p, [o.reshape(tp, ATT_WIDTH), s5o.reshape(tp, S5_WIDTH)], [w_out_a, w_out_s], _TM,
                              "even_out_p")
            outs["k_p"].append(k.reshape(bp, sp, ATT_HEADS, HEAD_W))
            outs["v_p"].append(v.reshape(bp, sp, ATT_HEADS, HEAD_W))
            outs["re_p"].append(h_re.reshape(bp, S5_GROUPS, S5_STATE))
            outs["im_p"].append(h_im.reshape(bp, S5_GROUPS, S5_STATE))
            q, k, v, u = _norm_matmul(y_s, norm_mix[li], ws32, [(F32,), (F32,), (F32,), (F32,)], bs, "even_in_s", True)
            o = _attn_decode(q, k, v, cache_k, cache_v, page_table + e * cache_k.shape[1], lam, diff_subln[e],
                             1.0 - lam_init)
            s5o, h_re, h_im = _s5_step(u, state_s5_re[e].reshape(bs, S5_LANES), state_s5_im[e].reshape(bs, S5_LANES), s5p)
            y_s = _matmul_res(y_s, [o, s5o], w_out32, bs, "even_out_s", True)
            outs["k_s"].append(k.reshape(bs, 1, ATT_HEADS, HEAD_W))
            outs["v_s"].append(v.reshape(bs, 1, ATT_HEADS, HEAD_W))
            outs["re_s"].append(h_re.reshape(bs, S5_GROUPS, S5_STATE))
            outs["im_s"].append(h_im.reshape(bs, S5_GROUPS, S5_STATE))
        else:
            o_ = li // 2
            sp_ = _ssd_params(ssd_w_in[o_], ssd_conv_w[o_], ssd_conv_b[o_], ssd_dt_bias[o_], ssd_a_log[o_], ssd_d[o_],
                              ssd_norm[o_], ssd_w_out[o_])
            z, xc, dtg, tail = _ssd_in(y_p, norm_mix[li], sp_, sp, 256, "ssd_in_p")
            yssd, st = _ssd_prompt(xc.reshape(bp, sp, SSD_CONV_DIM), dtg.reshape(bp, sp, SSD_GROUPS * LANES), sp_)
            y_p = _gated_norm_matmul(y_p, yssd.reshape(tp, SSD_INNER), z, sp_["norm"], sp_["w_out"], 256, "ssd_out_p")
            outs["ssd_p"].append(st)
            outs["cv_p"].append(tail[:, _CONV_PAD - (SSD_CONV - 1):, :])
            z, xbc, dtc = _norm_matmul(y_s, norm_mix[li], [sp_["w_z_f32"], sp_["w_xbc_f32"], sp_["w_dt_f32"]],
                                       [(F32,), (F32,), (F32,)], bs, "ssd_in_s", True)
            dt64, dt128 = _dt_expand(dtc, sp_)
            yssd, st = _ssd_step(xbc, state_conv[o_], dt64, dt128, state_ssd[o_], sp_)
            y_s = _gated_norm_matmul(y_s, yssd, z, sp_["norm"], sp_["w_out_f32"], bs, "ssd_out_s", True)
            outs["ssd_s"].append(st)
            outs["cv_s"].append(jnp.concatenate([state_conv[o_][:, 1:], xbc[:, None, :]], axis=1))
        mp = _moe_params(li, moe_w_group[li], moe_b_group[li], moe_w_expert[li], moe_b_expert[li], moe_w_gate,
                         moe_w_up, moe_w_down)
        y_p = _moe_sparse(y_p, norm_ffn[li], mp, _TM, 256, "p")
        y_s = _moe(y_s, norm_ffn[li], mp, bs, bs, "s", li + 1 < depth)

    y_prompt = _rmsnorm(y_p, norm_final, _TM, "final_p").reshape(bp, sp, d)
    y_sample = _rmsnorm(y_s, norm_final, bs, "final_s").reshape(bs, 1, d)
    st = lambda n: jnp.stack(outs[n])
    return (y_prompt, y_sample, st("k_p"), st("v_p"), st("k_s"), st("v_s"), st("re_p"), st("im_p"), st("re_s"),
            st("im_s"), st("ssd_p"), st("ssd_s"), st("cv_p"), st("cv_s"))
```

```python
import functools
import math

import jax
import jax.numpy as jnp
from jax import lax
from jax.experimental import pallas as pl
from jax.experimental.pallas import tpu as pltpu

F32 = jnp.float32
BF16 = jnp.bfloat16
HIGHEST = lax.Precision.HIGHEST

D_MODEL = 1024
NORM_EPS = 1e-6
PAGE_SIZE = 128
ATT_HEADS = 4
ATT_D = 64
ATT_WIDTH = ATT_HEADS * 2 * ATT_D
HEAD_W = 2 * ATT_D
S5_WIDTH = 512
S5_GROUP = 16
S5_GROUPS = 32
S5_STATE = 64
S5_LANES = S5_GROUPS * S5_STATE
SSD_INNER = 2048
SSD_HEADDIM = 64
SSD_HEADS = 32
SSD_GROUPS = 4
SSD_HPG = 8
SSD_STATE = 128
SSD_CONV = 4
SSD_CONV_DIM = SSD_INNER + 2 * SSD_GROUPS * SSD_STATE
SSD_CHUNK = 128
SSD_GW = SSD_HPG * SSD_HEADDIM
MOE_GROUPS = 4
MOE_EPG = 8
MOE_EXPERTS = 32
MOE_FF = 256
LANES = 128
NEG = -1e30

_NT = (((1,), (1,)), ((), ()))
_TN = (((0,), (0,)), ((), ()))


def _rms(x, g):
    return x * lax.rsqrt(jnp.mean(x * x, axis=-1, keepdims=True) + NORM_EPS) * g


def _split(x):
    hi = x.astype(BF16)
    return hi, (x - hi.astype(F32)).astype(BF16)


def _mm(a, b, precise=False, dims=None):
    dot = jnp.dot if dims is None else functools.partial(lax.dot_general, dimension_numbers=dims)
    if not precise:
        return dot(a.astype(BF16), b.astype(BF16), preferred_element_type=F32)
    m = a.shape[0]
    a_hi, a_lo = _split(a)
    b_hi, b_lo = _split(b)
    r = dot(jnp.concatenate([a_hi, a_lo], axis=0), b_hi, preferred_element_type=F32)
    return r[:m] + r[m:] + dot(a_hi, b_lo, preferred_element_type=F32)


def _norm_matmul_kernel(x_ref, g_ref, *refs, out_dtypes, precise):
    n = len(out_dtypes)
    w_refs = refs[:n]
    o_refs = refs[n:]
    h = _rms(x_ref[...], g_ref[...])
    if not precise:
        h = h.astype(BF16)
    k = 0
    for w_ref, dts in zip(w_refs, out_dtypes):
        r = _mm(h, w_ref[...], precise)
        for dt in dts:
            if dt == _TILE_T:
                o_refs[k][0] = r.T.astype(BF16)
            else:
                o_refs[k][...] = r.astype(dt)
            k += 1


_TILE_T = "bf16 row tiles, each transposed"


def _norm_matmul(x, g, ws, out_dtypes, tm, name, precise=False):
    m, kdim = x.shape
    out_shape, out_specs = [], []
    for w, dts in zip(ws, out_dtypes):
        for dt in dts:
            if dt == _TILE_T:
                out_shape.append(jax.ShapeDtypeStruct((m // tm, w.shape[1], tm), BF16))
                out_specs.append(pl.BlockSpec((1, w.shape[1], tm), lambda i: (i, 0, 0)))
            else:
                out_shape.append(jax.ShapeDtypeStruct((m, w.shape[1]), dt))
                out_specs.append(pl.BlockSpec((tm, w.shape[1]), lambda i: (i, 0)))
    return pl.pallas_call(
        functools.partial(_norm_matmul_kernel, out_dtypes=out_dtypes, precise=precise),
        grid=(m // tm,),
        in_specs=[pl.BlockSpec((tm, kdim), lambda i: (i, 0)), pl.BlockSpec((1, kdim), lambda i: (0, 0))]
        + [pl.BlockSpec(w.shape, lambda i: (0, 0)) for w in ws],
        out_specs=out_specs,
        out_shape=out_shape,
        compiler_params=pltpu.CompilerParams(dimension_semantics=("parallel",)),
        name=name,
    )(x, g.reshape(1, kdim), *ws)


def _matmul_res_kernel(res_ref, *refs, precise):
    n = (len(refs) - 1) // 2
    acc = res_ref[...]
    for a_ref, w_ref in zip(refs[:n], refs[n:2 * n]):
        acc = acc + _mm(a_ref[...], w_ref[...], precise)
    refs[-1][...] = acc


def _matmul_res(res, a_list, w_list, tm, name, precise=False):
    m, n = res.shape
    return pl.pallas_call(
        functools.partial(_matmul_res_kernel, precise=precise),
        grid=(m // tm,),
        in_specs=[pl.BlockSpec((tm, n), lambda i: (i, 0))]
        + [pl.BlockSpec((tm, a.shape[1]), lambda i: (i, 0)) for a in a_list]
        + [pl.BlockSpec(w.shape, lambda i: (0, 0)) for w in w_list],
        out_specs=pl.BlockSpec((tm, n), lambda i: (i, 0)),
        out_shape=jax.ShapeDtypeStruct((m, n), F32),
        compiler_params=pltpu.CompilerParams(dimension_semantics=("parallel",)),
        name=name,
    )(res, *a_list, *w_list)


def _gated_norm_matmul_kernel(res_ref, y_ref, z_ref, g_ref, w_ref, o_ref, *, precise):
    y = y_ref[...] * jax.nn.silu(z_ref[...])
    o_ref[...] = res_ref[...] + _mm(_rms(y, g_ref[...]), w_ref[...], precise)


def _gated_norm_matmul(res, y, z, g, w, tm, name, precise=False):
    m, n = res.shape
    kdim = y.shape[1]
    return pl.pallas_call(
        functools.partial(_gated_norm_matmul_kernel, precise=precise),
        grid=(m // tm,),
        in_specs=[pl.BlockSpec((tm, n), lambda i: (i, 0)), pl.BlockSpec((tm, kdim), lambda i: (i, 0)),
                  pl.BlockSpec((tm, kdim), lambda i: (i, 0)), pl.BlockSpec((1, kdim), lambda i: (0, 0)),
                  pl.BlockSpec(w.shape, lambda i: (0, 0))],
        out_specs=pl.BlockSpec((tm, n), lambda i: (i, 0)),
        out_shape=jax.ShapeDtypeStruct((m, n), F32),
        compiler_params=pltpu.CompilerParams(dimension_semantics=("parallel",)),
        name=name,
    )(res, y, z, g.reshape(1, kdim), w)


def _rmsnorm_kernel(x_ref, g_ref, o_ref):
    o_ref[...] = _rms(x_ref[...], g_ref[...])


def _rmsnorm(x, g, tm, name):
    m, n = x.shape
    return pl.pallas_call(
        _rmsnorm_kernel,
        grid=(m // tm,),
        in_specs=[pl.BlockSpec((tm, n), lambda i: (i, 0)), pl.BlockSpec((1, n), lambda i: (0, 0))],
        out_specs=pl.BlockSpec((tm, n), lambda i: (i, 0)),
        out_shape=jax.ShapeDtypeStruct((m, n), F32),
        compiler_params=pltpu.CompilerParams(dimension_semantics=("parallel",)),
        name=name,
    )(x, g.reshape(1, n))


def _split_q(q):
    lane = lax.broadcasted_iota(jnp.int32, q.shape, 1) % HEAD_W
    scale = ATT_D ** -0.5
    qs = q * jnp.asarray(scale, q.dtype)
    zero = jnp.zeros_like(qs)
    return jnp.where(lane < ATT_D, qs, zero), jnp.where(lane >= ATT_D, qs, zero)


def _subln(o, g, out_scale):
    return _rms(o, g) * out_scale


def _attn_prompt_kernel(lam_ref, q_ref, k_ref, vt_ref, g_ref, o_ref, qt_sc, m_sc, l_sc, acc_sc, *, tq, tk, out_scale):
    qi = pl.program_id(1)
    heads = range(ATT_HEADS)
    for h in heads:
        q1, q2 = _split_q(q_ref[0, :, h * HEAD_W:(h + 1) * HEAD_W])
        qt_sc[h] = jnp.concatenate([q1, q2], axis=0).astype(F32).T.astype(BF16)
    m_sc[...] = jnp.full(m_sc.shape, NEG, F32)
    l_sc[...] = jnp.zeros(l_sc.shape, F32)
    acc_sc[...] = jnp.zeros(acc_sc.shape, F32)

    def kv_step(j, masked):
        start = pl.multiple_of(j * tk, tk)
        for h in heads:
            kt = k_ref[0, pl.ds(start, tk), h * HEAD_W:(h + 1) * HEAD_W]
            st = jnp.dot(kt, qt_sc[h], preferred_element_type=F32)
            if masked:
                k_pos = start + lax.broadcasted_iota(jnp.int32, st.shape, 0)
                q_pos = qi * tq + lax.broadcasted_iota(jnp.int32, st.shape, 1) % tq
                st = jnp.where(k_pos <= q_pos, st, NEG)
            m_old = m_sc[h]
            m_new = jnp.maximum(m_old, jnp.max(st, axis=0, keepdims=True))
            p = jnp.exp(st - m_new)
            alpha = jnp.exp(m_old - m_new)
            l_sc[h] = alpha * l_sc[h] + jnp.sum(p, axis=0, keepdims=True)
            m_sc[h] = m_new
            vt = vt_ref[j, h * HEAD_W:(h + 1) * HEAD_W, :]
            acc_sc[h] = alpha * acc_sc[h] + jnp.dot(vt, p.astype(BF16), preferred_element_type=F32)

    n_full = (qi * tq) // tk

    def full_step(j, carry):
        kv_step(j, False)
        return carry

    lax.fori_loop(0, n_full, full_step, 0)
    kv_step(n_full, True)
    for h in heads:
        acc, l = acc_sc[h], l_sc[h]
        ot = acc[:, 0:tq] / l[:, 0:tq] - lam_ref[0, 0] * (acc[:, tq:2 * tq] / l[:, tq:2 * tq])
        o_ref[0, :, h * HEAD_W:(h + 1) * HEAD_W] = _subln(ot.T, g_ref[...], out_scale).astype(o_ref.dtype)


def _attn_prompt(q, k, vt, lam, subln, out_scale, tq=512):
    b, s, _ = q.shape
    tk = vt.shape[2]
    nblk = s // tk
    kern = functools.partial(_attn_prompt_kernel, tq=tq, tk=tk, out_scale=out_scale)
    return pl.pallas_call(
        kern,
        grid=(b, s // tq),
        in_specs=[pl.BlockSpec(memory_space=pltpu.SMEM),
                  pl.BlockSpec((1, tq, ATT_WIDTH), lambda bi, i: (bi, i, 0)),
                  pl.BlockSpec((1, s, ATT_WIDTH), lambda bi, i: (bi, 0, 0)),
                  pl.BlockSpec((nblk, ATT_WIDTH, tk), lambda bi, i: (bi, 0, 0)),
                  pl.BlockSpec((1, HEAD_W), lambda bi, i: (0, 0))],
        out_specs=pl.BlockSpec((1, tq, ATT_WIDTH), lambda bi, i: (bi, i, 0)),
        out_shape=jax.ShapeDtypeStruct((b, s, ATT_WIDTH), BF16),
        scratch_shapes=[pltpu.VMEM((ATT_HEADS, HEAD_W, 2 * tq), BF16), pltpu.VMEM((ATT_HEADS, 1, 2 * tq), F32),
                        pltpu.VMEM((ATT_HEADS, 1, 2 * tq), F32), pltpu.VMEM((ATT_HEADS, HEAD_W, 2 * tq), F32)],
        compiler_params=pltpu.CompilerParams(dimension_semantics=("parallel", "parallel")),
        name="attn_prompt",
    )(lam.reshape(1, 1), q, k, vt, subln.reshape(1, HEAD_W))


_DEC_PAGES = 16
_DEC_PROW = PAGE_SIZE * ATT_HEADS
_DEC_COLS = _DEC_PAGES * _DEC_PROW
_DEC_ROWS = 2 * ATT_HEADS


def _attn_decode_kernel(pt_ref, lam_ref, q_ref, kn_ref, vn_ref, g_ref, *refs, out_scale):
    del pt_ref
    k_refs = refs[:_DEC_PAGES]
    v_refs = refs[_DEC_PAGES:2 * _DEC_PAGES]
    o_ref = refs[2 * _DEC_PAGES]
    m_sc, l_sc, acc_sc = refs[2 * _DEC_PAGES + 1:]
    j = pl.program_id(1)

    q4 = q_ref[0] * (ATT_D ** -0.5)
    q8 = jnp.concatenate([q4, q4], axis=0)
    r8 = lax.broadcasted_iota(jnp.int32, (_DEC_ROWS, HEAD_W), 0)
    l8 = lax.broadcasted_iota(jnp.int32, (_DEC_ROWS, HEAD_W), 1)
    q8 = jnp.where(l8 // ATT_D == r8 // ATT_HEADS, q8, 0.0)

    @pl.when(j == 0)
    def _():
        m_sc[...] = jnp.full(m_sc.shape, NEG, F32)
        l_sc[...] = jnp.zeros(l_sc.shape, F32)
        acc_sc[...] = jnp.zeros(acc_sc.shape, F32)

    s = jnp.concatenate([_mm(q8, k_ref[0], True, _NT) for k_ref in k_refs], axis=1)
    rs = lax.broadcasted_iota(jnp.int32, s.shape, 0)
    cs = lax.broadcasted_iota(jnp.int32, s.shape, 1)
    s = jnp.where(cs % ATT_HEADS == rs % ATT_HEADS, s, NEG)
    m = m_sc[...]
    mn = jnp.maximum(m, jnp.max(s, axis=-1, keepdims=True))
    p = jnp.exp(s - mn)
    alpha = jnp.exp(m - mn)
    l = alpha * l_sc[...] + jnp.sum(p, axis=-1, keepdims=True)
    acc = alpha * acc_sc[...]
    for i, v_ref in enumerate(v_refs):
        acc = acc + _mm(p[:, i * _DEC_PROW:(i + 1) * _DEC_PROW], v_ref[0], True)
    m_sc[...] = mn
    l_sc[...] = l
    acc_sc[...] = acc

    @pl.when(j == pl.num_programs(1) - 1)
    def _():
        kn = jnp.concatenate([kn_ref[0], kn_ref[0]], axis=0)
        vn = jnp.concatenate([vn_ref[0], vn_ref[0]], axis=0)
        sn = jnp.sum(q8 * kn, axis=-1, keepdims=True)
        mf = jnp.maximum(mn, sn)
        pn = jnp.exp(sn - mf)
        af = jnp.exp(mn - mf)
        o8 = (af * acc + pn * vn) / (af * l + pn)
        o4 = o8[:ATT_HEADS] - lam_ref[0, 0] * o8[ATT_HEADS:]
        o_ref[0] = _subln(o4, g_ref[...], out_scale)


def _attn_decode(q, k_new, v_new, cache_k, cache_v, page_table, lam, subln, out_scale):
    b = q.shape[0]
    n_pages = page_table.shape[1]
    steps = n_pages // _DEC_PAGES
    ck = cache_k.reshape(-1, _DEC_PROW, HEAD_W)
    cv = cache_v.reshape(-1, _DEC_PROW, HEAD_W)

    def page_map(i):
        return lambda bi, j, pt: (pt[bi, j * _DEC_PAGES + i], 0, 0)

    row = lambda bi, j, pt: (bi, 0, 0)
    head_rows = pl.BlockSpec((1, ATT_HEADS, HEAD_W), row)
    page_specs = [pl.BlockSpec((1, _DEC_PROW, HEAD_W), page_map(i)) for i in range(_DEC_PAGES)]
    grid_spec = pltpu.PrefetchScalarGridSpec(
        num_scalar_prefetch=1,
        grid=(b, steps),
        in_specs=[pl.BlockSpec(memory_space=pltpu.SMEM), head_rows, head_rows, head_rows,
                  pl.BlockSpec((1, HEAD_W), lambda bi, j, pt: (0, 0))] + page_specs + page_specs,
        out_specs=head_rows,
        scratch_shapes=[pltpu.VMEM((_DEC_ROWS, 1), F32), pltpu.VMEM((_DEC_ROWS, 1), F32),
                        pltpu.VMEM((_DEC_ROWS, HEAD_W), F32)],
    )
    out = pl.pallas_call(
        functools.partial(_attn_decode_kernel, out_scale=out_scale),
        grid_spec=grid_spec,
        out_shape=jax.ShapeDtypeStruct((b, ATT_HEADS, HEAD_W), F32),
        compiler_params=pltpu.CompilerParams(dimension_semantics=("parallel", "arbitrary")),
        name="attn_decode",
    )(page_table, lam.reshape(1, 1), q.reshape(b, ATT_HEADS, HEAD_W), k_new.reshape(b, ATT_HEADS, HEAD_W),
      v_new.reshape(b, ATT_HEADS, HEAD_W), subln.reshape(1, HEAD_W), *([ck] * _DEC_PAGES), *([cv] * _DEC_PAGES))
    return out.reshape(b, ATT_WIDTH)


_S5_KT = 2
_S5_KW = S5_WIDTH // _S5_KT
_S5_KL = S5_LANES // _S5_KT
_S5_NT = S5_LANES // LANES


def _s5_input(u, bmat_ref, precise):
    re, im = [], []
    for kt in range(_S5_KT):
        r = _mm(u[:, kt * _S5_KW:(kt + 1) * _S5_KW], bmat_ref[kt], precise)
        re.append(r[:, :_S5_KL])
        im.append(r[:, _S5_KL:])
    return jnp.concatenate(re, axis=1), jnp.concatenate(im, axis=1)


def _s5_output(h_re, h_im, u, cmat_ref, d_ref, wglu_ref, bglu_ref, precise):
    if not precise:
        h_re, h_im = h_re.astype(BF16), h_im.astype(BF16)
    ys = []
    for kt in range(_S5_KT):
        sl = slice(kt * _S5_KL, (kt + 1) * _S5_KL)
        y = _mm(h_re[:, sl], cmat_ref[kt, :_S5_KL, :], precise)
        y = y + _mm(h_im[:, sl], cmat_ref[kt, _S5_KL:, :], precise)
        ys.append(y)
    y = jnp.concatenate(ys, axis=1) + d_ref[...] * u
    g = jax.nn.gelu(y)
    gate = _mm(g, wglu_ref[...], precise) + bglu_ref[...]
    return g * jax.nn.sigmoid(gate)


def _s5_scan_kernel(u_ref, are_ref, aim_ref, bmat_ref, cmat_ref, d_ref, wglu_ref, bglu_ref,
                    o_ref, hre_ref, him_ref, bu_re, bu_im, *, nb, lc, fold):
    c = pl.program_id(0)
    slots = _S5_NT // fold
    bp = fold * nb

    @pl.when(c == 0)
    def _():
        hre_ref[...] = jnp.zeros(hre_ref.shape, F32)
        him_ref[...] = jnp.zeros(him_ref.shape, F32)

    def rows_of(jt, b):
        return jt % slots, pl.ds((jt // slots) * nb + b, lc, stride=bp)

    for b in range(nb):
        re, im = _s5_input(u_ref[b], bmat_ref, False)
        for jt in range(_S5_NT):
            slot, rows = rows_of(jt, b)
            bu_re[slot, rows, :] = re[:, jt * LANES:(jt + 1) * LANES]
            bu_im[slot, rows, :] = im[:, jt * LANES:(jt + 1) * LANES]

    a_re = are_ref[...]
    a_im = aim_ref[...]

    def step(t, carry):
        hr, hi = carry
        rows = pl.ds(pl.multiple_of(t * bp, bp), bp)
        nr = hr * a_re - hi * a_im + bu_re[:, rows, :]
        ni = hr * a_im + hi * a_re + bu_im[:, rows, :]
        bu_re[:, rows, :] = nr
        bu_im[:, rows, :] = ni
        return nr, ni

    hr, hi = lax.fori_loop(0, lc, step, (hre_ref[...], him_ref[...]), unroll=2)
    hre_ref[...] = hr
    him_ref[...] = hi

    for b in range(nb):
        h_re = jnp.concatenate([bu_re[rows_of(jt, b)] for jt in range(_S5_NT)], axis=1)
        h_im = jnp.concatenate([bu_im[rows_of(jt, b)] for jt in range(_S5_NT)], axis=1)
        o_ref[b] = _s5_output(h_re, h_im, u_ref[b], cmat_ref, d_ref, wglu_ref, bglu_ref, False).astype(o_ref.dtype)


def _s5_prompt(u, prm, lc=128):
    nb, l, _ = u.shape
    bp = 8
    assert bp % nb == 0 and _S5_NT % (bp // nb) == 0
    fold = bp // nb
    slots = _S5_NT // fold
    full = lambda shape: pl.BlockSpec(shape, lambda c: (0,) * len(shape))

    def a_rows(a):
        a = jnp.transpose(a.reshape(fold, slots, LANES), (1, 0, 2))
        return jnp.repeat(a, nb, axis=1)

    o, h_re, h_im = pl.pallas_call(
        functools.partial(_s5_scan_kernel, nb=nb, lc=lc, fold=fold),
        grid=(l // lc,),
        in_specs=[pl.BlockSpec((nb, lc, S5_WIDTH), lambda c: (0, c, 0)),
                  full((slots, bp, LANES)), full((slots, bp, LANES)),
                  full((_S5_KT, _S5_KW, 2 * _S5_KL)), full((_S5_KT, 2 * _S5_KL, _S5_KW)),
                  full((1, S5_WIDTH)), full((S5_WIDTH, S5_WIDTH)), full((1, S5_WIDTH))],
        out_specs=[pl.BlockSpec((nb, lc, S5_WIDTH), lambda c: (0, c, 0)),
                   full((slots, bp, LANES)), full((slots, bp, LANES))],
        out_shape=[jax.ShapeDtypeStruct((nb, l, S5_WIDTH), BF16),
                   jax.ShapeDtypeStruct((slots, bp, LANES), F32), jax.ShapeDtypeStruct((slots, bp, LANES), F32)],
        scratch_shapes=[pltpu.VMEM((slots, lc * bp, LANES), F32) for _ in range(2)],
        compiler_params=pltpu.CompilerParams(dimension_semantics=("arbitrary",)),
        name="s5_prompt",
    )(u, a_rows(prm["a_re"]), a_rows(prm["a_im"]), prm["bmat"].astype(BF16),
      prm["cmat"].astype(BF16), prm["d"], prm["w_glu"].astype(BF16), prm["b_glu"])
    rows = lambda h: jnp.transpose(h.reshape(slots, fold, nb, LANES), (2, 1, 0, 3)).reshape(nb, S5_LANES)
    return o, rows(h_re), rows(h_im)


def _s5_step_kernel(u_ref, h0re_ref, h0im_ref, are_ref, aim_ref, bmat_ref, cmat_ref, d_ref, wglu_ref, bglu_ref,
                    o_ref, hre_ref, him_ref):
    u = u_ref[...]
    bu_re, bu_im = _s5_input(u, bmat_ref, True)
    h_re, h_im = h0re_ref[...], h0im_ref[...]
    a_re, a_im = are_ref[...], aim_ref[...]
    n_re = h_re * a_re - h_im * a_im + bu_re
    n_im = h_re * a_im + h_im * a_re + bu_im
    hre_ref[...] = n_re
    him_ref[...] = n_im
    o_ref[...] = _s5_output(n_re, n_im, u, cmat_ref, d_ref, wglu_ref, bglu_ref, True)


def _s5_step(u, h0_re, h0_im, prm):
    nb = u.shape[0]
    return pl.pallas_call(
        _s5_step_kernel,
        out_shape=[jax.ShapeDtypeStruct((nb, S5_WIDTH), F32),
                   jax.ShapeDtypeStruct((nb, S5_LANES), F32), jax.ShapeDtypeStruct((nb, S5_LANES), F32)],
        name="s5_step",
    )(u, h0_re, h0_im, prm["a_re"], prm["a_im"], prm["bmat"], prm["cmat"], prm["d"], prm["w_glu"], prm["b_glu"])


def _s5_params(lam_re, lam_im, log_dt, b_re, b_im, c_re, c_im, d, w_glu, b_glu):
    lam = lax.complex(lam_re, lam_im)
    dt = jnp.exp(log_dt)[:, None]
    a_bar = jnp.exp(lam * dt)
    b_bar = ((a_bar - 1.0) / lam)[..., None] * lax.complex(b_re, b_im)
    gk = S5_GROUPS // _S5_KT
    eye = jnp.eye(gk, dtype=F32)

    def in_tile(x):
        x = x.reshape(_S5_KT, gk, S5_STATE, S5_GROUP)
        return jnp.einsum("kgpc,gh->kgchp", x, eye).reshape(_S5_KT, _S5_KW, _S5_KL)

    def out_tile(x):
        x = x.reshape(_S5_KT, gk, S5_GROUP, S5_STATE)
        return jnp.einsum("kgcp,gh->kgphc", x, eye).reshape(_S5_KT, _S5_KL, _S5_KW)

    bmat = jnp.concatenate([in_tile(jnp.real(b_bar)), in_tile(jnp.imag(b_bar))], axis=2)
    cmat = jnp.concatenate([out_tile(c_re), out_tile(-c_im)], axis=1)
    return {
        "a_re": jnp.real(a_bar).reshape(1, S5_LANES), "a_im": jnp.imag(a_bar).reshape(1, S5_LANES),
        "bmat": bmat, "cmat": cmat, "d": d.reshape(1, S5_WIDTH), "w_glu": w_glu, "b_glu": b_glu.reshape(1, S5_WIDTH),
    }


_CONV_PAD = 8


def _conv_silu(x, prev_ref, w_ref, b_ref):
    rows = x.shape[0]
    w = w_ref[...]
    prev = prev_ref[...]
    sub = lax.broadcasted_iota(jnp.int32, prev.shape, 0)
    out = b_ref[...]
    for i in range(SSD_CONV):
        s = SSD_CONV - 1 - i
        if s == 0:
            shifted = x
        else:
            rolled = pltpu.roll(x, s, 0)
            head = jnp.where(sub < s, pltpu.roll(prev, s, 0), rolled[0:_CONV_PAD])
            shifted = jnp.concatenate([head, rolled[_CONV_PAD:]], axis=0)
        out = out + w[i:i + 1] * shifted
    prev_ref[...] = x[rows - _CONV_PAD:rows, :]
    return jax.nn.silu(out)


def _ssd_in_kernel(x_ref, g_ref, wz_ref, wx_ref, wdt_ref, cw_ref, cb_ref, z_ref, xc_ref, dt_ref, tail_ref, xf,
                   *, tiles_per_seq):
    i = pl.program_id(0)
    tm = x_ref.shape[0]

    @pl.when(i % tiles_per_seq == 0)
    def _():
        xf[...] = jnp.zeros(xf.shape, F32)

    h = _rms(x_ref[...], g_ref[...]).astype(BF16)
    z_ref[...] = _mm(h, wz_ref[...])
    dt_ref[...] = _mm(h, wdt_ref[...])
    xbc = _mm(h, wx_ref[...])
    tail_ref[0] = xbc[tm - _CONV_PAD:tm, :]
    xc_ref[...] = _conv_silu(xbc, xf, cw_ref, cb_ref)


def _ssd_in(x, g, prm, seq_len, tm, name):
    m, kdim = x.shape
    tiles_per_seq = seq_len // tm
    full = lambda a: pl.BlockSpec(a.shape, lambda i: (0,) * a.ndim)
    rows = lambda n: pl.BlockSpec((tm, n), lambda i: (i, 0))
    wz, wx, wdt = prm["w_z"], prm["w_xbc"], prm["w_dt_g"]
    return pl.pallas_call(
        functools.partial(_ssd_in_kernel, tiles_per_seq=tiles_per_seq),
        grid=(m // tm,),
        in_specs=[rows(kdim), pl.BlockSpec((1, kdim), lambda i: (0, 0)), full(wz), full(wx), full(wdt),
                  full(prm["conv_w"]), full(prm["conv_b"])],
        out_specs=[rows(wz.shape[1]), rows(wx.shape[1]), rows(wdt.shape[1]),
                   pl.BlockSpec((1, _CONV_PAD, wx.shape[1]), lambda i: (i // tiles_per_seq, 0, 0))],
        out_shape=[jax.ShapeDtypeStruct((m, wz.shape[1]), F32), jax.ShapeDtypeStruct((m, wx.shape[1]), F32),
                   jax.ShapeDtypeStruct((m, wdt.shape[1]), F32),
                   jax.ShapeDtypeStruct((m // seq_len, _CONV_PAD, wx.shape[1]), F32)],
        scratch_shapes=[pltpu.VMEM((_CONV_PAD, wx.shape[1]), F32)],
        compiler_params=pltpu.CompilerParams(dimension_semantics=("arbitrary",)),
        name=name,
    )(x, g.reshape(1, kdim), wz, wx, wdt, prm["conv_w"], prm["conv_b"])


def _ssd_chunk_kernel(xs_ref, bm_ref, cm_ref, dt_ref, dtb_ref, a_ref, dexp_ref, y_ref, st_ref):
    c = pl.program_id(2)
    q = SSD_CHUNK

    @pl.when(c == 0)
    def _():
        st_ref[...] = jnp.zeros(st_ref.shape, F32)

    xs = xs_ref[0]
    bm = bm_ref[0]
    bm16 = bm.astype(BF16)
    cm16 = cm_ref[0].astype(BF16)

    dt = jax.nn.softplus(dt_ref[0] + dtb_ref[0])
    da = dt * a_ref[0]
    row = lax.broadcasted_iota(jnp.int32, (q, q), 0)
    col = lax.broadcasted_iota(jnp.int32, (q, q), 1)
    causal = row >= col
    acs = jnp.dot(causal.astype(F32), da, preferred_element_type=F32, precision=HIGHEST)
    acs_t = acs.T
    acs_last = acs[q - 1:q, :]
    e_acs = jnp.exp(acs)
    w_s = dt * jnp.exp(acs_last - acs)
    dt_t = dt.T
    chunk_dec = jnp.exp(acs_last)
    cb = _mm(cm16, bm16, dims=_NT)

    first = lax.broadcasted_iota(jnp.int32, (q, LANES), 1) < SSD_HEADDIM
    first_rows = lax.broadcasted_iota(jnp.int32, (LANES, SSD_STATE), 0) < SSD_HEADDIM

    ys = []
    for i in range(SSD_HPG // 2):
        xs16 = xs[:, i * LANES:(i + 1) * LANES].astype(BF16)
        heads = (2 * i, 2 * i + 1)
        y_head, st_head = [], []
        for j in heads:
            seg = acs[:, j:j + 1] - acs_t[j:j + 1, :]
            lmat = jnp.exp(jnp.where(causal, seg, -jnp.inf))
            y_head.append(_mm(cb * lmat * dt_t[j:j + 1, :], xs16))
            st_head.append(_mm(xs16, bm * w_s[:, j:j + 1], dims=_TN))
        y_diag = jnp.where(first, y_head[0], y_head[1])
        h = st_ref[0, 2 * i:2 * i + 2].reshape(LANES, SSD_STATE)
        y_off = _mm(cm16, h, dims=_NT) * jnp.where(first, e_acs[:, heads[0]:heads[0] + 1],
                                                    e_acs[:, heads[1]:heads[1] + 1])
        cdec = jnp.concatenate([jnp.broadcast_to(chunk_dec[:, j:j + 1], (SSD_HEADDIM, SSD_STATE)) for j in heads],
                               axis=0)
        h_new = cdec * h + jnp.where(first_rows, st_head[0], st_head[1])
        st_ref[0, 2 * i:2 * i + 2] = h_new.reshape(2, SSD_HEADDIM, SSD_STATE)
        ys.append(y_diag + y_off)
    y_ref[0] = jnp.concatenate(ys, axis=1) + dexp_ref[...] * xs


def _ssd_prompt(xc, dtg, prm):
    nb, l, _ = xc.shape
    nc = l // SSD_CHUNK
    q = SSD_CHUNK
    boff = SSD_INNER // LANES
    coff = boff + SSD_GROUPS
    grp = lambda b, g, c: (g, 0, 0)
    return pl.pallas_call(
        _ssd_chunk_kernel,
        grid=(nb, SSD_GROUPS, nc),
        in_specs=[pl.BlockSpec((1, q, SSD_GW), lambda b, g, c: (b, c, g)),
                  pl.BlockSpec((1, q, LANES), lambda b, g, c: (b, c, boff + g)),
                  pl.BlockSpec((1, q, LANES), lambda b, g, c: (b, c, coff + g)),
                  pl.BlockSpec((1, q, LANES), lambda b, g, c: (b, c, g)),
                  pl.BlockSpec((1, 1, LANES), grp), pl.BlockSpec((1, 1, LANES), grp),
                  pl.BlockSpec((1, SSD_GW), lambda b, g, c: (0, g))],
        out_specs=[pl.BlockSpec((1, q, SSD_GW), lambda b, g, c: (b, c, g)),
                   pl.BlockSpec((1, SSD_HPG, SSD_HEADDIM, SSD_STATE), lambda b, g, c: (b, g, 0, 0))],
        out_shape=[jax.ShapeDtypeStruct((nb, l, SSD_INNER), F32),
                   jax.ShapeDtypeStruct((nb, SSD_HEADS, SSD_HEADDIM, SSD_STATE), F32)],
        compiler_params=pltpu.CompilerParams(dimension_semantics=("parallel", "parallel", "arbitrary")),
        name="ssd_prompt",
    )(xc, xc, xc, dtg, prm["dt_bias_g"], prm["a_g"], prm["d_exp"])


def _ssd_step_kernel(x_ref, buf_ref, cw_ref, cb_ref, dt64_ref, dtb64_ref, dt128_ref, dtb128_ref, a128_ref,
                     dexp_ref, st_ref, y_ref, so_ref):
    w = cw_ref[...]
    buf = buf_ref[0]
    conv = cb_ref[...]
    for i in range(SSD_CONV - 1):
        conv = conv + w[i:i + 1] * buf[i:i + 1]
    conv = conv + w[SSD_CONV - 1:SSD_CONV] * x_ref[0]
    xc = jax.nn.silu(conv)
    xs = xc[:, :SSD_INNER]
    xdt = xs * jax.nn.softplus(dt64_ref[0] + dtb64_ref[...])
    dt = jax.nn.softplus(dt128_ref[0] + dtb128_ref[...])
    dec = jnp.exp(dt * a128_ref[...])
    row = lax.broadcasted_iota(jnp.int32, (LANES, LANES), 0)
    col = lax.broadcasted_iota(jnp.int32, (LANES, LANES), 1)
    diag = row == col
    ys = []
    for i in range(SSD_HEADS // 2):
        g = (2 * i) // SSD_HPG
        bm = xc[:, SSD_INNER + g * SSD_STATE:SSD_INNER + (g + 1) * SSD_STATE]
        cm = xc[:, SSD_INNER + (SSD_GROUPS + g) * SSD_STATE:SSD_INNER + (SSD_GROUPS + g + 1) * SSD_STATE]
        xp = jnp.broadcast_to(xdt[:, i * LANES:(i + 1) * LANES], (LANES, LANES))
        outer = _mm(jnp.where(diag, xp, 0.0), jnp.broadcast_to(bm, (LANES, LANES)), True)
        h = st_ref[0, 2 * i:2 * i + 2].reshape(LANES, SSD_STATE)
        dpair = jnp.concatenate([jnp.broadcast_to(dec[2 * i:2 * i + 1], (SSD_HEADDIM, LANES)),
                                 jnp.broadcast_to(dec[2 * i + 1:2 * i + 2], (SSD_HEADDIM, LANES))], axis=0)
        hn = dpair * h + outer
        so_ref[0, 2 * i:2 * i + 2] = hn.reshape(2, SSD_HEADDIM, SSD_STATE)
        ys.append(_mm(jnp.broadcast_to(cm, (8, SSD_STATE)), hn, True, _NT)[0:1])
    y_ref[0] = jnp.concatenate(ys, axis=1) + dexp_ref[...] * xs


def _dt_expand_kernel(dt_ref, e64_ref, e128_ref, o64_ref, o128_ref):
    dt = dt_ref[...]
    o64_ref[...] = _mm(dt, e64_ref[...], True)
    o128_ref[...] = _mm(dt, e128_ref[...], True)


def _dt_expand(dt, prm):
    nb = dt.shape[0]
    return pl.pallas_call(
        _dt_expand_kernel,
        out_shape=[jax.ShapeDtypeStruct((nb, SSD_INNER), F32), jax.ShapeDtypeStruct((nb, SSD_HEADS * LANES), F32)],
        name="ssd_dt_expand",
    )(dt, prm["expand_64"], prm["expand_128"])


def _ssd_step(xbc, conv_buf, dt64, dt128, state, prm):
    nb = xbc.shape[0]
    full = lambda shape: pl.BlockSpec(shape, lambda b: (0,) * len(shape))
    y, so = pl.pallas_call(
        _ssd_step_kernel,
        grid=(nb,),
        in_specs=[pl.BlockSpec((1, 1, SSD_CONV_DIM), lambda b: (b, 0, 0)),
                  pl.BlockSpec((1, SSD_CONV - 1, SSD_CONV_DIM), lambda b: (b, 0, 0)),
                  full((SSD_CONV, SSD_CONV_DIM)), full((1, SSD_CONV_DIM)),
                  pl.BlockSpec((1, 1, SSD_INNER), lambda b: (b, 0, 0)), full((1, SSD_INNER)),
                  pl.BlockSpec((1, SSD_HEADS, LANES), lambda b: (b, 0, 0)), full((SSD_HEADS, LANES)),
                  full((SSD_HEADS, LANES)), full((1, SSD_INNER)),
                  pl.BlockSpec((1, SSD_HEADS, SSD_HEADDIM, SSD_STATE), lambda b: (b, 0, 0, 0))],
        out_specs=[pl.BlockSpec((1, 1, SSD_INNER), lambda b: (b, 0, 0)),
                   pl.BlockSpec((1, SSD_HEADS, SSD_HEADDIM, SSD_STATE), lambda b: (b, 0, 0, 0))],
        out_shape=[jax.ShapeDtypeStruct((nb, 1, SSD_INNER), F32),
                   jax.ShapeDtypeStruct((nb, SSD_HEADS, SSD_HEADDIM, SSD_STATE), F32)],
        compiler_params=pltpu.CompilerParams(dimension_semantics=("parallel",)),
        name="ssd_step",
    )(xbc.reshape(nb, 1, SSD_CONV_DIM), conv_buf, prm["conv_w"], prm["conv_b"],
      dt64.reshape(nb, 1, SSD_INNER), prm["dt_bias_64"], dt128.reshape(nb, SSD_HEADS, LANES),
      prm["dt_bias_128"], prm["a_128"], prm["d_exp"], state)
    return y.reshape(nb, SSD_INNER), so


_GROUP_LANE0 = MOE_EXPERTS
_ROUTE_LANE0 = 64


def _pack_bf16_halves(x):
    n = x.shape[1] // 2
    lo = lax.bitcast_convert_type(x[:, :n].astype(BF16).astype(F32), jnp.uint32)
    hi = lax.bitcast_convert_type(x[:, n:].astype(BF16).astype(F32), jnp.uint32)
    return (lo >> 16) | hi


def _unpack_bf16_halves(p):
    lo = lax.bitcast_convert_type(p << 16, F32)
    hi = lax.bitcast_convert_type(p & jnp.uint32(0xFFFF0000), F32)
    return jnp.concatenate([lo, hi], axis=1)


def _moe_router_kernel(y_ref, g_ref, wr_ref, br_ref, xn_ref, comb_ref, cnt_ref):
    @pl.when(pl.program_id(0) == 0)
    def _():
        cnt_ref[...] = jnp.zeros(cnt_ref.shape, F32)

    xn = _rms(y_ref[...], g_ref[...])
    xn_ref[...] = _pack_bf16_halves(xn) if xn_ref.dtype == jnp.uint32 else xn.astype(xn_ref.dtype)
    lg = _mm(xn, wr_ref[...], True) + br_ref[...]
    lane = lax.broadcasted_iota(jnp.int32, lg.shape, 1)
    is_group = (lane >= _GROUP_LANE0) & (lane < _GROUP_LANE0 + MOE_GROUPS)
    gl = jnp.where(is_group, lg, NEG)
    gmax = jnp.max(gl, axis=-1, keepdims=True)
    g_p = 1.0 / jnp.sum(jnp.exp(gl - gmax), axis=-1, keepdims=True)
    gidx = jnp.min(jnp.where(gl == gmax, lane - _GROUP_LANE0, MOE_GROUPS), axis=-1, keepdims=True)
    el = jnp.where((lane < MOE_EXPERTS) & (lane // MOE_EPG == gidx), lg, NEG)
    m1 = jnp.max(el, axis=-1, keepdims=True)
    i1 = jnp.min(jnp.where(el == m1, lane, LANES), axis=-1, keepdims=True)
    el2 = jnp.where(lane == i1, NEG, el)
    m2 = jnp.max(el2, axis=-1, keepdims=True)
    i2 = jnp.min(jnp.where(el2 == m2, lane, LANES), axis=-1, keepdims=True)
    e2 = jnp.exp(m2 - m1)
    den = 1.0 + e2
    g1 = g_p / den
    g2 = g_p * (e2 / den)
    oh1 = lane == i1
    oh2 = lane == i2
    comb = jnp.where(oh1, g1, 0.0) + jnp.where(oh2, g2, 0.0)
    picks = oh1.astype(F32) + oh2.astype(F32)
    tm = picks.shape[0]
    before = lax.broadcasted_iota(jnp.int32, (tm, tm), 0) > lax.broadcasted_iota(jnp.int32, (tm, tm), 1)
    seen = cnt_ref[...] + _mm(before.astype(F32), picks)
    r1 = jnp.sum(jnp.where(oh1, seen, 0.0), axis=-1, keepdims=True)
    r2 = jnp.sum(jnp.where(oh2, seen, 0.0), axis=-1, keepdims=True)
    cnt_ref[...] += jnp.sum(picks, axis=0, keepdims=True)
    for k, v in enumerate((i1.astype(F32), i2.astype(F32), g1, g2, r1, r2)):
        comb = jnp.where(lane == _ROUTE_LANE0 + k, v, comb)
    comb_ref[...] = comb


def _moe_router(y, g, wr, br, tm, name, xn_dtype):
    m, d = y.shape
    dx = d // 2 if xn_dtype == jnp.uint32 else d
    return pl.pallas_call(
        _moe_router_kernel,
        grid=(m // tm,),
        in_specs=[pl.BlockSpec((tm, d), lambda i: (i, 0)), pl.BlockSpec((1, d), lambda i: (0, 0)),
                  pl.BlockSpec((d, LANES), lambda i: (0, 0)), pl.BlockSpec((1, LANES), lambda i: (0, 0))],
        out_specs=[pl.BlockSpec((tm, dx), lambda i: (i, 0)), pl.BlockSpec((tm, LANES), lambda i: (i, 0)),
                   pl.BlockSpec((1, LANES), lambda i: (0, 0))],
        out_shape=[jax.ShapeDtypeStruct((m, dx), xn_dtype), jax.ShapeDtypeStruct((m, LANES), F32),
                   jax.ShapeDtypeStruct((1, LANES), F32)],
        compiler_params=pltpu.CompilerParams(dimension_semantics=("arbitrary",)),
        name=name,
    )(y, g.reshape(1, d), wr, br)


def _moe_dense_kernel(res_ref, xn_ref, comb_ref, wg_ref, wu_ref, wd_ref, o_ref, acc, *, precise):
    e = pl.program_id(1)

    @pl.when(e == 0)
    def _():
        acc[...] = jnp.zeros(acc.shape, F32)

    x = xn_ref[...]
    hdn = jax.nn.silu(_mm(x, wg_ref[0], precise)) * _mm(x, wu_ref[0], precise)
    comb = comb_ref[...]
    lane = lax.broadcasted_iota(jnp.int32, comb.shape, 1)
    gate = jnp.sum(jnp.where(lane == e, comb, 0.0), axis=-1, keepdims=True)
    acc[...] += gate * _mm(hdn, wd_ref[0], precise)

    @pl.when(e == pl.num_programs(1) - 1)
    def _():
        o_ref[...] = res_ref[...] + acc[...]


def _moe_dense(res, xn, comb, wg, wu, wd, e0, tm, name, precise):
    m, d = res.shape
    return pl.pallas_call(
        functools.partial(_moe_dense_kernel, precise=precise),
        grid=(m // tm, MOE_EXPERTS),
        in_specs=[pl.BlockSpec((tm, d), lambda i, e: (i, 0)), pl.BlockSpec((tm, d), lambda i, e: (i, 0)),
                  pl.BlockSpec((tm, LANES), lambda i, e: (i, 0)),
                  pl.BlockSpec((1, d, MOE_FF), lambda i, e: (e0 + e, 0, 0)),
                  pl.BlockSpec((1, d, MOE_FF), lambda i, e: (e0 + e, 0, 0)),
                  pl.BlockSpec((1, MOE_FF, d), lambda i, e: (e0 + e, 0, 0))],
        out_specs=pl.BlockSpec((tm, d), lambda i, e: (i, 0)),
        out_shape=jax.ShapeDtypeStruct((m, d), F32),
        scratch_shapes=[pltpu.VMEM((tm, d), F32)],
        compiler_params=pltpu.CompilerParams(dimension_semantics=("parallel", "arbitrary")),
        name=name,
    )(res, xn, comb, wg, wu, wd)


def _moe(y, g, prm, tm_r, tm_e, tag, precise):
    xn, comb, _ = _moe_router(y, g, prm["wr"], prm["br"], tm_r, "moe_router_" + tag, F32 if precise else BF16)
    return _moe_dense(y, xn, comb, prm["wg"], prm["wu"], prm["wd"], prm["e0"], tm_e, "moe_experts_" + tag, precise)


def _moe_grouped_kernel(te_ref, nt_ref, x_ref, wg_ref, wu_ref, wd_ref, o_ref):
    del te_ref
    active = pl.program_id(0) < nt_ref[0]

    @pl.when(active)
    def _():
        x = _unpack_bf16_halves(x_ref[...]).astype(BF16)
        hdn = jax.nn.silu(_mm(x, wg_ref[0])) * _mm(x, wu_ref[0])
        o_ref[...] = _mm(hdn, wd_ref[0]).astype(o_ref.dtype)

    @pl.when(jnp.logical_not(active))
    def _():
        o_ref[...] = jnp.zeros(o_ref.shape, o_ref.dtype)


def _moe_grouped(x_rows, tile_expert, n_tiles, wg, wu, wd, tm, name):
    r = x_rows.shape[0]
    d = wg.shape[1]
    grid_spec = pltpu.PrefetchScalarGridSpec(
        num_scalar_prefetch=2,
        grid=(r // tm,),
        in_specs=[pl.BlockSpec((tm, x_rows.shape[1]), lambda i, te, nt: (i, 0)),
                  pl.BlockSpec((1, d, MOE_FF), lambda i, te, nt: (te[i], 0, 0)),
                  pl.BlockSpec((1, d, MOE_FF), lambda i, te, nt: (te[i], 0, 0)),
                  pl.BlockSpec((1, MOE_FF, d), lambda i, te, nt: (te[i], 0, 0))],
        out_specs=pl.BlockSpec((tm, d), lambda i, te, nt: (i, 0)),
    )
    return pl.pallas_call(
        _moe_grouped_kernel,
        grid_spec=grid_spec,
        out_shape=jax.ShapeDtypeStruct((r, d), F32),
        compiler_params=pltpu.CompilerParams(dimension_semantics=("arbitrary",)),
        name=name,
    )(tile_expert, n_tiles, x_rows, wg, wu, wd)


def _moe_sparse(y, g, prm, tm_r, tm_e, tag):
    t, d = y.shape
    xn, route, cnt = _moe_router(y, g, prm["wr"], prm["br"], tm_r, "moe_router_" + tag, jnp.uint32)
    lanes = lambda k: route[:, _ROUTE_LANE0 + k:_ROUTE_LANE0 + k + 2]
    ids = lanes(0).astype(jnp.int32)
    gates = lanes(2)
    rank = lanes(4).astype(jnp.int32)
    counts = cnt[0, :MOE_EXPERTS].astype(jnp.int32)
    rows = 2 * t + MOE_EXPERTS * tm_e
    pcounts = ((counts + tm_e - 1) // tm_e) * tm_e
    pend = jnp.cumsum(pcounts)
    pstart = pend - pcounts
    pos = pstart[ids] + rank
    tile_start = jnp.arange(rows // tm_e, dtype=jnp.int32) * tm_e
    tile_expert = jnp.minimum(jnp.sum((pend[None, :] <= tile_start[:, None]).astype(jnp.int32), axis=1),
                              MOE_EXPERTS - 1)
    n_tiles = (pend[-1:] // tm_e).astype(jnp.int32)
    token = jnp.broadcast_to(jnp.arange(t, dtype=jnp.int32)[:, None], (t, 2))
    src = jnp.zeros((rows,), jnp.int32).at[pos.reshape(-1)].set(token.reshape(-1), unique_indices=True)
    x_rows = xn[src]
    out_rows = _moe_grouped(x_rows, tile_expert + prm["e0"], n_tiles, prm["wg"], prm["wu"], prm["wd"], tm_e,
                            "moe_experts_" + tag)
    return y + gates[:, 0:1] * out_rows[pos[:, 0]] + gates[:, 1:2] * out_rows[pos[:, 1]]


def _moe_params(layer, w_group, b_group, w_expert, b_expert, w_gate, w_up, w_down):
    d = w_group.shape[0]
    pad = LANES - MOE_EXPERTS - MOE_GROUPS
    wr = jnp.concatenate([w_expert, w_group, jnp.zeros((d, pad), F32)], axis=1)
    br = jnp.concatenate([b_expert, b_group, jnp.zeros((pad,), F32)]).reshape(1, LANES)
    flat = lambda w: w.reshape((-1,) + w.shape[2:])
    return {"wr": wr, "br": br, "wg": flat(w_gate), "wu": flat(w_up), "wd": flat(w_down), "e0": layer * MOE_EXPERTS}


def _ssd_params(w_in, conv_w, conv_b, dt_bias, a_log, d, norm, w_out):
    a = -jnp.exp(a_log)
    w_dt = w_in[:, SSD_INNER + SSD_CONV_DIM:]

    def grouped(x):
        x = x.reshape(x.shape[:-1] + (SSD_GROUPS, SSD_HPG))
        x = jnp.pad(x, [(0, 0)] * (x.ndim - 1) + [(0, LANES - SSD_HPG)])
        return x.reshape(x.shape[:-2] + (SSD_GROUPS * LANES,))

    w_z = w_in[:, :SSD_INNER]
    w_xbc = w_in[:, SSD_INNER:SSD_INNER + SSD_CONV_DIM]
    eye = jnp.eye(SSD_HEADS, dtype=F32)
    return {
        "w_z": w_z.astype(BF16), "w_xbc": w_xbc.astype(BF16), "w_dt_g": grouped(w_dt).astype(BF16),
        "w_z_f32": w_z, "w_xbc_f32": w_xbc, "w_dt_f32": w_dt, "w_out_f32": w_out,
        "expand_64": jnp.repeat(eye, SSD_HEADDIM, axis=1), "expand_128": jnp.repeat(eye, LANES, axis=1),
        "conv_w": conv_w, "conv_b": conv_b.reshape(1, SSD_CONV_DIM),
        "dt_bias_g": grouped(dt_bias).reshape(SSD_GROUPS, 1, LANES),
        "a_g": grouped(a).reshape(SSD_GROUPS, 1, LANES),
        "dt_bias_64": jnp.repeat(dt_bias, SSD_HEADDIM).reshape(1, SSD_INNER),
        "dt_bias_128": jnp.broadcast_to(dt_bias[:, None], (SSD_HEADS, LANES)),
        "a_128": jnp.broadcast_to(a[:, None], (SSD_HEADS, LANES)),
        "d_exp": jnp.repeat(d, SSD_HEADDIM).reshape(1, SSD_INNER),
        "norm": norm, "w_out": w_out.astype(BF16),
    }


_TM = 512


def kernel(x_prompt, x_sample, cache_k, cache_v, page_table, state_s5_re, state_s5_im, state_ssd, state_conv, norm_mix, norm_ffn, norm_final, even_w_in, even_w_out, diff_lam_q1, diff_lam_k1, diff_lam_q2, diff_lam_k2, diff_subln, s5_lam_re, s5_lam_im, s5_log_dt, s5_b_re, s5_b_im, s5_c_re, s5_c_im, s5_d, s5_w_glu, s5_b_glu, ssd_w_in, ssd_conv_w, ssd_conv_b, ssd_dt_bias, ssd_a_log, ssd_d, ssd_norm, ssd_w_out, moe_w_group, moe_b_group, moe_w_expert, moe_b_expert, moe_w_gate, moe_w_up, moe_w_down):
    bp, sp, d = x_prompt.shape
    bs = x_sample.shape[0]
    tp = bp * sp
    depth = norm_mix.shape[0]
    y_p = x_prompt.reshape(tp, d)
    y_s = x_sample.reshape(bs, d)
    outs = {n: [] for n in ("k_p", "v_p", "k_s", "v_s", "re_p", "im_p", "re_s", "im_s", "ssd_p", "ssd_s", "cv_p", "cv_s")}

    for li in range(depth):
        if li % 2 == 0:
            e = li // 2
            lam_init = 0.8 - 0.6 * math.exp(-0.3 * li)
            lam = (jnp.exp(jnp.sum(diff_lam_q1[e] * diff_lam_k1[e])) - jnp.exp(jnp.sum(diff_lam_q2[e] * diff_lam_k2[e]))
                   + lam_init).astype(F32)
            ws32 = [even_w_in[e][:, i * ATT_WIDTH:(i + 1) * ATT_WIDTH] for i in range(4)]
            ws = [w.astype(BF16) for w in ws32]
            w_out32 = [even_w_out[e][:ATT_WIDTH], even_w_out[e][ATT_WIDTH:]]
            w_out_a, w_out_s = [w.astype(BF16) for w in w_out32]
            s5p = _s5_params(s5_lam_re[e], s5_lam_im[e], s5_log_dt[e], s5_b_re[e], s5_b_im[e], s5_c_re[e], s5_c_im[e],
                             s5_d[e], s5_w_glu[e], s5_b_glu[e])
            dts = [(BF16,), (F32, BF16), (F32, _TILE_T), (F32,)]
            q, k, k16, v, vt, u = _norm_matmul(y_p, norm_mix[li], ws, dts, _TM, "even_in_p")
            o = _attn_prompt(q.reshape(bp, sp, ATT_WIDTH), k16.reshape(bp, sp, ATT_WIDTH), vt, lam, diff_subln[e],
                             1.0 - lam_init)
            s5o, h_re, h_im = _s5_prompt(u.reshape(bp, sp, S5_WIDTH), s5p)
            y_p = _matmul_res(y_p, [o.reshape(tp, ATT_WIDTH), s5o.reshape(tp, S5_WIDTH)], [w_out_a, w_out_s], _TM,
                              "even_out_p")
            outs["k_p"].append(k.reshape(bp, sp, ATT_HEADS, HEAD_W))
            outs["v_p"].append(v.reshape(bp, sp, ATT_HEADS, HEAD_W))
            outs["re_p"].append(h_re.reshape(bp, S5_GROUPS, S5_STATE))
            outs["im_p"].append(h_im.reshape(bp, S5_GROUPS, S5_STATE))
            q, k, v, u = _norm_matmul(y_s, norm_mix[li], ws32, [(F32,), (F32,), (F32,), (F32,)], bs, "even_in_s", True)
            o = _attn_decode(q, k, v, cache_k, cache_v, page_table + e * cache_k.shape[1], lam, diff_subln[e],
                             1.0 - lam_init)
            s5o, h_re, h_im = _s5_step(u, state_s5_re[e].reshape(bs, S5_LANES), state_s5_im[e].reshape(bs, S5_LANES), s5p)
            y_s = _matmul_res(y_s, [o, s5o], w_out32, bs, "even_out_s", True)
            outs["k_s"].append(k.reshape(bs, 1, ATT_HEADS, HEAD_W))
            outs["v_s"].append(v.reshape(bs, 1, ATT_HEADS, HEAD_W))
            outs["re_s"].append(h_re.reshape(bs, S5_GROUPS, S5_STATE))
            outs["im_s"].append(h_im.reshape(bs, S5_GROUPS, S5_STATE))
        else:
            o_ = li // 2
            sp_ = _ssd_params(ssd_w_in[o_], ssd_conv_w[o_], ssd_conv_b[o_], ssd_dt_bias[o_], ssd_a_log[o_], ssd_d[o_],
                              ssd_norm[o_], ssd_w_out[o_])
            z, xc, dtg, tail = _ssd_in(y_p, norm_mix[li], sp_, sp, 256, "ssd_in_p")
            yssd, st = _ssd_prompt(xc.reshape(bp, sp, SSD_CONV_DIM), dtg.reshape(bp, sp, SSD_GROUPS * LANES), sp_)
            y_p = _gated_norm_matmul(y_p, yssd.reshape(tp, SSD_INNER), z, sp_["norm"], sp_["w_out"], 256, "ssd_out_p")
            outs["ssd_p"].append(st)
            outs["cv_p"].append(tail[:, _CONV_PAD - (SSD_CONV - 1):, :])
            z, xbc, dtc = _norm_matmul(y_s, norm_mix[li], [sp_["w_z_f32"], sp_["w_xbc_f32"], sp_["w_dt_f32"]],
                                       [(F32,), (F32,), (F32,)], bs, "ssd_in_s", True)
            dt64, dt128 = _dt_expand(dtc, sp_)
            yssd, st = _ssd_step(xbc, state_conv[o_], dt64, dt128, state_ssd[o_], sp_)
            y_s = _gated_norm_matmul(y_s, yssd, z, sp_["norm"], sp_["w_out_f32"], bs, "ssd_out_s", True)
            outs["ssd_s"].append(st)
            outs["cv_s"].append(jnp.concatenate([state_conv[o_][:, 1:], xbc[:, None, :]], axis=1))
        mp = _moe_params(li, moe_w_group[li], moe_b_group[li], moe_w_expert[li], moe_b_expert[li], moe_w_gate,
                         moe_w_up, moe_w_down)
        y_p = _moe_sparse(y_p, norm_ffn[li], mp, _TM, 256, "p")
        y_s = _moe(y_s, norm_ffn[li], mp, bs, bs, "s", li + 1 < depth)

    y_prompt = _rmsnorm(y_p, norm_final, _TM, "final_p").reshape(bp, sp, d)
    y_sample = _rmsnorm(y_s, norm_final, bs, "final_s").reshape(bs, 1, d)
    st = lambda n: jnp.stack(outs[n])
    return (y_prompt, y_sample, st("k_p"), st("v_p"), st("k_s"), st("v_s"), st("re_p"), st("im_p"), st("re_s"),
            st("im_s"), st("ssd_p"), st("ssd_s"), st("cv_p"), st("cv_s"))
```

```python
import functools
import math

import jax
import jax.numpy as jnp
from jax import lax
from jax.experimental import pallas as pl
from jax.experimental.pallas import tpu as pltpu

F32 = jnp.float32
BF16 = jnp.bfloat16
HIGHEST = lax.Precision.HIGHEST

D_MODEL = 1024
NORM_EPS = 1e-6
PAGE_SIZE = 128
ATT_HEADS = 4
ATT_D = 64
ATT_WIDTH = ATT_HEADS * 2 * ATT_D
HEAD_W = 2 * ATT_D
S5_WIDTH = 512
S5_GROUP = 16
S5_GROUPS = 32
S5_STATE = 64
S5_LANES = S5_GROUPS * S5_STATE
SSD_INNER = 2048
SSD_HEADDIM = 64
SSD_HEADS = 32
SSD_GROUPS = 4
SSD_HPG = 8
SSD_STATE = 128
SSD_CONV = 4
SSD_CONV_DIM = SSD_INNER + 2 * SSD_GROUPS * SSD_STATE
SSD_CHUNK = 128
SSD_GW = SSD_HPG * SSD_HEADDIM
MOE_GROUPS = 4
MOE_EPG = 8
MOE_EXPERTS = 32
MOE_FF = 256
LANES = 128
NEG = -1e30

_NT = (((1,), (1,)), ((), ()))
_TN = (((0,), (0,)), ((), ()))


def _rms(x, g):
    return x * lax.rsqrt(jnp.mean(x * x, axis=-1, keepdims=True) + NORM_EPS) * g


def _split(x):
    hi = x.astype(BF16)
    return hi, (x - hi.astype(F32)).astype(BF16)


def _mm(a, b, precise=False, dims=None):
    dot = jnp.dot if dims is None else functools.partial(lax.dot_general, dimension_numbers=dims)
    if not precise:
        return dot(a.astype(BF16), b.astype(BF16), preferred_element_type=F32)
    m = a.shape[0]
    a_hi, a_lo = _split(a)
    b_hi, b_lo = _split(b)
    r = dot(jnp.concatenate([a_hi, a_lo], axis=0), b_hi, preferred_element_type=F32)
    return r[:m] + r[m:] + dot(a_hi, b_lo, preferred_element_type=F32)


def _norm_matmul_kernel(x_ref, g_ref, *refs, out_dtypes, precise):
    n = len(out_dtypes)
    w_refs = refs[:n]
    o_refs = refs[n:]
    h = _rms(x_ref[...], g_ref[...])
    if not precise:
        h = h.astype(BF16)
    k = 0
    for w_ref, dts in zip(w_refs, out_dtypes):
        r = _mm(h, w_ref[...], precise)
        for dt in dts:
            if dt == _TILE_T:
                o_refs[k][0] = r.T.astype(BF16)
            elif dt == _HEAD_ROWS:
                nh = r.shape[1] // LANES
                for hd in range(nh):
                    o_refs[k][pl.ds(hd, r.shape[0], stride=nh), :] = r[:, hd * LANES:(hd + 1) * LANES]
            else:
                o_refs[k][...] = r.astype(dt)
            k += 1


_TILE_T = "bf16 row tiles, each transposed"
_HEAD_ROWS = "f32, one 128-lane row per head"


def _norm_matmul(x, g, ws, out_dtypes, tm, name, precise=False):
    m, kdim = x.shape
    out_shape, out_specs = [], []
    for w, dts in zip(ws, out_dtypes):
        for dt in dts:
            if dt == _TILE_T:
                out_shape.append(jax.ShapeDtypeStruct((m // tm, w.shape[1], tm), BF16))
                out_specs.append(pl.BlockSpec((1, w.shape[1], tm), lambda i: (i, 0, 0)))
            elif dt == _HEAD_ROWS:
                nh = w.shape[1] // LANES
                out_shape.append(jax.ShapeDtypeStruct((m * nh, LANES), F32))
                out_specs.append(pl.BlockSpec((tm * nh, LANES), lambda i: (i, 0)))
            else:
                out_shape.append(jax.ShapeDtypeStruct((m, w.shape[1]), dt))
                out_specs.append(pl.BlockSpec((tm, w.shape[1]), lambda i: (i, 0)))
    return pl.pallas_call(
        functools.partial(_norm_matmul_kernel, out_dtypes=out_dtypes, precise=precise),
        grid=(m // tm,),
        in_specs=[pl.BlockSpec((tm, kdim), lambda i: (i, 0)), pl.BlockSpec((1, kdim), lambda i: (0, 0))]
        + [pl.BlockSpec(w.shape, lambda i: (0, 0)) for w in ws],
        out_specs=out_specs,
        out_shape=out_shape,
        compiler_params=pltpu.CompilerParams(dimension_semantics=("parallel",)),
        name=name,
    )(x, g.reshape(1, kdim), *ws)


def _matmul_res_kernel(res_ref, *refs, precise):
    n = (len(refs) - 1) // 2
    acc = res_ref[...]
    for a_ref, w_ref in zip(refs[:n], refs[n:2 * n]):
        acc = acc + _mm(a_ref[...], w_ref[...], precise)
    refs[-1][...] = acc


def _matmul_res(res, a_list, w_list, tm, name, precise=False):
    m, n = res.shape
    return pl.pallas_call(
        functools.partial(_matmul_res_kernel, precise=precise),
        grid=(m // tm,),
        in_specs=[pl.BlockSpec((tm, n), lambda i: (i, 0))]
        + [pl.BlockSpec((tm, a.shape[1]), lambda i: (i, 0)) for a in a_list]
        + [pl.BlockSpec(w.shape, lambda i: (0, 0)) for w in w_list],
        out_specs=pl.BlockSpec((tm, n), lambda i: (i, 0)),
        out_shape=jax.ShapeDtypeStruct((m, n), F32),
        compiler_params=pltpu.CompilerParams(dimension_semantics=("parallel",)),
        name=name,
    )(res, *a_list, *w_list)


def _gated_norm_matmul_kernel(res_ref, y_ref, z_ref, g_ref, w_ref, o_ref, *, precise):
    y = y_ref[...] * jax.nn.silu(z_ref[...])
    o_ref[...] = res_ref[...] + _mm(_rms(y, g_ref[...]), w_ref[...], precise)


def _gated_norm_matmul(res, y, z, g, w, tm, name, precise=False):
    m, n = res.shape
    kdim = y.shape[1]
    return pl.pallas_call(
        functools.partial(_gated_norm_matmul_kernel, precise=precise),
        grid=(m // tm,),
        in_specs=[pl.BlockSpec((tm, n), lambda i: (i, 0)), pl.BlockSpec((tm, kdim), lambda i: (i, 0)),
                  pl.BlockSpec((tm, kdim), lambda i: (i, 0)), pl.BlockSpec((1, kdim), lambda i: (0, 0)),
                  pl.BlockSpec(w.shape, lambda i: (0, 0))],
        out_specs=pl.BlockSpec((tm, n), lambda i: (i, 0)),
        out_shape=jax.ShapeDtypeStruct((m, n), F32),
        compiler_params=pltpu.CompilerParams(dimension_semantics=("parallel",)),
        name=name,
    )(res, y, z, g.reshape(1, kdim), w)


def _rmsnorm_kernel(x_ref, g_ref, o_ref):
    o_ref[...] = _rms(x_ref[...], g_ref[...])


def _rmsnorm(x, g, tm, name):
    m, n = x.shape
    return pl.pallas_call(
        _rmsnorm_kernel,
        grid=(m // tm,),
        in_specs=[pl.BlockSpec((tm, n), lambda i: (i, 0)), pl.BlockSpec((1, n), lambda i: (0, 0))],
        out_specs=pl.BlockSpec((tm, n), lambda i: (i, 0)),
        out_shape=jax.ShapeDtypeStruct((m, n), F32),
        compiler_params=pltpu.CompilerParams(dimension_semantics=("parallel",)),
        name=name,
    )(x, g.reshape(1, n))


def _split_q(q):
    lane = lax.broadcasted_iota(jnp.int32, q.shape, 1) % HEAD_W
    scale = ATT_D ** -0.5
    qs = q * jnp.asarray(scale, q.dtype)
    zero = jnp.zeros_like(qs)
    return jnp.where(lane < ATT_D, qs, zero), jnp.where(lane >= ATT_D, qs, zero)


def _subln(o, g, out_scale):
    return _rms(o, g) * out_scale


def _attn_prompt_kernel(lam_ref, q_ref, k_ref, vt_ref, g_ref, o_ref, qt_sc, m_sc, l_sc, acc_sc, *, tq, tk, out_scale):
    qi = pl.program_id(1)
    heads = range(ATT_HEADS)
    for h in heads:
        q1, q2 = _split_q(q_ref[0, :, h * HEAD_W:(h + 1) * HEAD_W])
        qt_sc[h] = jnp.concatenate([q1, q2], axis=0).astype(F32).T.astype(BF16)
    m_sc[...] = jnp.full(m_sc.shape, NEG, F32)
    l_sc[...] = jnp.zeros(l_sc.shape, F32)
    acc_sc[...] = jnp.zeros(acc_sc.shape, F32)

    def kv_step(j, masked):
        start = pl.multiple_of(j * tk, tk)
        for h in heads:
            kt = k_ref[0, pl.ds(start, tk), h * HEAD_W:(h + 1) * HEAD_W]
            st = jnp.dot(kt, qt_sc[h], preferred_element_type=F32)
            if masked:
                k_pos = start + lax.broadcasted_iota(jnp.int32, st.shape, 0)
                q_pos = qi * tq + lax.broadcasted_iota(jnp.int32, st.shape, 1) % tq
                st = jnp.where(k_pos <= q_pos, st, NEG)
            m_old = m_sc[h]
            m_new = jnp.maximum(m_old, jnp.max(st, axis=0, keepdims=True))
            p = jnp.exp(st - m_new)
            alpha = jnp.exp(m_old - m_new)
            l_sc[h] = alpha * l_sc[h] + jnp.sum(p, axis=0, keepdims=True)
            m_sc[h] = m_new
            vt = vt_ref[j, h * HEAD_W:(h + 1) * HEAD_W, :]
            acc_sc[h] = alpha * acc_sc[h] + jnp.dot(vt, p.astype(BF16), preferred_element_type=F32)

    n_full = (qi * tq) // tk

    def full_step(j, carry):
        kv_step(j, False)
        return carry

    lax.fori_loop(0, n_full, full_step, 0)
    kv_step(n_full, True)
    for h in heads:
        acc, l = acc_sc[h], l_sc[h]
        ot = acc[:, 0:tq] / l[:, 0:tq] - lam_ref[0, 0] * (acc[:, tq:2 * tq] / l[:, tq:2 * tq])
        o_ref[0, :, h * HEAD_W:(h + 1) * HEAD_W] = _subln(ot.T, g_ref[...], out_scale).astype(o_ref.dtype)


def _attn_prompt(q, k, vt, lam, subln, out_scale, tq=512):
    b, s, _ = q.shape
    tk = vt.shape[2]
    nblk = s // tk
    kern = functools.partial(_attn_prompt_kernel, tq=tq, tk=tk, out_scale=out_scale)
    return pl.pallas_call(
        kern,
        grid=(b, s // tq),
        in_specs=[pl.BlockSpec(memory_space=pltpu.SMEM),
                  pl.BlockSpec((1, tq, ATT_WIDTH), lambda bi, i: (bi, i, 0)),
                  pl.BlockSpec((1, s, ATT_WIDTH), lambda bi, i: (bi, 0, 0)),
                  pl.BlockSpec((nblk, ATT_WIDTH, tk), lambda bi, i: (bi, 0, 0)),
                  pl.BlockSpec((1, HEAD_W), lambda bi, i: (0, 0))],
        out_specs=pl.BlockSpec((1, tq, ATT_WIDTH), lambda bi, i: (bi, i, 0)),
        out_shape=jax.ShapeDtypeStruct((b, s, ATT_WIDTH), BF16),
        scratch_shapes=[pltpu.VMEM((ATT_HEADS, HEAD_W, 2 * tq), BF16), pltpu.VMEM((ATT_HEADS, 1, 2 * tq), F32),
                        pltpu.VMEM((ATT_HEADS, 1, 2 * tq), F32), pltpu.VMEM((ATT_HEADS, HEAD_W, 2 * tq), F32)],
        compiler_params=pltpu.CompilerParams(dimension_semantics=("parallel", "parallel")),
        name="attn_prompt",
    )(lam.reshape(1, 1), q, k, vt, subln.reshape(1, HEAD_W))


_DEC_PAGES = 16
_DEC_PROW = PAGE_SIZE * ATT_HEADS
_DEC_COLS = _DEC_PAGES * _DEC_PROW
_DEC_ROWS = 2 * ATT_HEADS


def _attn_decode_kernel(pt_ref, lam_ref, q_ref, kn_ref, vn_ref, g_ref, *refs, out_scale):
    del pt_ref
    k_refs = refs[:_DEC_PAGES]
    v_refs = refs[_DEC_PAGES:2 * _DEC_PAGES]
    o_ref = refs[2 * _DEC_PAGES]
    m_sc, l_sc, acc_sc = refs[2 * _DEC_PAGES + 1:]
    j = pl.program_id(1)

    q4 = q_ref[0] * (ATT_D ** -0.5)
    q8 = jnp.concatenate([q4, q4], axis=0)
    r8 = lax.broadcasted_iota(jnp.int32, (_DEC_ROWS, HEAD_W), 0)
    l8 = lax.broadcasted_iota(jnp.int32, (_DEC_ROWS, HEAD_W), 1)
    q8 = jnp.where(l8 // ATT_D == r8 // ATT_HEADS, q8, 0.0)

    @pl.when(j == 0)
    def _():
        m_sc[...] = jnp.full(m_sc.shape, NEG, F32)
        l_sc[...] = jnp.zeros(l_sc.shape, F32)
        acc_sc[...] = jnp.zeros(acc_sc.shape, F32)

    s = jnp.concatenate([_mm(q8, k_ref[0], True, _NT) for k_ref in k_refs], axis=1)
    rs = lax.broadcasted_iota(jnp.int32, s.shape, 0)
    cs = lax.broadcasted_iota(jnp.int32, s.shape, 1)
    s = jnp.where(cs % ATT_HEADS == rs % ATT_HEADS, s, NEG)
    m = m_sc[...]
    mn = jnp.maximum(m, jnp.max(s, axis=-1, keepdims=True))
    p = jnp.exp(s - mn)
    alpha = jnp.exp(m - mn)
    l = alpha * l_sc[...] + jnp.sum(p, axis=-1, keepdims=True)
    acc = alpha * acc_sc[...]
    for i, v_ref in enumerate(v_refs):
        acc = acc + _mm(p[:, i * _DEC_PROW:(i + 1) * _DEC_PROW], v_ref[0], True)
    m_sc[...] = mn
    l_sc[...] = l
    acc_sc[...] = acc

    @pl.when(j == pl.num_programs(1) - 1)
    def _():
        kn = jnp.concatenate([kn_ref[0], kn_ref[0]], axis=0)
        vn = jnp.concatenate([vn_ref[0], vn_ref[0]], axis=0)
        sn = jnp.sum(q8 * kn, axis=-1, keepdims=True)
        mf = jnp.maximum(mn, sn)
        pn = jnp.exp(sn - mf)
        af = jnp.exp(mn - mf)
        o8 = (af * acc + pn * vn) / (af * l + pn)
        o4 = o8[:ATT_HEADS] - lam_ref[0, 0] * o8[ATT_HEADS:]
        o_ref[0] = _subln(o4, g_ref[...], out_scale)


def _attn_decode(q, k_new, v_new, cache_k, cache_v, page_table, lam, subln, out_scale):
    b = q.shape[0]
    n_pages = page_table.shape[1]
    steps = n_pages // _DEC_PAGES
    ck = cache_k.reshape(-1, _DEC_PROW, HEAD_W)
    cv = cache_v.reshape(-1, _DEC_PROW, HEAD_W)

    def page_map(i):
        return lambda bi, j, pt: (pt[bi, j * _DEC_PAGES + i], 0, 0)

    row = lambda bi, j, pt: (bi, 0, 0)
    head_rows = pl.BlockSpec((1, ATT_HEADS, HEAD_W), row)
    page_specs = [pl.BlockSpec((1, _DEC_PROW, HEAD_W), page_map(i)) for i in range(_DEC_PAGES)]
    grid_spec = pltpu.PrefetchScalarGridSpec(
        num_scalar_prefetch=1,
        grid=(b, steps),
        in_specs=[pl.BlockSpec(memory_space=pltpu.SMEM), head_rows, head_rows, head_rows,
                  pl.BlockSpec((1, HEAD_W), lambda bi, j, pt: (0, 0))] + page_specs + page_specs,
        out_specs=head_rows,
        scratch_shapes=[pltpu.VMEM((_DEC_ROWS, 1), F32), pltpu.VMEM((_DEC_ROWS, 1), F32),
                        pltpu.VMEM((_DEC_ROWS, HEAD_W), F32)],
    )
    out = pl.pallas_call(
        functools.partial(_attn_decode_kernel, out_scale=out_scale),
        grid_spec=grid_spec,
        out_shape=jax.ShapeDtypeStruct((b, ATT_HEADS, HEAD_W), F32),
        compiler_params=pltpu.CompilerParams(dimension_semantics=("parallel", "arbitrary")),
        name="attn_decode",
    )(page_table, lam.reshape(1, 1), q.reshape(b, ATT_HEADS, HEAD_W), k_new.reshape(b, ATT_HEADS, HEAD_W),
      v_new.reshape(b, ATT_HEADS, HEAD_W), subln.reshape(1, HEAD_W), *([ck] * _DEC_PAGES), *([cv] * _DEC_PAGES))
    return out.reshape(b, ATT_WIDTH)


_S5_KT = 2
_S5_KW = S5_WIDTH // _S5_KT
_S5_KL = S5_LANES // _S5_KT
_S5_NT = S5_LANES // LANES


def _s5_input(u, bmat_ref, precise):
    re, im = [], []
    for kt in range(_S5_KT):
        r = _mm(u[:, kt * _S5_KW:(kt + 1) * _S5_KW], bmat_ref[kt], precise)
        re.append(r[:, :_S5_KL])
        im.append(r[:, _S5_KL:])
    return jnp.concatenate(re, axis=1), jnp.concatenate(im, axis=1)


def _s5_output(h_re, h_im, u, cmat_ref, d_ref, wglu_ref, bglu_ref, precise):
    if not precise:
        h_re, h_im = h_re.astype(BF16), h_im.astype(BF16)
    ys = []
    for kt in range(_S5_KT):
        sl = slice(kt * _S5_KL, (kt + 1) * _S5_KL)
        y = _mm(h_re[:, sl], cmat_ref[kt, :_S5_KL, :], precise)
        y = y + _mm(h_im[:, sl], cmat_ref[kt, _S5_KL:, :], precise)
        ys.append(y)
    y = jnp.concatenate(ys, axis=1) + d_ref[...] * u
    g = jax.nn.gelu(y)
    gate = _mm(g, wglu_ref[...], precise) + bglu_ref[...]
    return g * jax.nn.sigmoid(gate)


def _s5_scan_kernel(u_ref, are_ref, aim_ref, bmat_ref, cmat_ref, d_ref, wglu_ref, bglu_ref,
                    o_ref, hre_ref, him_ref, bu_re, bu_im, *, nb, lc, fold):
    c = pl.program_id(0)
    slots = _S5_NT // fold
    bp = fold * nb

    @pl.when(c == 0)
    def _():
        hre_ref[...] = jnp.zeros(hre_ref.shape, F32)
        him_ref[...] = jnp.zeros(him_ref.shape, F32)

    def rows_of(jt, b):
        return jt % slots, pl.ds((jt // slots) * nb + b, lc, stride=bp)

    for b in range(nb):
        re, im = _s5_input(u_ref[b], bmat_ref, False)
        for jt in range(_S5_NT):
            slot, rows = rows_of(jt, b)
            bu_re[slot, rows, :] = re[:, jt * LANES:(jt + 1) * LANES]
            bu_im[slot, rows, :] = im[:, jt * LANES:(jt + 1) * LANES]

    a_re = are_ref[...]
    a_im = aim_ref[...]

    def step(t, carry):
        hr, hi = carry
        rows = pl.ds(pl.multiple_of(t * bp, bp), bp)
        nr = hr * a_re - hi * a_im + bu_re[:, rows, :]
        ni = hr * a_im + hi * a_re + bu_im[:, rows, :]
        bu_re[:, rows, :] = nr
        bu_im[:, rows, :] = ni
        return nr, ni

    hr, hi = lax.fori_loop(0, lc, step, (hre_ref[...], him_ref[...]), unroll=2)
    hre_ref[...] = hr
    him_ref[...] = hi

    for b in range(nb):
        h_re = jnp.concatenate([bu_re[rows_of(jt, b)] for jt in range(_S5_NT)], axis=1)
        h_im = jnp.concatenate([bu_im[rows_of(jt, b)] for jt in range(_S5_NT)], axis=1)
        o_ref[b] = _s5_output(h_re, h_im, u_ref[b], cmat_ref, d_ref, wglu_ref, bglu_ref, False).astype(o_ref.dtype)


def _s5_prompt(u, prm, lc=128):
    nb, l, _ = u.shape
    bp = 8
    assert bp % nb == 0 and _S5_NT % (bp // nb) == 0
    fold = bp // nb
    slots = _S5_NT // fold
    full = lambda shape: pl.BlockSpec(shape, lambda c: (0,) * len(shape))

    def a_rows(a):
        a = jnp.transpose(a.reshape(fold, slots, LANES), (1, 0, 2))
        return jnp.repeat(a, nb, axis=1)

    o, h_re, h_im = pl.pallas_call(
        functools.partial(_s5_scan_kernel, nb=nb, lc=lc, fold=fold),
        grid=(l // lc,),
        in_specs=[pl.BlockSpec((nb, lc, S5_WIDTH), lambda c: (0, c, 0)),
                  full((slots, bp, LANES)), full((slots, bp, LANES)),
                  full((_S5_KT, _S5_KW, 2 * _S5_KL)), full((_S5_KT, 2 * _S5_KL, _S5_KW)),
                  full((1, S5_WIDTH)), full((S5_WIDTH, S5_WIDTH)), full((1, S5_WIDTH))],
        out_specs=[pl.BlockSpec((nb, lc, S5_WIDTH), lambda c: (0, c, 0)),
                   full((slots, bp, LANES)), full((slots, bp, LANES))],
        out_shape=[jax.ShapeDtypeStruct((nb, l, S5_WIDTH), BF16),
                   jax.ShapeDtypeStruct((slots, bp, LANES), F32), jax.ShapeDtypeStruct((slots, bp, LANES), F32)],
        scratch_shapes=[pltpu.VMEM((slots, lc * bp, LANES), F32) for _ in range(2)],
        compiler_params=pltpu.CompilerParams(dimension_semantics=("arbitrary",)),
        name="s5_prompt",
    )(u, a_rows(prm["a_re"]), a_rows(prm["a_im"]), prm["bmat"].astype(BF16),
      prm["cmat"].astype(BF16), prm["d"], prm["w_glu"].astype(BF16), prm["b_glu"])
    rows = lambda h: jnp.transpose(h.reshape(slots, fold, nb, LANES), (2, 1, 0, 3)).reshape(nb, S5_LANES)
    return o, rows(h_re), rows(h_im)


def _s5_step_kernel(u_ref, h0re_ref, h0im_ref, are_ref, aim_ref, bmat_ref, cmat_ref, d_ref, wglu_ref, bglu_ref,
                    o_ref, hre_ref, him_ref):
    u = u_ref[...]
    bu_re, bu_im = _s5_input(u, bmat_ref, True)
    h_re, h_im = h0re_ref[...], h0im_ref[...]
    a_re, a_im = are_ref[...], aim_ref[...]
    n_re = h_re * a_re - h_im * a_im + bu_re
    n_im = h_re * a_im + h_im * a_re + bu_im
    hre_ref[...] = n_re
    him_ref[...] = n_im
    o_ref[...] = _s5_output(n_re, n_im, u, cmat_ref, d_ref, wglu_ref, bglu_ref, True)


def _s5_step(u, h0_re, h0_im, prm):
    nb = u.shape[0]
    return pl.pallas_call(
        _s5_step_kernel,
        out_shape=[jax.ShapeDtypeStruct((nb, S5_WIDTH), F32),
                   jax.ShapeDtypeStruct((nb, S5_LANES), F32), jax.ShapeDtypeStruct((nb, S5_LANES), F32)],
        name="s5_step",
    )(u, h0_re, h0_im, prm["a_re"], prm["a_im"], prm["bmat"], prm["cmat"], prm["d"], prm["w_glu"], prm["b_glu"])


def _s5_params(lam_re, lam_im, log_dt, b_re, b_im, c_re, c_im, d, w_glu, b_glu):
    lam = lax.complex(lam_re, lam_im)
    dt = jnp.exp(log_dt)[:, None]
    a_bar = jnp.exp(lam * dt)
    b_bar = ((a_bar - 1.0) / lam)[..., None] * lax.complex(b_re, b_im)
    gk = S5_GROUPS // _S5_KT
    eye = jnp.eye(gk, dtype=F32)

    def in_tile(x):
        x = x.reshape(_S5_KT, gk, S5_STATE, S5_GROUP)
        return jnp.einsum("kgpc,gh->kgchp", x, eye).reshape(_S5_KT, _S5_KW, _S5_KL)

    def out_tile(x):
        x = x.reshape(_S5_KT, gk, S5_GROUP, S5_STATE)
        return jnp.einsum("kgcp,gh->kgphc", x, eye).reshape(_S5_KT, _S5_KL, _S5_KW)

    bmat = jnp.concatenate([in_tile(jnp.real(b_bar)), in_tile(jnp.imag(b_bar))], axis=2)
    cmat = jnp.concatenate([out_tile(c_re), out_tile(-c_im)], axis=1)
    return {
        "a_re": jnp.real(a_bar).reshape(1, S5_LANES), "a_im": jnp.imag(a_bar).reshape(1, S5_LANES),
        "bmat": bmat, "cmat": cmat, "d": d.reshape(1, S5_WIDTH), "w_glu": w_glu, "b_glu": b_glu.reshape(1, S5_WIDTH),
    }


_CONV_PAD = 8


def _conv_silu(x, prev_ref, w_ref, b_ref):
    rows = x.shape[0]
    w = w_ref[...]
    prev = prev_ref[...]
    sub = lax.broadcasted_iota(jnp.int32, prev.shape, 0)
    out = b_ref[...]
    for i in range(SSD_CONV):
        s = SSD_CONV - 1 - i
        if s == 0:
            shifted = x
        else:
            rolled = pltpu.roll(x, s, 0)
            head = jnp.where(sub < s, pltpu.roll(prev, s, 0), rolled[0:_CONV_PAD])
            shifted = jnp.concatenate([head, rolled[_CONV_PAD:]], axis=0)
        out = out + w[i:i + 1] * shifted
    prev_ref[...] = x[rows - _CONV_PAD:rows, :]
    return jax.nn.silu(out)


def _ssd_in_kernel(x_ref, g_ref, wz_ref, wx_ref, wdt_ref, cw_ref, cb_ref, z_ref, xc_ref, dt_ref, tail_ref, xf,
                   *, tiles_per_seq):
    i = pl.program_id(0)
    tm = x_ref.shape[0]

    @pl.when(i % tiles_per_seq == 0)
    def _():
        xf[...] = jnp.zeros(xf.shape, F32)

    h = _rms(x_ref[...], g_ref[...]).astype(BF16)
    z_ref[...] = _mm(h, wz_ref[...])
    dt_ref[...] = _mm(h, wdt_ref[...])
    xbc = _mm(h, wx_ref[...])
    tail_ref[0] = xbc[tm - _CONV_PAD:tm, :]
    xc_ref[...] = _conv_silu(xbc, xf, cw_ref, cb_ref)


def _ssd_in(x, g, prm, seq_len, tm, name):
    m, kdim = x.shape
    tiles_per_seq = seq_len // tm
    full = lambda a: pl.BlockSpec(a.shape, lambda i: (0,) * a.ndim)
    rows = lambda n: pl.BlockSpec((tm, n), lambda i: (i, 0))
    wz, wx, wdt = prm["w_z"], prm["w_xbc"], prm["w_dt_g"]
    return pl.pallas_call(
        functools.partial(_ssd_in_kernel, tiles_per_seq=tiles_per_seq),
        grid=(m // tm,),
        in_specs=[rows(kdim), pl.BlockSpec((1, kdim), lambda i: (0, 0)), full(wz), full(wx), full(wdt),
                  full(prm["conv_w"]), full(prm["conv_b"])],
        out_specs=[rows(wz.shape[1]), rows(wx.shape[1]), rows(wdt.shape[1]),
                   pl.BlockSpec((1, _CONV_PAD, wx.shape[1]), lambda i: (i // tiles_per_seq, 0, 0))],
        out_shape=[jax.ShapeDtypeStruct((m, wz.shape[1]), F32), jax.ShapeDtypeStruct((m, wx.shape[1]), F32),
                   jax.ShapeDtypeStruct((m, wdt.shape[1]), F32),
                   jax.ShapeDtypeStruct((m // seq_len, _CONV_PAD, wx.shape[1]), F32)],
        scratch_shapes=[pltpu.VMEM((_CONV_PAD, wx.shape[1]), F32)],
        compiler_params=pltpu.CompilerParams(dimension_semantics=("arbitrary",)),
        name=name,
    )(x, g.reshape(1, kdim), wz, wx, wdt, prm["conv_w"], prm["conv_b"])


def _ssd_chunk_kernel(xs_ref, bm_ref, cm_ref, dt_ref, dtb_ref, a_ref, dexp_ref, y_ref, st_ref):
    c = pl.program_id(2)
    q = SSD_CHUNK

    @pl.when(c == 0)
    def _():
        st_ref[...] = jnp.zeros(st_ref.shape, F32)

    xs = xs_ref[0]
    bm = bm_ref[0]
    bm16 = bm.astype(BF16)
    cm16 = cm_ref[0].astype(BF16)

    dt = jax.nn.softplus(dt_ref[0] + dtb_ref[0])
    da = dt * a_ref[0]
    row = lax.broadcasted_iota(jnp.int32, (q, q), 0)
    col = lax.broadcasted_iota(jnp.int32, (q, q), 1)
    causal = row >= col
    acs = jnp.dot(causal.astype(F32), da, preferred_element_type=F32, precision=HIGHEST)
    acs_t = acs.T
    acs_last = acs[q - 1:q, :]
    e_acs = jnp.exp(acs)
    w_s = dt * jnp.exp(acs_last - acs)
    dt_t = dt.T
    chunk_dec = jnp.exp(acs_last)
    cb = _mm(cm16, bm16, dims=_NT)

    first = lax.broadcasted_iota(jnp.int32, (q, LANES), 1) < SSD_HEADDIM
    first_rows = lax.broadcasted_iota(jnp.int32, (LANES, SSD_STATE), 0) < SSD_HEADDIM

    ys = []
    for i in range(SSD_HPG // 2):
        xs16 = xs[:, i * LANES:(i + 1) * LANES].astype(BF16)
        heads = (2 * i, 2 * i + 1)
        y_head, st_head = [], []
        for j in heads:
            seg = acs[:, j:j + 1] - acs_t[j:j + 1, :]
            lmat = jnp.exp(jnp.where(causal, seg, -jnp.inf))
            y_head.append(_mm(cb * lmat * dt_t[j:j + 1, :], xs16))
            st_head.append(_mm(xs16, bm * w_s[:, j:j + 1], dims=_TN))
        y_diag = jnp.where(first, y_head[0], y_head[1])
        h = st_ref[0, 2 * i:2 * i + 2].reshape(LANES, SSD_STATE)
        y_off = _mm(cm16, h, dims=_NT) * jnp.where(first, e_acs[:, heads[0]:heads[0] + 1],
                                                    e_acs[:, heads[1]:heads[1] + 1])
        cdec = jnp.concatenate([jnp.broadcast_to(chunk_dec[:, j:j + 1], (SSD_HEADDIM, SSD_STATE)) for j in heads],
                               axis=0)
        h_new = cdec * h + jnp.where(first_rows, st_head[0], st_head[1])
        st_ref[0, 2 * i:2 * i + 2] = h_new.reshape(2, SSD_HEADDIM, SSD_STATE)
        ys.append(y_diag + y_off)
    y_ref[0] = jnp.concatenate(ys, axis=1) + dexp_ref[...] * xs


def _ssd_prompt(xc, dtg, prm):
    nb, l, _ = xc.shape
    nc = l // SSD_CHUNK
    q = SSD_CHUNK
    boff = SSD_INNER // LANES
    coff = boff + SSD_GROUPS
    grp = lambda b, g, c: (g, 0, 0)
    return pl.pallas_call(
        _ssd_chunk_kernel,
        grid=(nb, SSD_GROUPS, nc),
        in_specs=[pl.BlockSpec((1, q, SSD_GW), lambda b, g, c: (b, c, g)),
                  pl.BlockSpec((1, q, LANES), lambda b, g, c: (b, c, boff + g)),
                  pl.BlockSpec((1, q, LANES), lambda b, g, c: (b, c, coff + g)),
                  pl.BlockSpec((1, q, LANES), lambda b, g, c: (b, c, g)),
                  pl.BlockSpec((1, 1, LANES), grp), pl.BlockSpec((1, 1, LANES), grp),
                  pl.BlockSpec((1, SSD_GW), lambda b, g, c: (0, g))],
        out_specs=[pl.BlockSpec((1, q, SSD_GW), lambda b, g, c: (b, c, g)),
                   pl.BlockSpec((1, SSD_HPG, SSD_HEADDIM, SSD_STATE), lambda b, g, c: (b, g, 0, 0))],
        out_shape=[jax.ShapeDtypeStruct((nb, l, SSD_INNER), F32),
                   jax.ShapeDtypeStruct((nb, SSD_HEADS, SSD_HEADDIM, SSD_STATE), F32)],
        compiler_params=pltpu.CompilerParams(dimension_semantics=("parallel", "parallel", "arbitrary")),
        name="ssd_prompt",
    )(xc, xc, xc, dtg, prm["dt_bias_g"], prm["a_g"], prm["d_exp"])


def _ssd_step_kernel(x_ref, buf_ref, cw_ref, cb_ref, dt64_ref, dtb64_ref, dt128_ref, dtb128_ref, a128_ref,
                     dexp_ref, st_ref, y_ref, so_ref):
    w = cw_ref[...]
    buf = buf_ref[0]
    conv = cb_ref[...]
    for i in range(SSD_CONV - 1):
        conv = conv + w[i:i + 1] * buf[i:i + 1]
    conv = conv + w[SSD_CONV - 1:SSD_CONV] * x_ref[0]
    xc = jax.nn.silu(conv)
    xs = xc[:, :SSD_INNER]
    xdt = xs * jax.nn.softplus(dt64_ref[0] + dtb64_ref[...])
    dt = jax.nn.softplus(dt128_ref[0] + dtb128_ref[...])
    dec = jnp.exp(dt * a128_ref[...])
    row = lax.broadcasted_iota(jnp.int32, (LANES, LANES), 0)
    col = lax.broadcasted_iota(jnp.int32, (LANES, LANES), 1)
    diag = row == col
    ys = []
    for i in range(SSD_HEADS // 2):
        g = (2 * i) // SSD_HPG
        bm = xc[:, SSD_INNER + g * SSD_STATE:SSD_INNER + (g + 1) * SSD_STATE]
        cm = xc[:, SSD_INNER + (SSD_GROUPS + g) * SSD_STATE:SSD_INNER + (SSD_GROUPS + g + 1) * SSD_STATE]
        xp = jnp.broadcast_to(xdt[:, i * LANES:(i + 1) * LANES], (LANES, LANES))
        outer = _mm(jnp.where(diag, xp, 0.0), jnp.broadcast_to(bm, (LANES, LANES)), True)
        h = st_ref[0, 2 * i:2 * i + 2].reshape(LANES, SSD_STATE)
        dpair = jnp.concatenate([jnp.broadcast_to(dec[2 * i:2 * i + 1], (SSD_HEADDIM, LANES)),
                                 jnp.broadcast_to(dec[2 * i + 1:2 * i + 2], (SSD_HEADDIM, LANES))], axis=0)
        hn = dpair * h + outer
        so_ref[0, 2 * i:2 * i + 2] = hn.reshape(2, SSD_HEADDIM, SSD_STATE)
        ys.append(_mm(jnp.broadcast_to(cm, (8, SSD_STATE)), hn, True, _NT)[0:1])
    y_ref[0] = jnp.concatenate(ys, axis=1) + dexp_ref[...] * xs


def _dt_expand_kernel(dt_ref, e64_ref, e128_ref, o64_ref, o128_ref):
    dt = dt_ref[...]
    o64_ref[...] = _mm(dt, e64_ref[...], True)
    o128_ref[...] = _mm(dt, e128_ref[...], True)


def _dt_expand(dt, prm):
    nb = dt.shape[0]
    return pl.pallas_call(
        _dt_expand_kernel,
        out_shape=[jax.ShapeDtypeStruct((nb, SSD_INNER), F32), jax.ShapeDtypeStruct((nb, SSD_HEADS * LANES), F32)],
        name="ssd_dt_expand",
    )(dt, prm["expand_64"], prm["expand_128"])


def _ssd_step(xbc, conv_buf, dt64, dt128, state, prm):
    nb = xbc.shape[0]
    full = lambda shape: pl.BlockSpec(shape, lambda b: (0,) * len(shape))
    y, so = pl.pallas_call(
        _ssd_step_kernel,
        grid=(nb,),
        in_specs=[pl.BlockSpec((1, 1, SSD_CONV_DIM), lambda b: (b, 0, 0)),
                  pl.BlockSpec((1, SSD_CONV - 1, SSD_CONV_DIM), lambda b: (b, 0, 0)),
                  full((SSD_CONV, SSD_CONV_DIM)), full((1, SSD_CONV_DIM)),
                  pl.BlockSpec((1, 1, SSD_INNER), lambda b: (b, 0, 0)), full((1, SSD_INNER)),
                  pl.BlockSpec((1, SSD_HEADS, LANES), lambda b: (b, 0, 0)), full((SSD_HEADS, LANES)),
                  full((SSD_HEADS, LANES)), full((1, SSD_INNER)),
                  pl.BlockSpec((1, SSD_HEADS, SSD_HEADDIM, SSD_STATE), lambda b: (b, 0, 0, 0))],
        out_specs=[pl.BlockSpec((1, 1, SSD_INNER), lambda b: (b, 0, 0)),
                   pl.BlockSpec((1, SSD_HEADS, SSD_HEADDIM, SSD_STATE), lambda b: (b, 0, 0, 0))],
        out_shape=[jax.ShapeDtypeStruct((nb, 1, SSD_INNER), F32),
                   jax.ShapeDtypeStruct((nb, SSD_HEADS, SSD_HEADDIM, SSD_STATE), F32)],
        compiler_params=pltpu.CompilerParams(dimension_semantics=("parallel",)),
        name="ssd_step",
    )(xbc.reshape(nb, 1, SSD_CONV_DIM), conv_buf, prm["conv_w"], prm["conv_b"],
      dt64.reshape(nb, 1, SSD_INNER), prm["dt_bias_64"], dt128.reshape(nb, SSD_HEADS, LANES),
      prm["dt_bias_128"], prm["a_128"], prm["d_exp"], state)
    return y.reshape(nb, SSD_INNER), so


_GROUP_LANE0 = MOE_EXPERTS
_ROUTE_LANE0 = 64


def _pack_bf16_halves(x):
    n = x.shape[1] // 2
    lo = lax.bitcast_convert_type(x[:, :n].astype(BF16).astype(F32), jnp.uint32)
    hi = lax.bitcast_convert_type(x[:, n:].astype(BF16).astype(F32), jnp.uint32)
    return (lo >> 16) | hi


def _unpack_bf16_halves(p):
    lo = lax.bitcast_convert_type(p << 16, F32)
    hi = lax.bitcast_convert_type(p & jnp.uint32(0xFFFF0000), F32)
    return jnp.concatenate([lo, hi], axis=1)


def _moe_router_kernel(y_ref, g_ref, wr_ref, br_ref, xn_ref, comb_ref, cnt_ref):
    @pl.when(pl.program_id(0) == 0)
    def _():
        cnt_ref[...] = jnp.zeros(cnt_ref.shape, F32)

    xn = _rms(y_ref[...], g_ref[...])
    xn_ref[...] = _pack_bf16_halves(xn) if xn_ref.dtype == jnp.uint32 else xn.astype(xn_ref.dtype)
    lg = _mm(xn, wr_ref[...], True) + br_ref[...]
    lane = lax.broadcasted_iota(jnp.int32, lg.shape, 1)
    is_group = (lane >= _GROUP_LANE0) & (lane < _GROUP_LANE0 + MOE_GROUPS)
    gl = jnp.where(is_group, lg, NEG)
    gmax = jnp.max(gl, axis=-1, keepdims=True)
    g_p = 1.0 / jnp.sum(jnp.exp(gl - gmax), axis=-1, keepdims=True)
    gidx = jnp.min(jnp.where(gl == gmax, lane - _GROUP_LANE0, MOE_GROUPS), axis=-1, keepdims=True)
    el = jnp.where((lane < MOE_EXPERTS) & (lane // MOE_EPG == gidx), lg, NEG)
    m1 = jnp.max(el, axis=-1, keepdims=True)
    i1 = jnp.min(jnp.where(el == m1, lane, LANES), axis=-1, keepdims=True)
    el2 = jnp.where(lane == i1, NEG, el)
    m2 = jnp.max(el2, axis=-1, keepdims=True)
    i2 = jnp.min(jnp.where(el2 == m2, lane, LANES), axis=-1, keepdims=True)
    e2 = jnp.exp(m2 - m1)
    den = 1.0 + e2
    g1 = g_p / den
    g2 = g_p * (e2 / den)
    oh1 = lane == i1
    oh2 = lane == i2
    comb = jnp.where(oh1, g1, 0.0) + jnp.where(oh2, g2, 0.0)
    picks = oh1.astype(F32) + oh2.astype(F32)
    tm = picks.shape[0]
    before = lax.broadcasted_iota(jnp.int32, (tm, tm), 0) > lax.broadcasted_iota(jnp.int32, (tm, tm), 1)
    seen = cnt_ref[...] + _mm(before.astype(F32), picks)
    r1 = jnp.sum(jnp.where(oh1, seen, 0.0), axis=-1, keepdims=True)
    r2 = jnp.sum(jnp.where(oh2, seen, 0.0), axis=-1, keepdims=True)
    cnt_ref[...] += jnp.sum(picks, axis=0, keepdims=True)
    for k, v in enumerate((i1.astype(F32), i2.astype(F32), g1, g2, r1, r2)):
        comb = jnp.where(lane == _ROUTE_LANE0 + k, v, comb)
    comb_ref[...] = comb


def _moe_router(y, g, wr, br, tm, name, xn_dtype):
    m, d = y.shape
    dx = d // 2 if xn_dtype == jnp.uint32 else d
    return pl.pallas_call(
        _moe_router_kernel,
        grid=(m // tm,),
        in_specs=[pl.BlockSpec((tm, d), lambda i: (i, 0)), pl.BlockSpec((1, d), lambda i: (0, 0)),
                  pl.BlockSpec((d, LANES), lambda i: (0, 0)), pl.BlockSpec((1, LANES), lambda i: (0, 0))],
        out_specs=[pl.BlockSpec((tm, dx), lambda i: (i, 0)), pl.BlockSpec((tm, LANES), lambda i: (i, 0)),
                   pl.BlockSpec((1, LANES), lambda i: (0, 0))],
        out_shape=[jax.ShapeDtypeStruct((m, dx), xn_dtype), jax.ShapeDtypeStruct((m, LANES), F32),
                   jax.ShapeDtypeStruct((1, LANES), F32)],
        compiler_params=pltpu.CompilerParams(dimension_semantics=("arbitrary",)),
        name=name,
    )(y, g.reshape(1, d), wr, br)


def _moe_dense_kernel(res_ref, xn_ref, comb_ref, wg_ref, wu_ref, wd_ref, o_ref, acc, *, precise):
    e = pl.program_id(1)

    @pl.when(e == 0)
    def _():
        acc[...] = jnp.zeros(acc.shape, F32)

    x = xn_ref[...]
    hdn = jax.nn.silu(_mm(x, wg_ref[0], precise)) * _mm(x, wu_ref[0], precise)
    comb = comb_ref[...]
    lane = lax.broadcasted_iota(jnp.int32, comb.shape, 1)
    gate = jnp.sum(jnp.where(lane == e, comb, 0.0), axis=-1, keepdims=True)
    acc[...] += gate * _mm(hdn, wd_ref[0], precise)

    @pl.when(e == pl.num_programs(1) - 1)
    def _():
        o_ref[...] = res_ref[...] + acc[...]


def _moe_dense(res, xn, comb, wg, wu, wd, e0, tm, name, precise):
    m, d = res.shape
    return pl.pallas_call(
        functools.partial(_moe_dense_kernel, precise=precise),
        grid=(m // tm, MOE_EXPERTS),
        in_specs=[pl.BlockSpec((tm, d), lambda i, e: (i, 0)), pl.BlockSpec((tm, d), lambda i, e: (i, 0)),
                  pl.BlockSpec((tm, LANES), lambda i, e: (i, 0)),
                  pl.BlockSpec((1, d, MOE_FF), lambda i, e: (e0 + e, 0, 0)),
                  pl.BlockSpec((1, d, MOE_FF), lambda i, e: (e0 + e, 0, 0)),
                  pl.BlockSpec((1, MOE_FF, d), lambda i, e: (e0 + e, 0, 0))],
        out_specs=pl.BlockSpec((tm, d), lambda i, e: (i, 0)),
        out_shape=jax.ShapeDtypeStruct((m, d), F32),
        scratch_shapes=[pltpu.VMEM((tm, d), F32)],
        compiler_params=pltpu.CompilerParams(dimension_semantics=("parallel", "arbitrary")),
        name=name,
    )(res, xn, comb, wg, wu, wd)


def _moe(y, g, prm, tm_r, tm_e, tag, precise):
    xn, comb, _ = _moe_router(y, g, prm["wr"], prm["br"], tm_r, "moe_router_" + tag, F32 if precise else BF16)
    return _moe_dense(y, xn, comb, prm["wg"], prm["wu"], prm["wd"], prm["e0"], tm_e, "moe_experts_" + tag, precise)


def _moe_grouped_kernel(te_ref, nt_ref, x_ref, wg_ref, wu_ref, wd_ref, o_ref):
    del te_ref
    active = pl.program_id(0) < nt_ref[0]

    @pl.when(active)
    def _():
        x = _unpack_bf16_halves(x_ref[...]).astype(BF16)
        hdn = jax.nn.silu(_mm(x, wg_ref[0])) * _mm(x, wu_ref[0])
        o_ref[...] = _mm(hdn, wd_ref[0]).astype(o_ref.dtype)

    @pl.when(jnp.logical_not(active))
    def _():
        o_ref[...] = jnp.zeros(o_ref.shape, o_ref.dtype)


def _moe_grouped(x_rows, tile_expert, n_tiles, wg, wu, wd, tm, name):
    r = x_rows.shape[0]
    d = wg.shape[1]
    grid_spec = pltpu.PrefetchScalarGridSpec(
        num_scalar_prefetch=2,
        grid=(r // tm,),
        in_specs=[pl.BlockSpec((tm, x_rows.shape[1]), lambda i, te, nt: (i, 0)),
                  pl.BlockSpec((1, d, MOE_FF), lambda i, te, nt: (te[i], 0, 0)),
                  pl.BlockSpec((1, d, MOE_FF), lambda i, te, nt: (te[i], 0, 0)),
                  pl.BlockSpec((1, MOE_FF, d), lambda i, te, nt: (te[i], 0, 0))],
        out_specs=pl.BlockSpec((tm, d), lambda i, te, nt: (i, 0)),
    )
    return pl.pallas_call(
        _moe_grouped_kernel,
        grid_spec=grid_spec,
        out_shape=jax.ShapeDtypeStruct((r, d), F32),
        compiler_params=pltpu.CompilerParams(dimension_semantics=("arbitrary",)),
        name=name,
    )(tile_expert, n_tiles, x_rows, wg, wu, wd)


def _moe_sparse(y, g, prm, tm_r, tm_e, tag):
    t, d = y.shape
    xn, route, cnt = _moe_router(y, g, prm["wr"], prm["br"], tm_r, "moe_router_" + tag, jnp.uint32)
    lanes = lambda k: route[:, _ROUTE_LANE0 + k:_ROUTE_LANE0 + k + 2]
    ids = lanes(0).astype(jnp.int32)
    gates = lanes(2)
    rank = lanes(4).astype(jnp.int32)
    counts = cnt[0, :MOE_EXPERTS].astype(jnp.int32)
    rows = 2 * t + MOE_EXPERTS * tm_e
    pcounts = ((counts + tm_e - 1) // tm_e) * tm_e
    pend = jnp.cumsum(pcounts)
    pstart = pend - pcounts
    pos = pstart[ids] + rank
    tile_start = jnp.arange(rows // tm_e, dtype=jnp.int32) * tm_e
    tile_expert = jnp.minimum(jnp.sum((pend[None, :] <= tile_start[:, None]).astype(jnp.int32), axis=1),
                              MOE_EXPERTS - 1)
    n_tiles = (pend[-1:] // tm_e).astype(jnp.int32)
    token = jnp.broadcast_to(jnp.arange(t, dtype=jnp.int32)[:, None], (t, 2))
    src = jnp.zeros((rows,), jnp.int32).at[pos.reshape(-1)].set(token.reshape(-1), unique_indices=True)
    x_rows = xn[src]
    out_rows = _moe_grouped(x_rows, tile_expert + prm["e0"], n_tiles, prm["wg"], prm["wu"], prm["wd"], tm_e,
                            "moe_experts_" + tag)
    return y + gates[:, 0:1] * out_rows[pos[:, 0]] + gates[:, 1:2] * out_rows[pos[:, 1]]


def _moe_params(layer, w_group, b_group, w_expert, b_expert, w_gate, w_up, w_down):
    d = w_group.shape[0]
    pad = LANES - MOE_EXPERTS - MOE_GROUPS
    wr = jnp.concatenate([w_expert, w_group, jnp.zeros((d, pad), F32)], axis=1)
    br = jnp.concatenate([b_expert, b_group, jnp.zeros((pad,), F32)]).reshape(1, LANES)
    flat = lambda w: w.reshape((-1,) + w.shape[2:])
    return {"wr": wr, "br": br, "wg": flat(w_gate), "wu": flat(w_up), "wd": flat(w_down), "e0": layer * MOE_EXPERTS}


def _ssd_params(w_in, conv_w, conv_b, dt_bias, a_log, d, norm, w_out):
    a = -jnp.exp(a_log)
    w_dt = w_in[:, SSD_INNER + SSD_CONV_DIM:]

    def grouped(x):
        x = x.reshape(x.shape[:-1] + (SSD_GROUPS, SSD_HPG))
        x = jnp.pad(x, [(0, 0)] * (x.ndim - 1) + [(0, LANES - SSD_HPG)])
        return x.reshape(x.shape[:-2] + (SSD_GROUPS * LANES,))

    w_z = w_in[:, :SSD_INNER]
    w_xbc = w_in[:, SSD_INNER:SSD_INNER + SSD_CONV_DIM]
    eye = jnp.eye(SSD_HEADS, dtype=F32)
    return {
        "w_z": w_z.astype(BF16), "w_xbc": w_xbc.astype(BF16), "w_dt_g": grouped(w_dt).astype(BF16),
        "w_z_f32": w_z, "w_xbc_f32": w_xbc, "w_dt_f32": w_dt, "w_out_f32": w_out,
        "expand_64": jnp.repeat(eye, SSD_HEADDIM, axis=1), "expand_128": jnp.repeat(eye, LANES, axis=1),
        "conv_w": conv_w, "conv_b": conv_b.reshape(1, SSD_CONV_DIM),
        "dt_bias_g": grouped(dt_bias).reshape(SSD_GROUPS, 1, LANES),
        "a_g": grouped(a).reshape(SSD_GROUPS, 1, LANES),
        "dt_bias_64": jnp.repeat(dt_bias, SSD_HEADDIM).reshape(1, SSD_INNER),
        "dt_bias_128": jnp.broadcast_to(dt_bias[:, None], (SSD_HEADS, LANES)),
        "a_128": jnp.broadcast_to(a[:, None], (SSD_HEADS, LANES)),
        "d_exp": jnp.repeat(d, SSD_HEADDIM).reshape(1, SSD_INNER),
        "norm": norm, "w_out": w_out.astype(BF16),
    }


_TM = 512


def kernel(x_prompt, x_sample, cache_k, cache_v, page_table, state_s5_re, state_s5_im, state_ssd, state_conv, norm_mix, norm_ffn, norm_final, even_w_in, even_w_out, diff_lam_q1, diff_lam_k1, diff_lam_q2, diff_lam_k2, diff_subln, s5_lam_re, s5_lam_im, s5_log_dt, s5_b_re, s5_b_im, s5_c_re, s5_c_im, s5_d, s5_w_glu, s5_b_glu, ssd_w_in, ssd_conv_w, ssd_conv_b, ssd_dt_bias, ssd_a_log, ssd_d, ssd_norm, ssd_w_out, moe_w_group, moe_b_group, moe_w_expert, moe_b_expert, moe_w_gate, moe_w_up, moe_w_down):
    bp, sp, d = x_prompt.shape
    bs = x_sample.shape[0]
    tp = bp * sp
    depth = norm_mix.shape[0]
    y_p = x_prompt.reshape(tp, d)
    y_s = x_sample.reshape(bs, d)
    outs = {n: [] for n in ("k_p", "v_p", "k_s", "v_s", "re_p", "im_p", "re_s", "im_s", "ssd_p", "ssd_s", "cv_p", "cv_s")}

    for li in range(depth):
        if li % 2 == 0:
            e = li // 2
            lam_init = 0.8 - 0.6 * math.exp(-0.3 * li)
            lam = (jnp.exp(jnp.sum(diff_lam_q1[e] * diff_lam_k1[e])) - jnp.exp(jnp.sum(diff_lam_q2[e] * diff_lam_k2[e]))
                   + lam_init).astype(F32)
            ws32 = [even_w_in[e][:, i * ATT_WIDTH:(i + 1) * ATT_WIDTH] for i in range(4)]
            ws = [w.astype(BF16) for w in ws32]
            w_out32 = [even_w_out[e][:ATT_WIDTH], even_w_out[e][ATT_WIDTH:]]
            w_out_a, w_out_s = [w.astype(BF16) for w in w_out32]
            s5p = _s5_params(s5_lam_re[e], s5_lam_im[e], s5_log_dt[e], s5_b_re[e], s5_b_im[e], s5_c_re[e], s5_c_im[e],
                             s5_d[e], s5_w_glu[e], s5_b_glu[e])
            dts = [(BF16,), (_HEAD_ROWS, BF16), (_HEAD_ROWS, _TILE_T), (F32,)]
            q, k, k16, v, vt, u = _norm_matmul(y_p, norm_mix[li], ws, dts, _TM, "even_in_p")
            o = _attn_prompt(q.reshape(bp, sp, ATT_WIDTH), k16.reshape(bp, sp, ATT_WIDTH), vt, lam, diff_subln[e],
                             1.0 - lam_init)
            s5o, h_re, h_im = _s5_prompt(u.reshape(bp, sp, S5_WIDTH), s5p)
            y_p = _matmul_res(y_p, [o.reshape(tp, ATT_WIDTH), s5o.reshape(tp, S5_WIDTH)], [w_out_a, w_out_s], _TM,
                              "even_out_p")
            outs["k_p"].append(k.reshape(bp, sp, ATT_HEADS, HEAD_W))
            outs["v_p"].append(v.reshape(bp, sp, ATT_HEADS, HEAD_W))
            outs["re_p"].append(h_re.reshape(bp, S5_GROUPS, S5_STATE))
            outs["im_p"].append(h_im.reshape(bp, S5_GROUPS, S5_STATE))
            q, k, v, u = _norm_matmul(y_s, norm_mix[li], ws32, [(F32,), (_HEAD_ROWS,), (_HEAD_ROWS,), (F32,)], bs,
                                      "even_in_s", True)
            o = _attn_decode(q, k, v, cache_k, cache_v, page_table + e * cache_k.shape[1], lam, diff_subln[e],
                             1.0 - lam_init)
            s5o, h_re, h_im = _s5_step(u, state_s5_re[e].reshape(bs, S5_LANES), state_s5_im[e].reshape(bs, S5_LANES), s5p)
            y_s = _matmul_res(y_s, [o, s5o], w_out32, bs, "even_out_s", True)
            outs["k_s"].append(k.reshape(bs, 1, ATT_HEADS, HEAD_W))
            outs["v_s"].append(v.reshape(bs, 1, ATT_HEADS, HEAD_W))
            outs["re_s"].append(h_re.reshape(bs, S5_GROUPS, S5_STATE))
            outs["im_s"].append(h_im.reshape(bs, S5_GROUPS, S5_STATE))
        else:
            o_ = li // 2
            sp_ = _ssd_params(ssd_w_in[o_], ssd_conv_w[o_], ssd_conv_b[o_], ssd_dt_bias[o_], ssd_a_log[o_], ssd_d[o_],
                              ssd_norm[o_], ssd_w_out[o_])
            z, xc, dtg, tail = _ssd_in(y_p, norm_mix[li], sp_, sp, 256, "ssd_in_p")
            yssd, st = _ssd_prompt(xc.reshape(bp, sp, SSD_CONV_DIM), dtg.reshape(bp, sp, SSD_GROUPS * LANES), sp_)
            y_p = _gated_norm_matmul(y_p, yssd.reshape(tp, SSD_INNER), z, sp_["norm"], sp_["w_out"], 256, "ssd_out_p")
            outs["ssd_p"].append(st)
            outs["cv_p"].append(tail[:, _CONV_PAD - (SSD_CONV - 1):, :])
            z, xbc, dtc = _norm_matmul(y_s, norm_mix[li], [sp_["w_z_f32"], sp_["w_xbc_f32"], sp_["w_dt_f32"]],
                                       [(F32,), (F32,), (F32,)], bs, "ssd_in_s", True)
            dt64, dt128 = _dt_expand(dtc, sp_)
            yssd, st = _ssd_step(xbc, state_conv[o_], dt64, dt128, state_ssd[o_], sp_)
            y_s = _gated_norm_matmul(y_s, yssd, z, sp_["norm"], sp_["w_out_f32"], bs, "ssd_out_s", True)
            outs["ssd_s"].append(st)
            outs["cv_s"].append(jnp.concatenate([state_conv[o_][:, 1:], xbc[:, None, :]], axis=1))
        mp = _moe_params(li, moe_w_group[li], moe_b_group[li], moe_w_expert[li], moe_b_expert[li], moe_w_gate,
                         moe_w_up, moe_w_down)
        y_p = _moe_sparse(y_p, norm_ffn[li], mp, _TM, 256, "p")
        y_s = _moe(y_s, norm_ffn[li], mp, bs, bs, "s", li + 1 < depth)

    y_prompt = _rmsnorm(y_p, norm_final, _TM, "final_p").reshape(bp, sp, d)
    y_sample = _rmsnorm(y_s, norm_final, bs, "final_s").reshape(bs, 1, d)
    st = lambda n: jnp.stack(outs[n])
    return (y_prompt, y_sample, st("k_p"), st("v_p"), st("k_s"), st("v_s"), st("re_p"), st("im_p"), st("re_s"),
            st("im_s"), st("ssd_p"), st("ssd_s"), st("cv_p"), st("cv_s"))
```

```python
import functools
import math

import jax
import jax.numpy as jnp
from jax import lax
from jax.experimental import pallas as pl
from jax.experimental.pallas import tpu as pltpu

F32 = jnp.float32
BF16 = jnp.bfloat16
HIGHEST = lax.Precision.HIGHEST

D_MODEL = 1024
NORM_EPS = 1e-6
PAGE_SIZE = 128
ATT_HEADS = 4
ATT_D = 64
ATT_WIDTH = ATT_HEADS * 2 * ATT_D
HEAD_W = 2 * ATT_D
S5_WIDTH = 512
S5_GROUP = 16
S5_GROUPS = 32
S5_STATE = 64
S5_LANES = S5_GROUPS * S5_STATE
SSD_INNER = 2048
SSD_HEADDIM = 64
SSD_HEADS = 32
SSD_GROUPS = 4
SSD_HPG = 8
SSD_STATE = 128
SSD_CONV = 4
SSD_CONV_DIM = SSD_INNER + 2 * SSD_GROUPS * SSD_STATE
SSD_CHUNK = 128
SSD_GW = SSD_HPG * SSD_HEADDIM
MOE_GROUPS = 4
MOE_EPG = 8
MOE_EXPERTS = 32
MOE_FF = 256
LANES = 128
NEG = -1e30

_NT = (((1,), (1,)), ((), ()))
_TN = (((0,), (0,)), ((), ()))


def _rms(x, g):
    return x * lax.rsqrt(jnp.mean(x * x, axis=-1, keepdims=True) + NORM_EPS) * g


def _split(x):
    hi = x.astype(BF16)
    return hi, (x - hi.astype(F32)).astype(BF16)


def _mm(a, b, precise=False, dims=None):
    dot = jnp.dot if dims is None else functools.partial(lax.dot_general, dimension_numbers=dims)
    if not precise:
        return dot(a.astype(BF16), b.astype(BF16), preferred_element_type=F32)
    m = a.shape[0]
    a_hi, a_lo = _split(a)
    b_hi, b_lo = _split(b)
    r = dot(jnp.concatenate([a_hi, a_lo], axis=0), b_hi, preferred_element_type=F32)
    return r[:m] + r[m:] + dot(a_hi, b_lo, preferred_element_type=F32)


def _norm_matmul_kernel(x_ref, g_ref, *refs, out_dtypes, precise):
    n = len(out_dtypes)
    w_refs = refs[:n]
    o_refs = refs[n:]
    h = _rms(x_ref[...], g_ref[...])
    if not precise:
        h = h.astype(BF16)
    k = 0
    for w_ref, dts in zip(w_refs, out_dtypes):
        r = _mm(h, w_ref[...], precise)
        for dt in dts:
            if dt == _TILE_T:
                o_refs[k][0] = r.T.astype(BF16)
            elif dt == _HEAD_ROWS:
                nh = r.shape[1] // LANES
                for hd in range(nh):
                    o_refs[k][pl.ds(hd, r.shape[0], stride=nh), :] = r[:, hd * LANES:(hd + 1) * LANES]
            else:
                o_refs[k][...] = r.astype(dt)
            k += 1


_TILE_T = "bf16 row tiles, each transposed"
_HEAD_ROWS = "f32, one 128-lane row per head"


def _norm_matmul(x, g, ws, out_dtypes, tm, name, precise=False):
    m, kdim = x.shape
    out_shape, out_specs = [], []
    for w, dts in zip(ws, out_dtypes):
        for dt in dts:
            if dt == _TILE_T:
                out_shape.append(jax.ShapeDtypeStruct((m // tm, w.shape[1], tm), BF16))
                out_specs.append(pl.BlockSpec((1, w.shape[1], tm), lambda i: (i, 0, 0)))
            elif dt == _HEAD_ROWS:
                nh = w.shape[1] // LANES
                out_shape.append(jax.ShapeDtypeStruct((m * nh, LANES), F32))
                out_specs.append(pl.BlockSpec((tm * nh, LANES), lambda i: (i, 0)))
            else:
                out_shape.append(jax.ShapeDtypeStruct((m, w.shape[1]), dt))
                out_specs.append(pl.BlockSpec((tm, w.shape[1]), lambda i: (i, 0)))
    return pl.pallas_call(
        functools.partial(_norm_matmul_kernel, out_dtypes=out_dtypes, precise=precise),
        grid=(m // tm,),
        in_specs=[pl.BlockSpec((tm, kdim), lambda i: (i, 0)), pl.BlockSpec((1, kdim), lambda i: (0, 0))]
        + [pl.BlockSpec(w.shape, lambda i: (0, 0)) for w in ws],
        out_specs=out_specs,
        out_shape=out_shape,
        compiler_params=pltpu.CompilerParams(dimension_semantics=("parallel",)),
        name=name,
    )(x, g.reshape(1, kdim), *ws)


def _matmul_res_kernel(res_ref, *refs, precise):
    n = (len(refs) - 1) // 2
    acc = res_ref[...]
    for a_ref, w_ref in zip(refs[:n], refs[n:2 * n]):
        acc = acc + _mm(a_ref[...], w_ref[...], precise)
    refs[-1][...] = acc


def _matmul_res(res, a_list, w_list, tm, name, precise=False):
    m, n = res.shape
    return pl.pallas_call(
        functools.partial(_matmul_res_kernel, precise=precise),
        grid=(m // tm,),
        in_specs=[pl.BlockSpec((tm, n), lambda i: (i, 0))]
        + [pl.BlockSpec((tm, a.shape[1]), lambda i: (i, 0)) for a in a_list]
        + [pl.BlockSpec(w.shape, lambda i: (0, 0)) for w in w_list],
        out_specs=pl.BlockSpec((tm, n), lambda i: (i, 0)),
        out_shape=jax.ShapeDtypeStruct((m, n), F32),
        compiler_params=pltpu.CompilerParams(dimension_semantics=("parallel",)),
        name=name,
    )(res, *a_list, *w_list)


def _gated_norm_matmul_kernel(res_ref, y_ref, z_ref, g_ref, w_ref, o_ref, *, precise):
    y = y_ref[...].astype(F32) * jax.nn.silu(z_ref[...].astype(F32))
    o_ref[...] = res_ref[...] + _mm(_rms(y, g_ref[...]), w_ref[...], precise)


def _gated_norm_matmul(res, y, z, g, w, tm, name, precise=False):
    m, n = res.shape
    kdim = y.shape[1]
    return pl.pallas_call(
        functools.partial(_gated_norm_matmul_kernel, precise=precise),
        grid=(m // tm,),
        in_specs=[pl.BlockSpec((tm, n), lambda i: (i, 0)), pl.BlockSpec((tm, kdim), lambda i: (i, 0)),
                  pl.BlockSpec((tm, kdim), lambda i: (i, 0)), pl.BlockSpec((1, kdim), lambda i: (0, 0)),
                  pl.BlockSpec(w.shape, lambda i: (0, 0))],
        out_specs=pl.BlockSpec((tm, n), lambda i: (i, 0)),
        out_shape=jax.ShapeDtypeStruct((m, n), F32),
        compiler_params=pltpu.CompilerParams(dimension_semantics=("parallel",)),
        name=name,
    )(res, y, z, g.reshape(1, kdim), w)


def _rmsnorm_kernel(x_ref, g_ref, o_ref):
    o_ref[...] = _rms(x_ref[...], g_ref[...])


def _rmsnorm(x, g, tm, name):
    m, n = x.shape
    return pl.pallas_call(
        _rmsnorm_kernel,
        grid=(m // tm,),
        in_specs=[pl.BlockSpec((tm, n), lambda i: (i, 0)), pl.BlockSpec((1, n), lambda i: (0, 0))],
        out_specs=pl.BlockSpec((tm, n), lambda i: (i, 0)),
        out_shape=jax.ShapeDtypeStruct((m, n), F32),
        compiler_params=pltpu.CompilerParams(dimension_semantics=("parallel",)),
        name=name,
    )(x, g.reshape(1, n))


def _split_q(q):
    lane = lax.broadcasted_iota(jnp.int32, q.shape, 1) % HEAD_W
    scale = ATT_D ** -0.5
    qs = q * jnp.asarray(scale, q.dtype)
    zero = jnp.zeros_like(qs)
    return jnp.where(lane < ATT_D, qs, zero), jnp.where(lane >= ATT_D, qs, zero)


def _subln(o, g, out_scale):
    return _rms(o, g) * out_scale


def _attn_prompt_kernel(lam_ref, q_ref, k_ref, vt_ref, g_ref, o_ref, qt_sc, m_sc, l_sc, acc_sc, *, tq, tk, out_scale):
    qi = pl.program_id(1)
    heads = range(ATT_HEADS)
    for h in heads:
        q1, q2 = _split_q(q_ref[0, :, h * HEAD_W:(h + 1) * HEAD_W])
        qt_sc[h] = jnp.concatenate([q1, q2], axis=0).astype(F32).T.astype(BF16)
    m_sc[...] = jnp.full(m_sc.shape, NEG, F32)
    l_sc[...] = jnp.zeros(l_sc.shape, F32)
    acc_sc[...] = jnp.zeros(acc_sc.shape, F32)

    def kv_step(j, masked):
        start = pl.multiple_of(j * tk, tk)
        for h in heads:
            kt = k_ref[0, pl.ds(start, tk), h * HEAD_W:(h + 1) * HEAD_W]
            st = jnp.dot(kt, qt_sc[h], preferred_element_type=F32)
            if masked:
                k_pos = start + lax.broadcasted_iota(jnp.int32, st.shape, 0)
                q_pos = qi * tq + lax.broadcasted_iota(jnp.int32, st.shape, 1) % tq
                st = jnp.where(k_pos <= q_pos, st, NEG)
            m_old = m_sc[h]
            m_new = jnp.maximum(m_old, jnp.max(st, axis=0, keepdims=True))
            p = jnp.exp(st - m_new)
            alpha = jnp.exp(m_old - m_new)
            l_sc[h] = alpha * l_sc[h] + jnp.sum(p, axis=0, keepdims=True)
            m_sc[h] = m_new
            vt = vt_ref[j, h * HEAD_W:(h + 1) * HEAD_W, :]
            acc_sc[h] = alpha * acc_sc[h] + jnp.dot(vt, p.astype(BF16), preferred_element_type=F32)

    n_full = (qi * tq) // tk

    def full_step(j, carry):
        kv_step(j, False)
        return carry

    lax.fori_loop(0, n_full, full_step, 0)
    kv_step(n_full, True)
    for h in heads:
        acc, l = acc_sc[h], l_sc[h]
        ot = acc[:, 0:tq] / l[:, 0:tq] - lam_ref[0, 0] * (acc[:, tq:2 * tq] / l[:, tq:2 * tq])
        o_ref[0, :, h * HEAD_W:(h + 1) * HEAD_W] = _subln(ot.T, g_ref[...], out_scale).astype(o_ref.dtype)


def _attn_prompt(q, k, vt, lam, subln, out_scale, tq=512):
    b, s, _ = q.shape
    tk = vt.shape[2]
    nblk = s // tk
    kern = functools.partial(_attn_prompt_kernel, tq=tq, tk=tk, out_scale=out_scale)
    return pl.pallas_call(
        kern,
        grid=(b, s // tq),
        in_specs=[pl.BlockSpec(memory_space=pltpu.SMEM),
                  pl.BlockSpec((1, tq, ATT_WIDTH), lambda bi, i: (bi, i, 0)),
                  pl.BlockSpec((1, s, ATT_WIDTH), lambda bi, i: (bi, 0, 0)),
                  pl.BlockSpec((nblk, ATT_WIDTH, tk), lambda bi, i: (bi, 0, 0)),
                  pl.BlockSpec((1, HEAD_W), lambda bi, i: (0, 0))],
        out_specs=pl.BlockSpec((1, tq, ATT_WIDTH), lambda bi, i: (bi, i, 0)),
        out_shape=jax.ShapeDtypeStruct((b, s, ATT_WIDTH), BF16),
        scratch_shapes=[pltpu.VMEM((ATT_HEADS, HEAD_W, 2 * tq), BF16), pltpu.VMEM((ATT_HEADS, 1, 2 * tq), F32),
                        pltpu.VMEM((ATT_HEADS, 1, 2 * tq), F32), pltpu.VMEM((ATT_HEADS, HEAD_W, 2 * tq), F32)],
        compiler_params=pltpu.CompilerParams(dimension_semantics=("parallel", "parallel")),
        name="attn_prompt",
    )(lam.reshape(1, 1), q, k, vt, subln.reshape(1, HEAD_W))


_DEC_PAGES = 16
_DEC_PROW = PAGE_SIZE * ATT_HEADS
_DEC_COLS = _DEC_PAGES * _DEC_PROW
_DEC_ROWS = 2 * ATT_HEADS


def _attn_decode_kernel(pt_ref, lam_ref, q_ref, kn_ref, vn_ref, g_ref, *refs, out_scale):
    del pt_ref
    k_refs = refs[:_DEC_PAGES]
    v_refs = refs[_DEC_PAGES:2 * _DEC_PAGES]
    o_ref = refs[2 * _DEC_PAGES]
    m_sc, l_sc, acc_sc = refs[2 * _DEC_PAGES + 1:]
    j = pl.program_id(1)

    q4 = q_ref[0] * (ATT_D ** -0.5)
    q8 = jnp.concatenate([q4, q4], axis=0)
    r8 = lax.broadcasted_iota(jnp.int32, (_DEC_ROWS, HEAD_W), 0)
    l8 = lax.broadcasted_iota(jnp.int32, (_DEC_ROWS, HEAD_W), 1)
    q8 = jnp.where(l8 // ATT_D == r8 // ATT_HEADS, q8, 0.0)

    @pl.when(j == 0)
    def _():
        m_sc[...] = jnp.full(m_sc.shape, NEG, F32)
        l_sc[...] = jnp.zeros(l_sc.shape, F32)
        acc_sc[...] = jnp.zeros(acc_sc.shape, F32)

    s = jnp.concatenate([_mm(q8, k_ref[0], True, _NT) for k_ref in k_refs], axis=1)
    rs = lax.broadcasted_iota(jnp.int32, s.shape, 0)
    cs = lax.broadcasted_iota(jnp.int32, s.shape, 1)
    s = jnp.where(cs % ATT_HEADS == rs % ATT_HEADS, s, NEG)
    m = m_sc[...]
    mn = jnp.maximum(m, jnp.max(s, axis=-1, keepdims=True))
    p = jnp.exp(s - mn)
    alpha = jnp.exp(m - mn)
    l = alpha * l_sc[...] + jnp.sum(p, axis=-1, keepdims=True)
    acc = alpha * acc_sc[...]
    for i, v_ref in enumerate(v_refs):
        acc = acc + _mm(p[:, i * _DEC_PROW:(i + 1) * _DEC_PROW], v_ref[0], True)
    m_sc[...] = mn
    l_sc[...] = l
    acc_sc[...] = acc

    @pl.when(j == pl.num_programs(1) - 1)
    def _():
        kn = jnp.concatenate([kn_ref[0], kn_ref[0]], axis=0)
        vn = jnp.concatenate([vn_ref[0], vn_ref[0]], axis=0)
        sn = jnp.sum(q8 * kn, axis=-1, keepdims=True)
        mf = jnp.maximum(mn, sn)
        pn = jnp.exp(sn - mf)
        af = jnp.exp(mn - mf)
        o8 = (af * acc + pn * vn) / (af * l + pn)
        o4 = o8[:ATT_HEADS] - lam_ref[0, 0] * o8[ATT_HEADS:]
        o_ref[0] = _subln(o4, g_ref[...], out_scale)


def _attn_decode(q, k_new, v_new, cache_k, cache_v, page_table, lam, subln, out_scale):
    b = q.shape[0]
    n_pages = page_table.shape[1]
    steps = n_pages // _DEC_PAGES
    ck = cache_k.reshape(-1, _DEC_PROW, HEAD_W)
    cv = cache_v.reshape(-1, _DEC_PROW, HEAD_W)

    def page_map(i):
        return lambda bi, j, pt: (pt[bi, j * _DEC_PAGES + i], 0, 0)

    row = lambda bi, j, pt: (bi, 0, 0)
    head_rows = pl.BlockSpec((1, ATT_HEADS, HEAD_W), row)
    page_specs = [pl.BlockSpec((1, _DEC_PROW, HEAD_W), page_map(i)) for i in range(_DEC_PAGES)]
    grid_spec = pltpu.PrefetchScalarGridSpec(
        num_scalar_prefetch=1,
        grid=(b, steps),
        in_specs=[pl.BlockSpec(memory_space=pltpu.SMEM), head_rows, head_rows, head_rows,
                  pl.BlockSpec((1, HEAD_W), lambda bi, j, pt: (0, 0))] + page_specs + page_specs,
        out_specs=head_rows,
        scratch_shapes=[pltpu.VMEM((_DEC_ROWS, 1), F32), pltpu.VMEM((_DEC_ROWS, 1), F32),
                        pltpu.VMEM((_DEC_ROWS, HEAD_W), F32)],
    )
    out = pl.pallas_call(
        functools.partial(_attn_decode_kernel, out_scale=out_scale),
        grid_spec=grid_spec,
        out_shape=jax.ShapeDtypeStruct((b, ATT_HEADS, HEAD_W), F32),
        compiler_params=pltpu.CompilerParams(dimension_semantics=("parallel", "arbitrary")),
        name="attn_decode",
    )(page_table, lam.reshape(1, 1), q.reshape(b, ATT_HEADS, HEAD_W), k_new.reshape(b, ATT_HEADS, HEAD_W),
      v_new.reshape(b, ATT_HEADS, HEAD_W), subln.reshape(1, HEAD_W), *([ck] * _DEC_PAGES), *([cv] * _DEC_PAGES))
    return out.reshape(b, ATT_WIDTH)


_S5_KT = 2
_S5_KW = S5_WIDTH // _S5_KT
_S5_KL = S5_LANES // _S5_KT
_S5_NT = S5_LANES // LANES


def _s5_input(u, bmat_ref, precise):
    re, im = [], []
    for kt in range(_S5_KT):
        r = _mm(u[:, kt * _S5_KW:(kt + 1) * _S5_KW], bmat_ref[kt], precise)
        re.append(r[:, :_S5_KL])
        im.append(r[:, _S5_KL:])
    return jnp.concatenate(re, axis=1), jnp.concatenate(im, axis=1)


def _s5_output(h_re, h_im, u, cmat_ref, d_ref, wglu_ref, bglu_ref, precise):
    if not precise:
        h_re, h_im = h_re.astype(BF16), h_im.astype(BF16)
    ys = []
    for kt in range(_S5_KT):
        sl = slice(kt * _S5_KL, (kt + 1) * _S5_KL)
        y = _mm(h_re[:, sl], cmat_ref[kt, :_S5_KL, :], precise)
        y = y + _mm(h_im[:, sl], cmat_ref[kt, _S5_KL:, :], precise)
        ys.append(y)
    y = jnp.concatenate(ys, axis=1) + d_ref[...] * u
    g = jax.nn.gelu(y)
    gate = _mm(g, wglu_ref[...], precise) + bglu_ref[...]
    return g * jax.nn.sigmoid(gate)


def _s5_scan_kernel(u_ref, are_ref, aim_ref, bmat_ref, cmat_ref, d_ref, wglu_ref, bglu_ref,
                    o_ref, hre_ref, him_ref, bu_re, bu_im, *, nb, lc, fold):
    c = pl.program_id(0)
    slots = _S5_NT // fold
    bp = fold * nb

    @pl.when(c == 0)
    def _():
        hre_ref[...] = jnp.zeros(hre_ref.shape, F32)
        him_ref[...] = jnp.zeros(him_ref.shape, F32)

    def rows_of(jt, b):
        return jt % slots, pl.ds((jt // slots) * nb + b, lc, stride=bp)

    for b in range(nb):
        re, im = _s5_input(u_ref[b], bmat_ref, False)
        for jt in range(_S5_NT):
            slot, rows = rows_of(jt, b)
            bu_re[slot, rows, :] = re[:, jt * LANES:(jt + 1) * LANES]
            bu_im[slot, rows, :] = im[:, jt * LANES:(jt + 1) * LANES]

    a_re = are_ref[...]
    a_im = aim_ref[...]

    def step(t, carry):
        hr, hi = carry
        rows = pl.ds(pl.multiple_of(t * bp, bp), bp)
        nr = hr * a_re - hi * a_im + bu_re[:, rows, :]
        ni = hr * a_im + hi * a_re + bu_im[:, rows, :]
        bu_re[:, rows, :] = nr
        bu_im[:, rows, :] = ni
        return nr, ni

    hr, hi = lax.fori_loop(0, lc, step, (hre_ref[...], him_ref[...]), unroll=2)
    hre_ref[...] = hr
    him_ref[...] = hi

    for b in range(nb):
        h_re = jnp.concatenate([bu_re[rows_of(jt, b)] for jt in range(_S5_NT)], axis=1)
        h_im = jnp.concatenate([bu_im[rows_of(jt, b)] for jt in range(_S5_NT)], axis=1)
        o_ref[b] = _s5_output(h_re, h_im, u_ref[b], cmat_ref, d_ref, wglu_ref, bglu_ref, False).astype(o_ref.dtype)


def _s5_prompt(u, prm, lc=128):
    nb, l, _ = u.shape
    bp = 8
    assert bp % nb == 0 and _S5_NT % (bp // nb) == 0
    fold = bp // nb
    slots = _S5_NT // fold
    full = lambda shape: pl.BlockSpec(shape, lambda c: (0,) * len(shape))

    def a_rows(a):
        a = jnp.transpose(a.reshape(fold, slots, LANES), (1, 0, 2))
        return jnp.repeat(a, nb, axis=1)

    o, h_re, h_im = pl.pallas_call(
        functools.partial(_s5_scan_kernel, nb=nb, lc=lc, fold=fold),
        grid=(l // lc,),
        in_specs=[pl.BlockSpec((nb, lc, S5_WIDTH), lambda c: (0, c, 0)),
                  full((slots, bp, LANES)), full((slots, bp, LANES)),
                  full((_S5_KT, _S5_KW, 2 * _S5_KL)), full((_S5_KT, 2 * _S5_KL, _S5_KW)),
                  full((1, S5_WIDTH)), full((S5_WIDTH, S5_WIDTH)), full((1, S5_WIDTH))],
        out_specs=[pl.BlockSpec((nb, lc, S5_WIDTH), lambda c: (0, c, 0)),
                   full((slots, bp, LANES)), full((slots, bp, LANES))],
        out_shape=[jax.ShapeDtypeStruct((nb, l, S5_WIDTH), BF16),
                   jax.ShapeDtypeStruct((slots, bp, LANES), F32), jax.ShapeDtypeStruct((slots, bp, LANES), F32)],
        scratch_shapes=[pltpu.VMEM((slots, lc * bp, LANES), F32) for _ in range(2)],
        compiler_params=pltpu.CompilerParams(dimension_semantics=("arbitrary",)),
        name="s5_prompt",
    )(u, a_rows(prm["a_re"]), a_rows(prm["a_im"]), prm["bmat"].astype(BF16),
      prm["cmat"].astype(BF16), prm["d"], prm["w_glu"].astype(BF16), prm["b_glu"])
    rows = lambda h: jnp.transpose(h.reshape(slots, fold, nb, LANES), (2, 1, 0, 3)).reshape(nb, S5_LANES)
    return o, rows(h_re), rows(h_im)


def _s5_step_kernel(u_ref, h0re_ref, h0im_ref, are_ref, aim_ref, bmat_ref, cmat_ref, d_ref, wglu_ref, bglu_ref,
                    o_ref, hre_ref, him_ref):
    u = u_ref[...]
    bu_re, bu_im = _s5_input(u, bmat_ref, True)
    h_re, h_im = h0re_ref[...], h0im_ref[...]
    a_re, a_im = are_ref[...], aim_ref[...]
    n_re = h_re * a_re - h_im * a_im + bu_re
    n_im = h_re * a_im + h_im * a_re + bu_im
    hre_ref[...] = n_re
    him_ref[...] = n_im
    o_ref[...] = _s5_output(n_re, n_im, u, cmat_ref, d_ref, wglu_ref, bglu_ref, True)


def _s5_step(u, h0_re, h0_im, prm):
    nb = u.shape[0]
    return pl.pallas_call(
        _s5_step_kernel,
        out_shape=[jax.ShapeDtypeStruct((nb, S5_WIDTH), F32),
                   jax.ShapeDtypeStruct((nb, S5_LANES), F32), jax.ShapeDtypeStruct((nb, S5_LANES), F32)],
        name="s5_step",
    )(u, h0_re, h0_im, prm["a_re"], prm["a_im"], prm["bmat"], prm["cmat"], prm["d"], prm["w_glu"], prm["b_glu"])


def _s5_params(lam_re, lam_im, log_dt, b_re, b_im, c_re, c_im, d, w_glu, b_glu):
    lam = lax.complex(lam_re, lam_im)
    dt = jnp.exp(log_dt)[:, None]
    a_bar = jnp.exp(lam * dt)
    b_bar = ((a_bar - 1.0) / lam)[..., None] * lax.complex(b_re, b_im)
    gk = S5_GROUPS // _S5_KT
    eye = jnp.eye(gk, dtype=F32)

    def in_tile(x):
        x = x.reshape(_S5_KT, gk, S5_STATE, S5_GROUP)
        return jnp.einsum("kgpc,gh->kgchp", x, eye).reshape(_S5_KT, _S5_KW, _S5_KL)

    def out_tile(x):
        x = x.reshape(_S5_KT, gk, S5_GROUP, S5_STATE)
        return jnp.einsum("kgcp,gh->kgphc", x, eye).reshape(_S5_KT, _S5_KL, _S5_KW)

    bmat = jnp.concatenate([in_tile(jnp.real(b_bar)), in_tile(jnp.imag(b_bar))], axis=2)
    cmat = jnp.concatenate([out_tile(c_re), out_tile(-c_im)], axis=1)
    return {
        "a_re": jnp.real(a_bar).reshape(1, S5_LANES), "a_im": jnp.imag(a_bar).reshape(1, S5_LANES),
        "bmat": bmat, "cmat": cmat, "d": d.reshape(1, S5_WIDTH), "w_glu": w_glu, "b_glu": b_glu.reshape(1, S5_WIDTH),
    }


_CONV_PAD = 8


def _conv_silu(x, prev_ref, w_ref, b_ref):
    rows = x.shape[0]
    w = w_ref[...]
    prev = prev_ref[...]
    sub = lax.broadcasted_iota(jnp.int32, prev.shape, 0)
    out = b_ref[...]
    for i in range(SSD_CONV):
        s = SSD_CONV - 1 - i
        if s == 0:
            shifted = x
        else:
            rolled = pltpu.roll(x, s, 0)
            head = jnp.where(sub < s, pltpu.roll(prev, s, 0), rolled[0:_CONV_PAD])
            shifted = jnp.concatenate([head, rolled[_CONV_PAD:]], axis=0)
        out = out + w[i:i + 1] * shifted
    prev_ref[...] = x[rows - _CONV_PAD:rows, :]
    return jax.nn.silu(out)


def _ssd_in_kernel(x_ref, g_ref, wz_ref, wx_ref, wdt_ref, cw_ref, cb_ref, z_ref, xc_ref, dt_ref, tail_ref, xf,
                   *, tiles_per_seq):
    i = pl.program_id(0)
    tm = x_ref.shape[0]

    @pl.when(i % tiles_per_seq == 0)
    def _():
        xf[...] = jnp.zeros(xf.shape, F32)

    h = _rms(x_ref[...], g_ref[...]).astype(BF16)
    z_ref[...] = _mm(h, wz_ref[...]).astype(z_ref.dtype)
    dt_ref[...] = _mm(h, wdt_ref[...])
    xbc = _mm(h, wx_ref[...])
    tail_ref[0] = xbc[tm - _CONV_PAD:tm, :]
    xc_ref[...] = _conv_silu(xbc, xf, cw_ref, cb_ref)


def _ssd_in(x, g, prm, seq_len, tm, name):
    m, kdim = x.shape
    tiles_per_seq = seq_len // tm
    full = lambda a: pl.BlockSpec(a.shape, lambda i: (0,) * a.ndim)
    rows = lambda n: pl.BlockSpec((tm, n), lambda i: (i, 0))
    wz, wx, wdt = prm["w_z"], prm["w_xbc"], prm["w_dt_g"]
    return pl.pallas_call(
        functools.partial(_ssd_in_kernel, tiles_per_seq=tiles_per_seq),
        grid=(m // tm,),
        in_specs=[rows(kdim), pl.BlockSpec((1, kdim), lambda i: (0, 0)), full(wz), full(wx), full(wdt),
                  full(prm["conv_w"]), full(prm["conv_b"])],
        out_specs=[rows(wz.shape[1]), rows(wx.shape[1]), rows(wdt.shape[1]),
                   pl.BlockSpec((1, _CONV_PAD, wx.shape[1]), lambda i: (i // tiles_per_seq, 0, 0))],
        out_shape=[jax.ShapeDtypeStruct((m, wz.shape[1]), BF16), jax.ShapeDtypeStruct((m, wx.shape[1]), F32),
                   jax.ShapeDtypeStruct((m, wdt.shape[1]), F32),
                   jax.ShapeDtypeStruct((m // seq_len, _CONV_PAD, wx.shape[1]), F32)],
        scratch_shapes=[pltpu.VMEM((_CONV_PAD, wx.shape[1]), F32)],
        compiler_params=pltpu.CompilerParams(dimension_semantics=("arbitrary",)),
        name=name,
    )(x, g.reshape(1, kdim), wz, wx, wdt, prm["conv_w"], prm["conv_b"])


def _ssd_chunk_kernel(xs_ref, bm_ref, cm_ref, dt_ref, dtb_ref, a_ref, dexp_ref, y_ref, st_ref):
    c = pl.program_id(2)
    q = SSD_CHUNK

    @pl.when(c == 0)
    def _():
        st_ref[...] = jnp.zeros(st_ref.shape, F32)

    xs = xs_ref[0]
    bm = bm_ref[0]
    bm16 = bm.astype(BF16)
    cm16 = cm_ref[0].astype(BF16)

    dt = jax.nn.softplus(dt_ref[0] + dtb_ref[0])
    da = dt * a_ref[0]
    row = lax.broadcasted_iota(jnp.int32, (q, q), 0)
    col = lax.broadcasted_iota(jnp.int32, (q, q), 1)
    causal = row >= col
    acs = jnp.dot(causal.astype(F32), da, preferred_element_type=F32, precision=HIGHEST)
    acs_t = acs.T
    acs_last = acs[q - 1:q, :]
    e_acs = jnp.exp(acs)
    w_s = dt * jnp.exp(acs_last - acs)
    dt_t = dt.T
    chunk_dec = jnp.exp(acs_last)
    cb = _mm(cm16, bm16, dims=_NT)

    first = lax.broadcasted_iota(jnp.int32, (q, LANES), 1) < SSD_HEADDIM
    first_rows = lax.broadcasted_iota(jnp.int32, (LANES, SSD_STATE), 0) < SSD_HEADDIM

    ys = []
    for i in range(SSD_HPG // 2):
        xs16 = xs[:, i * LANES:(i + 1) * LANES].astype(BF16)
        heads = (2 * i, 2 * i + 1)
        y_head, st_head = [], []
        for j in heads:
            seg = acs[:, j:j + 1] - acs_t[j:j + 1, :]
            lmat = jnp.exp(jnp.where(causal, seg, -jnp.inf))
            y_head.append(_mm(cb * lmat * dt_t[j:j + 1, :], xs16))
            st_head.append(_mm(xs16, bm * w_s[:, j:j + 1], dims=_TN))
        y_diag = jnp.where(first, y_head[0], y_head[1])
        h = st_ref[0, 2 * i:2 * i + 2].reshape(LANES, SSD_STATE)
        y_off = _mm(cm16, h, dims=_NT) * jnp.where(first, e_acs[:, heads[0]:heads[0] + 1],
                                                    e_acs[:, heads[1]:heads[1] + 1])
        cdec = jnp.concatenate([jnp.broadcast_to(chunk_dec[:, j:j + 1], (SSD_HEADDIM, SSD_STATE)) for j in heads],
                               axis=0)
        h_new = cdec * h + jnp.where(first_rows, st_head[0], st_head[1])
        st_ref[0, 2 * i:2 * i + 2] = h_new.reshape(2, SSD_HEADDIM, SSD_STATE)
        ys.append(y_diag + y_off)
    y_ref[0] = (jnp.concatenate(ys, axis=1) + dexp_ref[...] * xs).astype(y_ref.dtype)


def _ssd_prompt(xc, dtg, prm):
    nb, l, _ = xc.shape
    nc = l // SSD_CHUNK
    q = SSD_CHUNK
    boff = SSD_INNER // LANES
    coff = boff + SSD_GROUPS
    grp = lambda b, g, c: (g, 0, 0)
    return pl.pallas_call(
        _ssd_chunk_kernel,
        grid=(nb, SSD_GROUPS, nc),
        in_specs=[pl.BlockSpec((1, q, SSD_GW), lambda b, g, c: (b, c, g)),
                  pl.BlockSpec((1, q, LANES), lambda b, g, c: (b, c, boff + g)),
                  pl.BlockSpec((1, q, LANES), lambda b, g, c: (b, c, coff + g)),
                  pl.BlockSpec((1, q, LANES), lambda b, g, c: (b, c, g)),
                  pl.BlockSpec((1, 1, LANES), grp), pl.BlockSpec((1, 1, LANES), grp),
                  pl.BlockSpec((1, SSD_GW), lambda b, g, c: (0, g))],
        out_specs=[pl.BlockSpec((1, q, SSD_GW), lambda b, g, c: (b, c, g)),
                   pl.BlockSpec((1, SSD_HPG, SSD_HEADDIM, SSD_STATE), lambda b, g, c: (b, g, 0, 0))],
        out_shape=[jax.ShapeDtypeStruct((nb, l, SSD_INNER), BF16),
                   jax.ShapeDtypeStruct((nb, SSD_HEADS, SSD_HEADDIM, SSD_STATE), F32)],
        compiler_params=pltpu.CompilerParams(dimension_semantics=("parallel", "parallel", "arbitrary")),
        name="ssd_prompt",
    )(xc, xc, xc, dtg, prm["dt_bias_g"], prm["a_g"], prm["d_exp"])


def _ssd_step_kernel(x_ref, buf_ref, cw_ref, cb_ref, dt64_ref, dtb64_ref, dt128_ref, dtb128_ref, a128_ref,
                     dexp_ref, st_ref, y_ref, so_ref):
    w = cw_ref[...]
    buf = buf_ref[0]
    conv = cb_ref[...]
    for i in range(SSD_CONV - 1):
        conv = conv + w[i:i + 1] * buf[i:i + 1]
    conv = conv + w[SSD_CONV - 1:SSD_CONV] * x_ref[0]
    xc = jax.nn.silu(conv)
    xs = xc[:, :SSD_INNER]
    xdt = xs * jax.nn.softplus(dt64_ref[0] + dtb64_ref[...])
    dt = jax.nn.softplus(dt128_ref[0] + dtb128_ref[...])
    dec = jnp.exp(dt * a128_ref[...])
    row = lax.broadcasted_iota(jnp.int32, (LANES, LANES), 0)
    col = lax.broadcasted_iota(jnp.int32, (LANES, LANES), 1)
    diag = row == col
    ys = []
    for i in range(SSD_HEADS // 2):
        g = (2 * i) // SSD_HPG
        bm = xc[:, SSD_INNER + g * SSD_STATE:SSD_INNER + (g + 1) * SSD_STATE]
        cm = xc[:, SSD_INNER + (SSD_GROUPS + g) * SSD_STATE:SSD_INNER + (SSD_GROUPS + g + 1) * SSD_STATE]
        xp = jnp.broadcast_to(xdt[:, i * LANES:(i + 1) * LANES], (LANES, LANES))
        outer = _mm(jnp.where(diag, xp, 0.0), jnp.broadcast_to(bm, (LANES, LANES)), True)
        h = st_ref[0, 2 * i:2 * i + 2].reshape(LANES, SSD_STATE)
        dpair = jnp.concatenate([jnp.broadcast_to(dec[2 * i:2 * i + 1], (SSD_HEADDIM, LANES)),
                                 jnp.broadcast_to(dec[2 * i + 1:2 * i + 2], (SSD_HEADDIM, LANES))], axis=0)
        hn = dpair * h + outer
        so_ref[0, 2 * i:2 * i + 2] = hn.reshape(2, SSD_HEADDIM, SSD_STATE)
        ys.append(_mm(jnp.broadcast_to(cm, (8, SSD_STATE)), hn, True, _NT)[0:1])
    y_ref[0] = jnp.concatenate(ys, axis=1) + dexp_ref[...] * xs


def _dt_expand_kernel(dt_ref, e64_ref, e128_ref, o64_ref, o128_ref):
    dt = dt_ref[...]
    o64_ref[...] = _mm(dt, e64_ref[...], True)
    o128_ref[...] = _mm(dt, e128_ref[...], True)


def _dt_expand(dt, prm):
    nb = dt.shape[0]
    return pl.pallas_call(
        _dt_expand_kernel,
        out_shape=[jax.ShapeDtypeStruct((nb, SSD_INNER), F32), jax.ShapeDtypeStruct((nb, SSD_HEADS * LANES), F32)],
        name="ssd_dt_expand",
    )(dt, prm["expand_64"], prm["expand_128"])


def _ssd_step(xbc, conv_buf, dt64, dt128, state, prm):
    nb = xbc.shape[0]
    full = lambda shape: pl.BlockSpec(shape, lambda b: (0,) * len(shape))
    y, so = pl.pallas_call(
        _ssd_step_kernel,
        grid=(nb,),
        in_specs=[pl.BlockSpec((1, 1, SSD_CONV_DIM), lambda b: (b, 0, 0)),
                  pl.BlockSpec((1, SSD_CONV - 1, SSD_CONV_DIM), lambda b: (b, 0, 0)),
                  full((SSD_CONV, SSD_CONV_DIM)), full((1, SSD_CONV_DIM)),
                  pl.BlockSpec((1, 1, SSD_INNER), lambda b: (b, 0, 0)), full((1, SSD_INNER)),
                  pl.BlockSpec((1, SSD_HEADS, LANES), lambda b: (b, 0, 0)), full((SSD_HEADS, LANES)),
                  full((SSD_HEADS, LANES)), full((1, SSD_INNER)),
                  pl.BlockSpec((1, SSD_HEADS, SSD_HEADDIM, SSD_STATE), lambda b: (b, 0, 0, 0))],
        out_specs=[pl.BlockSpec((1, 1, SSD_INNER), lambda b: (b, 0, 0)),
                   pl.BlockSpec((1, SSD_HEADS, SSD_HEADDIM, SSD_STATE), lambda b: (b, 0, 0, 0))],
        out_shape=[jax.ShapeDtypeStruct((nb, 1, SSD_INNER), F32),
                   jax.ShapeDtypeStruct((nb, SSD_HEADS, SSD_HEADDIM, SSD_STATE), F32)],
        compiler_params=pltpu.CompilerParams(dimension_semantics=("parallel",)),
        name="ssd_step",
    )(xbc.reshape(nb, 1, SSD_CONV_DIM), conv_buf, prm["conv_w"], prm["conv_b"],
      dt64.reshape(nb, 1, SSD_INNER), prm["dt_bias_64"], dt128.reshape(nb, SSD_HEADS, LANES),
      prm["dt_bias_128"], prm["a_128"], prm["d_exp"], state)
    return y.reshape(nb, SSD_INNER), so


_GROUP_LANE0 = MOE_EXPERTS
_ROUTE_LANE0 = 64


def _pack_bf16_halves(x):
    n = x.shape[1] // 2
    lo = lax.bitcast_convert_type(x[:, :n].astype(BF16).astype(F32), jnp.uint32)
    hi = lax.bitcast_convert_type(x[:, n:].astype(BF16).astype(F32), jnp.uint32)
    return (lo >> 16) | hi


def _unpack_bf16_halves(p):
    lo = lax.bitcast_convert_type(p << 16, F32)
    hi = lax.bitcast_convert_type(p & jnp.uint32(0xFFFF0000), F32)
    return jnp.concatenate([lo, hi], axis=1)


def _moe_router_kernel(y_ref, g_ref, wr_ref, br_ref, xn_ref, comb_ref, cnt_ref):
    @pl.when(pl.program_id(0) == 0)
    def _():
        cnt_ref[...] = jnp.zeros(cnt_ref.shape, F32)

    xn = _rms(y_ref[...], g_ref[...])
    xn_ref[...] = _pack_bf16_halves(xn) if xn_ref.dtype == jnp.uint32 else xn.astype(xn_ref.dtype)
    lg = _mm(xn, wr_ref[...], True) + br_ref[...]
    lane = lax.broadcasted_iota(jnp.int32, lg.shape, 1)
    is_group = (lane >= _GROUP_LANE0) & (lane < _GROUP_LANE0 + MOE_GROUPS)
    gl = jnp.where(is_group, lg, NEG)
    gmax = jnp.max(gl, axis=-1, keepdims=True)
    g_p = 1.0 / jnp.sum(jnp.exp(gl - gmax), axis=-1, keepdims=True)
    gidx = jnp.min(jnp.where(gl == gmax, lane - _GROUP_LANE0, MOE_GROUPS), axis=-1, keepdims=True)
    el = jnp.where((lane < MOE_EXPERTS) & (lane // MOE_EPG == gidx), lg, NEG)
    m1 = jnp.max(el, axis=-1, keepdims=True)
    i1 = jnp.min(jnp.where(el == m1, lane, LANES), axis=-1, keepdims=True)
    el2 = jnp.where(lane == i1, NEG, el)
    m2 = jnp.max(el2, axis=-1, keepdims=True)
    i2 = jnp.min(jnp.where(el2 == m2, lane, LANES), axis=-1, keepdims=True)
    e2 = jnp.exp(m2 - m1)
    den = 1.0 + e2
    g1 = g_p / den
    g2 = g_p * (e2 / den)
    oh1 = lane == i1
    oh2 = lane == i2
    comb = jnp.where(oh1, g1, 0.0) + jnp.where(oh2, g2, 0.0)
    picks = oh1.astype(F32) + oh2.astype(F32)
    tm = picks.shape[0]
    before = lax.broadcasted_iota(jnp.int32, (tm, tm), 0) > lax.broadcasted_iota(jnp.int32, (tm, tm), 1)
    seen = cnt_ref[...] + _mm(before.astype(F32), picks)
    r1 = jnp.sum(jnp.where(oh1, seen, 0.0), axis=-1, keepdims=True)
    r2 = jnp.sum(jnp.where(oh2, seen, 0.0), axis=-1, keepdims=True)
    cnt_ref[...] += jnp.sum(picks, axis=0, keepdims=True)
    for k, v in enumerate((i1.astype(F32), i2.astype(F32), g1, g2, r1, r2)):
        comb = jnp.where(lane == _ROUTE_LANE0 + k, v, comb)
    comb_ref[...] = comb


def _moe_router(y, g, wr, br, tm, name, xn_dtype):
    m, d = y.shape
    dx = d // 2 if xn_dtype == jnp.uint32 else d
    return pl.pallas_call(
        _moe_router_kernel,
        grid=(m // tm,),
        in_specs=[pl.BlockSpec((tm, d), lambda i: (i, 0)), pl.BlockSpec((1, d), lambda i: (0, 0)),
                  pl.BlockSpec((d, LANES), lambda i: (0, 0)), pl.BlockSpec((1, LANES), lambda i: (0, 0))],
        out_specs=[pl.BlockSpec((tm, dx), lambda i: (i, 0)), pl.BlockSpec((tm, LANES), lambda i: (i, 0)),
                   pl.BlockSpec((1, LANES), lambda i: (0, 0))],
        out_shape=[jax.ShapeDtypeStruct((m, dx), xn_dtype), jax.ShapeDtypeStruct((m, LANES), F32),
                   jax.ShapeDtypeStruct((1, LANES), F32)],
        compiler_params=pltpu.CompilerParams(dimension_semantics=("arbitrary",)),
        name=name,
    )(y, g.reshape(1, d), wr, br)


def _moe_dense_kernel(res_ref, xn_ref, comb_ref, wg_ref, wu_ref, wd_ref, o_ref, acc, *, precise):
    e = pl.program_id(1)

    @pl.when(e == 0)
    def _():
        acc[...] = jnp.zeros(acc.shape, F32)

    x = xn_ref[...]
    hdn = jax.nn.silu(_mm(x, wg_ref[0], precise)) * _mm(x, wu_ref[0], precise)
    comb = comb_ref[...]
    lane = lax.broadcasted_iota(jnp.int32, comb.shape, 1)
    gate = jnp.sum(jnp.where(lane == e, comb, 0.0), axis=-1, keepdims=True)
    acc[...] += gate * _mm(hdn, wd_ref[0], precise)

    @pl.when(e == pl.num_programs(1) - 1)
    def _():
        o_ref[...] = res_ref[...] + acc[...]


def _moe_dense(res, xn, comb, wg, wu, wd, e0, tm, name, precise):
    m, d = res.shape
    return pl.pallas_call(
        functools.partial(_moe_dense_kernel, precise=precise),
        grid=(m // tm, MOE_EXPERTS),
        in_specs=[pl.BlockSpec((tm, d), lambda i, e: (i, 0)), pl.BlockSpec((tm, d), lambda i, e: (i, 0)),
                  pl.BlockSpec((tm, LANES), lambda i, e: (i, 0)),
                  pl.BlockSpec((1, d, MOE_FF), lambda i, e: (e0 + e, 0, 0)),
                  pl.BlockSpec((1, d, MOE_FF), lambda i, e: (e0 + e, 0, 0)),
                  pl.BlockSpec((1, MOE_FF, d), lambda i, e: (e0 + e, 0, 0))],
        out_specs=pl.BlockSpec((tm, d), lambda i, e: (i, 0)),
        out_shape=jax.ShapeDtypeStruct((m, d), F32),
        scratch_shapes=[pltpu.VMEM((tm, d), F32)],
        compiler_params=pltpu.CompilerParams(dimension_semantics=("parallel", "arbitrary")),
        name=name,
    )(res, xn, comb, wg, wu, wd)


def _moe(y, g, prm, tm_r, tm_e, tag, precise):
    xn, comb, _ = _moe_router(y, g, prm["wr"], prm["br"], tm_r, "moe_router_" + tag, F32 if precise else BF16)
    return _moe_dense(y, xn, comb, prm["wg"], prm["wu"], prm["wd"], prm["e0"], tm_e, "moe_experts_" + tag, precise)


def _moe_grouped_kernel(te_ref, nt_ref, x_ref, wg_ref, wu_ref, wd_ref, o_ref, wg16, wu16, wd16):
    i = pl.program_id(0)
    active = i < nt_ref[0]

    @pl.when((i == 0) | (te_ref[i] != te_ref[jnp.maximum(i - 1, 0)]))
    def _():
        wg16[...] = wg_ref[0].astype(BF16)
        wu16[...] = wu_ref[0].astype(BF16)
        wd16[...] = wd_ref[0].astype(BF16)

    @pl.when(active)
    def _():
        x = _unpack_bf16_halves(x_ref[...]).astype(BF16)
        hdn = jax.nn.silu(_mm(x, wg16[...])) * _mm(x, wu16[...])
        o_ref[...] = _mm(hdn, wd16[...]).astype(o_ref.dtype)

    @pl.when(jnp.logical_not(active))
    def _():
        o_ref[...] = jnp.zeros(o_ref.shape, o_ref.dtype)


def _moe_grouped(x_rows, tile_expert, n_tiles, wg, wu, wd, tm, name):
    r = x_rows.shape[0]
    d = wg.shape[1]
    grid_spec = pltpu.PrefetchScalarGridSpec(
        num_scalar_prefetch=2,
        grid=(r // tm,),
        in_specs=[pl.BlockSpec((tm, x_rows.shape[1]), lambda i, te, nt: (i, 0)),
                  pl.BlockSpec((1, d, MOE_FF), lambda i, te, nt: (te[i], 0, 0)),
                  pl.BlockSpec((1, d, MOE_FF), lambda i, te, nt: (te[i], 0, 0)),
                  pl.BlockSpec((1, MOE_FF, d), lambda i, te, nt: (te[i], 0, 0))],
        out_specs=pl.BlockSpec((tm, d), lambda i, te, nt: (i, 0)),
        scratch_shapes=[pltpu.VMEM((d, MOE_FF), BF16), pltpu.VMEM((d, MOE_FF), BF16), pltpu.VMEM((MOE_FF, d), BF16)],
    )
    return pl.pallas_call(
        _moe_grouped_kernel,
        grid_spec=grid_spec,
        out_shape=jax.ShapeDtypeStruct((r, d), F32),
        compiler_params=pltpu.CompilerParams(dimension_semantics=("arbitrary",)),
        name=name,
    )(tile_expert, n_tiles, x_rows, wg, wu, wd)


def _moe_sparse(y, g, prm, tm_r, tm_e, tag):
    t, d = y.shape
    xn, route, cnt = _moe_router(y, g, prm["wr"], prm["br"], tm_r, "moe_router_" + tag, jnp.uint32)
    lanes = lambda k: route[:, _ROUTE_LANE0 + k:_ROUTE_LANE0 + k + 2]
    ids = lanes(0).astype(jnp.int32)
    gates = lanes(2)
    rank = lanes(4).astype(jnp.int32)
    counts = cnt[0, :MOE_EXPERTS].astype(jnp.int32)
    rows = 2 * t + MOE_EXPERTS * tm_e
    pcounts = ((counts + tm_e - 1) // tm_e) * tm_e
    pend = jnp.cumsum(pcounts)
    pstart = pend - pcounts
    pos = pstart[ids] + rank
    tile_start = jnp.arange(rows // tm_e, dtype=jnp.int32) * tm_e
    tile_expert = jnp.minimum(jnp.sum((pend[None, :] <= tile_start[:, None]).astype(jnp.int32), axis=1),
                              MOE_EXPERTS - 1)
    n_tiles = (pend[-1:] // tm_e).astype(jnp.int32)
    token = jnp.broadcast_to(jnp.arange(t, dtype=jnp.int32)[:, None], (t, 2))
    src = jnp.zeros((rows,), jnp.int32).at[pos.reshape(-1)].set(token.reshape(-1), unique_indices=True)
    x_rows = xn[src]
    out_rows = _moe_grouped(x_rows, tile_expert + prm["e0"], n_tiles, prm["wg"], prm["wu"], prm["wd"], tm_e,
                            "moe_experts_" + tag)
    return y + gates[:, 0:1] * out_rows[pos[:, 0]] + gates[:, 1:2] * out_rows[pos[:, 1]]


def _moe_params(layer, w_group, b_group, w_expert, b_expert, w_gate, w_up, w_down):
    d = w_group.shape[0]
    pad = LANES - MOE_EXPERTS - MOE_GROUPS
    wr = jnp.concatenate([w_expert, w_group, jnp.zeros((d, pad), F32)], axis=1)
    br = jnp.concatenate([b_expert, b_group, jnp.zeros((pad,), F32)]).reshape(1, LANES)
    flat = lambda w: w.reshape((-1,) + w.shape[2:])
    return {"wr": wr, "br": br, "wg": flat(w_gate), "wu": flat(w_up), "wd": flat(w_down), "e0": layer * MOE_EXPERTS}


def _ssd_params(w_in, conv_w, conv_b, dt_bias, a_log, d, norm, w_out):
    a = -jnp.exp(a_log)
    w_dt = w_in[:, SSD_INNER + SSD_CONV_DIM:]

    def grouped(x):
        x = x.reshape(x.shape[:-1] + (SSD_GROUPS, SSD_HPG))
        x = jnp.pad(x, [(0, 0)] * (x.ndim - 1) + [(0, LANES - SSD_HPG)])
        return x.reshape(x.shape[:-2] + (SSD_GROUPS * LANES,))

    w_z = w_in[:, :SSD_INNER]
    w_xbc = w_in[:, SSD_INNER:SSD_INNER + SSD_CONV_DIM]
    eye = jnp.eye(SSD_HEADS, dtype=F32)
    return {
        "w_z": w_z.astype(BF16), "w_xbc": w_xbc.astype(BF16), "w_dt_g": grouped(w_dt).astype(BF16),
        "w_z_f32": w_z, "w_xbc_f32": w_xbc, "w_dt_f32": w_dt, "w_out_f32": w_out,
        "expand_64": jnp.repeat(eye, SSD_HEADDIM, axis=1), "expand_128": jnp.repeat(eye, LANES, axis=1),
        "conv_w": conv_w, "conv_b": conv_b.reshape(1, SSD_CONV_DIM),
        "dt_bias_g": grouped(dt_bias).reshape(SSD_GROUPS, 1, LANES),
        "a_g": grouped(a).reshape(SSD_GROUPS, 1, LANES),
        "dt_bias_64": jnp.repeat(dt_bias, SSD_HEADDIM).reshape(1, SSD_INNER),
        "dt_bias_128": jnp.broadcast_to(dt_bias[:, None], (SSD_HEADS, LANES)),
        "a_128": jnp.broadcast_to(a[:, None], (SSD_HEADS, LANES)),
        "d_exp": jnp.repeat(d, SSD_HEADDIM).reshape(1, SSD_INNER),
        "norm": norm, "w_out": w_out.astype(BF16),
    }


_TM = 512


def kernel(x_prompt, x_sample, cache_k, cache_v, page_table, state_s5_re, state_s5_im, state_ssd, state_conv, norm_mix, norm_ffn, norm_final, even_w_in, even_w_out, diff_lam_q1, diff_lam_k1, diff_lam_q2, diff_lam_k2, diff_subln, s5_lam_re, s5_lam_im, s5_log_dt, s5_b_re, s5_b_im, s5_c_re, s5_c_im, s5_d, s5_w_glu, s5_b_glu, ssd_w_in, ssd_conv_w, ssd_conv_b, ssd_dt_bias, ssd_a_log, ssd_d, ssd_norm, ssd_w_out, moe_w_group, moe_b_group, moe_w_expert, moe_b_expert, moe_w_gate, moe_w_up, moe_w_down):
    bp, sp, d = x_prompt.shape
    bs = x_sample.shape[0]
    tp = bp * sp
    depth = norm_mix.shape[0]
    y_p = x_prompt.reshape(tp, d)
    y_s = x_sample.reshape(bs, d)
    outs = {n: [] for n in ("k_p", "v_p", "k_s", "v_s", "re_p", "im_p", "re_s", "im_s", "ssd_p", "ssd_s", "cv_p", "cv_s")}

    for li in range(depth):
        if li % 2 == 0:
            e = li // 2
            lam_init = 0.8 - 0.6 * math.exp(-0.3 * li)
            lam = (jnp.exp(jnp.sum(diff_lam_q1[e] * diff_lam_k1[e])) - jnp.exp(jnp.sum(diff_lam_q2[e] * diff_lam_k2[e]))
                   + lam_init).astype(F32)
            ws32 = [even_w_in[e][:, i * ATT_WIDTH:(i + 1) * ATT_WIDTH] for i in range(4)]
            ws = [w.astype(BF16) for w in ws32]
            w_out32 = [even_w_out[e][:ATT_WIDTH], even_w_out[e][ATT_WIDTH:]]
            w_out_a, w_out_s = [w.astype(BF16) for w in w_out32]
            s5p = _s5_params(s5_lam_re[e], s5_lam_im[e], s5_log_dt[e], s5_b_re[e], s5_b_im[e], s5_c_re[e], s5_c_im[e],
                             s5_d[e], s5_w_glu[e], s5_b_glu[e])
            dts = [(BF16,), (_HEAD_ROWS, BF16), (_HEAD_ROWS, _TILE_T), (F32,)]
            q, k, k16, v, vt, u = _norm_matmul(y_p, norm_mix[li], ws, dts, _TM, "even_in_p")
            o = _attn_prompt(q.reshape(bp, sp, ATT_WIDTH), k16.reshape(bp, sp, ATT_WIDTH), vt, lam, diff_subln[e],
                             1.0 - lam_init)
            s5o, h_re, h_im = _s5_prompt(u.reshape(bp, sp, S5_WIDTH), s5p)
            y_p = _matmul_res(y_p, [o.reshape(tp, ATT_WIDTH), s5o.reshape(tp, S5_WIDTH)], [w_out_a, w_out_s], _TM,
                              "even_out_p")
            outs["k_p"].append(k.reshape(bp, sp, ATT_HEADS, HEAD_W))
            outs["v_p"].append(v.reshape(bp, sp, ATT_HEADS, HEAD_W))
            outs["re_p"].append(h_re.reshape(bp, S5_GROUPS, S5_STATE))
            outs["im_p"].append(h_im.reshape(bp, S5_GROUPS, S5_STATE))
            q, k, v, u = _norm_matmul(y_s, norm_mix[li], ws32, [(F32,), (_HEAD_ROWS,), (_HEAD_ROWS,), (F32,)], bs,
                                      "even_in_s", True)
            o = _attn_decode(q, k, v, cache_k, cache_v, page_table + e * cache_k.shape[1], lam, diff_subln[e],
                             1.0 - lam_init)
            s5o, h_re, h_im = _s5_step(u, state_s5_re[e].reshape(bs, S5_LANES), state_s5_im[e].reshape(bs, S5_LANES), s5p)
            y_s = _matmul_res(y_s, [o, s5o], w_out32, bs, "even_out_s", True)
            outs["k_s"].append(k.reshape(bs, 1, ATT_HEADS, HEAD_W))
            outs["v_s"].append(v.reshape(bs, 1, ATT_HEADS, HEAD_W))
            outs["re_s"].append(h_re.reshape(bs, S5_GROUPS, S5_STATE))
            outs["im_s"].append(h_im.reshape(bs, S5_GROUPS, S5_STATE))
        else:
            o_ = li // 2
            sp_ = _ssd_params(ssd_w_in[o_], ssd_conv_w[o_], ssd_conv_b[o_], ssd_dt_bias[o_], ssd_a_log[o_], ssd_d[o_],
                              ssd_norm[o_], ssd_w_out[o_])
            z, xc, dtg, tail = _ssd_in(y_p, norm_mix[li], sp_, sp, 256, "ssd_in_p")
            yssd, st = _ssd_prompt(xc.reshape(bp, sp, SSD_CONV_DIM), dtg.reshape(bp, sp, SSD_GROUPS * LANES), sp_)
            y_p = _gated_norm_matmul(y_p, yssd.reshape(tp, SSD_INNER), z, sp_["norm"], sp_["w_out"], 256, "ssd_out_p")
            outs["ssd_p"].append(st)
            outs["cv_p"].append(tail[:, _CONV_PAD - (SSD_CONV - 1):, :])
            z, xbc, dtc = _norm_matmul(y_s, norm_mix[li], [sp_["w_z_f32"], sp_["w_xbc_f32"], sp_["w_dt_f32"]],
                                       [(F32,), (F32,), (F32,)], bs, "ssd_in_s", True)
            dt64, dt128 = _dt_expand(dtc, sp_)
            yssd, st = _ssd_step(xbc, state_conv[o_], dt64, dt128, state_ssd[o_], sp_)
            y_s = _gated_norm_matmul(y_s, yssd, z, sp_["norm"], sp_["w_out_f32"], bs, "ssd_out_s", True)
            outs["ssd_s"].append(st)
            outs["cv_s"].append(jnp.concatenate([state_conv[o_][:, 1:], xbc[:, None, :]], axis=1))
        mp = _moe_params(li, moe_w_group[li], moe_b_group[li], moe_w_expert[li], moe_b_expert[li], moe_w_gate,
                         moe_w_up, moe_w_down)
        y_p = _moe_sparse(y_p, norm_ffn[li], mp, _TM, 256, "p")
        y_s = _moe(y_s, norm_ffn[li], mp, bs, bs, "s", li + 1 < depth)

    y_prompt = _rmsnorm(y_p, norm_final, _TM, "final_p").reshape(bp, sp, d)
    y_sample = _rmsnorm(y_s, norm_final, bs, "final_s").reshape(bs, 1, d)
    st = lambda n: jnp.stack(outs[n])
    return (y_prompt, y_sample, st("k_p"), st("v_p"), st("k_s"), st("v_s"), st("re_p"), st("im_p"), st("re_s"),
            st("im_s"), st("ssd_p"), st("ssd_s"), st("cv_p"), st("cv_s"))
```

```python
import functools
import math

import jax
import jax.numpy as jnp
from jax import lax
from jax.experimental import pallas as pl
from jax.experimental.pallas import tpu as pltpu

F32 = jnp.float32
BF16 = jnp.bfloat16
HIGHEST = lax.Precision.HIGHEST

D_MODEL = 1024
NORM_EPS = 1e-6
PAGE_SIZE = 128
ATT_HEADS = 4
ATT_D = 64
ATT_WIDTH = ATT_HEADS * 2 * ATT_D
HEAD_W = 2 * ATT_D
S5_WIDTH = 512
S5_GROUP = 16
S5_GROUPS = 32
S5_STATE = 64
S5_LANES = S5_GROUPS * S5_STATE
SSD_INNER = 2048
SSD_HEADDIM = 64
SSD_HEADS = 32
SSD_GROUPS = 4
SSD_HPG = 8
SSD_STATE = 128
SSD_CONV = 4
SSD_CONV_DIM = SSD_INNER + 2 * SSD_GROUPS * SSD_STATE
SSD_CHUNK = 128
SSD_GW = SSD_HPG * SSD_HEADDIM
MOE_GROUPS = 4
MOE_EPG = 8
MOE_EXPERTS = 32
MOE_FF = 256
LANES = 128
NEG = -1e30

_NT = (((1,), (1,)), ((), ()))
_TN = (((0,), (0,)), ((), ()))


def _rms(x, g):
    return x * lax.rsqrt(jnp.mean(x * x, axis=-1, keepdims=True) + NORM_EPS) * g


def _split(x):
    hi = x.astype(BF16)
    return hi, (x - hi.astype(F32)).astype(BF16)


def _mm(a, b, precise=False, dims=None):
    dot = jnp.dot if dims is None else functools.partial(lax.dot_general, dimension_numbers=dims)
    if not precise:
        return dot(a.astype(BF16), b.astype(BF16), preferred_element_type=F32)
    m = a.shape[0]
    a_hi, a_lo = _split(a)
    b_hi, b_lo = _split(b)
    r = dot(jnp.concatenate([a_hi, a_lo], axis=0), b_hi, preferred_element_type=F32)
    return r[:m] + r[m:] + dot(a_hi, b_lo, preferred_element_type=F32)


def _norm_matmul_kernel(x_ref, g_ref, *refs, out_dtypes, precise):
    n = len(out_dtypes)
    w_refs = refs[:n]
    o_refs = refs[n:]
    h = _rms(x_ref[...], g_ref[...])
    if not precise:
        h = h.astype(BF16)
    k = 0
    for w_ref, dts in zip(w_refs, out_dtypes):
        r = _mm(h, w_ref[...], precise)
        for dt in dts:
            if dt == _TILE_T:
                o_refs[k][0] = r.T.astype(BF16)
            elif dt == _HEAD_ROWS:
                nh = r.shape[1] // LANES
                for hd in range(nh):
                    o_refs[k][pl.ds(hd, r.shape[0], stride=nh), :] = r[:, hd * LANES:(hd + 1) * LANES]
            else:
                o_refs[k][...] = r.astype(dt)
            k += 1


_TILE_T = "bf16 row tiles, each transposed"
_HEAD_ROWS = "f32, one 128-lane row per head"


def _norm_matmul(x, g, ws, out_dtypes, tm, name, precise=False):
    m, kdim = x.shape
    out_shape, out_specs = [], []
    for w, dts in zip(ws, out_dtypes):
        for dt in dts:
            if dt == _TILE_T:
                out_shape.append(jax.ShapeDtypeStruct((m // tm, w.shape[1], tm), BF16))
                out_specs.append(pl.BlockSpec((1, w.shape[1], tm), lambda i: (i, 0, 0)))
            elif dt == _HEAD_ROWS:
                nh = w.shape[1] // LANES
                out_shape.append(jax.ShapeDtypeStruct((m * nh, LANES), F32))
                out_specs.append(pl.BlockSpec((tm * nh, LANES), lambda i: (i, 0)))
            else:
                out_shape.append(jax.ShapeDtypeStruct((m, w.shape[1]), dt))
                out_specs.append(pl.BlockSpec((tm, w.shape[1]), lambda i: (i, 0)))
    return pl.pallas_call(
        functools.partial(_norm_matmul_kernel, out_dtypes=out_dtypes, precise=precise),
        grid=(m // tm,),
        in_specs=[pl.BlockSpec((tm, kdim), lambda i: (i, 0)), pl.BlockSpec((1, kdim), lambda i: (0, 0))]
        + [pl.BlockSpec(w.shape, lambda i: (0, 0)) for w in ws],
        out_specs=out_specs,
        out_shape=out_shape,
        compiler_params=pltpu.CompilerParams(dimension_semantics=("parallel",)),
        name=name,
    )(x, g.reshape(1, kdim), *ws)


def _matmul_res_kernel(res_ref, *refs, precise):
    n = (len(refs) - 1) // 2
    acc = res_ref[...]
    for a_ref, w_ref in zip(refs[:n], refs[n:2 * n]):
        acc = acc + _mm(a_ref[...], w_ref[...], precise)
    refs[-1][...] = acc


def _matmul_res(res, a_list, w_list, tm, name, precise=False):
    m, n = res.shape
    return pl.pallas_call(
        functools.partial(_matmul_res_kernel, precise=precise),
        grid=(m // tm,),
        in_specs=[pl.BlockSpec((tm, n), lambda i: (i, 0))]
        + [pl.BlockSpec((tm, a.shape[1]), lambda i: (i, 0)) for a in a_list]
        + [pl.BlockSpec(w.shape, lambda i: (0, 0)) for w in w_list],
        out_specs=pl.BlockSpec((tm, n), lambda i: (i, 0)),
        out_shape=jax.ShapeDtypeStruct((m, n), F32),
        compiler_params=pltpu.CompilerParams(dimension_semantics=("parallel",)),
        name=name,
    )(res, *a_list, *w_list)


def _gated_norm_matmul_kernel(res_ref, y_ref, z_ref, g_ref, w_ref, o_ref, *, precise):
    y = y_ref[...] * jax.nn.silu(z_ref[...])
    o_ref[...] = res_ref[...] + _mm(_rms(y, g_ref[...]), w_ref[...], precise)


def _gated_norm_matmul(res, y, z, g, w, tm, name, precise=False):
    m, n = res.shape
    kdim = y.shape[1]
    return pl.pallas_call(
        functools.partial(_gated_norm_matmul_kernel, precise=precise),
        grid=(m // tm,),
        in_specs=[pl.BlockSpec((tm, n), lambda i: (i, 0)), pl.BlockSpec((tm, kdim), lambda i: (i, 0)),
                  pl.BlockSpec((tm, kdim), lambda i: (i, 0)), pl.BlockSpec((1, kdim), lambda i: (0, 0)),
                  pl.BlockSpec(w.shape, lambda i: (0, 0))],
        out_specs=pl.BlockSpec((tm, n), lambda i: (i, 0)),
        out_shape=jax.ShapeDtypeStruct((m, n), F32),
        compiler_params=pltpu.CompilerParams(dimension_semantics=("parallel",)),
        name=name,
    )(res, y, z, g.reshape(1, kdim), w)


def _rmsnorm_kernel(x_ref, g_ref, o_ref):
    o_ref[...] = _rms(x_ref[...], g_ref[...])


def _rmsnorm(x, g, tm, name):
    m, n = x.shape
    return pl.pallas_call(
        _rmsnorm_kernel,
        grid=(m // tm,),
        in_specs=[pl.BlockSpec((tm, n), lambda i: (i, 0)), pl.BlockSpec((1, n), lambda i: (0, 0))],
        out_specs=pl.BlockSpec((tm, n), lambda i: (i, 0)),
        out_shape=jax.ShapeDtypeStruct((m, n), F32),
        compiler_params=pltpu.CompilerParams(dimension_semantics=("parallel",)),
        name=name,
    )(x, g.reshape(1, n))


def _split_q(q):
    lane = lax.broadcasted_iota(jnp.int32, q.shape, 1) % HEAD_W
    scale = ATT_D ** -0.5
    qs = q * jnp.asarray(scale, q.dtype)
    zero = jnp.zeros_like(qs)
    return jnp.where(lane < ATT_D, qs, zero), jnp.where(lane >= ATT_D, qs, zero)


def _subln(o, g, out_scale):
    return _rms(o, g) * out_scale


def _attn_prompt_kernel(lam_ref, q_ref, k_ref, vt_ref, g_ref, o_ref, qt_sc, m_sc, l_sc, acc_sc, *, tq, tk, out_scale):
    qi = pl.program_id(1)
    heads = range(ATT_HEADS)
    for h in heads:
        q1, q2 = _split_q(q_ref[0, :, h * HEAD_W:(h + 1) * HEAD_W])
        qt_sc[h] = jnp.concatenate([q1, q2], axis=0).astype(F32).T.astype(BF16)
    m_sc[...] = jnp.full(m_sc.shape, NEG, F32)
    l_sc[...] = jnp.zeros(l_sc.shape, F32)
    acc_sc[...] = jnp.zeros(acc_sc.shape, F32)

    def kv_step(j, masked):
        start = pl.multiple_of(j * tk, tk)
        for h in heads:
            kt = k_ref[0, pl.ds(start, tk), h * HEAD_W:(h + 1) * HEAD_W]
            st = jnp.dot(kt, qt_sc[h], preferred_element_type=F32)
            if masked:
                k_pos = start + lax.broadcasted_iota(jnp.int32, st.shape, 0)
                q_pos = qi * tq + lax.broadcasted_iota(jnp.int32, st.shape, 1) % tq
                st = jnp.where(k_pos <= q_pos, st, NEG)
            m_old = m_sc[h]
            m_new = jnp.maximum(m_old, jnp.max(st, axis=0, keepdims=True))
            p = jnp.exp(st - m_new)
            alpha = jnp.exp(m_old - m_new)
            l_sc[h] = alpha * l_sc[h] + jnp.sum(p, axis=0, keepdims=True)
            m_sc[h] = m_new
            vt = vt_ref[j, h * HEAD_W:(h + 1) * HEAD_W, :]
            acc_sc[h] = alpha * acc_sc[h] + jnp.dot(vt, p.astype(BF16), preferred_element_type=F32)

    n_full = (qi * tq) // tk

    def full_step(j, carry):
        kv_step(j, False)
        return carry

    lax.fori_loop(0, n_full, full_step, 0)
    kv_step(n_full, True)
    for h in heads:
        acc, l = acc_sc[h], l_sc[h]
        ot = acc[:, 0:tq] / l[:, 0:tq] - lam_ref[0, 0] * (acc[:, tq:2 * tq] / l[:, tq:2 * tq])
        o_ref[0, :, h * HEAD_W:(h + 1) * HEAD_W] = _subln(ot.T, g_ref[...], out_scale).astype(o_ref.dtype)


def _attn_prompt(q, k, vt, lam, subln, out_scale, tq=512):
    b, s, _ = q.shape
    tk = vt.shape[2]
    nblk = s // tk
    kern = functools.partial(_attn_prompt_kernel, tq=tq, tk=tk, out_scale=out_scale)
    return pl.pallas_call(
        kern,
        grid=(b, s // tq),
        in_specs=[pl.BlockSpec(memory_space=pltpu.SMEM),
                  pl.BlockSpec((1, tq, ATT_WIDTH), lambda bi, i: (bi, i, 0)),
                  pl.BlockSpec((1, s, ATT_WIDTH), lambda bi, i: (bi, 0, 0)),
                  pl.BlockSpec((nblk, ATT_WIDTH, tk), lambda bi, i: (bi, 0, 0)),
                  pl.BlockSpec((1, HEAD_W), lambda bi, i: (0, 0))],
        out_specs=pl.BlockSpec((1, tq, ATT_WIDTH), lambda bi, i: (bi, i, 0)),
        out_shape=jax.ShapeDtypeStruct((b, s, ATT_WIDTH), BF16),
        scratch_shapes=[pltpu.VMEM((ATT_HEADS, HEAD_W, 2 * tq), BF16), pltpu.VMEM((ATT_HEADS, 1, 2 * tq), F32),
                        pltpu.VMEM((ATT_HEADS, 1, 2 * tq), F32), pltpu.VMEM((ATT_HEADS, HEAD_W, 2 * tq), F32)],
        compiler_params=pltpu.CompilerParams(dimension_semantics=("parallel", "parallel")),
        name="attn_prompt",
    )(lam.reshape(1, 1), q, k, vt, subln.reshape(1, HEAD_W))


_DEC_PAGES = 32
_DEC_PROW = PAGE_SIZE * ATT_HEADS
_DEC_COLS = _DEC_PAGES * _DEC_PROW
_DEC_ROWS = 2 * ATT_HEADS


def _attn_decode_kernel(pt_ref, lam_ref, q_ref, kn_ref, vn_ref, g_ref, *refs, out_scale):
    del pt_ref
    k_refs = refs[:_DEC_PAGES]
    v_refs = refs[_DEC_PAGES:2 * _DEC_PAGES]
    o_ref = refs[2 * _DEC_PAGES]
    m_sc, l_sc, acc_sc = refs[2 * _DEC_PAGES + 1:]
    j = pl.program_id(1)

    q4 = q_ref[0] * (ATT_D ** -0.5)
    q8 = jnp.concatenate([q4, q4], axis=0)
    r8 = lax.broadcasted_iota(jnp.int32, (_DEC_ROWS, HEAD_W), 0)
    l8 = lax.broadcasted_iota(jnp.int32, (_DEC_ROWS, HEAD_W), 1)
    q8 = jnp.where(l8 // ATT_D == r8 // ATT_HEADS, q8, 0.0)

    @pl.when(j == 0)
    def _():
        m_sc[...] = jnp.full(m_sc.shape, NEG, F32)
        l_sc[...] = jnp.zeros(l_sc.shape, F32)
        acc_sc[...] = jnp.zeros(acc_sc.shape, F32)

    s = jnp.concatenate([_mm(q8, k_ref[0], True, _NT) for k_ref in k_refs], axis=1)
    rs = lax.broadcasted_iota(jnp.int32, s.shape, 0)
    cs = lax.broadcasted_iota(jnp.int32, s.shape, 1)
    s = jnp.where(cs % ATT_HEADS == rs % ATT_HEADS, s, NEG)
    m = m_sc[...]
    mn = jnp.maximum(m, jnp.max(s, axis=-1, keepdims=True))
    p = jnp.exp(s - mn)
    alpha = jnp.exp(m - mn)
    l = alpha * l_sc[...] + jnp.sum(p, axis=-1, keepdims=True)
    acc = alpha * acc_sc[...]
    for i, v_ref in enumerate(v_refs):
        acc = acc + _mm(p[:, i * _DEC_PROW:(i + 1) * _DEC_PROW], v_ref[0], True)
    m_sc[...] = mn
    l_sc[...] = l
    acc_sc[...] = acc

    @pl.when(j == pl.num_programs(1) - 1)
    def _():
        kn = jnp.concatenate([kn_ref[0], kn_ref[0]], axis=0)
        vn = jnp.concatenate([vn_ref[0], vn_ref[0]], axis=0)
        sn = jnp.sum(q8 * kn, axis=-1, keepdims=True)
        mf = jnp.maximum(mn, sn)
        pn = jnp.exp(sn - mf)
        af = jnp.exp(mn - mf)
        o8 = (af * acc + pn * vn) / (af * l + pn)
        o4 = o8[:ATT_HEADS] - lam_ref[0, 0] * o8[ATT_HEADS:]
        o_ref[0] = _subln(o4, g_ref[...], out_scale)


def _attn_decode(q, k_new, v_new, cache_k, cache_v, page_table, lam, subln, out_scale):
    b = q.shape[0]
    n_pages = page_table.shape[1]
    steps = n_pages // _DEC_PAGES
    ck = cache_k.reshape(-1, _DEC_PROW, HEAD_W)
    cv = cache_v.reshape(-1, _DEC_PROW, HEAD_W)

    def page_map(i):
        return lambda bi, j, pt: (pt[bi, j * _DEC_PAGES + i], 0, 0)

    row = lambda bi, j, pt: (bi, 0, 0)
    head_rows = pl.BlockSpec((1, ATT_HEADS, HEAD_W), row)
    page_specs = [pl.BlockSpec((1, _DEC_PROW, HEAD_W), page_map(i)) for i in range(_DEC_PAGES)]
    grid_spec = pltpu.PrefetchScalarGridSpec(
        num_scalar_prefetch=1,
        grid=(b, steps),
        in_specs=[pl.BlockSpec(memory_space=pltpu.SMEM), head_rows, head_rows, head_rows,
                  pl.BlockSpec((1, HEAD_W), lambda bi, j, pt: (0, 0))] + page_specs + page_specs,
        out_specs=head_rows,
        scratch_shapes=[pltpu.VMEM((_DEC_ROWS, 1), F32), pltpu.VMEM((_DEC_ROWS, 1), F32),
                        pltpu.VMEM((_DEC_ROWS, HEAD_W), F32)],
    )
    out = pl.pallas_call(
        functools.partial(_attn_decode_kernel, out_scale=out_scale),
        grid_spec=grid_spec,
        out_shape=jax.ShapeDtypeStruct((b, ATT_HEADS, HEAD_W), F32),
        compiler_params=pltpu.CompilerParams(dimension_semantics=("parallel", "arbitrary")),
        name="attn_decode",
    )(page_table, lam.reshape(1, 1), q.reshape(b, ATT_HEADS, HEAD_W), k_new.reshape(b, ATT_HEADS, HEAD_W),
      v_new.reshape(b, ATT_HEADS, HEAD_W), subln.reshape(1, HEAD_W), *([ck] * _DEC_PAGES), *([cv] * _DEC_PAGES))
    return out.reshape(b, ATT_WIDTH)


_S5_KT = 2
_S5_KW = S5_WIDTH // _S5_KT
_S5_KL = S5_LANES // _S5_KT
_S5_NT = S5_LANES // LANES


def _s5_input(u, bmat_ref, precise):
    re, im = [], []
    for kt in range(_S5_KT):
        r = _mm(u[:, kt * _S5_KW:(kt + 1) * _S5_KW], bmat_ref[kt], precise)
        re.append(r[:, :_S5_KL])
        im.append(r[:, _S5_KL:])
    return jnp.concatenate(re, axis=1), jnp.concatenate(im, axis=1)


def _s5_output(h_re, h_im, u, cmat_ref, d_ref, wglu_ref, bglu_ref, precise):
    if not precise:
        h_re, h_im = h_re.astype(BF16), h_im.astype(BF16)
    ys = []
    for kt in range(_S5_KT):
        sl = slice(kt * _S5_KL, (kt + 1) * _S5_KL)
        y = _mm(h_re[:, sl], cmat_ref[kt, :_S5_KL, :], precise)
        y = y + _mm(h_im[:, sl], cmat_ref[kt, _S5_KL:, :], precise)
        ys.append(y)
    y = jnp.concatenate(ys, axis=1) + d_ref[...] * u
    g = jax.nn.gelu(y)
    gate = _mm(g, wglu_ref[...], precise) + bglu_ref[...]
    return g * jax.nn.sigmoid(gate)


def _s5_scan_kernel(u_ref, are_ref, aim_ref, bmat_ref, cmat_ref, d_ref, wglu_ref, bglu_ref,
                    o_ref, hre_ref, him_ref, bu_re, bu_im, *, nb, lc, fold):
    c = pl.program_id(0)
    slots = _S5_NT // fold
    bp = fold * nb

    @pl.when(c == 0)
    def _():
        hre_ref[...] = jnp.zeros(hre_ref.shape, F32)
        him_ref[...] = jnp.zeros(him_ref.shape, F32)

    def rows_of(jt, b):
        return jt % slots, pl.ds((jt // slots) * nb + b, lc, stride=bp)

    for b in range(nb):
        re, im = _s5_input(u_ref[b], bmat_ref, False)
        for jt in range(_S5_NT):
            slot, rows = rows_of(jt, b)
            bu_re[slot, rows, :] = re[:, jt * LANES:(jt + 1) * LANES]
            bu_im[slot, rows, :] = im[:, jt * LANES:(jt + 1) * LANES]

    a_re = are_ref[...]
    a_im = aim_ref[...]

    def step(t, carry):
        hr, hi = carry
        rows = pl.ds(pl.multiple_of(t * bp, bp), bp)
        nr = hr * a_re - hi * a_im + bu_re[:, rows, :]
        ni = hr * a_im + hi * a_re + bu_im[:, rows, :]
        bu_re[:, rows, :] = nr
        bu_im[:, rows, :] = ni
        return nr, ni

    hr, hi = lax.fori_loop(0, lc, step, (hre_ref[...], him_ref[...]), unroll=2)
    hre_ref[...] = hr
    him_ref[...] = hi

    for b in range(nb):
        h_re = jnp.concatenate([bu_re[rows_of(jt, b)] for jt in range(_S5_NT)], axis=1)
        h_im = jnp.concatenate([bu_im[rows_of(jt, b)] for jt in range(_S5_NT)], axis=1)
        o_ref[b] = _s5_output(h_re, h_im, u_ref[b], cmat_ref, d_ref, wglu_ref, bglu_ref, False).astype(o_ref.dtype)


def _s5_prompt(u, prm, lc=128):
    nb, l, _ = u.shape
    bp = 8
    assert bp % nb == 0 and _S5_NT % (bp // nb) == 0
    fold = bp // nb
    slots = _S5_NT // fold
    full = lambda shape: pl.BlockSpec(shape, lambda c: (0,) * len(shape))

    def a_rows(a):
        a = jnp.transpose(a.reshape(fold, slots, LANES), (1, 0, 2))
        return jnp.repeat(a, nb, axis=1)

    o, h_re, h_im = pl.pallas_call(
        functools.partial(_s5_scan_kernel, nb=nb, lc=lc, fold=fold),
        grid=(l // lc,),
        in_specs=[pl.BlockSpec((nb, lc, S5_WIDTH), lambda c: (0, c, 0)),
                  full((slots, bp, LANES)), full((slots, bp, LANES)),
                  full((_S5_KT, _S5_KW, 2 * _S5_KL)), full((_S5_KT, 2 * _S5_KL, _S5_KW)),
                  full((1, S5_WIDTH)), full((S5_WIDTH, S5_WIDTH)), full((1, S5_WIDTH))],
        out_specs=[pl.BlockSpec((nb, lc, S5_WIDTH), lambda c: (0, c, 0)),
                   full((slots, bp, LANES)), full((slots, bp, LANES))],
        out_shape=[jax.ShapeDtypeStruct((nb, l, S5_WIDTH), BF16),
                   jax.ShapeDtypeStruct((slots, bp, LANES), F32), jax.ShapeDtypeStruct((slots, bp, LANES), F32)],
        scratch_shapes=[pltpu.VMEM((slots, lc * bp, LANES), F32) for _ in range(2)],
        compiler_params=pltpu.CompilerParams(dimension_semantics=("arbitrary",)),
        name="s5_prompt",
    )(u, a_rows(prm["a_re"]), a_rows(prm["a_im"]), prm["bmat"].astype(BF16),
      prm["cmat"].astype(BF16), prm["d"], prm["w_glu"].astype(BF16), prm["b_glu"])
    rows = lambda h: jnp.transpose(h.reshape(slots, fold, nb, LANES), (2, 1, 0, 3)).reshape(nb, S5_LANES)
    return o, rows(h_re), rows(h_im)


def _s5_step_kernel(u_ref, h0re_ref, h0im_ref, are_ref, aim_ref, bmat_ref, cmat_ref, d_ref, wglu_ref, bglu_ref,
                    o_ref, hre_ref, him_ref):
    u = u_ref[...]
    bu_re, bu_im = _s5_input(u, bmat_ref, True)
    h_re, h_im = h0re_ref[...], h0im_ref[...]
    a_re, a_im = are_ref[...], aim_ref[...]
    n_re = h_re * a_re - h_im * a_im + bu_re
    n_im = h_re * a_im + h_im * a_re + bu_im
    hre_ref[...] = n_re
    him_ref[...] = n_im
    o_ref[...] = _s5_output(n_re, n_im, u, cmat_ref, d_ref, wglu_ref, bglu_ref, True)


def _s5_step(u, h0_re, h0_im, prm):
    nb = u.shape[0]
    return pl.pallas_call(
        _s5_step_kernel,
        out_shape=[jax.ShapeDtypeStruct((nb, S5_WIDTH), F32),
                   jax.ShapeDtypeStruct((nb, S5_LANES), F32), jax.ShapeDtypeStruct((nb, S5_LANES), F32)],
        name="s5_step",
    )(u, h0_re, h0_im, prm["a_re"], prm["a_im"], prm["bmat"], prm["cmat"], prm["d"], prm["w_glu"], prm["b_glu"])


def _s5_params(lam_re, lam_im, log_dt, b_re, b_im, c_re, c_im, d, w_glu, b_glu):
    lam = lax.complex(lam_re, lam_im)
    dt = jnp.exp(log_dt)[:, None]
    a_bar = jnp.exp(lam * dt)
    b_bar = ((a_bar - 1.0) / lam)[..., None] * lax.complex(b_re, b_im)
    gk = S5_GROUPS // _S5_KT
    eye = jnp.eye(gk, dtype=F32)

    def in_tile(x):
        x = x.reshape(_S5_KT, gk, S5_STATE, S5_GROUP)
        return jnp.einsum("kgpc,gh->kgchp", x, eye).reshape(_S5_KT, _S5_KW, _S5_KL)

    def out_tile(x):
        x = x.reshape(_S5_KT, gk, S5_GROUP, S5_STATE)
        return jnp.einsum("kgcp,gh->kgphc", x, eye).reshape(_S5_KT, _S5_KL, _S5_KW)

    bmat = jnp.concatenate([in_tile(jnp.real(b_bar)), in_tile(jnp.imag(b_bar))], axis=2)
    cmat = jnp.concatenate([out_tile(c_re), out_tile(-c_im)], axis=1)
    return {
        "a_re": jnp.real(a_bar).reshape(1, S5_LANES), "a_im": jnp.imag(a_bar).reshape(1, S5_LANES),
        "bmat": bmat, "cmat": cmat, "d": d.reshape(1, S5_WIDTH), "w_glu": w_glu, "b_glu": b_glu.reshape(1, S5_WIDTH),
    }


_CONV_PAD = 8


def _conv_silu(x, prev_ref, w_ref, b_ref):
    rows = x.shape[0]
    w = w_ref[...]
    prev = prev_ref[...]
    sub = lax.broadcasted_iota(jnp.int32, prev.shape, 0)
    out = b_ref[...]
    for i in range(SSD_CONV):
        s = SSD_CONV - 1 - i
        if s == 0:
            shifted = x
        else:
            rolled = pltpu.roll(x, s, 0)
            head = jnp.where(sub < s, pltpu.roll(prev, s, 0), rolled[0:_CONV_PAD])
            shifted = jnp.concatenate([head, rolled[_CONV_PAD:]], axis=0)
        out = out + w[i:i + 1] * shifted
    prev_ref[...] = x[rows - _CONV_PAD:rows, :]
    return jax.nn.silu(out)


def _ssd_in_kernel(x_ref, g_ref, wz_ref, wx_ref, wdt_ref, cw_ref, cb_ref, z_ref, xc_ref, dt_ref, tail_ref, xf,
                   *, tiles_per_seq):
    i = pl.program_id(0)
    tm = x_ref.shape[0]

    @pl.when(i % tiles_per_seq == 0)
    def _():
        xf[...] = jnp.zeros(xf.shape, F32)

    h = _rms(x_ref[...], g_ref[...]).astype(BF16)
    z_ref[...] = _mm(h, wz_ref[...])
    dt_ref[...] = _mm(h, wdt_ref[...])
    xbc = _mm(h, wx_ref[...])
    tail_ref[0] = xbc[tm - _CONV_PAD:tm, :]
    xc_ref[...] = _conv_silu(xbc, xf, cw_ref, cb_ref)


def _ssd_in(x, g, prm, seq_len, tm, name):
    m, kdim = x.shape
    tiles_per_seq = seq_len // tm
    full = lambda a: pl.BlockSpec(a.shape, lambda i: (0,) * a.ndim, pipeline_mode=pl.Buffered(1))
    rows = lambda n: pl.BlockSpec((tm, n), lambda i: (i, 0))
    wz, wx, wdt = prm["w_z"], prm["w_xbc"], prm["w_dt_g"]
    return pl.pallas_call(
        functools.partial(_ssd_in_kernel, tiles_per_seq=tiles_per_seq),
        grid=(m // tm,),
        in_specs=[rows(kdim), pl.BlockSpec((1, kdim), lambda i: (0, 0)), full(wz), full(wx), full(wdt),
                  full(prm["conv_w"]), full(prm["conv_b"])],
        out_specs=[rows(wz.shape[1]), rows(wx.shape[1]), rows(wdt.shape[1]),
                   pl.BlockSpec((1, _CONV_PAD, wx.shape[1]), lambda i: (i // tiles_per_seq, 0, 0))],
        out_shape=[jax.ShapeDtypeStruct((m, wz.shape[1]), F32), jax.ShapeDtypeStruct((m, wx.shape[1]), F32),
                   jax.ShapeDtypeStruct((m, wdt.shape[1]), F32),
                   jax.ShapeDtypeStruct((m // seq_len, _CONV_PAD, wx.shape[1]), F32)],
        scratch_shapes=[pltpu.VMEM((_CONV_PAD, wx.shape[1]), F32)],
        compiler_params=pltpu.CompilerParams(dimension_semantics=("arbitrary",)),
        name=name,
    )(x, g.reshape(1, kdim), wz, wx, wdt, prm["conv_w"], prm["conv_b"])


def _ssd_chunk_kernel(xs_ref, bm_ref, cm_ref, dt_ref, dtb_ref, a_ref, dexp_ref, y_ref, st_ref):
    c = pl.program_id(2)
    q = SSD_CHUNK

    @pl.when(c == 0)
    def _():
        st_ref[...] = jnp.zeros(st_ref.shape, F32)

    xs = xs_ref[0]
    bm = bm_ref[0]
    bm16 = bm.astype(BF16)
    cm16 = cm_ref[0].astype(BF16)

    dt = jax.nn.softplus(dt_ref[0] + dtb_ref[0])
    da = dt * a_ref[0]
    row = lax.broadcasted_iota(jnp.int32, (q, q), 0)
    col = lax.broadcasted_iota(jnp.int32, (q, q), 1)
    causal = row >= col
    acs = jnp.dot(causal.astype(F32), da, preferred_element_type=F32, precision=HIGHEST)
    acs_t = acs.T
    acs_last = acs[q - 1:q, :]
    e_acs = jnp.exp(acs)
    w_s = dt * jnp.exp(acs_last - acs)
    dt_t = dt.T
    chunk_dec = jnp.exp(acs_last)
    cb = _mm(cm16, bm16, dims=_NT)

    first = lax.broadcasted_iota(jnp.int32, (q, LANES), 1) < SSD_HEADDIM
    first_rows = lax.broadcasted_iota(jnp.int32, (LANES, SSD_STATE), 0) < SSD_HEADDIM

    ys = []
    for i in range(SSD_HPG // 2):
        xs16 = xs[:, i * LANES:(i + 1) * LANES].astype(BF16)
        heads = (2 * i, 2 * i + 1)
        y_head, st_head = [], []
        for j in heads:
            seg = acs[:, j:j + 1] - acs_t[j:j + 1, :]
            lmat = jnp.exp(jnp.where(causal, seg, -jnp.inf))
            y_head.append(_mm(cb * lmat * dt_t[j:j + 1, :], xs16))
            st_head.append(_mm(xs16, bm * w_s[:, j:j + 1], dims=_TN))
        y_diag = jnp.where(first, y_head[0], y_head[1])
        h = st_ref[0, 2 * i:2 * i + 2].reshape(LANES, SSD_STATE)
        y_off = _mm(cm16, h, dims=_NT) * jnp.where(first, e_acs[:, heads[0]:heads[0] + 1],
                                                    e_acs[:, heads[1]:heads[1] + 1])
        cdec = jnp.concatenate([jnp.broadcast_to(chunk_dec[:, j:j + 1], (SSD_HEADDIM, SSD_STATE)) for j in heads],
                               axis=0)
        h_new = cdec * h + jnp.where(first_rows, st_head[0], st_head[1])
        st_ref[0, 2 * i:2 * i + 2] = h_new.reshape(2, SSD_HEADDIM, SSD_STATE)
        ys.append(y_diag + y_off)
    y_ref[0] = jnp.concatenate(ys, axis=1) + dexp_ref[...] * xs


def _ssd_prompt(xc, dtg, prm):
    nb, l, _ = xc.shape
    nc = l // SSD_CHUNK
    q = SSD_CHUNK
    boff = SSD_INNER // LANES
    coff = boff + SSD_GROUPS
    grp = lambda b, g, c: (g, 0, 0)
    return pl.pallas_call(
        _ssd_chunk_kernel,
        grid=(nb, SSD_GROUPS, nc),
        in_specs=[pl.BlockSpec((1, q, SSD_GW), lambda b, g, c: (b, c, g)),
                  pl.BlockSpec((1, q, LANES), lambda b, g, c: (b, c, boff + g)),
                  pl.BlockSpec((1, q, LANES), lambda b, g, c: (b, c, coff + g)),
                  pl.BlockSpec((1, q, LANES), lambda b, g, c: (b, c, g)),
                  pl.BlockSpec((1, 1, LANES), grp), pl.BlockSpec((1, 1, LANES), grp),
                  pl.BlockSpec((1, SSD_GW), lambda b, g, c: (0, g))],
        out_specs=[pl.BlockSpec((1, q, SSD_GW), lambda b, g, c: (b, c, g)),
                   pl.BlockSpec((1, SSD_HPG, SSD_HEADDIM, SSD_STATE), lambda b, g, c: (b, g, 0, 0))],
        out_shape=[jax.ShapeDtypeStruct((nb, l, SSD_INNER), F32),
                   jax.ShapeDtypeStruct((nb, SSD_HEADS, SSD_HEADDIM, SSD_STATE), F32)],
        compiler_params=pltpu.CompilerParams(dimension_semantics=("parallel", "parallel", "arbitrary")),
        name="ssd_prompt",
    )(xc, xc, xc, dtg, prm["dt_bias_g"], prm["a_g"], prm["d_exp"])


def _ssd_step_kernel(x_ref, buf_ref, cw_ref, cb_ref, dt64_ref, dtb64_ref, dt128_ref, dtb128_ref, a128_ref,
                     dexp_ref, st_ref, y_ref, so_ref):
    w = cw_ref[...]
    buf = buf_ref[0]
    conv = cb_ref[...]
    for i in range(SSD_CONV - 1):
        conv = conv + w[i:i + 1] * buf[i:i + 1]
    conv = conv + w[SSD_CONV - 1:SSD_CONV] * x_ref[0]
    xc = jax.nn.silu(conv)
    xs = xc[:, :SSD_INNER]
    xdt = xs * jax.nn.softplus(dt64_ref[0] + dtb64_ref[...])
    dt = jax.nn.softplus(dt128_ref[0] + dtb128_ref[...])
    dec = jnp.exp(dt * a128_ref[...])
    row = lax.broadcasted_iota(jnp.int32, (LANES, LANES), 0)
    col = lax.broadcasted_iota(jnp.int32, (LANES, LANES), 1)
    diag = row == col
    ys = []
    for i in range(SSD_HEADS // 2):
        g = (2 * i) // SSD_HPG
        bm = xc[:, SSD_INNER + g * SSD_STATE:SSD_INNER + (g + 1) * SSD_STATE]
        cm = xc[:, SSD_INNER + (SSD_GROUPS + g) * SSD_STATE:SSD_INNER + (SSD_GROUPS + g + 1) * SSD_STATE]
        xp = jnp.broadcast_to(xdt[:, i * LANES:(i + 1) * LANES], (LANES, LANES))
        outer = _mm(jnp.where(diag, xp, 0.0), jnp.broadcast_to(bm, (LANES, LANES)), True)
        h = st_ref[0, 2 * i:2 * i + 2].reshape(LANES, SSD_STATE)
        dpair = jnp.concatenate([jnp.broadcast_to(dec[2 * i:2 * i + 1], (SSD_HEADDIM, LANES)),
                                 jnp.broadcast_to(dec[2 * i + 1:2 * i + 2], (SSD_HEADDIM, LANES))], axis=0)
        hn = dpair * h + outer
        so_ref[0, 2 * i:2 * i + 2] = hn.reshape(2, SSD_HEADDIM, SSD_STATE)
        ys.append(_mm(jnp.broadcast_to(cm, (8, SSD_STATE)), hn, True, _NT)[0:1])
    y_ref[0] = jnp.concatenate(ys, axis=1) + dexp_ref[...] * xs


def _dt_expand_kernel(dt_ref, e64_ref, e128_ref, o64_ref, o128_ref):
    dt = dt_ref[...]
    o64_ref[...] = _mm(dt, e64_ref[...], True)
    o128_ref[...] = _mm(dt, e128_ref[...], True)


def _dt_expand(dt, prm):
    nb = dt.shape[0]
    return pl.pallas_call(
        _dt_expand_kernel,
        out_shape=[jax.ShapeDtypeStruct((nb, SSD_INNER), F32), jax.ShapeDtypeStruct((nb, SSD_HEADS * LANES), F32)],
        name="ssd_dt_expand",
    )(dt, prm["expand_64"], prm["expand_128"])


def _ssd_step(xbc, conv_buf, dt64, dt128, state, prm):
    nb = xbc.shape[0]
    full = lambda shape: pl.BlockSpec(shape, lambda b: (0,) * len(shape))
    y, so = pl.pallas_call(
        _ssd_step_kernel,
        grid=(nb,),
        in_specs=[pl.BlockSpec((1, 1, SSD_CONV_DIM), lambda b: (b, 0, 0)),
                  pl.BlockSpec((1, SSD_CONV - 1, SSD_CONV_DIM), lambda b: (b, 0, 0)),
                  full((SSD_CONV, SSD_CONV_DIM)), full((1, SSD_CONV_DIM)),
                  pl.BlockSpec((1, 1, SSD_INNER), lambda b: (b, 0, 0)), full((1, SSD_INNER)),
                  pl.BlockSpec((1, SSD_HEADS, LANES), lambda b: (b, 0, 0)), full((SSD_HEADS, LANES)),
                  full((SSD_HEADS, LANES)), full((1, SSD_INNER)),
                  pl.BlockSpec((1, SSD_HEADS, SSD_HEADDIM, SSD_STATE), lambda b: (b, 0, 0, 0))],
        out_specs=[pl.BlockSpec((1, 1, SSD_INNER), lambda b: (b, 0, 0)),
                   pl.BlockSpec((1, SSD_HEADS, SSD_HEADDIM, SSD_STATE), lambda b: (b, 0, 0, 0))],
        out_shape=[jax.ShapeDtypeStruct((nb, 1, SSD_INNER), F32),
                   jax.ShapeDtypeStruct((nb, SSD_HEADS, SSD_HEADDIM, SSD_STATE), F32)],
        compiler_params=pltpu.CompilerParams(dimension_semantics=("parallel",)),
        name="ssd_step",
    )(xbc.reshape(nb, 1, SSD_CONV_DIM), conv_buf, prm["conv_w"], prm["conv_b"],
      dt64.reshape(nb, 1, SSD_INNER), prm["dt_bias_64"], dt128.reshape(nb, SSD_HEADS, LANES),
      prm["dt_bias_128"], prm["a_128"], prm["d_exp"], state)
    return y.reshape(nb, SSD_INNER), so


_GROUP_LANE0 = MOE_EXPERTS
_ROUTE_LANE0 = 64


def _pack_bf16_halves(x):
    n = x.shape[1] // 2
    lo = lax.bitcast_convert_type(x[:, :n].astype(BF16).astype(F32), jnp.uint32)
    hi = lax.bitcast_convert_type(x[:, n:].astype(BF16).astype(F32), jnp.uint32)
    return (lo >> 16) | hi


def _unpack_bf16_halves(p):
    lo = lax.bitcast_convert_type(p << 16, F32)
    hi = lax.bitcast_convert_type(p & jnp.uint32(0xFFFF0000), F32)
    return jnp.concatenate([lo, hi], axis=1)


def _moe_router_kernel(y_ref, g_ref, wr_ref, br_ref, xn_ref, comb_ref, cnt_ref):
    @pl.when(pl.program_id(0) == 0)
    def _():
        cnt_ref[...] = jnp.zeros(cnt_ref.shape, F32)

    xn = _rms(y_ref[...], g_ref[...])
    xn_ref[...] = _pack_bf16_halves(xn) if xn_ref.dtype == jnp.uint32 else xn.astype(xn_ref.dtype)
    lg = _mm(xn, wr_ref[...], True) + br_ref[...]
    lane = lax.broadcasted_iota(jnp.int32, lg.shape, 1)
    is_group = (lane >= _GROUP_LANE0) & (lane < _GROUP_LANE0 + MOE_GROUPS)
    gl = jnp.where(is_group, lg, NEG)
    gmax = jnp.max(gl, axis=-1, keepdims=True)
    g_p = 1.0 / jnp.sum(jnp.exp(gl - gmax), axis=-1, keepdims=True)
    gidx = jnp.min(jnp.where(gl == gmax, lane - _GROUP_LANE0, MOE_GROUPS), axis=-1, keepdims=True)
    el = jnp.where((lane < MOE_EXPERTS) & (lane // MOE_EPG == gidx), lg, NEG)
    m1 = jnp.max(el, axis=-1, keepdims=True)
    i1 = jnp.min(jnp.where(el == m1, lane, LANES), axis=-1, keepdims=True)
    el2 = jnp.where(lane == i1, NEG, el)
    m2 = jnp.max(el2, axis=-1, keepdims=True)
    i2 = jnp.min(jnp.where(el2 == m2, lane, LANES), axis=-1, keepdims=True)
    e2 = jnp.exp(m2 - m1)
    den = 1.0 + e2
    g1 = g_p / den
    g2 = g_p * (e2 / den)
    oh1 = lane == i1
    oh2 = lane == i2
    comb = jnp.where(oh1, g1, 0.0) + jnp.where(oh2, g2, 0.0)
    picks = oh1.astype(F32) + oh2.astype(F32)
    tm = picks.shape[0]
    before = lax.broadcasted_iota(jnp.int32, (tm, tm), 0) > lax.broadcasted_iota(jnp.int32, (tm, tm), 1)
    seen = cnt_ref[...] + _mm(before.astype(F32), picks)
    r1 = jnp.sum(jnp.where(oh1, seen, 0.0), axis=-1, keepdims=True)
    r2 = jnp.sum(jnp.where(oh2, seen, 0.0), axis=-1, keepdims=True)
    cnt_ref[...] += jnp.sum(picks, axis=0, keepdims=True)
    for k, v in enumerate((i1.astype(F32), i2.astype(F32), g1, g2, r1, r2)):
        comb = jnp.where(lane == _ROUTE_LANE0 + k, v, comb)
    comb_ref[...] = comb


def _moe_router(y, g, wr, br, tm, name, xn_dtype):
    m, d = y.shape
    dx = d // 2 if xn_dtype == jnp.uint32 else d
    return pl.pallas_call(
        _moe_router_kernel,
        grid=(m // tm,),
        in_specs=[pl.BlockSpec((tm, d), lambda i: (i, 0)), pl.BlockSpec((1, d), lambda i: (0, 0)),
                  pl.BlockSpec((d, LANES), lambda i: (0, 0)), pl.BlockSpec((1, LANES), lambda i: (0, 0))],
        out_specs=[pl.BlockSpec((tm, dx), lambda i: (i, 0)), pl.BlockSpec((tm, LANES), lambda i: (i, 0)),
                   pl.BlockSpec((1, LANES), lambda i: (0, 0))],
        out_shape=[jax.ShapeDtypeStruct((m, dx), xn_dtype), jax.ShapeDtypeStruct((m, LANES), F32),
                   jax.ShapeDtypeStruct((1, LANES), F32)],
        compiler_params=pltpu.CompilerParams(dimension_semantics=("arbitrary",)),
        name=name,
    )(y, g.reshape(1, d), wr, br)


def _moe_dense_kernel(res_ref, xn_ref, comb_ref, wg_ref, wu_ref, wd_ref, o_ref, acc, *, precise):
    e = pl.program_id(1)

    @pl.when(e == 0)
    def _():
        acc[...] = jnp.zeros(acc.shape, F32)

    x = xn_ref[...]
    hdn = jax.nn.silu(_mm(x, wg_ref[0], precise)) * _mm(x, wu_ref[0], precise)
    comb = comb_ref[...]
    lane = lax.broadcasted_iota(jnp.int32, comb.shape, 1)
    gate = jnp.sum(jnp.where(lane == e, comb, 0.0), axis=-1, keepdims=True)
    acc[...] += gate * _mm(hdn, wd_ref[0], precise)

    @pl.when(e == pl.num_programs(1) - 1)
    def _():
        o_ref[...] = res_ref[...] + acc[...]


def _moe_dense(res, xn, comb, wg, wu, wd, e0, tm, name, precise):
    m, d = res.shape
    return pl.pallas_call(
        functools.partial(_moe_dense_kernel, precise=precise),
        grid=(m // tm, MOE_EXPERTS),
        in_specs=[pl.BlockSpec((tm, d), lambda i, e: (i, 0)), pl.BlockSpec((tm, d), lambda i, e: (i, 0)),
                  pl.BlockSpec((tm, LANES), lambda i, e: (i, 0)),
                  pl.BlockSpec((1, d, MOE_FF), lambda i, e: (e0 + e, 0, 0)),
                  pl.BlockSpec((1, d, MOE_FF), lambda i, e: (e0 + e, 0, 0)),
                  pl.BlockSpec((1, MOE_FF, d), lambda i, e: (e0 + e, 0, 0))],
        out_specs=pl.BlockSpec((tm, d), lambda i, e: (i, 0)),
        out_shape=jax.ShapeDtypeStruct((m, d), F32),
        scratch_shapes=[pltpu.VMEM((tm, d), F32)],
        compiler_params=pltpu.CompilerParams(dimension_semantics=("parallel", "arbitrary")),
        name=name,
    )(res, xn, comb, wg, wu, wd)


def _moe(y, g, prm, tm_r, tm_e, tag, precise):
    xn, comb, _ = _moe_router(y, g, prm["wr"], prm["br"], tm_r, "moe_router_" + tag, F32 if precise else BF16)
    return _moe_dense(y, xn, comb, prm["wg"], prm["wu"], prm["wd"], prm["e0"], tm_e, "moe_experts_" + tag, precise)


def _moe_grouped_kernel(te_ref, nt_ref, x_ref, wg_ref, wu_ref, wd_ref, o_ref):
    del te_ref
    active = pl.program_id(0) < nt_ref[0]

    @pl.when(active)
    def _():
        x = _unpack_bf16_halves(x_ref[...]).astype(BF16)
        hdn = jax.nn.silu(_mm(x, wg_ref[0])) * _mm(x, wu_ref[0])
        o_ref[...] = _mm(hdn, wd_ref[0]).astype(o_ref.dtype)

    @pl.when(jnp.logical_not(active))
    def _():
        o_ref[...] = jnp.zeros(o_ref.shape, o_ref.dtype)


def _moe_grouped(x_rows, tile_expert, n_tiles, wg, wu, wd, tm, name):
    r = x_rows.shape[0]
    d = wg.shape[1]
    grid_spec = pltpu.PrefetchScalarGridSpec(
        num_scalar_prefetch=2,
        grid=(r // tm,),
        in_specs=[pl.BlockSpec((tm, x_rows.shape[1]), lambda i, te, nt: (i, 0)),
                  pl.BlockSpec((1, d, MOE_FF), lambda i, te, nt: (te[i], 0, 0)),
                  pl.BlockSpec((1, d, MOE_FF), lambda i, te, nt: (te[i], 0, 0)),
                  pl.BlockSpec((1, MOE_FF, d), lambda i, te, nt: (te[i], 0, 0))],
        out_specs=pl.BlockSpec((tm, d), lambda i, te, nt: (i, 0)),
    )
    return pl.pallas_call(
        _moe_grouped_kernel,
        grid_spec=grid_spec,
        out_shape=jax.ShapeDtypeStruct((r, d), F32),
        compiler_params=pltpu.CompilerParams(dimension_semantics=("arbitrary",)),
        name=name,
    )(tile_expert, n_tiles, x_rows, wg, wu, wd)


def _moe_sparse(y, g, prm, tm_r, tm_e, tag):
    t, d = y.shape
    xn, route, cnt = _moe_router(y, g, prm["wr"], prm["br"], tm_r, "moe_router_" + tag, jnp.uint32)
    lanes = lambda k: route[:, _ROUTE_LANE0 + k:_ROUTE_LANE0 + k + 2]
    ids = lanes(0).astype(jnp.int32)
    gates = lanes(2)
    rank = lanes(4).astype(jnp.int32)
    counts = cnt[0, :MOE_EXPERTS].astype(jnp.int32)
    rows = 2 * t + MOE_EXPERTS * tm_e
    pcounts = ((counts + tm_e - 1) // tm_e) * tm_e
    pend = jnp.cumsum(pcounts)
    pstart = pend - pcounts
    pos = pstart[ids] + rank
    tile_start = jnp.arange(rows // tm_e, dtype=jnp.int32) * tm_e
    tile_expert = jnp.minimum(jnp.sum((pend[None, :] <= tile_start[:, None]).astype(jnp.int32), axis=1),
                              MOE_EXPERTS - 1)
    n_tiles = (pend[-1:] // tm_e).astype(jnp.int32)
    token = jnp.broadcast_to(jnp.arange(t, dtype=jnp.int32)[:, None], (t, 2))
    src = jnp.zeros((rows,), jnp.int32).at[pos.reshape(-1)].set(token.reshape(-1), unique_indices=True)
    x_rows = xn[src]
    out_rows = _moe_grouped(x_rows, tile_expert + prm["e0"], n_tiles, prm["wg"], prm["wu"], prm["wd"], tm_e,
                            "moe_experts_" + tag)
    return y + gates[:, 0:1] * out_rows[pos[:, 0]] + gates[:, 1:2] * out_rows[pos[:, 1]]


def _moe_params(layer, w_group, b_group, w_expert, b_expert, w_gate, w_up, w_down):
    d = w_group.shape[0]
    pad = LANES - MOE_EXPERTS - MOE_GROUPS
    wr = jnp.concatenate([w_expert, w_group, jnp.zeros((d, pad), F32)], axis=1)
    br = jnp.concatenate([b_expert, b_group, jnp.zeros((pad,), F32)]).reshape(1, LANES)
    flat = lambda w: w.reshape((-1,) + w.shape[2:])
    return {"wr": wr, "br": br, "wg": flat(w_gate), "wu": flat(w_up), "wd": flat(w_down), "e0": layer * MOE_EXPERTS}


def _ssd_params(w_in, conv_w, conv_b, dt_bias, a_log, d, norm, w_out):
    a = -jnp.exp(a_log)
    w_dt = w_in[:, SSD_INNER + SSD_CONV_DIM:]

    def grouped(x):
        x = x.reshape(x.shape[:-1] + (SSD_GROUPS, SSD_HPG))
        x = jnp.pad(x, [(0, 0)] * (x.ndim - 1) + [(0, LANES - SSD_HPG)])
        return x.reshape(x.shape[:-2] + (SSD_GROUPS * LANES,))

    w_z = w_in[:, :SSD_INNER]
    w_xbc = w_in[:, SSD_INNER:SSD_INNER + SSD_CONV_DIM]
    eye = jnp.eye(SSD_HEADS, dtype=F32)
    return {
        "w_z": w_z.astype(BF16), "w_xbc": w_xbc.astype(BF16), "w_dt_g": grouped(w_dt).astype(BF16),
        "w_z_f32": w_z, "w_xbc_f32": w_xbc, "w_dt_f32": w_dt, "w_out_f32": w_out,
        "expand_64": jnp.repeat(eye, SSD_HEADDIM, axis=1), "expand_128": jnp.repeat(eye, LANES, axis=1),
        "conv_w": conv_w, "conv_b": conv_b.reshape(1, SSD_CONV_DIM),
        "dt_bias_g": grouped(dt_bias).reshape(SSD_GROUPS, 1, LANES),
        "a_g": grouped(a).reshape(SSD_GROUPS, 1, LANES),
        "dt_bias_64": jnp.repeat(dt_bias, SSD_HEADDIM).reshape(1, SSD_INNER),
        "dt_bias_128": jnp.broadcast_to(dt_bias[:, None], (SSD_HEADS, LANES)),
        "a_128": jnp.broadcast_to(a[:, None], (SSD_HEADS, LANES)),
        "d_exp": jnp.repeat(d, SSD_HEADDIM).reshape(1, SSD_INNER),
        "norm": norm, "w_out": w_out.astype(BF16),
    }


_TM = 512


def kernel(x_prompt, x_sample, cache_k, cache_v, page_table, state_s5_re, state_s5_im, state_ssd, state_conv, norm_mix, norm_ffn, norm_final, even_w_in, even_w_out, diff_lam_q1, diff_lam_k1, diff_lam_q2, diff_lam_k2, diff_subln, s5_lam_re, s5_lam_im, s5_log_dt, s5_b_re, s5_b_im, s5_c_re, s5_c_im, s5_d, s5_w_glu, s5_b_glu, ssd_w_in, ssd_conv_w, ssd_conv_b, ssd_dt_bias, ssd_a_log, ssd_d, ssd_norm, ssd_w_out, moe_w_group, moe_b_group, moe_w_expert, moe_b_expert, moe_w_gate, moe_w_up, moe_w_down):
    bp, sp, d = x_prompt.shape
    bs = x_sample.shape[0]
    tp = bp * sp
    depth = norm_mix.shape[0]
    y_p = x_prompt.reshape(tp, d)
    y_s = x_sample.reshape(bs, d)
    outs = {n: [] for n in ("k_p", "v_p", "k_s", "v_s", "re_p", "im_p", "re_s", "im_s", "ssd_p", "ssd_s", "cv_p", "cv_s")}

    for li in range(depth):
        if li % 2 == 0:
            e = li // 2
            lam_init = 0.8 - 0.6 * math.exp(-0.3 * li)
            lam = (jnp.exp(jnp.sum(diff_lam_q1[e] * diff_lam_k1[e])) - jnp.exp(jnp.sum(diff_lam_q2[e] * diff_lam_k2[e]))
                   + lam_init).astype(F32)
            ws32 = [even_w_in[e][:, i * ATT_WIDTH:(i + 1) * ATT_WIDTH] for i in range(4)]
            ws = [w.astype(BF16) for w in ws32]
            w_out32 = [even_w_out[e][:ATT_WIDTH], even_w_out[e][ATT_WIDTH:]]
            w_out_a, w_out_s = [w.astype(BF16) for w in w_out32]
            s5p = _s5_params(s5_lam_re[e], s5_lam_im[e], s5_log_dt[e], s5_b_re[e], s5_b_im[e], s5_c_re[e], s5_c_im[e],
                             s5_d[e], s5_w_glu[e], s5_b_glu[e])
            dts = [(BF16,), (_HEAD_ROWS, BF16), (_HEAD_ROWS, _TILE_T), (F32,)]
            q, k, k16, v, vt, u = _norm_matmul(y_p, norm_mix[li], ws, dts, _TM, "even_in_p")
            o = _attn_prompt(q.reshape(bp, sp, ATT_WIDTH), k16.reshape(bp, sp, ATT_WIDTH), vt, lam, diff_subln[e],
                             1.0 - lam_init)
            s5o, h_re, h_im = _s5_prompt(u.reshape(bp, sp, S5_WIDTH), s5p)
            y_p = _matmul_res(y_p, [o.reshape(tp, ATT_WIDTH), s5o.reshape(tp, S5_WIDTH)], [w_out_a, w_out_s], _TM,
                              "even_out_p")
            outs["k_p"].append(k.reshape(bp, sp, ATT_HEADS, HEAD_W))
            outs["v_p"].append(v.reshape(bp, sp, ATT_HEADS, HEAD_W))
            outs["re_p"].append(h_re.reshape(bp, S5_GROUPS, S5_STATE))
            outs["im_p"].append(h_im.reshape(bp, S5_GROUPS, S5_STATE))
            q, k, v, u = _norm_matmul(y_s, norm_mix[li], ws32, [(F32,), (_HEAD_ROWS,), (_HEAD_ROWS,), (F32,)], bs,
                                      "even_in_s", True)
            o = _attn_decode(q, k, v, cache_k, cache_v, page_table + e * cache_k.shape[1], lam, diff_subln[e],
                             1.0 - lam_init)
            s5o, h_re, h_im = _s5_step(u, state_s5_re[e].reshape(bs, S5_LANES), state_s5_im[e].reshape(bs, S5_LANES), s5p)
            y_s = _matmul_res(y_s, [o, s5o], w_out32, bs, "even_out_s", True)
            outs["k_s"].append(k.reshape(bs, 1, ATT_HEADS, HEAD_W))
            outs["v_s"].append(v.reshape(bs, 1, ATT_HEADS, HEAD_W))
            outs["re_s"].append(h_re.reshape(bs, S5_GROUPS, S5_STATE))
            outs["im_s"].append(h_im.reshape(bs, S5_GROUPS, S5_STATE))
        else:
            o_ = li // 2
            sp_ = _ssd_params(ssd_w_in[o_], ssd_conv_w[o_], ssd_conv_b[o_], ssd_dt_bias[o_], ssd_a_log[o_], ssd_d[o_],
                              ssd_norm[o_], ssd_w_out[o_])
            z, xc, dtg, tail = _ssd_in(y_p, norm_mix[li], sp_, sp, 512, "ssd_in_p")
            yssd, st = _ssd_prompt(xc.reshape(bp, sp, SSD_CONV_DIM), dtg.reshape(bp, sp, SSD_GROUPS * LANES), sp_)
            y_p = _gated_norm_matmul(y_p, yssd.reshape(tp, SSD_INNER), z, sp_["norm"], sp_["w_out"], 256, "ssd_out_p")
            outs["ssd_p"].append(st)
            outs["cv_p"].append(tail[:, _CONV_PAD - (SSD_CONV - 1):, :])
            z, xbc, dtc = _norm_matmul(y_s, norm_mix[li], [sp_["w_z_f32"], sp_["w_xbc_f32"], sp_["w_dt_f32"]],
                                       [(F32,), (F32,), (F32,)], bs, "ssd_in_s", True)
            dt64, dt128 = _dt_expand(dtc, sp_)
            yssd, st = _ssd_step(xbc, state_conv[o_], dt64, dt128, state_ssd[o_], sp_)
            y_s = _gated_norm_matmul(y_s, yssd, z, sp_["norm"], sp_["w_out_f32"], bs, "ssd_out_s", True)
            outs["ssd_s"].append(st)
            outs["cv_s"].append(jnp.concatenate([state_conv[o_][:, 1:], xbc[:, None, :]], axis=1))
        mp = _moe_params(li, moe_w_group[li], moe_b_group[li], moe_w_expert[li], moe_b_expert[li], moe_w_gate,
                         moe_w_up, moe_w_down)
        y_p = _moe_sparse(y_p, norm_ffn[li], mp, _TM, 256, "p")
        y_s = _moe(y_s, norm_ffn[li], mp, bs, bs, "s", li + 1 < depth)

    y_prompt = _rmsnorm(y_p, norm_final, _TM, "final_p").reshape(bp, sp, d)
    y_sample = _rmsnorm(y_s, norm_final, bs, "final_s").reshape(bs, 1, d)
    st = lambda n: jnp.stack(outs[n])
    return (y_prompt, y_sample, st("k_p"), st("v_p"), st("k_s"), st("v_s"), st("re_p"), st("im_p"), st("re_s"),
            st("im_s"), st("ssd_p"), st("ssd_s"), st("cv_p"), st("cv_s"))
```

```python
import functools
import math

import jax
import jax.numpy as jnp
from jax import lax
from jax.experimental import pallas as pl
from jax.experimental.pallas import tpu as pltpu

F32 = jnp.float32
BF16 = jnp.bfloat16
HIGHEST = lax.Precision.HIGHEST

D_MODEL = 1024
NORM_EPS = 1e-6
PAGE_SIZE = 128
ATT_HEADS = 4
ATT_D = 64
ATT_WIDTH = ATT_HEADS * 2 * ATT_D
HEAD_W = 2 * ATT_D
S5_WIDTH = 512
S5_GROUP = 16
S5_GROUPS = 32
S5_STATE = 64
S5_LANES = S5_GROUPS * S5_STATE
SSD_INNER = 2048
SSD_HEADDIM = 64
SSD_HEADS = 32
SSD_GROUPS = 4
SSD_HPG = 8
SSD_STATE = 128
SSD_CONV = 4
SSD_CONV_DIM = SSD_INNER + 2 * SSD_GROUPS * SSD_STATE
SSD_CHUNK = 128
SSD_GW = SSD_HPG * SSD_HEADDIM
MOE_GROUPS = 4
MOE_EPG = 8
MOE_EXPERTS = 32
MOE_FF = 256
LANES = 128
NEG = -1e30

_NT = (((1,), (1,)), ((), ()))
_TN = (((0,), (0,)), ((), ()))


def _rms(x, g):
    return x * lax.rsqrt(jnp.mean(x * x, axis=-1, keepdims=True) + NORM_EPS) * g


def _split(x):
    hi = x.astype(BF16)
    return hi, (x - hi.astype(F32)).astype(BF16)


def _mm(a, b, precise=False, dims=None):
    dot = jnp.dot if dims is None else functools.partial(lax.dot_general, dimension_numbers=dims)
    if not precise:
        return dot(a.astype(BF16), b.astype(BF16), preferred_element_type=F32)
    m = a.shape[0]
    a_hi, a_lo = _split(a)
    b_hi, b_lo = _split(b)
    r = dot(jnp.concatenate([a_hi, a_lo], axis=0), b_hi, preferred_element_type=F32)
    return r[:m] + r[m:] + dot(a_hi, b_lo, preferred_element_type=F32)


def _norm_matmul_kernel(x_ref, g_ref, *refs, out_dtypes, precise):
    n = len(out_dtypes)
    w_refs = refs[:n]
    o_refs = refs[n:]
    h = _rms(x_ref[...], g_ref[...])
    if not precise:
        h = h.astype(BF16)
    k = 0
    for w_ref, dts in zip(w_refs, out_dtypes):
        r = _mm(h, w_ref[...], precise)
        for dt in dts:
            if dt == _TILE_T:
                o_refs[k][0] = r.T.astype(BF16)
            elif dt == _HEAD_ROWS:
                nh = r.shape[1] // LANES
                for hd in range(nh):
                    o_refs[k][pl.ds(hd, r.shape[0], stride=nh), :] = r[:, hd * LANES:(hd + 1) * LANES]
            else:
                o_refs[k][...] = r.astype(dt)
            k += 1


_TILE_T = "bf16 row tiles, each transposed"
_HEAD_ROWS = "f32, one 128-lane row per head"


def _norm_matmul(x, g, ws, out_dtypes, tm, name, precise=False):
    m, kdim = x.shape
    out_shape, out_specs = [], []
    for w, dts in zip(ws, out_dtypes):
        for dt in dts:
            if dt == _TILE_T:
                out_shape.append(jax.ShapeDtypeStruct((m // tm, w.shape[1], tm), BF16))
                out_specs.append(pl.BlockSpec((1, w.shape[1], tm), lambda i: (i, 0, 0)))
            elif dt == _HEAD_ROWS:
                nh = w.shape[1] // LANES
                out_shape.append(jax.ShapeDtypeStruct((m * nh, LANES), F32))
                out_specs.append(pl.BlockSpec((tm * nh, LANES), lambda i: (i, 0)))
            else:
                out_shape.append(jax.ShapeDtypeStruct((m, w.shape[1]), dt))
                out_specs.append(pl.BlockSpec((tm, w.shape[1]), lambda i: (i, 0)))
    return pl.pallas_call(
        functools.partial(_norm_matmul_kernel, out_dtypes=out_dtypes, precise=precise),
        grid=(m // tm,),
        in_specs=[pl.BlockSpec((tm, kdim), lambda i: (i, 0)), pl.BlockSpec((1, kdim), lambda i: (0, 0))]
        + [pl.BlockSpec(w.shape, lambda i: (0, 0)) for w in ws],
        out_specs=out_specs,
        out_shape=out_shape,
        compiler_params=pltpu.CompilerParams(dimension_semantics=("parallel",)),
        name=name,
    )(x, g.reshape(1, kdim), *ws)


def _matmul_res_kernel(res_ref, *refs, precise):
    n = (len(refs) - 1) // 2
    acc = res_ref[...]
    for a_ref, w_ref in zip(refs[:n], refs[n:2 * n]):
        acc = acc + _mm(a_ref[...], w_ref[...], precise)
    refs[-1][...] = acc


def _matmul_res(res, a_list, w_list, tm, name, precise=False):
    m, n = res.shape
    return pl.pallas_call(
        functools.partial(_matmul_res_kernel, precise=precise),
        grid=(m // tm,),
        in_specs=[pl.BlockSpec((tm, n), lambda i: (i, 0))]
        + [pl.BlockSpec((tm, a.shape[1]), lambda i: (i, 0)) for a in a_list]
        + [pl.BlockSpec(w.shape, lambda i: (0, 0)) for w in w_list],
        out_specs=pl.BlockSpec((tm, n), lambda i: (i, 0)),
        out_shape=jax.ShapeDtypeStruct((m, n), F32),
        compiler_params=pltpu.CompilerParams(dimension_semantics=("parallel",)),
        name=name,
    )(res, *a_list, *w_list)


def _gated_norm_matmul_kernel(res_ref, y_ref, z_ref, g_ref, w_ref, o_ref, *, precise):
    y = y_ref[...] * jax.nn.silu(z_ref[...])
    o_ref[...] = res_ref[...] + _mm(_rms(y, g_ref[...]), w_ref[...], precise)


def _gated_norm_matmul(res, y, z, g, w, tm, name, precise=False):
    m, n = res.shape
    kdim = y.shape[1]
    return pl.pallas_call(
        functools.partial(_gated_norm_matmul_kernel, precise=precise),
        grid=(m // tm,),
        in_specs=[pl.BlockSpec((tm, n), lambda i: (i, 0)), pl.BlockSpec((tm, kdim), lambda i: (i, 0)),
                  pl.BlockSpec((tm, kdim), lambda i: (i, 0)), pl.BlockSpec((1, kdim), lambda i: (0, 0)),
                  pl.BlockSpec(w.shape, lambda i: (0, 0))],
        out_specs=pl.BlockSpec((tm, n), lambda i: (i, 0)),
        out_shape=jax.ShapeDtypeStruct((m, n), F32),
        compiler_params=pltpu.CompilerParams(dimension_semantics=("parallel",)),
        name=name,
    )(res, y, z, g.reshape(1, kdim), w)


def _rmsnorm_kernel(x_ref, g_ref, o_ref):
    o_ref[...] = _rms(x_ref[...], g_ref[...])


def _rmsnorm(x, g, tm, name):
    m, n = x.shape
    return pl.pallas_call(
        _rmsnorm_kernel,
        grid=(m // tm,),
        in_specs=[pl.BlockSpec((tm, n), lambda i: (i, 0)), pl.BlockSpec((1, n), lambda i: (0, 0))],
        out_specs=pl.BlockSpec((tm, n), lambda i: (i, 0)),
        out_shape=jax.ShapeDtypeStruct((m, n), F32),
        compiler_params=pltpu.CompilerParams(dimension_semantics=("parallel",)),
        name=name,
    )(x, g.reshape(1, n))


def _split_q(q):
    lane = lax.broadcasted_iota(jnp.int32, q.shape, 1) % HEAD_W
    scale = ATT_D ** -0.5
    qs = q * jnp.asarray(scale, q.dtype)
    zero = jnp.zeros_like(qs)
    return jnp.where(lane < ATT_D, qs, zero), jnp.where(lane >= ATT_D, qs, zero)


def _subln(o, g, out_scale):
    return _rms(o, g) * out_scale


def _attn_prompt_kernel(lam_ref, q_ref, k_ref, vt_ref, g_ref, o_ref, qt_sc, m_sc, l_sc, acc_sc, *, tq, tk, out_scale):
    qi = pl.program_id(1)
    heads = range(ATT_HEADS)
    for h in heads:
        q1, q2 = _split_q(q_ref[0, :, h * HEAD_W:(h + 1) * HEAD_W])
        qt_sc[h] = jnp.concatenate([q1, q2], axis=0).astype(F32).T.astype(BF16)
    m_sc[...] = jnp.full(m_sc.shape, NEG, F32)
    l_sc[...] = jnp.zeros(l_sc.shape, F32)
    acc_sc[...] = jnp.zeros(acc_sc.shape, F32)

    def kv_step(j, masked):
        start = pl.multiple_of(j * tk, tk)
        for h in heads:
            kt = k_ref[0, pl.ds(start, tk), h * HEAD_W:(h + 1) * HEAD_W]
            st = jnp.dot(kt, qt_sc[h], preferred_element_type=F32)
            if masked:
                k_pos = start + lax.broadcasted_iota(jnp.int32, st.shape, 0)
                q_pos = qi * tq + lax.broadcasted_iota(jnp.int32, st.shape, 1) % tq
                st = jnp.where(k_pos <= q_pos, st, NEG)
            m_old = m_sc[h]
            m_new = jnp.maximum(m_old, jnp.max(st, axis=0, keepdims=True))
            p = jnp.exp(st - m_new)
            alpha = jnp.exp(m_old - m_new)
            l_sc[h] = alpha * l_sc[h] + jnp.sum(p, axis=0, keepdims=True)
            m_sc[h] = m_new
            vt = vt_ref[j, h * HEAD_W:(h + 1) * HEAD_W, :]
            acc_sc[h] = alpha * acc_sc[h] + jnp.dot(vt, p.astype(BF16), preferred_element_type=F32)

    n_full = (qi * tq) // tk

    def full_step(j, carry):
        kv_step(j, False)
        return carry

    lax.fori_loop(0, n_full, full_step, 0)
    kv_step(n_full, True)
    for h in heads:
        acc, l = acc_sc[h], l_sc[h]
        ot = acc[:, 0:tq] / l[:, 0:tq] - lam_ref[0, 0] * (acc[:, tq:2 * tq] / l[:, tq:2 * tq])
        o_ref[0, :, h * HEAD_W:(h + 1) * HEAD_W] = _subln(ot.T, g_ref[...], out_scale).astype(o_ref.dtype)


def _attn_prompt(q, k, vt, lam, subln, out_scale, tq=512):
    b, s, _ = q.shape
    tk = vt.shape[2]
    nblk = s // tk
    kern = functools.partial(_attn_prompt_kernel, tq=tq, tk=tk, out_scale=out_scale)
    return pl.pallas_call(
        kern,
        grid=(b, s // tq),
        in_specs=[pl.BlockSpec(memory_space=pltpu.SMEM),
                  pl.BlockSpec((1, tq, ATT_WIDTH), lambda bi, i: (bi, i, 0)),
                  pl.BlockSpec((1, s, ATT_WIDTH), lambda bi, i: (bi, 0, 0)),
                  pl.BlockSpec((nblk, ATT_WIDTH, tk), lambda bi, i: (bi, 0, 0)),
                  pl.BlockSpec((1, HEAD_W), lambda bi, i: (0, 0))],
        out_specs=pl.BlockSpec((1, tq, ATT_WIDTH), lambda bi, i: (bi, i, 0)),
        out_shape=jax.ShapeDtypeStruct((b, s, ATT_WIDTH), BF16),
        scratch_shapes=[pltpu.VMEM((ATT_HEADS, HEAD_W, 2 * tq), BF16), pltpu.VMEM((ATT_HEADS, 1, 2 * tq), F32),
                        pltpu.VMEM((ATT_HEADS, 1, 2 * tq), F32), pltpu.VMEM((ATT_HEADS, HEAD_W, 2 * tq), F32)],
        compiler_params=pltpu.CompilerParams(dimension_semantics=("parallel", "parallel")),
        name="attn_prompt",
    )(lam.reshape(1, 1), q, k, vt, subln.reshape(1, HEAD_W))


_DEC_PAGES = 32
_DEC_PROW = PAGE_SIZE * ATT_HEADS
_DEC_COLS = _DEC_PAGES * _DEC_PROW
_DEC_ROWS = 2 * ATT_HEADS


def _attn_decode_kernel(pt_ref, lam_ref, q_ref, kn_ref, vn_ref, g_ref, *refs, out_scale):
    del pt_ref
    k_refs = refs[:_DEC_PAGES]
    v_refs = refs[_DEC_PAGES:2 * _DEC_PAGES]
    o_ref = refs[2 * _DEC_PAGES]
    m_sc, l_sc, acc_sc = refs[2 * _DEC_PAGES + 1:]
    j = pl.program_id(1)

    q4 = q_ref[0] * (ATT_D ** -0.5)
    q8 = jnp.concatenate([q4, q4], axis=0)
    r8 = lax.broadcasted_iota(jnp.int32, (_DEC_ROWS, HEAD_W), 0)
    l8 = lax.broadcasted_iota(jnp.int32, (_DEC_ROWS, HEAD_W), 1)
    q8 = jnp.where(l8 // ATT_D == r8 // ATT_HEADS, q8, 0.0)

    @pl.when(j == 0)
    def _():
        m_sc[...] = jnp.full(m_sc.shape, NEG, F32)
        l_sc[...] = jnp.zeros(l_sc.shape, F32)
        acc_sc[...] = jnp.zeros(acc_sc.shape, F32)

    s = jnp.concatenate([_mm(q8, k_ref[0], True, _NT) for k_ref in k_refs], axis=1)
    rs = lax.broadcasted_iota(jnp.int32, s.shape, 0)
    cs = lax.broadcasted_iota(jnp.int32, s.shape, 1)
    s = jnp.where(cs % ATT_HEADS == rs % ATT_HEADS, s, NEG)
    m = m_sc[...]
    mn = jnp.maximum(m, jnp.max(s, axis=-1, keepdims=True))
    p = jnp.exp(s - mn)
    alpha = jnp.exp(m - mn)
    l = alpha * l_sc[...] + jnp.sum(p, axis=-1, keepdims=True)
    acc = alpha * acc_sc[...]
    for i, v_ref in enumerate(v_refs):
        acc = acc + _mm(p[:, i * _DEC_PROW:(i + 1) * _DEC_PROW], v_ref[0], True)
    m_sc[...] = mn
    l_sc[...] = l
    acc_sc[...] = acc

    @pl.when(j == pl.num_programs(1) - 1)
    def _():
        kn = jnp.concatenate([kn_ref[0], kn_ref[0]], axis=0)
        vn = jnp.concatenate([vn_ref[0], vn_ref[0]], axis=0)
        sn = jnp.sum(q8 * kn, axis=-1, keepdims=True)
        mf = jnp.maximum(mn, sn)
        pn = jnp.exp(sn - mf)
        af = jnp.exp(mn - mf)
        o8 = (af * acc + pn * vn) / (af * l + pn)
        o4 = o8[:ATT_HEADS] - lam_ref[0, 0] * o8[ATT_HEADS:]
        o_ref[0] = _subln(o4, g_ref[...], out_scale)


def _attn_decode(q, k_new, v_new, cache_k, cache_v, page_table, lam, subln, out_scale):
    b = q.shape[0]
    n_pages = page_table.shape[1]
    steps = n_pages // _DEC_PAGES
    ck = cache_k.reshape(-1, _DEC_PROW, HEAD_W)
    cv = cache_v.reshape(-1, _DEC_PROW, HEAD_W)

    def page_map(i):
        return lambda bi, j, pt: (pt[bi, j * _DEC_PAGES + i], 0, 0)

    row = lambda bi, j, pt: (bi, 0, 0)
    head_rows = pl.BlockSpec((1, ATT_HEADS, HEAD_W), row)
    page_specs = [pl.BlockSpec((1, _DEC_PROW, HEAD_W), page_map(i)) for i in range(_DEC_PAGES)]
    grid_spec = pltpu.PrefetchScalarGridSpec(
        num_scalar_prefetch=1,
        grid=(b, steps),
        in_specs=[pl.BlockSpec(memory_space=pltpu.SMEM), head_rows, head_rows, head_rows,
                  pl.BlockSpec((1, HEAD_W), lambda bi, j, pt: (0, 0))] + page_specs + page_specs,
        out_specs=head_rows,
        scratch_shapes=[pltpu.VMEM((_DEC_ROWS, 1), F32), pltpu.VMEM((_DEC_ROWS, 1), F32),
                        pltpu.VMEM((_DEC_ROWS, HEAD_W), F32)],
    )
    out = pl.pallas_call(
        functools.partial(_attn_decode_kernel, out_scale=out_scale),
        grid_spec=grid_spec,
        out_shape=jax.ShapeDtypeStruct((b, ATT_HEADS, HEAD_W), F32),
        compiler_params=pltpu.CompilerParams(dimension_semantics=("parallel", "arbitrary")),
        name="attn_decode",
    )(page_table, lam.reshape(1, 1), q.reshape(b, ATT_HEADS, HEAD_W), k_new.reshape(b, ATT_HEADS, HEAD_W),
      v_new.reshape(b, ATT_HEADS, HEAD_W), subln.reshape(1, HEAD_W), *([ck] * _DEC_PAGES), *([cv] * _DEC_PAGES))
    return out.reshape(b, ATT_WIDTH)


_S5_KT = 2
_S5_KW = S5_WIDTH // _S5_KT
_S5_KL = S5_LANES // _S5_KT
_S5_NT = S5_LANES // LANES


def _s5_input(u, bmat_ref, precise):
    re, im = [], []
    for kt in range(_S5_KT):
        r = _mm(u[:, kt * _S5_KW:(kt + 1) * _S5_KW], bmat_ref[kt], precise)
        re.append(r[:, :_S5_KL])
        im.append(r[:, _S5_KL:])
    return jnp.concatenate(re, axis=1), jnp.concatenate(im, axis=1)


def _s5_output(h_re, h_im, u, cmat_ref, d_ref, wglu_ref, bglu_ref, precise):
    if not precise:
        h_re, h_im = h_re.astype(BF16), h_im.astype(BF16)
    ys = []
    for kt in range(_S5_KT):
        sl = slice(kt * _S5_KL, (kt + 1) * _S5_KL)
        y = _mm(h_re[:, sl], cmat_ref[kt, :_S5_KL, :], precise)
        y = y + _mm(h_im[:, sl], cmat_ref[kt, _S5_KL:, :], precise)
        ys.append(y)
    y = jnp.concatenate(ys, axis=1) + d_ref[...] * u
    g = jax.nn.gelu(y)
    gate = _mm(g, wglu_ref[...], precise) + bglu_ref[...]
    return g * jax.nn.sigmoid(gate)


def _s5_scan_kernel(u_ref, are_ref, aim_ref, bmat_ref, cmat_ref, d_ref, wglu_ref, bglu_ref,
                    o_ref, hre_ref, him_ref, bu_re, bu_im, *, nb, lc, fold):
    c = pl.program_id(0)
    slots = _S5_NT // fold
    bp = fold * nb

    @pl.when(c == 0)
    def _():
        hre_ref[...] = jnp.zeros(hre_ref.shape, F32)
        him_ref[...] = jnp.zeros(him_ref.shape, F32)

    def rows_of(jt, b):
        return jt % slots, pl.ds((jt // slots) * nb + b, lc, stride=bp)

    for b in range(nb):
        re, im = _s5_input(u_ref[b], bmat_ref, False)
        for jt in range(_S5_NT):
            slot, rows = rows_of(jt, b)
            bu_re[slot, rows, :] = re[:, jt * LANES:(jt + 1) * LANES]
            bu_im[slot, rows, :] = im[:, jt * LANES:(jt + 1) * LANES]

    a_re = are_ref[...]
    a_im = aim_ref[...]

    def step(t, carry):
        hr, hi = carry
        rows = pl.ds(pl.multiple_of(t * bp, bp), bp)
        nr = hr * a_re - hi * a_im + bu_re[:, rows, :]
        ni = hr * a_im + hi * a_re + bu_im[:, rows, :]
        bu_re[:, rows, :] = nr
        bu_im[:, rows, :] = ni
        return nr, ni

    hr, hi = lax.fori_loop(0, lc, step, (hre_ref[...], him_ref[...]), unroll=2)
    hre_ref[...] = hr
    him_ref[...] = hi

    for b in range(nb):
        h_re = jnp.concatenate([bu_re[rows_of(jt, b)] for jt in range(_S5_NT)], axis=1)
        h_im = jnp.concatenate([bu_im[rows_of(jt, b)] for jt in range(_S5_NT)], axis=1)
        o_ref[b] = _s5_output(h_re, h_im, u_ref[b], cmat_ref, d_ref, wglu_ref, bglu_ref, False).astype(o_ref.dtype)


def _s5_prompt(u, prm, lc=256):
    nb, l, _ = u.shape
    bp = 8
    assert bp % nb == 0 and _S5_NT % (bp // nb) == 0
    fold = bp // nb
    slots = _S5_NT // fold
    full = lambda shape: pl.BlockSpec(shape, lambda c: (0,) * len(shape))

    def a_rows(a):
        a = jnp.transpose(a.reshape(fold, slots, LANES), (1, 0, 2))
        return jnp.repeat(a, nb, axis=1)

    o, h_re, h_im = pl.pallas_call(
        functools.partial(_s5_scan_kernel, nb=nb, lc=lc, fold=fold),
        grid=(l // lc,),
        in_specs=[pl.BlockSpec((nb, lc, S5_WIDTH), lambda c: (0, c, 0)),
                  full((slots, bp, LANES)), full((slots, bp, LANES)),
                  full((_S5_KT, _S5_KW, 2 * _S5_KL)), full((_S5_KT, 2 * _S5_KL, _S5_KW)),
                  full((1, S5_WIDTH)), full((S5_WIDTH, S5_WIDTH)), full((1, S5_WIDTH))],
        out_specs=[pl.BlockSpec((nb, lc, S5_WIDTH), lambda c: (0, c, 0)),
                   full((slots, bp, LANES)), full((slots, bp, LANES))],
        out_shape=[jax.ShapeDtypeStruct((nb, l, S5_WIDTH), BF16),
                   jax.ShapeDtypeStruct((slots, bp, LANES), F32), jax.ShapeDtypeStruct((slots, bp, LANES), F32)],
        scratch_shapes=[pltpu.VMEM((slots, lc * bp, LANES), F32) for _ in range(2)],
        compiler_params=pltpu.CompilerParams(dimension_semantics=("arbitrary",)),
        name="s5_prompt",
    )(u, a_rows(prm["a_re"]), a_rows(prm["a_im"]), prm["bmat"].astype(BF16),
      prm["cmat"].astype(BF16), prm["d"], prm["w_glu"].astype(BF16), prm["b_glu"])
    rows = lambda h: jnp.transpose(h.reshape(slots, fold, nb, LANES), (2, 1, 0, 3)).reshape(nb, S5_LANES)
    return o, rows(h_re), rows(h_im)


def _s5_step_kernel(u_ref, h0re_ref, h0im_ref, are_ref, aim_ref, bmat_ref, cmat_ref, d_ref, wglu_ref, bglu_ref,
                    o_ref, hre_ref, him_ref):
    u = u_ref[...]
    bu_re, bu_im = _s5_input(u, bmat_ref, True)
    h_re, h_im = h0re_ref[...], h0im_ref[...]
    a_re, a_im = are_ref[...], aim_ref[...]
    n_re = h_re * a_re - h_im * a_im + bu_re
    n_im = h_re * a_im + h_im * a_re + bu_im
    hre_ref[...] = n_re
    him_ref[...] = n_im
    o_ref[...] = _s5_output(n_re, n_im, u, cmat_ref, d_ref, wglu_ref, bglu_ref, True)


def _s5_step(u, h0_re, h0_im, prm):
    nb = u.shape[0]
    return pl.pallas_call(
        _s5_step_kernel,
        out_shape=[jax.ShapeDtypeStruct((nb, S5_WIDTH), F32),
                   jax.ShapeDtypeStruct((nb, S5_LANES), F32), jax.ShapeDtypeStruct((nb, S5_LANES), F32)],
        name="s5_step",
    )(u, h0_re, h0_im, prm["a_re"], prm["a_im"], prm["bmat"], prm["cmat"], prm["d"], prm["w_glu"], prm["b_glu"])


def _s5_params(lam_re, lam_im, log_dt, b_re, b_im, c_re, c_im, d, w_glu, b_glu):
    lam = lax.complex(lam_re, lam_im)
    dt = jnp.exp(log_dt)[:, None]
    a_bar = jnp.exp(lam * dt)
    b_bar = ((a_bar - 1.0) / lam)[..., None] * lax.complex(b_re, b_im)
    gk = S5_GROUPS // _S5_KT
    eye = jnp.eye(gk, dtype=F32)

    def in_tile(x):
        x = x.reshape(_S5_KT, gk, S5_STATE, S5_GROUP)
        return jnp.einsum("kgpc,gh->kgchp", x, eye).reshape(_S5_KT, _S5_KW, _S5_KL)

    def out_tile(x):
        x = x.reshape(_S5_KT, gk, S5_GROUP, S5_STATE)
        return jnp.einsum("kgcp,gh->kgphc", x, eye).reshape(_S5_KT, _S5_KL, _S5_KW)

    bmat = jnp.concatenate([in_tile(jnp.real(b_bar)), in_tile(jnp.imag(b_bar))], axis=2)
    cmat = jnp.concatenate([out_tile(c_re), out_tile(-c_im)], axis=1)
    return {
        "a_re": jnp.real(a_bar).reshape(1, S5_LANES), "a_im": jnp.imag(a_bar).reshape(1, S5_LANES),
        "bmat": bmat, "cmat": cmat, "d": d.reshape(1, S5_WIDTH), "w_glu": w_glu, "b_glu": b_glu.reshape(1, S5_WIDTH),
    }


_CONV_PAD = 8


def _conv_silu(x, prev_ref, w_ref, b_ref):
    rows = x.shape[0]
    w = w_ref[...]
    prev = prev_ref[...]
    sub = lax.broadcasted_iota(jnp.int32, prev.shape, 0)
    out = b_ref[...]
    for i in range(SSD_CONV):
        s = SSD_CONV - 1 - i
        if s == 0:
            shifted = x
        else:
            rolled = pltpu.roll(x, s, 0)
            head = jnp.where(sub < s, pltpu.roll(prev, s, 0), rolled[0:_CONV_PAD])
            shifted = jnp.concatenate([head, rolled[_CONV_PAD:]], axis=0)
        out = out + w[i:i + 1] * shifted
    prev_ref[...] = x[rows - _CONV_PAD:rows, :]
    return jax.nn.silu(out)


def _ssd_in_kernel(x_ref, g_ref, wz_ref, wx_ref, wdt_ref, cw_ref, cb_ref, z_ref, xc_ref, dt_ref, tail_ref, xf,
                   *, tiles_per_seq):
    i = pl.program_id(0)
    tm = x_ref.shape[0]

    @pl.when(i % tiles_per_seq == 0)
    def _():
        xf[...] = jnp.zeros(xf.shape, F32)

    h = _rms(x_ref[...], g_ref[...]).astype(BF16)
    z_ref[...] = _mm(h, wz_ref[...])
    dt_ref[...] = _mm(h, wdt_ref[...])
    xbc = _mm(h, wx_ref[...])
    tail_ref[0] = xbc[tm - _CONV_PAD:tm, :]
    xc_ref[...] = _conv_silu(xbc, xf, cw_ref, cb_ref)


def _ssd_in(x, g, prm, seq_len, tm, name):
    m, kdim = x.shape
    tiles_per_seq = seq_len // tm
    full = lambda a: pl.BlockSpec(a.shape, lambda i: (0,) * a.ndim, pipeline_mode=pl.Buffered(1))
    rows = lambda n: pl.BlockSpec((tm, n), lambda i: (i, 0))
    wz, wx, wdt = prm["w_z"], prm["w_xbc"], prm["w_dt_g"]
    return pl.pallas_call(
        functools.partial(_ssd_in_kernel, tiles_per_seq=tiles_per_seq),
        grid=(m // tm,),
        in_specs=[rows(kdim), pl.BlockSpec((1, kdim), lambda i: (0, 0)), full(wz), full(wx), full(wdt),
                  full(prm["conv_w"]), full(prm["conv_b"])],
        out_specs=[rows(wz.shape[1]), rows(wx.shape[1]), rows(wdt.shape[1]),
                   pl.BlockSpec((1, _CONV_PAD, wx.shape[1]), lambda i: (i // tiles_per_seq, 0, 0))],
        out_shape=[jax.ShapeDtypeStruct((m, wz.shape[1]), F32), jax.ShapeDtypeStruct((m, wx.shape[1]), F32),
                   jax.ShapeDtypeStruct((m, wdt.shape[1]), F32),
                   jax.ShapeDtypeStruct((m // seq_len, _CONV_PAD, wx.shape[1]), F32)],
        scratch_shapes=[pltpu.VMEM((_CONV_PAD, wx.shape[1]), F32)],
        compiler_params=pltpu.CompilerParams(dimension_semantics=("arbitrary",)),
        name=name,
    )(x, g.reshape(1, kdim), wz, wx, wdt, prm["conv_w"], prm["conv_b"])


def _ssd_chunk_kernel(xs_ref, bm_ref, cm_ref, dt_ref, dtb_ref, a_ref, dexp_ref, y_ref, st_ref):
    c = pl.program_id(2)
    q = SSD_CHUNK

    @pl.when(c == 0)
    def _():
        st_ref[...] = jnp.zeros(st_ref.shape, F32)

    xs = xs_ref[0]
    bm = bm_ref[0]
    bm16 = bm.astype(BF16)
    cm16 = cm_ref[0].astype(BF16)

    dt = jax.nn.softplus(dt_ref[0] + dtb_ref[0])
    da = dt * a_ref[0]
    row = lax.broadcasted_iota(jnp.int32, (q, q), 0)
    col = lax.broadcasted_iota(jnp.int32, (q, q), 1)
    causal = row >= col
    acs = jnp.dot(causal.astype(F32), da, preferred_element_type=F32, precision=HIGHEST)
    acs_t = acs.T
    acs_last = acs[q - 1:q, :]
    e_acs = jnp.exp(acs)
    w_s = dt * jnp.exp(acs_last - acs)
    dt_t = dt.T
    chunk_dec = jnp.exp(acs_last)
    cb = _mm(cm16, bm16, dims=_NT)

    first = lax.broadcasted_iota(jnp.int32, (q, LANES), 1) < SSD_HEADDIM
    first_rows = lax.broadcasted_iota(jnp.int32, (LANES, SSD_STATE), 0) < SSD_HEADDIM

    ys = []
    for i in range(SSD_HPG // 2):
        xs16 = xs[:, i * LANES:(i + 1) * LANES].astype(BF16)
        heads = (2 * i, 2 * i + 1)
        y_head, st_head = [], []
        for j in heads:
            seg = acs[:, j:j + 1] - acs_t[j:j + 1, :]
            lmat = jnp.exp(jnp.where(causal, seg, -jnp.inf))
            y_head.append(_mm(cb * lmat * dt_t[j:j + 1, :], xs16))
            st_head.append(_mm(xs16, bm * w_s[:, j:j + 1], dims=_TN))
        y_diag = jnp.where(first, y_head[0], y_head[1])
        h = st_ref[0, 2 * i:2 * i + 2].reshape(LANES, SSD_STATE)
        y_off = _mm(cm16, h, dims=_NT) * jnp.where(first, e_acs[:, heads[0]:heads[0] + 1],
                                                    e_acs[:, heads[1]:heads[1] + 1])
        cdec = jnp.concatenate([jnp.broadcast_to(chunk_dec[:, j:j + 1], (SSD_HEADDIM, SSD_STATE)) for j in heads],
                               axis=0)
        h_new = cdec * h + jnp.where(first_rows, st_head[0], st_head[1])
        st_ref[0, 2 * i:2 * i + 2] = h_new.reshape(2, SSD_HEADDIM, SSD_STATE)
        ys.append(y_diag + y_off)
    y_ref[0] = jnp.concatenate(ys, axis=1) + dexp_ref[...] * xs


def _ssd_prompt(xc, dtg, prm):
    nb, l, _ = xc.shape
    nc = l // SSD_CHUNK
    q = SSD_CHUNK
    boff = SSD_INNER // LANES
    coff = boff + SSD_GROUPS
    grp = lambda b, g, c: (g, 0, 0)
    return pl.pallas_call(
        _ssd_chunk_kernel,
        grid=(nb, SSD_GROUPS, nc),
        in_specs=[pl.BlockSpec((1, q, SSD_GW), lambda b, g, c: (b, c, g)),
                  pl.BlockSpec((1, q, LANES), lambda b, g, c: (b, c, boff + g)),
                  pl.BlockSpec((1, q, LANES), lambda b, g, c: (b, c, coff + g)),
                  pl.BlockSpec((1, q, LANES), lambda b, g, c: (b, c, g)),
                  pl.BlockSpec((1, 1, LANES), grp), pl.BlockSpec((1, 1, LANES), grp),
                  pl.BlockSpec((1, SSD_GW), lambda b, g, c: (0, g))],
        out_specs=[pl.BlockSpec((1, q, SSD_GW), lambda b, g, c: (b, c, g)),
                   pl.BlockSpec((1, SSD_HPG, SSD_HEADDIM, SSD_STATE), lambda b, g, c: (b, g, 0, 0))],
        out_shape=[jax.ShapeDtypeStruct((nb, l, SSD_INNER), F32),
                   jax.ShapeDtypeStruct((nb, SSD_HEADS, SSD_HEADDIM, SSD_STATE), F32)],
        compiler_params=pltpu.CompilerParams(dimension_semantics=("parallel", "parallel", "arbitrary")),
        name="ssd_prompt",
    )(xc, xc, xc, dtg, prm["dt_bias_g"], prm["a_g"], prm["d_exp"])


def _ssd_step_kernel(x_ref, buf_ref, cw_ref, cb_ref, dt64_ref, dtb64_ref, dt128_ref, dtb128_ref, a128_ref,
                     dexp_ref, st_ref, y_ref, so_ref):
    w = cw_ref[...]
    buf = buf_ref[0]
    conv = cb_ref[...]
    for i in range(SSD_CONV - 1):
        conv = conv + w[i:i + 1] * buf[i:i + 1]
    conv = conv + w[SSD_CONV - 1:SSD_CONV] * x_ref[0]
    xc = jax.nn.silu(conv)
    xs = xc[:, :SSD_INNER]
    xdt = xs * jax.nn.softplus(dt64_ref[0] + dtb64_ref[...])
    dt = jax.nn.softplus(dt128_ref[0] + dtb128_ref[...])
    dec = jnp.exp(dt * a128_ref[...])
    row = lax.broadcasted_iota(jnp.int32, (LANES, LANES), 0)
    col = lax.broadcasted_iota(jnp.int32, (LANES, LANES), 1)
    diag = row == col
    ys = []
    for i in range(SSD_HEADS // 2):
        g = (2 * i) // SSD_HPG
        bm = xc[:, SSD_INNER + g * SSD_STATE:SSD_INNER + (g + 1) * SSD_STATE]
        cm = xc[:, SSD_INNER + (SSD_GROUPS + g) * SSD_STATE:SSD_INNER + (SSD_GROUPS + g + 1) * SSD_STATE]
        xp = jnp.broadcast_to(xdt[:, i * LANES:(i + 1) * LANES], (LANES, LANES))
        outer = _mm(jnp.where(diag, xp, 0.0), jnp.broadcast_to(bm, (LANES, LANES)), True)
        h = st_ref[0, 2 * i:2 * i + 2].reshape(LANES, SSD_STATE)
        dpair = jnp.concatenate([jnp.broadcast_to(dec[2 * i:2 * i + 1], (SSD_HEADDIM, LANES)),
                                 jnp.broadcast_to(dec[2 * i + 1:2 * i + 2], (SSD_HEADDIM, LANES))], axis=0)
        hn = dpair * h + outer
        so_ref[0, 2 * i:2 * i + 2] = hn.reshape(2, SSD_HEADDIM, SSD_STATE)
        ys.append(_mm(jnp.broadcast_to(cm, (8, SSD_STATE)), hn, True, _NT)[0:1])
    y_ref[0] = jnp.concatenate(ys, axis=1) + dexp_ref[...] * xs


def _dt_expand_kernel(dt_ref, e64_ref, e128_ref, o64_ref, o128_ref):
    dt = dt_ref[...]
    o64_ref[...] = _mm(dt, e64_ref[...], True)
    o128_ref[...] = _mm(dt, e128_ref[...], True)


def _dt_expand(dt, prm):
    nb = dt.shape[0]
    return pl.pallas_call(
        _dt_expand_kernel,
        out_shape=[jax.ShapeDtypeStruct((nb, SSD_INNER), F32), jax.ShapeDtypeStruct((nb, SSD_HEADS * LANES), F32)],
        name="ssd_dt_expand",
    )(dt, prm["expand_64"], prm["expand_128"])


def _ssd_step(xbc, conv_buf, dt64, dt128, state, prm):
    nb = xbc.shape[0]
    full = lambda shape: pl.BlockSpec(shape, lambda b: (0,) * len(shape))
    y, so = pl.pallas_call(
        _ssd_step_kernel,
        grid=(nb,),
        in_specs=[pl.BlockSpec((1, 1, SSD_CONV_DIM), lambda b: (b, 0, 0)),
                  pl.BlockSpec((1, SSD_CONV - 1, SSD_CONV_DIM), lambda b: (b, 0, 0)),
                  full((SSD_CONV, SSD_CONV_DIM)), full((1, SSD_CONV_DIM)),
                  pl.BlockSpec((1, 1, SSD_INNER), lambda b: (b, 0, 0)), full((1, SSD_INNER)),
                  pl.BlockSpec((1, SSD_HEADS, LANES), lambda b: (b, 0, 0)), full((SSD_HEADS, LANES)),
                  full((SSD_HEADS, LANES)), full((1, SSD_INNER)),
                  pl.BlockSpec((1, SSD_HEADS, SSD_HEADDIM, SSD_STATE), lambda b: (b, 0, 0, 0))],
        out_specs=[pl.BlockSpec((1, 1, SSD_INNER), lambda b: (b, 0, 0)),
                   pl.BlockSpec((1, SSD_HEADS, SSD_HEADDIM, SSD_STATE), lambda b: (b, 0, 0, 0))],
        out_shape=[jax.ShapeDtypeStruct((nb, 1, SSD_INNER), F32),
                   jax.ShapeDtypeStruct((nb, SSD_HEADS, SSD_HEADDIM, SSD_STATE), F32)],
        compiler_params=pltpu.CompilerParams(dimension_semantics=("parallel",)),
        name="ssd_step",
    )(xbc.reshape(nb, 1, SSD_CONV_DIM), conv_buf, prm["conv_w"], prm["conv_b"],
      dt64.reshape(nb, 1, SSD_INNER), prm["dt_bias_64"], dt128.reshape(nb, SSD_HEADS, LANES),
      prm["dt_bias_128"], prm["a_128"], prm["d_exp"], state)
    return y.reshape(nb, SSD_INNER), so


_GROUP_LANE0 = MOE_EXPERTS
_ROUTE_LANE0 = 64


def _pack_bf16_halves(x):
    n = x.shape[1] // 2
    lo = lax.bitcast_convert_type(x[:, :n].astype(BF16).astype(F32), jnp.uint32)
    hi = lax.bitcast_convert_type(x[:, n:].astype(BF16).astype(F32), jnp.uint32)
    return (lo >> 16) | hi


def _unpack_bf16_halves(p):
    lo = lax.bitcast_convert_type(p << 16, F32)
    hi = lax.bitcast_convert_type(p & jnp.uint32(0xFFFF0000), F32)
    return jnp.concatenate([lo, hi], axis=1)


def _moe_router_kernel(y_ref, g_ref, wr_ref, br_ref, xn_ref, comb_ref, cnt_ref):
    @pl.when(pl.program_id(0) == 0)
    def _():
        cnt_ref[...] = jnp.zeros(cnt_ref.shape, F32)

    xn = _rms(y_ref[...], g_ref[...])
    xn_ref[...] = _pack_bf16_halves(xn) if xn_ref.dtype == jnp.uint32 else xn.astype(xn_ref.dtype)
    lg = _mm(xn, wr_ref[...], True) + br_ref[...]
    lane = lax.broadcasted_iota(jnp.int32, lg.shape, 1)
    is_group = (lane >= _GROUP_LANE0) & (lane < _GROUP_LANE0 + MOE_GROUPS)
    gl = jnp.where(is_group, lg, NEG)
    gmax = jnp.max(gl, axis=-1, keepdims=True)
    g_p = 1.0 / jnp.sum(jnp.exp(gl - gmax), axis=-1, keepdims=True)
    gidx = jnp.min(jnp.where(gl == gmax, lane - _GROUP_LANE0, MOE_GROUPS), axis=-1, keepdims=True)
    el = jnp.where((lane < MOE_EXPERTS) & (lane // MOE_EPG == gidx), lg, NEG)
    m1 = jnp.max(el, axis=-1, keepdims=True)
    i1 = jnp.min(jnp.where(el == m1, lane, LANES), axis=-1, keepdims=True)
    el2 = jnp.where(lane == i1, NEG, el)
    m2 = jnp.max(el2, axis=-1, keepdims=True)
    i2 = jnp.min(jnp.where(el2 == m2, lane, LANES), axis=-1, keepdims=True)
    e2 = jnp.exp(m2 - m1)
    den = 1.0 + e2
    g1 = g_p / den
    g2 = g_p * (e2 / den)
    oh1 = lane == i1
    oh2 = lane == i2
    comb = jnp.where(oh1, g1, 0.0) + jnp.where(oh2, g2, 0.0)
    picks = oh1.astype(F32) + oh2.astype(F32)
    tm = picks.shape[0]
    before = lax.broadcasted_iota(jnp.int32, (tm, tm), 0) > lax.broadcasted_iota(jnp.int32, (tm, tm), 1)
    seen = cnt_ref[...] + _mm(before.astype(F32), picks)
    r1 = jnp.sum(jnp.where(oh1, seen, 0.0), axis=-1, keepdims=True)
    r2 = jnp.sum(jnp.where(oh2, seen, 0.0), axis=-1, keepdims=True)
    cnt_ref[...] += jnp.sum(picks, axis=0, keepdims=True)
    for k, v in enumerate((i1.astype(F32), i2.astype(F32), g1, g2, r1, r2)):
        comb = jnp.where(lane == _ROUTE_LANE0 + k, v, comb)
    comb_ref[...] = comb


def _moe_router(y, g, wr, br, tm, name, xn_dtype):
    m, d = y.shape
    dx = d // 2 if xn_dtype == jnp.uint32 else d
    return pl.pallas_call(
        _moe_router_kernel,
        grid=(m // tm,),
        in_specs=[pl.BlockSpec((tm, d), lambda i: (i, 0)), pl.BlockSpec((1, d), lambda i: (0, 0)),
                  pl.BlockSpec((d, LANES), lambda i: (0, 0)), pl.BlockSpec((1, LANES), lambda i: (0, 0))],
        out_specs=[pl.BlockSpec((tm, dx), lambda i: (i, 0)), pl.BlockSpec((tm, LANES), lambda i: (i, 0)),
                   pl.BlockSpec((1, LANES), lambda i: (0, 0))],
        out_shape=[jax.ShapeDtypeStruct((m, dx), xn_dtype), jax.ShapeDtypeStruct((m, LANES), F32),
                   jax.ShapeDtypeStruct((1, LANES), F32)],
        compiler_params=pltpu.CompilerParams(dimension_semantics=("arbitrary",)),
        name=name,
    )(y, g.reshape(1, d), wr, br)


def _moe_dense_kernel(res_ref, xn_ref, comb_ref, wg_ref, wu_ref, wd_ref, o_ref, acc, *, precise):
    e = pl.program_id(1)

    @pl.when(e == 0)
    def _():
        acc[...] = jnp.zeros(acc.shape, F32)

    x = xn_ref[...]
    hdn = jax.nn.silu(_mm(x, wg_ref[0], precise)) * _mm(x, wu_ref[0], precise)
    comb = comb_ref[...]
    lane = lax.broadcasted_iota(jnp.int32, comb.shape, 1)
    gate = jnp.sum(jnp.where(lane == e, comb, 0.0), axis=-1, keepdims=True)
    acc[...] += gate * _mm(hdn, wd_ref[0], precise)

    @pl.when(e == pl.num_programs(1) - 1)
    def _():
        o_ref[...] = res_ref[...] + acc[...]


def _moe_dense(res, xn, comb, wg, wu, wd, e0, tm, name, precise):
    m, d = res.shape
    return pl.pallas_call(
        functools.partial(_moe_dense_kernel, precise=precise),
        grid=(m // tm, MOE_EXPERTS),
        in_specs=[pl.BlockSpec((tm, d), lambda i, e: (i, 0)), pl.BlockSpec((tm, d), lambda i, e: (i, 0)),
                  pl.BlockSpec((tm, LANES), lambda i, e: (i, 0)),
                  pl.BlockSpec((1, d, MOE_FF), lambda i, e: (e0 + e, 0, 0)),
                  pl.BlockSpec((1, d, MOE_FF), lambda i, e: (e0 + e, 0, 0)),
                  pl.BlockSpec((1, MOE_FF, d), lambda i, e: (e0 + e, 0, 0))],
        out_specs=pl.BlockSpec((tm, d), lambda i, e: (i, 0)),
        out_shape=jax.ShapeDtypeStruct((m, d), F32),
        scratch_shapes=[pltpu.VMEM((tm, d), F32)],
        compiler_params=pltpu.CompilerParams(dimension_semantics=("parallel", "arbitrary")),
        name=name,
    )(res, xn, comb, wg, wu, wd)


def _moe(y, g, prm, tm_r, tm_e, tag, precise):
    xn, comb, _ = _moe_router(y, g, prm["wr"], prm["br"], tm_r, "moe_router_" + tag, F32 if precise else BF16)
    return _moe_dense(y, xn, comb, prm["wg"], prm["wu"], prm["wd"], prm["e0"], tm_e, "moe_experts_" + tag, precise)


def _moe_grouped_kernel(te_ref, nt_ref, x_ref, wg_ref, wu_ref, wd_ref, o_ref):
    del te_ref
    active = pl.program_id(0) < nt_ref[0]

    @pl.when(active)
    def _():
        x = _unpack_bf16_halves(x_ref[...]).astype(BF16)
        hdn = jax.nn.silu(_mm(x, wg_ref[0])) * _mm(x, wu_ref[0])
        o_ref[...] = _mm(hdn, wd_ref[0]).astype(o_ref.dtype)

    @pl.when(jnp.logical_not(active))
    def _():
        o_ref[...] = jnp.zeros(o_ref.shape, o_ref.dtype)


def _moe_grouped(x_rows, tile_expert, n_tiles, wg, wu, wd, tm, name):
    r = x_rows.shape[0]
    d = wg.shape[1]
    grid_spec = pltpu.PrefetchScalarGridSpec(
        num_scalar_prefetch=2,
        grid=(r // tm,),
        in_specs=[pl.BlockSpec((tm, x_rows.shape[1]), lambda i, te, nt: (i, 0)),
                  pl.BlockSpec((1, d, MOE_FF), lambda i, te, nt: (te[i], 0, 0)),
                  pl.BlockSpec((1, d, MOE_FF), lambda i, te, nt: (te[i], 0, 0)),
                  pl.BlockSpec((1, MOE_FF, d), lambda i, te, nt: (te[i], 0, 0))],
        out_specs=pl.BlockSpec((tm, d), lambda i, te, nt: (i, 0)),
    )
    return pl.pallas_call(
        _moe_grouped_kernel,
        grid_spec=grid_spec,
        out_shape=jax.ShapeDtypeStruct((r, d), F32),
        compiler_params=pltpu.CompilerParams(dimension_semantics=("arbitrary",)),
        name=name,
    )(tile_expert, n_tiles, x_rows, wg, wu, wd)


def _moe_sparse(y, g, prm, tm_r, tm_e, tag):
    t, d = y.shape
    xn, route, cnt = _moe_router(y, g, prm["wr"], prm["br"], tm_r, "moe_router_" + tag, jnp.uint32)
    lanes = lambda k: route[:, _ROUTE_LANE0 + k:_ROUTE_LANE0 + k + 2]
    ids = lanes(0).astype(jnp.int32)
    gates = lanes(2)
    rank = lanes(4).astype(jnp.int32)
    counts = cnt[0, :MOE_EXPERTS].astype(jnp.int32)
    rows = 2 * t + MOE_EXPERTS * tm_e
    pcounts = ((counts + tm_e - 1) // tm_e) * tm_e
    pend = jnp.cumsum(pcounts)
    pstart = pend - pcounts
    pos = pstart[ids] + rank
    tile_start = jnp.arange(rows // tm_e, dtype=jnp.int32) * tm_e
    tile_expert = jnp.minimum(jnp.sum((pend[None, :] <= tile_start[:, None]).astype(jnp.int32), axis=1),
                              MOE_EXPERTS - 1)
    n_tiles = (pend[-1:] // tm_e).astype(jnp.int32)
    token = jnp.broadcast_to(jnp.arange(t, dtype=jnp.int32)[:, None], (t, 2))
    src = jnp.zeros((rows,), jnp.int32).at[pos.reshape(-1)].set(token.reshape(-1), unique_indices=True)
    x_rows = xn[src]
    out_rows = _moe_grouped(x_rows, tile_expert + prm["e0"], n_tiles, prm["wg"], prm["wu"], prm["wd"], tm_e,
                            "moe_experts_" + tag)
    return y + gates[:, 0:1] * out_rows[pos[:, 0]] + gates[:, 1:2] * out_rows[pos[:, 1]]


def _moe_params(layer, w_group, b_group, w_expert, b_expert, w_gate, w_up, w_down):
    d = w_group.shape[0]
    pad = LANES - MOE_EXPERTS - MOE_GROUPS
    wr = jnp.concatenate([w_expert, w_group, jnp.zeros((d, pad), F32)], axis=1)
    br = jnp.concatenate([b_expert, b_group, jnp.zeros((pad,), F32)]).reshape(1, LANES)
    flat = lambda w: w.reshape((-1,) + w.shape[2:])
    return {"wr": wr, "br": br, "wg": flat(w_gate), "wu": flat(w_up), "wd": flat(w_down), "e0": layer * MOE_EXPERTS}


def _ssd_params(w_in, conv_w, conv_b, dt_bias, a_log, d, norm, w_out):
    a = -jnp.exp(a_log)
    w_dt = w_in[:, SSD_INNER + SSD_CONV_DIM:]

    def grouped(x):
        x = x.reshape(x.shape[:-1] + (SSD_GROUPS, SSD_HPG))
        x = jnp.pad(x, [(0, 0)] * (x.ndim - 1) + [(0, LANES - SSD_HPG)])
        return x.reshape(x.shape[:-2] + (SSD_GROUPS * LANES,))

    w_z = w_in[:, :SSD_INNER]
    w_xbc = w_in[:, SSD_INNER:SSD_INNER + SSD_CONV_DIM]
    eye = jnp.eye(SSD_HEADS, dtype=F32)
    return {
        "w_z": w_z.astype(BF16), "w_xbc": w_xbc.astype(BF16), "w_dt_g": grouped(w_dt).astype(BF16),
        "w_z_f32": w_z, "w_xbc_f32": w_xbc, "w_dt_f32": w_dt, "w_out_f32": w_out,
        "expand_64": jnp.repeat(eye, SSD_HEADDIM, axis=1), "expand_128": jnp.repeat(eye, LANES, axis=1),
        "conv_w": conv_w, "conv_b": conv_b.reshape(1, SSD_CONV_DIM),
        "dt_bias_g": grouped(dt_bias).reshape(SSD_GROUPS, 1, LANES),
        "a_g": grouped(a).reshape(SSD_GROUPS, 1, LANES),
        "dt_bias_64": jnp.repeat(dt_bias, SSD_HEADDIM).reshape(1, SSD_INNER),
        "dt_bias_128": jnp.broadcast_to(dt_bias[:, None], (SSD_HEADS, LANES)),
        "a_128": jnp.broadcast_to(a[:, None], (SSD_HEADS, LANES)),
        "d_exp": jnp.repeat(d, SSD_HEADDIM).reshape(1, SSD_INNER),
        "norm": norm, "w_out": w_out.astype(BF16),
    }


_TM = 512


def kernel(x_prompt, x_sample, cache_k, cache_v, page_table, state_s5_re, state_s5_im, state_ssd, state_conv, norm_mix, norm_ffn, norm_final, even_w_in, even_w_out, diff_lam_q1, diff_lam_k1, diff_lam_q2, diff_lam_k2, diff_subln, s5_lam_re, s5_lam_im, s5_log_dt, s5_b_re, s5_b_im, s5_c_re, s5_c_im, s5_d, s5_w_glu, s5_b_glu, ssd_w_in, ssd_conv_w, ssd_conv_b, ssd_dt_bias, ssd_a_log, ssd_d, ssd_norm, ssd_w_out, moe_w_group, moe_b_group, moe_w_expert, moe_b_expert, moe_w_gate, moe_w_up, moe_w_down):
    bp, sp, d = x_prompt.shape
    bs = x_sample.shape[0]
    tp = bp * sp
    depth = norm_mix.shape[0]
    y_p = x_prompt.reshape(tp, d)
    y_s = x_sample.reshape(bs, d)
    outs = {n: [] for n in ("k_p", "v_p", "k_s", "v_s", "re_p", "im_p", "re_s", "im_s", "ssd_p", "ssd_s", "cv_p", "cv_s")}

    for li in range(depth):
        if li % 2 == 0:
            e = li // 2
            lam_init = 0.8 - 0.6 * math.exp(-0.3 * li)
            lam = (jnp.exp(jnp.sum(diff_lam_q1[e] * diff_lam_k1[e])) - jnp.exp(jnp.sum(diff_lam_q2[e] * diff_lam_k2[e]))
                   + lam_init).astype(F32)
            ws32 = [even_w_in[e][:, i * ATT_WIDTH:(i + 1) * ATT_WIDTH] for i in range(4)]
            ws = [w.astype(BF16) for w in ws32]
            w_out32 = [even_w_out[e][:ATT_WIDTH], even_w_out[e][ATT_WIDTH:]]
            w_out_a, w_out_s = [w.astype(BF16) for w in w_out32]
            s5p = _s5_params(s5_lam_re[e], s5_lam_im[e], s5_log_dt[e], s5_b_re[e], s5_b_im[e], s5_c_re[e], s5_c_im[e],
                             s5_d[e], s5_w_glu[e], s5_b_glu[e])
            dts = [(BF16,), (_HEAD_ROWS, BF16), (_HEAD_ROWS, _TILE_T), (F32,)]
            q, k, k16, v, vt, u = _norm_matmul(y_p, norm_mix[li], ws, dts, _TM, "even_in_p")
            o = _attn_prompt(q.reshape(bp, sp, ATT_WIDTH), k16.reshape(bp, sp, ATT_WIDTH), vt, lam, diff_subln[e],
                             1.0 - lam_init)
            s5o, h_re, h_im = _s5_prompt(u.reshape(bp, sp, S5_WIDTH), s5p)
            y_p = _matmul_res(y_p, [o.reshape(tp, ATT_WIDTH), s5o.reshape(tp, S5_WIDTH)], [w_out_a, w_out_s], _TM,
                              "even_out_p")
            outs["k_p"].append(k.reshape(bp, sp, ATT_HEADS, HEAD_W))
            outs["v_p"].append(v.reshape(bp, sp, ATT_HEADS, HEAD_W))
            outs["re_p"].append(h_re.reshape(bp, S5_GROUPS, S5_STATE))
            outs["im_p"].append(h_im.reshape(bp, S5_GROUPS, S5_STATE))
            q, k, v, u = _norm_matmul(y_s, norm_mix[li], ws32, [(F32,), (_HEAD_ROWS,), (_HEAD_ROWS,), (F32,)], bs,
                                      "even_in_s", True)
            o = _attn_decode(q, k, v, cache_k, cache_v, page_table + e * cache_k.shape[1], lam, diff_subln[e],
                             1.0 - lam_init)
            s5o, h_re, h_im = _s5_step(u, state_s5_re[e].reshape(bs, S5_LANES), state_s5_im[e].reshape(bs, S5_LANES), s5p)
            y_s = _matmul_res(y_s, [o, s5o], w_out32, bs, "even_out_s", True)
            outs["k_s"].append(k.reshape(bs, 1, ATT_HEADS, HEAD_W))
            outs["v_s"].append(v.reshape(bs, 1, ATT_HEADS, HEAD_W))
            outs["re_s"].append(h_re.reshape(bs, S5_GROUPS, S5_STATE))
            outs["im_s"].append(h_im.reshape(bs, S5_GROUPS, S5_STATE))
        else:
            o_ = li // 2
            sp_ = _ssd_params(ssd_w_in[o_], ssd_conv_w[o_], ssd_conv_b[o_], ssd_dt_bias[o_], ssd_a_log[o_], ssd_d[o_],
                              ssd_norm[o_], ssd_w_out[o_])
            z, xc, dtg, tail = _ssd_in(y_p, norm_mix[li], sp_, sp, 512, "ssd_in_p")
            yssd, st = _ssd_prompt(xc.reshape(bp, sp, SSD_CONV_DIM), dtg.reshape(bp, sp, SSD_GROUPS * LANES), sp_)
            y_p = _gated_norm_matmul(y_p, yssd.reshape(tp, SSD_INNER), z, sp_["norm"], sp_["w_out"], 256, "ssd_out_p")
            outs["ssd_p"].append(st)
            outs["cv_p"].append(tail[:, _CONV_PAD - (SSD_CONV - 1):, :])
            z, xbc, dtc = _norm_matmul(y_s, norm_mix[li], [sp_["w_z_f32"], sp_["w_xbc_f32"], sp_["w_dt_f32"]],
                                       [(F32,), (F32,), (F32,)], bs, "ssd_in_s", True)
            dt64, dt128 = _dt_expand(dtc, sp_)
            yssd, st = _ssd_step(xbc, state_conv[o_], dt64, dt128, state_ssd[o_], sp_)
            y_s = _gated_norm_matmul(y_s, yssd, z, sp_["norm"], sp_["w_out_f32"], bs, "ssd_out_s", True)
            outs["ssd_s"].append(st)
            outs["cv_s"].append(jnp.concatenate([state_conv[o_][:, 1:], xbc[:, None, :]], axis=1))
        mp = _moe_params(li, moe_w_group[li], moe_b_group[li], moe_w_expert[li], moe_b_expert[li], moe_w_gate,
                         moe_w_up, moe_w_down)
        y_p = _moe_sparse(y_p, norm_ffn[li], mp, _TM, 256, "p")
        y_s = _moe(y_s, norm_ffn[li], mp, bs, bs, "s", li + 1 < depth)

    y_prompt = _rmsnorm(y_p, norm_final, _TM, "final_p").reshape(bp, sp, d)
    y_sample = _rmsnorm(y_s, norm_final, bs, "final_s").reshape(bs, 1, d)
    st = lambda n: jnp.stack(outs[n])
    return (y_prompt, y_sample, st("k_p"), st("v_p"), st("k_s"), st("v_s"), st("re_p"), st("im_p"), st("re_s"),
            st("im_s"), st("ssd_p"), st("ssd_s"), st("cv_p"), st("cv_s"))
```
